```python
import math
import jax, jax.numpy as jnp
from jax import lax
import numpy as np

D_MODEL = 1024
BATCH = 4
SEQ = 8192
DEPTH = 4

GRID_W = 64
CTX_LEN = 256
HEAD_DIM = 64
NA_HEADS = 6
NA_W = NA_HEADS * HEAD_DIM
NA_ROWS = 8
NA_COLS = 16
NA_QBLK = 16
NA_KBLK = 32
S5_GROUP_CH = 16
S5_CH = 256
S5_GROUPS = S5_CH // S5_GROUP_CH
S5_STATE = 64
S5_DT_MIN = 1e-3
S5_DT_MAX = 1e-1
S5_EIG_MAX = -1e-4
SW_HEADS = 6
SW_KV_HEADS = 2
SW_W = SW_HEADS * HEAD_DIM
SW_KV_W = SW_KV_HEADS * HEAD_DIM
SW_WINDOW = 128
SW_BLK = 128
ROPE_BASE = 10000.0
MIX_W = NA_W + S5_CH + SW_W
IN_COLS = 3 * NA_W + S5_CH + SW_W + 2 * SW_KV_W
PROJ_SPLITS = (NA_W, 2 * NA_W, 3 * NA_W, 3 * NA_W + S5_CH, 3 * NA_W + S5_CH + SW_W, 3 * NA_W + S5_CH + SW_W + SW_KV_W)
N_EXPERTS = 32
TOP_K = 4
EXPERT_FF = D_MODEL
MOE_BLK = 256
SWIGLU_LIMIT = 7.0
SWIGLU_ALPHA = 1.702
RMS_EPS = 1e-6
NEG_INF = -1e30

kernel_name = 'hybrid_na_s5_swa_moe_dit'


def _rmsnorm(x, g):
    x32 = x.astype(jnp.float32)
    y = x32 * lax.rsqrt(jnp.mean(x32 * x32, axis=-1, keepdims=True) + RMS_EPS)
    return (y * g.astype(jnp.float32)).astype(x.dtype)


def _modulate(h, shift, scale):
    return h * (1.0 + scale) + shift


def _heads(t):
    return t.reshape(t.shape[:-1] + (t.shape[-1] // HEAD_DIM, HEAD_DIM))


def _axial_rope(length, dim):
    t = jnp.arange(length)
    row = (t // GRID_W).astype(jnp.float32)
    col = (t % GRID_W).astype(jnp.float32)
    nf = dim // 4
    inv = ROPE_BASE ** (-jnp.arange(nf, dtype=jnp.float32) / nf)
    ar = row[:, None] * inv
    ac = col[:, None] * inv
    ang = jnp.concatenate([ar, ar, ac, ac], axis=-1)[:, None, :]
    return jnp.cos(ang), jnp.sin(ang)


def _apply_rope(x, cos, sin):
    xs = x.reshape(x.shape[:-1] + (2, 2, x.shape[-1] // 4))
    rot = jnp.stack([-xs[..., 1, :], xs[..., 0, :]], axis=-2).reshape(x.shape)
    return (x * cos + rot * sin).astype(x.dtype)


def _na_col_tables():
    n_qb = GRID_W // NA_QBLK
    kc0 = np.clip(np.arange(n_qb) * NA_QBLK - NA_COLS // 2, 0, GRID_W - NA_KBLK)
    qcol = np.arange(n_qb)[:, None] * NA_QBLK + np.arange(NA_QBLK)[None, :]
    kcol = kc0[:, None] + np.arange(NA_KBLK)[None, :]
    ws = np.clip(qcol - NA_COLS // 2, 0, GRID_W - NA_COLS)
    valid = (kcol[:, None, :] >= ws[:, :, None]) & (kcol[:, None, :] < ws[:, :, None] + NA_COLS)
    dc = np.clip(kcol[:, None, :] - qcol[:, :, None] + NA_COLS - 1, 0, 2 * NA_COLS - 2)
    return kc0, valid, dc


def _neighbourhood_attention(q, k, v, qc, kc, vc, rpb, with_ctx):
    bsz, seq, nh, hd = q.shape
    ctx_len = kc.shape[1]
    rows = seq // GRID_W
    kr = min(NA_ROWS, rows)
    scale = hd ** -0.5
    kc0, valid, dc = _na_col_tables()
    n_qb = kc0.shape[0]
    nk = kr * NA_KBLK
    mask = np.broadcast_to(valid[:, :, None, :], (n_qb, NA_QBLK, kr, NA_KBLK)).reshape(n_qb, NA_QBLK, nk)
    rpb_c = rpb[:, :, dc]
    kg = k.reshape(bsz, rows, GRID_W, nh, hd)
    vg = v.reshape(bsz, rows, GRID_W, nh, hd)
    qg = jnp.moveaxis(q.reshape(bsz, rows, GRID_W, nh, hd), 1, 0)

    def row_block(args):
        q_r, r = args
        rs = jnp.clip(r - kr // 2, 0, rows - kr)
        k_win = lax.dynamic_slice_in_dim(kg, rs, kr, axis=1)
        v_win = lax.dynamic_slice_in_dim(vg, rs, kr, axis=1)
        kb = jnp.stack([k_win[:, :, int(s):int(s) + NA_KBLK] for s in kc0], axis=1).reshape(bsz, n_qb, nk, nh, hd)
        vb = jnp.stack([v_win[:, :, int(s):int(s) + NA_KBLK] for s in kc0], axis=1).reshape(bsz, n_qb, nk, nh, hd)
        qb = q_r.reshape(bsz, n_qb, NA_QBLK, nh, hd)
        dr_idx = rs + jnp.arange(kr) - r + NA_ROWS - 1
        bias = jnp.take(rpb_c, dr_idx, axis=1)
        bias = bias.transpose(0, 2, 3, 1, 4).reshape(nh, n_qb, NA_QBLK, nk).astype(jnp.float32)
        s_nb = jnp.einsum('bjqhd,bjkhd->bhjqk', qb, kb).astype(jnp.float32) * scale + bias
        s_nb = jnp.where(mask, s_nb, NEG_INF)
        s_cx = jnp.einsum('bjqhd,bchd->bhjqc', qb, kc).astype(jnp.float32) * scale
        p = jax.nn.softmax(jnp.concatenate([s_nb, s_cx], axis=-1), axis=-1).astype(q.dtype)
        o = (jnp.einsum('bhjqk,bjkhd->bjqhd', p[..., :nk], vb)
             + jnp.einsum('bhjqc,bchd->bjqhd', p[..., nk:], vc))
        return o.reshape(bsz, GRID_W, nh, hd)

    o = lax.map(row_block, (qg, jnp.arange(rows)))
    o = jnp.moveaxis(o, 0, 1).reshape(bsz, seq, nh * hd)
    oc = None
    if with_ctx:
        sc = jnp.einsum('bqhd,bkhd->bhqk', qc, kc).astype(jnp.float32) * scale
        pc = jax.nn.softmax(sc, axis=-1).astype(q.dtype)
        oc = jnp.einsum('bhqk,bkhd->bqhd', pc, vc).reshape(bsz, ctx_len, nh * hd)
    return o, oc


def _window_gqa(q, k, v, qc, kc, vc, sinks, with_ctx):
    bsz, seq, n_q, hd = q.shape
    n_kv = k.shape[2]
    grp = n_q // n_kv
    ctx_len = kc.shape[1]
    nb = seq // SW_BLK
    scale = hd ** -0.5
    sink_logit = sinks.astype(jnp.float32).reshape(n_kv, grp, 1, 1)
    qb = q.reshape(bsz, nb, SW_BLK, n_kv, grp, hd)

    def band(t):
        tp = jnp.pad(t, ((0, 0), (SW_BLK, SW_BLK), (0, 0), (0, 0))).reshape(bsz, nb + 2, SW_BLK, n_kv, hd)
        return jnp.concatenate([tp[:, :-2], tp[:, 1:-1], tp[:, 2:]], axis=2)

    kb, vb = band(k), band(v)
    qpos = np.arange(nb)[:, None] * SW_BLK + np.arange(SW_BLK)[None, :]
    kpos = (np.arange(nb)[:, None] - 1) * SW_BLK + np.arange(3 * SW_BLK)[None, :]
    valid = ((np.abs(qpos[:, :, None] - kpos[:, None, :]) <= SW_WINDOW)
             & (kpos[:, None, :] >= 0) & (kpos[:, None, :] < seq))
    s_loc = jnp.einsum('bnqhgd,bnkhd->bnhgqk', qb, kb).astype(jnp.float32) * scale
    s_loc = jnp.where(valid[:, None, None], s_loc, NEG_INF)
    s_ctx = jnp.einsum('bnqhgd,bchd->bnhgqc', qb, kc).astype(jnp.float32) * scale
    s_sink = jnp.broadcast_to(sink_logit, s_ctx.shape[:-1] + (1,))
    p = jax.nn.softmax(jnp.concatenate([s_loc, s_ctx, s_sink], axis=-1), axis=-1).astype(q.dtype)
    nk = 3 * SW_BLK
    o = (jnp.einsum('bnhgqk,bnkhd->bnqhgd', p[..., :nk], vb)
         + jnp.einsum('bnhgqc,bchd->bnqhgd', p[..., nk:nk + ctx_len], vc))
    o = o.reshape(bsz, seq, n_q * hd)
    oc = None
    if with_ctx:
        qcb = qc.reshape(bsz, ctx_len, n_kv, grp, hd)
        sc = jnp.einsum('bqhgd,bkhd->bhgqk', qcb, kc).astype(jnp.float32) * scale
        sc_sink = jnp.broadcast_to(sink_logit, sc.shape[:-1] + (1,))
        pc = jax.nn.softmax(jnp.concatenate([sc, sc_sink], axis=-1), axis=-1).astype(q.dtype)
        oc = jnp.einsum('bhgqk,bkhd->bqhgd', pc[..., :ctx_len], vc).reshape(bsz, ctx_len, n_q * hd)
    return o, oc


def _ssm_drive(u, b_bar):
    return lax.complex(jnp.einsum('blgh,gph->blgp', u, jnp.real(b_bar)),
                       jnp.einsum('blgh,gph->blgp', u, jnp.imag(b_bar)))


def _ssm_readout(xs, c_re, c_im):
    return (jnp.einsum('blgp,ghp->blgh', jnp.real(xs), c_re.astype(jnp.float32))
            - jnp.einsum('blgp,ghp->blgh', jnp.imag(xs), c_im.astype(jnp.float32)))


def _diag_scan(lam_bar, drive):
    a = jnp.broadcast_to(lam_bar, drive.shape)

    def combine(e1, e2):
        a1, b1 = e1
        a2, b2 = e2
        return a1 * a2, a2 * b1 + b2

    _, xs = lax.associative_scan(combine, (a, drive), axis=1)
    return xs


def _s5_mixer(u, uc, a_re, a_im, log_step, b_re, b_im, c_re, c_im, d_skip, w_glu, b_glu, with_ctx):
    f32 = jnp.float32
    bsz, seq, _ = u.shape
    ctx_len = uc.shape[1]
    u32 = u.astype(f32).reshape(bsz, seq, S5_GROUPS, S5_GROUP_CH)
    uc32 = uc.astype(f32).reshape(bsz, ctx_len, S5_GROUPS, S5_GROUP_CH)
    lam = lax.complex(jnp.minimum(a_re.astype(f32), S5_EIG_MAX), a_im.astype(f32))
    step = jnp.exp(log_step.astype(f32))[..., None]
    lam_bar = jnp.exp(lam * step)
    b_bar = ((lam_bar - 1.0) / lam)[..., None] * lax.complex(b_re.astype(f32), b_im.astype(f32))
    d32 = d_skip.astype(f32).reshape(S5_GROUPS, S5_GROUP_CH)
    y = d32 * u32
    yc = d32 * uc32
    for direction in range(2):
        rev = direction == 1
        seq_u = jnp.flip(u32, axis=1) if rev else u32
        seq_c = jnp.flip(uc32, axis=1) if rev else uc32
        xc = _diag_scan(lam_bar[direction], _ssm_drive(seq_c, b_bar[direction]))
        drive = _ssm_drive(seq_u, b_bar[direction])
        drive = drive.at[:, 0].add(lam_bar[direction] * xc[:, -1])
        xs = _diag_scan(lam_bar[direction], drive)
        out = _ssm_readout(xs, c_re[direction], c_im[direction])
        y = y + (jnp.flip(out, axis=1) if rev else out)
        if with_ctx:
            outc = _ssm_readout(xc, c_re[direction], c_im[direction])
            yc = yc + (jnp.flip(outc, axis=1) if rev else outc)

    def glu(t):
        g = jax.nn.gelu(t.reshape(t.shape[:2] + (S5_CH,)))
        return (g * jax.nn.sigmoid(g @ w_glu.astype(f32) + b_glu.astype(f32))).astype(u.dtype)

    return glu(y), (glu(yc) if with_ctx else None)


def _clamped_swiglu(gu):
    gate, up = jnp.split(gu, 2, axis=-1)
    gate = jnp.minimum(gate, SWIGLU_LIMIT)
    up = jnp.clip(up, -SWIGLU_LIMIT, SWIGLU_LIMIT)
    return gate * jax.nn.sigmoid(SWIGLU_ALPHA * gate) * (up + 1.0)


def _moe_ffn(h, w_router, b_router, w_gate_up, b_gate_up, w_down, b_down):
    f32 = jnp.float32
    n_tok, dim = h.shape
    logits = (h @ w_router + b_router).astype(f32)
    top_val, top_idx = lax.top_k(logits, TOP_K)
    top_w = jax.nn.softmax(top_val, axis=-1)
    n_assign = n_tok * TOP_K
    e_flat = top_idx.reshape(n_assign)
    order = jnp.argsort(e_flat)
    e_sorted = e_flat[order]
    tok_sorted = (order // TOP_K).astype(jnp.int32)
    w_sorted = top_w.reshape(n_assign)[order]
    counts = jnp.bincount(e_flat, length=N_EXPERTS)
    padded = (counts + MOE_BLK - 1) // MOE_BLK * MOE_BLK
    ends = jnp.cumsum(padded)
    pstarts = ends - padded
    starts = jnp.cumsum(counts) - counts
    dest = pstarts[e_sorted] + jnp.arange(n_assign) - starts[e_sorted]
    n_blocks = -(-(n_assign + N_EXPERTS * (MOE_BLK - 1)) // MOE_BLK)
    n_rows = n_blocks * MOE_BLK
    tok_buf = jnp.zeros((n_rows,), jnp.int32).at[dest].set(tok_sorted)
    w_buf = jnp.zeros((n_rows,), f32).at[dest].set(w_sorted)
    blk_exp = jnp.minimum(jnp.searchsorted(ends, jnp.arange(n_blocks) * MOE_BLK, side='right'), N_EXPERTS - 1)

    def expert_block(args):
        tok, e = args
        gu = h[tok] @ w_gate_up[e] + b_gate_up[e]
        return _clamped_swiglu(gu) @ w_down[e] + b_down[e]

    y = lax.map(expert_block, (tok_buf.reshape(n_blocks, MOE_BLK), blk_exp))
    y = y.reshape(n_rows, dim).astype(f32) * w_buf[:, None]
    return jnp.zeros((n_tok, dim), f32).at[tok_buf].add(y).astype(h.dtype)


def setup_inputs(seed: int = 0) -> dict:
    key = jax.random.key(seed)
    ks = iter(jax.random.split(key, 40))
    f32 = jnp.float32

    def nrm(shape, s):
        return jax.random.normal(next(ks), shape, f32) * s

    d = D_MODEL
    x = nrm((BATCH, SEQ, d), 1.0)
    c = nrm((BATCH, d), 1.0)
    ctx = nrm((BATCH, CTX_LEN, d), 1.0)
    c_ctx = nrm((d,), 1.0)
    w_mod = nrm((DEPTH, d, 6 * d), 0.5 * d ** -0.5)
    b_mod = nrm((DEPTH, 6 * d), 0.02)
    g_mix = 1.0 + nrm((DEPTH, d), 0.05)
    w_in = nrm((DEPTH, d, IN_COLS), d ** -0.5)
    w_out = nrm((DEPTH, MIX_W, d), MIX_W ** -0.5)
    na_rpb = nrm((DEPTH, NA_HEADS, 2 * NA_ROWS - 1, 2 * NA_COLS - 1), 0.1)
    s5_a_re = -0.5 + nrm((DEPTH, 2, S5_GROUPS, S5_STATE), 0.01)
    s5_a_im = jnp.pi * jnp.arange(S5_STATE, dtype=f32) + nrm((DEPTH, 2, S5_GROUPS, S5_STATE), 0.01)
    s5_log_step = jax.random.uniform(next(ks), (DEPTH, 2, S5_GROUPS), f32, math.log(S5_DT_MIN), math.log(S5_DT_MAX))
    s5_b_re = nrm((DEPTH, 2, S5_GROUPS, S5_STATE, S5_GROUP_CH), (2 * S5_GROUP_CH) ** -0.5)
    s5_b_im = nrm((DEPTH, 2, S5_GROUPS, S5_STATE, S5_GROUP_CH), (2 * S5_GROUP_CH) ** -0.5)
    s5_c_re = nrm((DEPTH, 2, S5_GROUPS, S5_GROUP_CH, S5_STATE), S5_STATE ** -0.5)
    s5_c_im = nrm((DEPTH, 2, S5_GROUPS, S5_GROUP_CH, S5_STATE), S5_STATE ** -0.5)
    s5_d = nrm((DEPTH, S5_CH), 1.0)
    s5_w_glu = nrm((DEPTH, S5_CH, S5_CH), S5_CH ** -0.5)
    s5_b_glu = nrm((DEPTH, S5_CH), 0.01)
    sw_sinks = nrm((DEPTH, SW_HEADS), 0.5)
    g_ffn = 1.0 + nrm((DEPTH, d), 0.05)
    w_router = nrm((DEPTH, d, N_EXPERTS), d ** -0.5)
    b_router = nrm((DEPTH, N_EXPERTS), 0.01)
    w_gate_up = nrm((DEPTH, N_EXPERTS, d, 2 * EXPERT_FF), d ** -0.5)
    b_gate_up = nrm((DEPTH, N_EXPERTS, 2 * EXPERT_FF), 0.01)
    w_down = nrm((DEPTH, N_EXPERTS, EXPERT_FF, d), EXPERT_FF ** -0.5)
    b_down = nrm((DEPTH, N_EXPERTS, d), 0.01)
    g_final = 1.0 + nrm((d,), 0.05)
    return {'x': x, 'c': c, 'ctx': ctx, 'c_ctx': c_ctx, 'w_mod': w_mod, 'b_mod': b_mod,
            'g_mix': g_mix, 'w_in': w_in, 'w_out': w_out, 'na_rpb': na_rpb,
            's5_a_re': s5_a_re, 's5_a_im': s5_a_im, 's5_log_step': s5_log_step,
            's5_b_re': s5_b_re, 's5_b_im': s5_b_im, 's5_c_re': s5_c_re, 's5_c_im': s5_c_im,
            's5_d': s5_d, 's5_w_glu': s5_w_glu, 's5_b_glu': s5_b_glu, 'sw_sinks': sw_sinks,
            'g_ffn': g_ffn, 'w_router': w_router, 'b_router': b_router,
            'w_gate_up': w_gate_up, 'b_gate_up': b_gate_up, 'w_down': w_down, 'b_down': b_down,
            'g_final': g_final}


def reference(x, c, ctx, c_ctx, w_mod, b_mod, g_mix, w_in, w_out, na_rpb,
              s5_a_re, s5_a_im, s5_log_step, s5_b_re, s5_b_im, s5_c_re, s5_c_im,
              s5_d, s5_w_glu, s5_b_glu, sw_sinks, g_ffn, w_router, b_router,
              w_gate_up, b_gate_up, w_down, b_down, g_final):
    bsz, seq, dim = x.shape
    ctx_len = ctx.shape[1]
    cos, sin = _axial_rope(seq, HEAD_DIM)
    cond = jax.nn.silu(c)
    cond_ctx = jax.nn.silu(c_ctx)
    for l in range(DEPTH):
        with_ctx = l < DEPTH - 1
        mod = (cond @ w_mod[l] + b_mod[l]).reshape(bsz, 6, 1, dim)
        mod_c = (cond_ctx @ w_mod[l] + b_mod[l]).reshape(6, dim)
        h = _modulate(_rmsnorm(x, g_mix[l]), mod[:, 0], mod[:, 1])
        hc = _modulate(_rmsnorm(ctx, g_mix[l]), mod_c[0], mod_c[1])
        aq, ak, av, su, sq, sk, sv = jnp.split(h @ w_in[l], PROJ_SPLITS, axis=-1)
        caq, cak, cav, csu, csq, csk, csv = jnp.split(hc @ w_in[l], PROJ_SPLITS, axis=-1)
        ya, yca = _neighbourhood_attention(_heads(aq), _heads(ak), _heads(av),
                                           _heads(caq), _heads(cak), _heads(cav), na_rpb[l], with_ctx)
        yb, ycb = _s5_mixer(su, csu, s5_a_re[l], s5_a_im[l], s5_log_step[l], s5_b_re[l], s5_b_im[l],
                            s5_c_re[l], s5_c_im[l], s5_d[l], s5_w_glu[l], s5_b_glu[l], with_ctx)
        yc, ycc = _window_gqa(_apply_rope(_heads(sq), cos, sin), _apply_rope(_heads(sk), cos, sin), _heads(sv),
                              _heads(csq), _heads(csk), _heads(csv), sw_sinks[l], with_ctx)
        x = x + mod[:, 2] * (jnp.concatenate([ya, yb, yc], axis=-1) @ w_out[l])
        h = _modulate(_rmsnorm(x, g_ffn[l]), mod[:, 3], mod[:, 4])
        if with_ctx:
            ctx = ctx + mod_c[2] * (jnp.concatenate([yca, ycb, ycc], axis=-1) @ w_out[l])
            hc = _modulate(_rmsnorm(ctx, g_ffn[l]), mod_c[3], mod_c[4])
            f = _moe_ffn(jnp.concatenate([h.reshape(-1, dim), hc.reshape(-1, dim)], axis=0),
                         w_router[l], b_router[l], w_gate_up[l], b_gate_up[l], w_down[l], b_down[l])
            ctx = ctx + mod_c[5] * f[bsz * seq:].reshape(bsz, ctx_len, dim)
            f = f[:bsz * seq]
        else:
            f = _moe_ffn(h.reshape(-1, dim), w_router[l], b_router[l], w_gate_up[l], b_gate_up[l],
                         w_down[l], b_down[l])
        x = x + mod[:, 5] * f.reshape(bsz, seq, dim)
    return _rmsnorm(x, g_final)
```

```python
import functools
import math

import numpy as np
import jax
import jax.numpy as jnp
from jax import lax
from jax.experimental import pallas as pl
from jax.experimental.pallas import tpu as pltpu

F32 = jnp.float32
MXU_DTYPE = jnp.bfloat16

GRID_W = 64
HEAD_DIM = 64
NA_HEADS = 6
NA_W = NA_HEADS * HEAD_DIM
NA_ROWS = 8
NA_COLS = 16
S5_GROUP_CH = 16
S5_CH = 256
S5_GROUPS = S5_CH // S5_GROUP_CH
S5_STATE = 64
S5_EIG_MAX = -1e-4
SW_HEADS = 6
SW_KV_HEADS = 2
SW_GRP = SW_HEADS // SW_KV_HEADS
SW_W = SW_HEADS * HEAD_DIM
SW_KV_W = SW_KV_HEADS * HEAD_DIM
SW_WINDOW = 128
SW_BLK = 128
ROPE_BASE = 10000.0
N_EXPERTS = 32
TOP_K = 4
MOE_BLK = 256
SWIGLU_LIMIT = 7.0
SWIGLU_ALPHA = 1.702
RMS_EPS = 1e-6
NEG_INF = -1e30

LANES = 128
ROW_TILE = 256
S5_CHUNK = 16
MOD_ROWS = 8
VMEM_LIMIT = 56 << 20

SW_HEAD_ORDER = tuple(g * SW_GRP + t for t in range(SW_GRP) for g in range(SW_KV_HEADS))


def _params(*sem):
    return pltpu.CompilerParams(dimension_semantics=sem, vmem_limit_bytes=VMEM_LIMIT)


def _dot(a, b):
    return jnp.dot(a, b, preferred_element_type=F32)


def _dot_nt(a, b):
    return lax.dot_general(a, b, (((1,), (1,)), ((), ())), preferred_element_type=F32)


def _split(a):
    hi = a.astype(MXU_DTYPE)
    lo = (a - hi.astype(F32)).astype(MXU_DTYPE)
    return hi, lo


def _dot3(a, b, nt=False):
    f = _dot_nt if nt else _dot
    ah, al = _split(a)
    bh, bl = _split(b)
    return f(ah, bh) + (f(ah, bl) + f(al, bh))


def _mod_kernel(cond_ref, w_ref, b_ref, o_ref):
    c = cond_ref[...]
    a = c * jax.nn.sigmoid(c)
    o_ref[0] = _dot3(a, w_ref[0]) + b_ref[0]


def _mod_call(cond, w_mod, b_mod):
    depth, d, n = w_mod.shape
    tn = n // 6
    return pl.pallas_call(
        _mod_kernel,
        grid=(depth, n // tn),
        in_specs=[pl.BlockSpec((MOD_ROWS, d), lambda l, j: (0, 0)),
                  pl.BlockSpec((1, d, tn), lambda l, j: (l, 0, j)),
                  pl.BlockSpec((1, 1, tn), lambda l, j: (l, 0, j))],
        out_specs=pl.BlockSpec((1, MOD_ROWS, tn), lambda l, j: (l, 0, j)),
        out_shape=jax.ShapeDtypeStruct((depth, MOD_ROWS, n), F32),
        compiler_params=_params("parallel", "parallel"),
        name="mod",
    )(cond, w_mod, b_mod.reshape(depth, 1, n))


C_AQ = 0
C_AK = C_AQ + NA_W
C_AV = C_AK + NA_W
C_SQ = C_AV + NA_W
C_SQR = C_SQ + SW_W
C_SK = C_SQR + SW_W
C_SKR = C_SK + SW_KV_W
C_SV = C_SKR + SW_KV_W
C_SU = C_SV + SW_KV_W
C_END = C_SU + S5_CH


def _rms_mod(x, g, shift, scale):
    y = x * lax.rsqrt(jnp.mean(x * x, axis=-1, keepdims=True) + RMS_EPS) * g
    return y * (1.0 + scale) + shift


def _inproj_kernel(x_ref, mod_ref, g_ref, w_ref, cos_ref, sin_ref,
                   naq_ref, nak_ref, nav_ref, swq_ref, swk_ref, swv_ref, su_ref):
    m = mod_ref[0]
    h = _rms_mod(x_ref[0], g_ref[...], m[0:1], m[1:2]).astype(MXU_DTYPE)
    p = _dot(h, w_ref[...])
    cos = cos_ref[...]
    sin = sin_ref[...]
    cos3 = jnp.concatenate([cos] * (SW_W // LANES), axis=1)
    sin3 = jnp.concatenate([sin] * (SW_W // LANES), axis=1)
    qk_scale = HEAD_DIM ** -0.5
    naq_ref[0] = (p[:, C_AQ:C_AK] * qk_scale).astype(naq_ref.dtype)
    nak_ref[0] = p[:, C_AK:C_AV].astype(nak_ref.dtype)
    nav_ref[0] = p[:, C_AV:C_SQ].astype(nav_ref.dtype)
    swq_ref[0] = ((p[:, C_SQ:C_SQR] * cos3 + p[:, C_SQR:C_SK] * sin3) * qk_scale).astype(swq_ref.dtype)
    swk_ref[0] = (p[:, C_SK:C_SKR] * cos + p[:, C_SKR:C_SV] * sin).astype(swk_ref.dtype)
    swv_ref[0] = p[:, C_SV:C_SU].astype(swv_ref.dtype)
    su_ref[0] = p[:, C_SU:C_END]


def _mod_index(n_ctx_tiles, ctx_row):
    return lambda b, j: (jnp.where(j < n_ctx_tiles, ctx_row, b), 0, 0)


def _inproj_call(xs, mod_l, g, w_cat, cos2, sin2, ctx_len):
    bsz, s, d = xs.shape
    tm = ROW_TILE
    row = lambda b, j: (b, j, 0)
    const = lambda b, j: (0, 0)
    widths = (NA_W, NA_W, NA_W, SW_W, SW_KV_W, SW_KV_W, S5_CH)
    dtypes = (MXU_DTYPE,) * 6 + (F32,)
    return pl.pallas_call(
        _inproj_kernel,
        grid=(bsz, s // tm),
        in_specs=[pl.BlockSpec((1, tm, d), row),
                  pl.BlockSpec((1, 6, d), _mod_index(ctx_len // tm, bsz)),
                  pl.BlockSpec((1, d), const),
                  pl.BlockSpec((d, C_END), const),
                  pl.BlockSpec((tm, LANES), lambda b, j: (j, 0)),
                  pl.BlockSpec((tm, LANES), lambda b, j: (j, 0))],
        out_specs=[pl.BlockSpec((1, tm, w), row) for w in widths],
        out_shape=[jax.ShapeDtypeStruct((bsz, s, w), t) for w, t in zip(widths, dtypes)],
        compiler_params=_params("parallel", "parallel"),
        name="inproj",
    )(xs, mod_l, g, w_cat, cos2, sin2)


def _half_masks():
    lane = lax.broadcasted_iota(jnp.int32, (1, LANES), 1)
    return lane < HEAD_DIM, lane >= HEAD_DIM


def _na_kernel(q_ref, k_ref, v_ref, bias_ref, o_ref, *, ctx_len, rows):
    i = pl.program_id(1)
    n_ctx_q = ctx_len // GRID_W
    masks = _half_masks()
    nk = NA_ROWS * GRID_W

    def run(local):
        q = q_ref[0]
        if local:
            r = i - n_ctx_q
            rs = jnp.clip(r - NA_ROWS // 2, 0, rows - NA_ROWS)
            base = rs - r + NA_ROWS - 1
            start = pl.multiple_of(ctx_len + rs * GRID_W, GRID_W)
        outs = []
        for t in range(NA_W // LANES):
            sl = slice(LANES * t, LANES * (t + 1))
            qt = q[:, sl]
            kc = k_ref[0, 0:ctx_len, sl]
            vc = v_ref[0, 0:ctx_len, sl]
            if local:
                kw = k_ref[0, pl.ds(start, nk), sl]
                vw = v_ref[0, pl.ds(start, nk), sl]
            pair = []
            for a in range(2):
                h = 2 * t + a
                qm = jnp.where(masks[a], qt, jnp.zeros_like(qt))
                s_cx = _dot_nt(qm, kc)
                m = jnp.max(s_cx, axis=-1, keepdims=True)
                if local:
                    bias = jnp.concatenate(
                        [bias_ref[h, pl.ds(base + 2 * jj, 1)][0] for jj in range(NA_ROWS // 2)], axis=-1)
                    s_nb = _dot_nt(qm, kw) + bias
                    m = jnp.maximum(m, jnp.max(s_nb, axis=-1, keepdims=True))
                    p_nb = jnp.exp(s_nb - m)
                p_cx = jnp.exp(s_cx - m)
                den = jnp.sum(p_cx, axis=-1, keepdims=True)
                o = _dot(p_cx.astype(MXU_DTYPE), vc)
                if local:
                    den = den + jnp.sum(p_nb, axis=-1, keepdims=True)
                    o = o + _dot(p_nb.astype(MXU_DTYPE), vw)
                pair.append(o / den)
            outs.append(jnp.where(masks[0], pair[0], pair[1]))
        o_ref[0] = jnp.concatenate(outs, axis=-1).astype(o_ref.dtype)

    @pl.when(i < n_ctx_q)
    def _():
        run(False)

    @pl.when(i >= n_ctx_q)
    def _():
        run(True)


def _na_call(q, k, v, bias_tab, ctx_len):
    bsz, s, w = q.shape
    rows = (s - ctx_len) // GRID_W
    assert rows >= NA_ROWS
    whole = lambda b, i: (b, 0, 0)
    return pl.pallas_call(
        functools.partial(_na_kernel, ctx_len=ctx_len, rows=rows),
        grid=(bsz, s // GRID_W),
        in_specs=[pl.BlockSpec((1, GRID_W, w), lambda b, i: (b, i, 0)),
                  pl.BlockSpec((1, s, w), whole),
                  pl.BlockSpec((1, s, w), whole),
                  pl.BlockSpec(bias_tab.shape, lambda b, i: (0, 0, 0, 0))],
        out_specs=pl.BlockSpec((1, GRID_W, w), lambda b, i: (b, i, 0)),
        out_shape=jax.ShapeDtypeStruct((bsz, s, w), q.dtype),
        compiler_params=_params("parallel", "arbitrary"),
        name="na_attn",
    )(q, k, v, bias_tab)


def _na_bias_table(rpb):
    qcol = np.arange(GRID_W)[:, None]
    kcol = np.arange(GRID_W)[None, :]
    ws = np.clip(qcol - NA_COLS // 2, 0, GRID_W - NA_COLS)
    valid = (kcol >= ws) & (kcol < ws + NA_COLS)
    dc = np.clip(kcol - qcol + NA_COLS - 1, 0, 2 * NA_COLS - 2)
    full = jnp.where(valid[None, None], rpb[:, :, dc].astype(F32), NEG_INF)
    return jnp.concatenate([full[:, :-1], full[:, 1:]], axis=-1)


def _sw_kernel(sink_ref, q_ref, k_ref, v_ref, o_ref, *, ctx_len, seq):
    i = pl.program_id(1)
    n_ctx_q = ctx_len // SW_BLK
    masks = _half_masks()
    nk = 3 * SW_BLK

    def run(local):
        q = q_ref[0]
        kc = k_ref[0, 0:ctx_len, :]
        vc = v_ref[0, 0:ctx_len, :]
        if local:
            n = i - n_ctx_q
            start_lat = jnp.clip((n - 1) * SW_BLK, 0, seq - nk)
            start = pl.multiple_of(ctx_len + start_lat, SW_BLK)
            kw = k_ref[0, pl.ds(start, nk), :]
            vw = v_ref[0, pl.ds(start, nk), :]
            qpos = n * SW_BLK + lax.broadcasted_iota(jnp.int32, (SW_BLK, 1), 0)
            kpos = start_lat + lax.broadcasted_iota(jnp.int32, (1, nk), 1)
            valid = jnp.abs(qpos - kpos) <= SW_WINDOW
        outs = []
        for t in range(SW_W // LANES):
            qt = q[:, LANES * t:LANES * (t + 1)]
            pair = []
            for a in range(2):
                sink = sink_ref[SW_HEAD_ORDER[2 * t + a]]
                qm = jnp.where(masks[a], qt, jnp.zeros_like(qt))
                s_cx = _dot_nt(qm, kc)
                m = jnp.maximum(jnp.max(s_cx, axis=-1, keepdims=True), sink)
                if local:
                    s_loc = jnp.where(valid, _dot_nt(qm, kw), NEG_INF)
                    m = jnp.maximum(m, jnp.max(s_loc, axis=-1, keepdims=True))
                    p_loc = jnp.exp(s_loc - m)
                p_cx = jnp.exp(s_cx - m)
                den = jnp.sum(p_cx, axis=-1, keepdims=True) + jnp.exp(sink - m)
                o = _dot(p_cx.astype(MXU_DTYPE), vc)
                if local:
                    den = den + jnp.sum(p_loc, axis=-1, keepdims=True)
                    o = o + _dot(p_loc.astype(MXU_DTYPE), vw)
                pair.append(o / den)
            outs.append(jnp.where(masks[0], pair[0], pair[1]))
        o_ref[0] = jnp.concatenate(outs, axis=-1).astype(o_ref.dtype)

    @pl.when(i < n_ctx_q)
    def _():
        run(False)

    @pl.when(i >= n_ctx_q)
    def _():
        run(True)


def _sw_call(sinks, q, k, v, ctx_len):
    bsz, s, w = q.shape
    seq = s - ctx_len
    assert seq >= 3 * SW_BLK
    whole = lambda b, i: (b, 0, 0)
    return pl.pallas_call(
        functools.partial(_sw_kernel, ctx_len=ctx_len, seq=seq),
        grid=(bsz, s // SW_BLK),
        in_specs=[pl.BlockSpec(memory_space=pltpu.SMEM),
                  pl.BlockSpec((1, SW_BLK, w), lambda b, i: (b, i, 0)),
                  pl.BlockSpec((1, s, SW_KV_W), whole),
                  pl.BlockSpec((1, s, SW_KV_W), whole)],
        out_specs=pl.BlockSpec((1, SW_BLK, w), lambda b, i: (b, i, 0)),
        out_shape=jax.ShapeDtypeStruct((bsz, s, w), q.dtype),
        compiler_params=_params("parallel", "arbitrary"),
        name="sw_attn",
    )(sinks.astype(F32), q, k, v)


def _s5_kernel(u_ref, m_ref, wsr_ref, wsi_ref, wor_ref, woi_ref, lr_ref, li_ref, o_ref,
               sre, sim, xre, xim, *, n_ctx_chunks):
    d = pl.program_id(1)
    bsz, nc, _ = u_ref.shape
    for b in range(bsz):
        ub = u_ref[b]
        sre[b] = _dot(ub, wsr_ref[0, 0])
        sim[b] = _dot(ub, wsi_ref[0, 0])
    lr = lr_ref[0, 0]
    li = li_ref[0, 0]

    def step(c, carry):
        new = []
        for b in range(bsz):
            xr, xi = carry[2 * b], carry[2 * b + 1]
            xre[b, pl.ds(c, 1), :] = xr
            xim[b, pl.ds(c, 1), :] = xi
            sr = sre[b, pl.ds(c, 1), :]
            si = sim[b, pl.ds(c, 1), :]
            new.append(lr * xr - li * xi + sr)
            new.append(lr * xi + li * xr + si)
        return tuple(new)

    zero = tuple(jnp.zeros((1, LANES), F32) for _ in range(2 * bsz))

    @pl.when(d == 0)
    def _():
        lax.fori_loop(0, nc, step, zero)

    @pl.when(d == 1)
    def _():
        carry = lax.fori_loop(0, n_ctx_chunks, lambda k, cr: step(n_ctx_chunks - 1 - k, cr), zero)
        lax.fori_loop(0, nc - n_ctx_chunks, lambda k, cr: step(nc - 1 - k, cr), carry)

    half = u_ref.shape[2] // 2
    for b in range(bsz):
        ub = u_ref[b]
        y_intra = jnp.concatenate([_dot(ub[:, :half], m_ref[0, 0]), _dot(ub[:, half:], m_ref[0, 1])], axis=-1)
        y = (y_intra + _dot(xre[b].astype(MXU_DTYPE), wor_ref[0, 0])
             + _dot(xim[b].astype(MXU_DTYPE), woi_ref[0, 0]))

        @pl.when(d == 0)
        def _():
            o_ref[b] = y

        @pl.when(d == 1)
        def _():
            o_ref[b] = o_ref[b] + y


def _s5_call(u_t, wts, n_ctx_chunks):
    m, wsr, wsi, wor, woi, lr, li = wts
    bsz, nc, width = u_t.shape
    pw = 2 * S5_CHUNK * S5_GROUP_CH
    n_pairs = width // pw
    blk = lambda shp: pl.BlockSpec((1, 1) + shp, lambda j, d: (d, j, 0, 0))
    return pl.pallas_call(
        functools.partial(_s5_kernel, n_ctx_chunks=n_ctx_chunks),
        grid=(n_pairs, 2),
        in_specs=[pl.BlockSpec((bsz, nc, pw), lambda j, d: (0, 0, j)),
                  pl.BlockSpec((1, 2, pw // 2, pw // 2), lambda j, d: (d, j, 0, 0)),
                  blk((pw, LANES)), blk((pw, LANES)), blk((LANES, pw)), blk((LANES, pw)),
                  blk((1, LANES)), blk((1, LANES))],
        out_specs=pl.BlockSpec((bsz, nc, pw), lambda j, d: (0, 0, j)),
        out_shape=jax.ShapeDtypeStruct((bsz, nc, width), F32),
        scratch_shapes=[pltpu.VMEM((bsz, nc, LANES), F32) for _ in range(4)],
        compiler_params=_params("parallel", "arbitrary"),
        name="s5_scan",
    )(u_t, m, wsr, wsi, wor, woi, lr, li)


def _s5_weights(a_re, a_im, log_step, b_re, b_im, c_re, c_im):
    lc, g, p, h = S5_CHUNK, S5_GROUPS, S5_STATE, S5_GROUP_CH
    lam = lax.complex(jnp.minimum(a_re.astype(F32), S5_EIG_MAX), a_im.astype(F32))
    step = jnp.exp(log_step.astype(F32))[..., None]
    lam_bar = jnp.exp(lam * step)
    b_bar = ((lam_bar - 1.0) / lam)[..., None] * lax.complex(b_re.astype(F32), b_im.astype(F32))
    cc = lax.complex(c_re.astype(F32), c_im.astype(F32))
    dd = jnp.arange(lc + 1, dtype=F32)
    pw = jnp.exp((lam * step)[..., None] * dd)
    kern = jnp.real(jnp.einsum('zgop,zgpd,zgpi->zgdoi', cc, pw[..., :lc], b_bar))
    jj = np.arange(lc)[:, None]
    ii = np.arange(lc)[None, :]
    mats, wst, wout = [], [], []
    for z in range(2):
        lag = (ii - jj) if z == 0 else (jj - ii)
        ok = lag >= 0
        kz = kern[z][:, np.where(ok, lag, 0)]
        kz = jnp.where(ok[None, :, :, None, None], kz, 0.0)
        mats.append(kz.transpose(0, 1, 4, 2, 3).reshape(g, lc * h, lc * h))
        d_state = (lc - 1 - np.arange(lc)) if z == 0 else np.arange(lc)
        ws = pw[z][:, :, d_state][..., None] * b_bar[z][:, :, None, :]
        wst.append(ws.transpose(0, 2, 3, 1).reshape(g, lc * h, p))
        d_out = (np.arange(lc) + 1) if z == 0 else (lc - np.arange(lc))
        wo = cc[z][:, :, :, None] * pw[z][:, None, :, :][..., d_out]
        wout.append(wo.transpose(0, 2, 3, 1).reshape(g, p, lc * h))
    mats = jnp.stack(mats)
    wst = jnp.stack(wst)
    wout = jnp.stack(wout)

    def pair_rows(w):
        w = w.reshape(2, g // 2, 2, lc * h, p)
        z0 = jnp.zeros_like(w[:, :, 0])
        top = jnp.concatenate([w[:, :, 0], z0], axis=-1)
        bot = jnp.concatenate([z0, w[:, :, 1]], axis=-1)
        return jnp.concatenate([top, bot], axis=-2)

    def pair_cols(w):
        w = w.reshape(2, g // 2, 2, p, lc * h)
        z0 = jnp.zeros_like(w[:, :, 0])
        top = jnp.concatenate([w[:, :, 0], z0], axis=-1)
        bot = jnp.concatenate([z0, w[:, :, 1]], axis=-1)
        return jnp.concatenate([top, bot], axis=-2)

    lam_c = pw[..., lc].reshape(2, g // 2, 1, 2 * p)
    cast = lambda w: w.astype(MXU_DTYPE)
    return (cast(mats), cast(pair_rows(jnp.real(wst))), cast(pair_rows(jnp.imag(wst))),
            cast(pair_cols(jnp.real(wout))), cast(pair_cols(-jnp.imag(wout))),
            jnp.real(lam_c), jnp.imag(lam_c))


def _gelu_tanh(x):
    cdf = 0.5 * (1.0 + jnp.tanh(math.sqrt(2.0 / math.pi) * (x + 0.044715 * (x * x * x))))
    return x * cdf


def _outproj_kernel(x_ref, ya_ref, ys_ref, su_ref, yc_ref, mod_ref, woa_ref, wob_ref, woc_ref,
                    wglu_ref, bglu_ref, dsk_ref, g_ref, wr_ref, br_ref,
                    xo_ref, h_ref, topi_ref, topw_ref):
    m = mod_ref[0]
    y = dsk_ref[...] * su_ref[0] + ys_ref[0]
    gl = _gelu_tanh(y)
    yb = gl * jax.nn.sigmoid(_dot(gl.astype(MXU_DTYPE), wglu_ref[...]) + bglu_ref[...])
    mix = (_dot(ya_ref[0], woa_ref[...]) + _dot(yb.astype(MXU_DTYPE), wob_ref[...])
           + _dot(yc_ref[0], woc_ref[...]))
    x = x_ref[0] + m[2:3] * mix
    xo_ref[0] = x
    h = _rms_mod(x, g_ref[...], m[3:4], m[4:5])
    h_ref[0] = h
    logits = _dot3(wr_ref[...], h, nt=True) + br_ref[...]
    n_e, tm = logits.shape
    e_iota = lax.broadcasted_iota(jnp.int32, (n_e, tm), 0)
    vals, idxs = [], []
    for _ in range(TOP_K):
        mx = jnp.max(logits, axis=0, keepdims=True)
        ix = jnp.min(jnp.where(logits == mx, e_iota, n_e), axis=0, keepdims=True)
        vals.append(mx)
        idxs.append(ix)
        logits = jnp.where(e_iota == ix, -jnp.inf, logits)
    ex = [jnp.exp(v - vals[0]) for v in vals]
    den = ex[0] + ex[1] + ex[2] + ex[3]
    topi_ref[0] = jnp.concatenate(idxs, axis=0)
    w8 = jnp.concatenate([e / den for e in ex] + [jnp.zeros((MOD_ROWS - TOP_K, tm), F32)], axis=0)
    topw_ref[0] = w8.T


def _outproj_call(xs, ya, ys, su, yc, mod_l, wts, ctx_len):
    bsz, s, d = xs.shape
    tm = ROW_TILE
    row = lambda b, j: (b, j, 0)
    const = lambda b, j: (0, 0)
    full = lambda a: pl.BlockSpec(a.shape, const)
    return pl.pallas_call(
        _outproj_kernel,
        grid=(bsz, s // tm),
        in_specs=[pl.BlockSpec((1, tm, d), row),
                  pl.BlockSpec((1, tm, NA_W), row),
                  pl.BlockSpec((1, tm, S5_CH), row),
                  pl.BlockSpec((1, tm, S5_CH), row),
                  pl.BlockSpec((1, tm, SW_W), row),
                  pl.BlockSpec((1, 6, d), _mod_index(ctx_len // tm, bsz))] + [full(a) for a in wts],
        out_specs=[pl.BlockSpec((1, tm, d), row),
                   pl.BlockSpec((1, tm, d), row),
                   pl.BlockSpec((1, TOP_K, tm), lambda b, j: (b, 0, j)),
                   pl.BlockSpec((1, tm, MOD_ROWS), row)],
        out_shape=[jax.ShapeDtypeStruct((bsz, s, d), F32),
                   jax.ShapeDtypeStruct((bsz, s, d), F32),
                   jax.ShapeDtypeStruct((bsz, TOP_K, s), jnp.int32),
                   jax.ShapeDtypeStruct((bsz, s, MOD_ROWS), F32)],
        compiler_params=_params("parallel", "parallel"),
        name="outproj",
    )(xs, ya, ys, su, yc, mod_l, *wts)


def _route_kernel(topi_ref, rank_ref, cnt_ref, carry):
    first = (pl.program_id(0) == 0) & (pl.program_id(1) == 0)

    @pl.when(first)
    def _():
        carry[...] = jnp.zeros_like(carry)

    idx = topi_ref[0]
    tm = idx.shape[1]
    e_iota = lax.broadcasted_iota(jnp.int32, (N_EXPERTS, tm), 0)
    sel = [idx[k:k + 1] == e_iota for k in range(TOP_K)]
    chosen = sel[0] | sel[1] | sel[2] | sel[3]
    onehot = jnp.where(chosen, 1.0, 0.0)
    before = (lax.broadcasted_iota(jnp.int32, (tm, tm), 0) < lax.broadcasted_iota(jnp.int32, (tm, tm), 1))
    pfx = _dot(onehot.astype(MXU_DTYPE), jnp.where(before, 1.0, 0.0).astype(MXU_DTYPE)) + carry[:, 0:1]
    rank = [jnp.sum(jnp.where(sel[k], pfx, 0.0), axis=0, keepdims=True) for k in range(TOP_K)]
    rank_ref[0] = jnp.concatenate(rank, axis=0).astype(jnp.int32)
    carry[...] = carry[...] + jnp.sum(onehot, axis=1, keepdims=True)
    cnt_ref[...] = carry[...].astype(jnp.int32)


def _route_call(topi):
    bsz, _, s = topi.shape
    tm = ROW_TILE
    return pl.pallas_call(
        _route_kernel,
        grid=(bsz, s // tm),
        in_specs=[pl.BlockSpec((1, TOP_K, tm), lambda b, j: (b, 0, j))],
        out_specs=[pl.BlockSpec((1, TOP_K, tm), lambda b, j: (b, 0, j)),
                   pl.BlockSpec((N_EXPERTS, LANES), lambda b, j: (0, 0))],
        out_shape=[jax.ShapeDtypeStruct((bsz, TOP_K, s), jnp.int32),
                   jax.ShapeDtypeStruct((N_EXPERTS, LANES), jnp.int32)],
        scratch_shapes=[pltpu.VMEM((N_EXPERTS, LANES), F32)],
        compiler_params=_params("arbitrary", "arbitrary"),
        name="route_rank",
    )(topi)


def _dispatch_kernel(dest_ref, h_ref, xg_in_ref, xg_ref, sem):
    del xg_in_ref
    tm = h_ref.shape[1]

    def body(t, c):
        for k in range(TOP_K):
            pltpu.make_async_copy(h_ref.at[0, pl.ds(t, 1)], xg_ref.at[pl.ds(dest_ref[0, k, t], 1)], sem).start()
        return c

    lax.fori_loop(0, tm, body, 0, unroll=8)
    pltpu.make_async_copy(xg_ref.at[pl.ds(0, TOP_K * tm)], xg_ref.at[pl.ds(0, TOP_K * tm)], sem).wait()


def _dispatch_call(dest, h, n_rows):
    bsz, s, d = h.shape
    tm = ROW_TILE
    xg0 = jnp.zeros((n_rows, d), h.dtype)
    return pl.pallas_call(
        _dispatch_kernel,
        grid=(bsz, s // tm),
        in_specs=[pl.BlockSpec((1, TOP_K, tm), lambda b, j: (b, 0, j), memory_space=pltpu.SMEM),
                  pl.BlockSpec((1, tm, d), lambda b, j: (b, j, 0)),
                  pl.BlockSpec(memory_space=pl.ANY)],
        out_specs=pl.BlockSpec(memory_space=pl.ANY),
        out_shape=jax.ShapeDtypeStruct((n_rows, d), h.dtype),
        scratch_shapes=[pltpu.SemaphoreType.DMA(())],
        input_output_aliases={2: 0},
        compiler_params=_params("arbitrary", "arbitrary"),
        name="moe_dispatch",
    )(dest, h, xg0)


def _moe_kernel(blk_exp_ref, nact_ref, x_ref, wgu_ref, bgu_ref, wd_ref, bd_ref, y_ref, wgu_c, wd_c):
    i = pl.program_id(0)
    e = blk_exp_ref[i]
    prev = blk_exp_ref[jnp.maximum(i - 1, 0)]
    d, f2 = wgu_c.shape
    f = f2 // 2
    rows = 128

    @pl.when((i == 0) | (e != prev))
    def _():
        def cv(r, c):
            sl = pl.ds(pl.multiple_of(r * rows, rows), rows)
            wgu_c[sl, :] = wgu_ref[0, 0, sl, :].astype(wgu_c.dtype)
            return c
        lax.fori_loop(0, d // rows, cv, 0)

        def cv2(r, c):
            sl = pl.ds(pl.multiple_of(r * rows, rows), rows)
            wd_c[sl, :] = wd_ref[0, 0, sl, :].astype(wd_c.dtype)
            return c
        lax.fori_loop(0, f // rows, cv2, 0)

    @pl.when(i < nact_ref[0])
    def _():
        gu = _dot(x_ref[...].astype(MXU_DTYPE), wgu_c[...]) + bgu_ref[0, 0]
        gate = jnp.minimum(gu[:, :f], SWIGLU_LIMIT)
        up = jnp.clip(gu[:, f:], -SWIGLU_LIMIT, SWIGLU_LIMIT)
        act = gate * jax.nn.sigmoid(SWIGLU_ALPHA * gate) * (up + 1.0)
        y_ref[...] = _dot(act.astype(MXU_DTYPE), wd_c[...]) + bd_ref[0, 0]

    @pl.when(i >= nact_ref[0])
    def _():
        y_ref[...] = jnp.zeros_like(y_ref)


def _moe_call(layer, blk_exp, n_active, xg, w_gate_up, b_gate_up, w_down, b_down):
    n_rows, d = xg.shape
    depth, n_e, _, f2 = w_gate_up.shape
    f = f2 // 2
    n_blocks = n_rows // MOE_BLK
    wsel = lambda i, be, na: (layer, be[i], 0, 0)
    grid_spec = pltpu.PrefetchScalarGridSpec(
        num_scalar_prefetch=2,
        grid=(n_blocks,),
        in_specs=[pl.BlockSpec((MOE_BLK, d), lambda i, be, na: (i, 0)),
                  pl.BlockSpec((1, 1, d, f2), wsel),
                  pl.BlockSpec((1, 1, 1, f2), wsel),
                  pl.BlockSpec((1, 1, f, d), wsel),
                  pl.BlockSpec((1, 1, 1, d), wsel)],
        out_specs=pl.BlockSpec((MOE_BLK, d), lambda i, be, na: (i, 0)),
        scratch_shapes=[pltpu.VMEM((d, f2), MXU_DTYPE), pltpu.VMEM((f, d), MXU_DTYPE)],
    )
    return pl.pallas_call(
        _moe_kernel,
        grid_spec=grid_spec,
        out_shape=jax.ShapeDtypeStruct((n_rows, d), F32),
        compiler_params=_params("arbitrary"),
        name="moe_experts",
    )(blk_exp, n_active, xg, w_gate_up, b_gate_up.reshape(depth, n_e, 1, f2),
      w_down, b_down.reshape(depth, n_e, 1, d))


def _combine_kernel(dest_ref, w_ref, x_ref, mod_ref, yg_ref, o_ref, gbuf, sem):
    tm = x_ref.shape[1]

    def body(t, c):
        for k in range(TOP_K):
            pltpu.make_async_copy(yg_ref.at[pl.ds(dest_ref[0, k, t], 1)], gbuf.at[k, pl.ds(t, 1)], sem).start()
        return c

    lax.fori_loop(0, tm, body, 0, unroll=8)
    pltpu.make_async_copy(gbuf, gbuf, sem).wait()
    w = w_ref[0]
    f = gbuf[0] * w[:, 0:1]
    for k in range(1, TOP_K):
        f = f + gbuf[k] * w[:, k:k + 1]
    o_ref[0] = x_ref[0] + mod_ref[0][5:6] * f


def _combine_call(dest, topw, xs, mod_l, yg, ctx_len):
    bsz, s, d = xs.shape
    tm = ROW_TILE
    row = lambda b, j: (b, j, 0)
    return pl.pallas_call(
        _combine_kernel,
        grid=(bsz, s // tm),
        in_specs=[pl.BlockSpec((1, TOP_K, tm), lambda b, j: (b, 0, j), memory_space=pltpu.SMEM),
                  pl.BlockSpec((1, tm, MOD_ROWS), row),
                  pl.BlockSpec((1, tm, d), row),
                  pl.BlockSpec((1, 6, d), _mod_index(ctx_len // tm, bsz)),
                  pl.BlockSpec(memory_space=pl.ANY)],
        out_specs=pl.BlockSpec((1, tm, d), row),
        out_shape=jax.ShapeDtypeStruct((bsz, s, d), F32),
        scratch_shapes=[pltpu.VMEM((TOP_K, tm, d), F32), pltpu.SemaphoreType.DMA(())],
        compiler_params=_params("arbitrary", "arbitrary"),
        name="moe_combine",
    )(dest, topw, xs, mod_l, yg)


def _final_kernel(x_ref, g_ref, o_ref):
    x = x_ref[0]
    o_ref[0] = x * lax.rsqrt(jnp.mean(x * x, axis=-1, keepdims=True) + RMS_EPS) * g_ref[...]


def _final_call(xs, g, ctx_len):
    bsz, s, d = xs.shape
    tm = ROW_TILE
    off = ctx_len // tm
    return pl.pallas_call(
        _final_kernel,
        grid=(bsz, (s - ctx_len) // tm),
        in_specs=[pl.BlockSpec((1, tm, d), lambda b, j: (b, j + off, 0)),
                  pl.BlockSpec((1, d), lambda b, j: (0, 0))],
        out_specs=pl.BlockSpec((1, tm, d), lambda b, j: (b, j, 0)),
        out_shape=jax.ShapeDtypeStruct((bsz, s - ctx_len, d), F32),
        compiler_params=_params("parallel", "parallel"),
        name="final_norm",
    )(xs, g)


def _rope_tables(seq, ctx_len):
    t = jnp.arange(seq)
    row = (t // GRID_W).astype(F32)
    col = (t % GRID_W).astype(F32)
    nf = HEAD_DIM // 4
    inv = ROPE_BASE ** (-jnp.arange(nf, dtype=F32) / nf)
    ar = row[:, None] * inv
    ac = col[:, None] * inv
    ang = jnp.concatenate([ar, ar, ac, ac], axis=-1)
    cos = jnp.concatenate([jnp.ones((ctx_len, HEAD_DIM), F32), jnp.cos(ang)], axis=0)
    sin = jnp.concatenate([jnp.zeros((ctx_len, HEAD_DIM), F32), jnp.sin(ang)], axis=0)
    reps = LANES // HEAD_DIM
    return jnp.tile(cos, (1, reps)), jnp.tile(sin, (1, reps))


def _rot_cols(w):
    q = HEAD_DIM // 4
    j = np.arange(HEAD_DIM)
    first = (j % (2 * q)) < q
    src = np.where(first, j + q, j - q)
    sign = np.where(first, -1.0, 1.0).astype(np.float32)
    n_heads = w.shape[1] // HEAD_DIM
    src_all = (np.arange(n_heads)[:, None] * HEAD_DIM + src[None, :]).reshape(-1)
    return w[:, src_all] * jnp.asarray(np.tile(sign, n_heads))


def _head_perm_cols(order):
    return (np.asarray(order)[:, None] * HEAD_DIM + np.arange(HEAD_DIM)[None, :]).reshape(-1)


def _inproj_weight(w_in_l):
    aq, ak, av, su, sq, sk, sv = jnp.split(
        w_in_l, np.cumsum([NA_W, NA_W, NA_W, S5_CH, SW_W, SW_KV_W])[:6].tolist(), axis=1)
    sq = sq[:, _head_perm_cols(SW_HEAD_ORDER)]
    return jnp.concatenate([aq, ak, av, sq, _rot_cols(sq), sk, _rot_cols(sk), sv, su], axis=1).astype(MXU_DTYPE)


def _chunk_major(su):
    bsz, s, _ = su.shape
    u = su.reshape(bsz, s // S5_CHUNK, S5_CHUNK, S5_GROUPS, S5_GROUP_CH).transpose(0, 1, 3, 2, 4)
    return u.reshape(bsz, s // S5_CHUNK, S5_GROUPS * S5_CHUNK * S5_GROUP_CH)


def _token_major(y_t):
    bsz, nc, _ = y_t.shape
    y = y_t.reshape(bsz, nc, S5_GROUPS, S5_CHUNK, S5_GROUP_CH).transpose(0, 1, 3, 2, 4)
    return y.reshape(bsz, nc * S5_CHUNK, S5_CH)


def kernel(x, c, ctx, c_ctx, w_mod, b_mod, g_mix, w_in, w_out, na_rpb, s5_a_re, s5_a_im, s5_log_step,
           s5_b_re, s5_b_im, s5_c_re, s5_c_im, s5_d, s5_w_glu, s5_b_glu, sw_sinks, g_ffn, w_router, b_router,
           w_gate_up, b_gate_up, w_down, b_down, g_final):
    bsz, seq, d = x.shape
    ctx_len = ctx.shape[1]
    depth = w_mod.shape[0]
    s = ctx_len + seq
    assert bsz + 1 <= MOD_ROWS and ctx_len % ROW_TILE == 0 and seq % ROW_TILE == 0
    assert seq % GRID_W == 0 and ctx_len % S5_CHUNK == 0

    xs = jnp.concatenate([ctx, x], axis=1)
    cond = jnp.zeros((MOD_ROWS, d), F32).at[:bsz].set(c).at[bsz].set(c_ctx)
    mod = _mod_call(cond, w_mod, b_mod).reshape(depth, MOD_ROWS, 6, d)
    cos2, sin2 = _rope_tables(seq, ctx_len)

    n_assign = bsz * s * TOP_K
    n_blocks = -(-(n_assign + N_EXPERTS * (MOE_BLK - 1)) // MOE_BLK)
    n_rows = n_blocks * MOE_BLK
    sw_rows = _head_perm_cols(SW_HEAD_ORDER)

    for l in range(depth):
        mod_l = mod[l]
        naq, nak, nav, swq, swk, swv, su = _inproj_call(
            xs, mod_l, g_mix[l].reshape(1, d), _inproj_weight(w_in[l]), cos2, sin2, ctx_len)
        ya = _na_call(naq, nak, nav, _na_bias_table(na_rpb[l]), ctx_len)
        yc = _sw_call(sw_sinks[l], swq, swk, swv, ctx_len)
        s5w = _s5_weights(s5_a_re[l], s5_a_im[l], s5_log_step[l], s5_b_re[l], s5_b_im[l], s5_c_re[l], s5_c_im[l])
        ys = _token_major(_s5_call(_chunk_major(su).astype(MXU_DTYPE), s5w, ctx_len // S5_CHUNK))

        wo = w_out[l]
        out_wts = (wo[:NA_W].astype(MXU_DTYPE),
                   wo[NA_W:NA_W + S5_CH].astype(MXU_DTYPE),
                   wo[NA_W + S5_CH:][sw_rows].astype(MXU_DTYPE),
                   s5_w_glu[l].astype(MXU_DTYPE), s5_b_glu[l].reshape(1, S5_CH).astype(F32),
                   s5_d[l].reshape(1, S5_CH).astype(F32), g_ffn[l].reshape(1, d),
                   w_router[l].T.astype(F32), b_router[l].reshape(N_EXPERTS, 1).astype(F32))
        xs, h, topi, topw = _outproj_call(xs, ya, ys, su, yc, mod_l, out_wts, ctx_len)

        rank, cnt = _route_call(topi)
        counts = cnt[:, 0]
        padded = (counts + MOE_BLK - 1) // MOE_BLK * MOE_BLK
        ends = jnp.cumsum(padded)
        dest = (ends - padded)[topi] + rank
        blk_exp = jnp.minimum(jnp.searchsorted(ends, jnp.arange(n_blocks, dtype=jnp.int32) * MOE_BLK, side='right'),
                              N_EXPERTS - 1).astype(jnp.int32)
        n_active = (ends[-1:] // MOE_BLK).astype(jnp.int32)

        xg = _dispatch_call(dest, h, n_rows)
        yg = _moe_call(l, blk_exp, n_active, xg, w_gate_up, b_gate_up, w_down, b_down)
        xs = _combine_call(dest, topw, xs, mod_l, yg, ctx_len)

    return _final_call(xs, g_final.reshape(1, d), ctx_len)
```

```python
import functools
import math

import numpy as np
import jax
import jax.numpy as jnp
from jax import lax
from jax.experimental import pallas as pl
from jax.experimental.pallas import tpu as pltpu

F32 = jnp.float32
MXU_DTYPE = jnp.bfloat16

GRID_W = 64
HEAD_DIM = 64
NA_HEADS = 6
NA_W = NA_HEADS * HEAD_DIM
NA_ROWS = 8
NA_COLS = 16
S5_GROUP_CH = 16
S5_CH = 256
S5_GROUPS = S5_CH // S5_GROUP_CH
S5_STATE = 64
S5_EIG_MAX = -1e-4
SW_HEADS = 6
SW_KV_HEADS = 2
SW_GRP = SW_HEADS // SW_KV_HEADS
SW_W = SW_HEADS * HEAD_DIM
SW_KV_W = SW_KV_HEADS * HEAD_DIM
SW_WINDOW = 128
SW_BLK = 128
ROPE_BASE = 10000.0
N_EXPERTS = 32
TOP_K = 4
MOE_BLK = 256
SWIGLU_LIMIT = 7.0
SWIGLU_ALPHA = 1.702
RMS_EPS = 1e-6
NEG_INF = -1e30

LANES = 128
ROW_TILE = 256
S5_CHUNK = 16
NA_QROWS = 4
NA_KROWS = NA_QROWS + NA_ROWS
SW_QBLK = 2 * SW_BLK
SW_KBLK = SW_QBLK + 2 * SW_WINDOW
MOD_ROWS = 8
VMEM_LIMIT = 56 << 20

SW_HEAD_ORDER = tuple(g * SW_GRP + t for t in range(SW_GRP) for g in range(SW_KV_HEADS))


def _params(*sem):
    return pltpu.CompilerParams(dimension_semantics=sem, vmem_limit_bytes=VMEM_LIMIT)


def _dot(a, b):
    return jnp.dot(a, b, preferred_element_type=F32)


def _dot_nt(a, b):
    return lax.dot_general(a, b, (((1,), (1,)), ((), ())), preferred_element_type=F32)


def _split(a):
    hi = a.astype(MXU_DTYPE)
    lo = (a - hi.astype(F32)).astype(MXU_DTYPE)
    return hi, lo


def _dot3(a, b, nt=False):
    f = _dot_nt if nt else _dot
    ah, al = _split(a)
    bh, bl = _split(b)
    return f(ah, bh) + (f(ah, bl) + f(al, bh))


def _mod_kernel(cond_ref, w_ref, b_ref, o_ref):
    c = cond_ref[...]
    a = c * jax.nn.sigmoid(c)
    o_ref[0] = _dot3(a, w_ref[0]) + b_ref[0]


def _mod_call(cond, w_mod, b_mod):
    depth, d, n = w_mod.shape
    tn = n // 6
    return pl.pallas_call(
        _mod_kernel,
        grid=(depth, n // tn),
        in_specs=[pl.BlockSpec((MOD_ROWS, d), lambda l, j: (0, 0)),
                  pl.BlockSpec((1, d, tn), lambda l, j: (l, 0, j)),
                  pl.BlockSpec((1, 1, tn), lambda l, j: (l, 0, j))],
        out_specs=pl.BlockSpec((1, MOD_ROWS, tn), lambda l, j: (l, 0, j)),
        out_shape=jax.ShapeDtypeStruct((depth, MOD_ROWS, n), F32),
        compiler_params=_params("parallel", "parallel"),
        name="mod",
    )(cond, w_mod, b_mod.reshape(depth, 1, n))


C_AQ = 0
C_AK = C_AQ + NA_W
C_AV = C_AK + NA_W
C_SQ = C_AV + NA_W
C_SQR = C_SQ + SW_W
C_SK = C_SQR + SW_W
C_SKR = C_SK + SW_KV_W
C_SV = C_SKR + SW_KV_W
C_SU = C_SV + SW_KV_W
C_END = C_SU + S5_CH


def _rms_mod(x, g, shift, scale):
    y = x * lax.rsqrt(jnp.mean(x * x, axis=-1, keepdims=True) + RMS_EPS) * g
    return y * (1.0 + scale) + shift


def _inproj_kernel(x_ref, mod_ref, g_ref, w_ref, cos_ref, sin_ref,
                   naq_ref, nak_ref, nav_ref, swq_ref, swk_ref, swv_ref, su_ref):
    m = mod_ref[0]
    h = _rms_mod(x_ref[0], g_ref[...], m[0:1], m[1:2]).astype(MXU_DTYPE)
    p = _dot(h, w_ref[...])
    cos = cos_ref[...]
    sin = sin_ref[...]
    cos3 = jnp.concatenate([cos] * (SW_W // LANES), axis=1)
    sin3 = jnp.concatenate([sin] * (SW_W // LANES), axis=1)
    qk_scale = HEAD_DIM ** -0.5
    naq_ref[0] = (p[:, C_AQ:C_AK] * qk_scale).astype(naq_ref.dtype)
    nak_ref[0] = p[:, C_AK:C_AV].astype(nak_ref.dtype)
    nav_ref[0] = p[:, C_AV:C_SQ].astype(nav_ref.dtype)
    swq_ref[0] = ((p[:, C_SQ:C_SQR] * cos3 + p[:, C_SQR:C_SK] * sin3) * qk_scale).astype(swq_ref.dtype)
    swk_ref[0] = (p[:, C_SK:C_SKR] * cos + p[:, C_SKR:C_SV] * sin).astype(swk_ref.dtype)
    swv_ref[0] = p[:, C_SV:C_SU].astype(swv_ref.dtype)
    su_ref[0] = p[:, C_SU:C_END]


def _mod_index(n_ctx_tiles, ctx_row):
    return lambda b, j: (jnp.where(j < n_ctx_tiles, ctx_row, b), 0, 0)


def _inproj_call(xs, mod_l, g, w_cat, cos2, sin2, ctx_len):
    bsz, s, d = xs.shape
    tm = ROW_TILE
    row = lambda b, j: (b, j, 0)
    const = lambda b, j: (0, 0)
    widths = (NA_W, NA_W, NA_W, SW_W, SW_KV_W, SW_KV_W, S5_CH)
    dtypes = (MXU_DTYPE,) * 6 + (F32,)
    return pl.pallas_call(
        _inproj_kernel,
        grid=(bsz, s // tm),
        in_specs=[pl.BlockSpec((1, tm, d), row),
                  pl.BlockSpec((1, 6, d), _mod_index(ctx_len // tm, bsz)),
                  pl.BlockSpec((1, d), const),
                  pl.BlockSpec((d, C_END), const),
                  pl.BlockSpec((tm, LANES), lambda b, j: (j, 0)),
                  pl.BlockSpec((tm, LANES), lambda b, j: (j, 0))],
        out_specs=[pl.BlockSpec((1, tm, w), row) for w in widths],
        out_shape=[jax.ShapeDtypeStruct((bsz, s, w), t) for w, t in zip(widths, dtypes)],
        compiler_params=_params("parallel", "parallel"),
        name="inproj",
    )(xs, mod_l, g, w_cat, cos2, sin2)


def _half_masks():
    lane = lax.broadcasted_iota(jnp.int32, (1, LANES), 1)
    return lane < HEAD_DIM, lane >= HEAD_DIM


def _na_kernel(q_ref, k_ref, v_ref, bias_ref, o_ref, *, ctx_len, rows):
    i = pl.program_id(1)
    n_ctx_q = ctx_len // (NA_QROWS * GRID_W)
    masks = _half_masks()
    nk = NA_KROWS * GRID_W

    def run(local):
        q = q_ref[0]
        if local:
            r0 = (i - n_ctx_q) * NA_QROWS
            start0 = jnp.clip(r0 - NA_ROWS // 2, 0, rows - NA_KROWS)
            start = pl.multiple_of(ctx_len + start0 * GRID_W, GRID_W)
            tab_idx, row_mask = {}, {}
            for a in range(NA_QROWS):
                r = r0 + a
                s_r = jnp.clip(r - NA_ROWS // 2, 0, rows - NA_ROWS)
                for p in range(NA_KROWS // 2):
                    kr = start0 + 2 * p
                    ok_lo = (kr >= s_r) & (kr < s_r + NA_ROWS)
                    ok_hi = (kr + 1 >= s_r) & (kr + 1 < s_r + NA_ROWS)
                    tab_idx[a, p] = jnp.clip(kr - r + NA_ROWS, 0, 2 * NA_ROWS - 1)
                    row_mask[a, p] = jnp.where(masks[0], jnp.where(ok_lo, 0.0, NEG_INF),
                                               jnp.where(ok_hi, 0.0, NEG_INF))
        outs = []
        for t in range(NA_W // LANES):
            sl = slice(LANES * t, LANES * (t + 1))
            qt = q[:, sl]
            kc = k_ref[0, 0:ctx_len, sl]
            vc = v_ref[0, 0:ctx_len, sl]
            if local:
                kw = k_ref[0, pl.ds(start, nk), sl]
                vw = v_ref[0, pl.ds(start, nk), sl]
            pair = []
            for a in range(2):
                h = 2 * t + a
                qm = jnp.where(masks[a], qt, jnp.zeros_like(qt))
                s_cx = _dot_nt(qm, kc)
                m = jnp.max(s_cx, axis=-1, keepdims=True)
                if local:
                    bias = jnp.concatenate(
                        [jnp.concatenate([bias_ref[h, pl.ds(tab_idx[a2, p], 1)][0] + row_mask[a2, p]
                                          for p in range(NA_KROWS // 2)], axis=-1)
                         for a2 in range(NA_QROWS)], axis=0)
                    s_nb = _dot_nt(qm, kw) + bias
                    m = jnp.maximum(m, jnp.max(s_nb, axis=-1, keepdims=True))
                    p_nb = jnp.exp(s_nb - m)
                p_cx = jnp.exp(s_cx - m)
                den = jnp.sum(p_cx, axis=-1, keepdims=True)
                o = _dot(p_cx.astype(MXU_DTYPE), vc)
                if local:
                    den = den + jnp.sum(p_nb, axis=-1, keepdims=True)
                    o = o + _dot(p_nb.astype(MXU_DTYPE), vw)
                pair.append(o / den)
            outs.append(jnp.where(masks[0], pair[0], pair[1]))
        o_ref[0] = jnp.concatenate(outs, axis=-1).astype(o_ref.dtype)

    @pl.when(i < n_ctx_q)
    def _():
        run(False)

    @pl.when(i >= n_ctx_q)
    def _():
        run(True)


def _na_call(q, k, v, bias_tab, ctx_len):
    bsz, s, w = q.shape
    rows = (s - ctx_len) // GRID_W
    tq = NA_QROWS * GRID_W
    assert rows >= NA_KROWS and rows % NA_QROWS == 0 and ctx_len % tq == 0
    whole = lambda b, i: (b, 0, 0)
    return pl.pallas_call(
        functools.partial(_na_kernel, ctx_len=ctx_len, rows=rows),
        grid=(bsz, s // tq),
        in_specs=[pl.BlockSpec((1, tq, w), lambda b, i: (b, i, 0)),
                  pl.BlockSpec((1, s, w), whole),
                  pl.BlockSpec((1, s, w), whole),
                  pl.BlockSpec(bias_tab.shape, lambda b, i: (0, 0, 0, 0))],
        out_specs=pl.BlockSpec((1, tq, w), lambda b, i: (b, i, 0)),
        out_shape=jax.ShapeDtypeStruct((bsz, s, w), q.dtype),
        compiler_params=_params("parallel", "arbitrary"),
        name="na_attn",
    )(q, k, v, bias_tab)


def _na_bias_table(rpb):
    qcol = np.arange(GRID_W)[:, None]
    kcol = np.arange(GRID_W)[None, :]
    ws = np.clip(qcol - NA_COLS // 2, 0, GRID_W - NA_COLS)
    valid = (kcol >= ws) & (kcol < ws + NA_COLS)
    dc = np.clip(kcol - qcol + NA_COLS - 1, 0, 2 * NA_COLS - 2)
    full = jnp.where(valid[None, None], rpb[:, :, dc].astype(F32), NEG_INF)
    edge = jnp.full_like(full[:, :1], NEG_INF)
    full = jnp.concatenate([edge, full, edge], axis=1)
    return jnp.concatenate([full[:, :-1], full[:, 1:]], axis=-1)


def _sw_kernel(sink_ref, q_ref, k_ref, v_ref, o_ref, *, ctx_len, seq):
    i = pl.program_id(1)
    n_ctx_q = ctx_len // SW_QBLK
    masks = _half_masks()
    nk = SW_KBLK

    def run(local):
        q = q_ref[0]
        kc = k_ref[0, 0:ctx_len, :]
        vc = v_ref[0, 0:ctx_len, :]
        if local:
            n = i - n_ctx_q
            start_lat = jnp.clip(n * SW_QBLK - SW_WINDOW, 0, seq - nk)
            start = pl.multiple_of(ctx_len + start_lat, SW_BLK)
            kw = k_ref[0, pl.ds(start, nk), :]
            vw = v_ref[0, pl.ds(start, nk), :]
            qpos = n * SW_QBLK + lax.broadcasted_iota(jnp.int32, (SW_QBLK, 1), 0)
            kpos = start_lat + lax.broadcasted_iota(jnp.int32, (1, nk), 1)
            valid = jnp.abs(qpos - kpos) <= SW_WINDOW
        outs = []
        for t in range(SW_W // LANES):
            qt = q[:, LANES * t:LANES * (t + 1)]
            pair = []
            for a in range(2):
                sink = sink_ref[SW_HEAD_ORDER[2 * t + a]]
                qm = jnp.where(masks[a], qt, jnp.zeros_like(qt))
                s_cx = _dot_nt(qm, kc)
                m = jnp.maximum(jnp.max(s_cx, axis=-1, keepdims=True), sink)
                if local:
                    s_loc = jnp.where(valid, _dot_nt(qm, kw), NEG_INF)
                    m = jnp.maximum(m, jnp.max(s_loc, axis=-1, keepdims=True))
                    p_loc = jnp.exp(s_loc - m)
                p_cx = jnp.exp(s_cx - m)
                den = jnp.sum(p_cx, axis=-1, keepdims=True) + jnp.exp(sink - m)
                o = _dot(p_cx.astype(MXU_DTYPE), vc)
                if local:
                    den = den + jnp.sum(p_loc, axis=-1, keepdims=True)
                    o = o + _dot(p_loc.astype(MXU_DTYPE), vw)
                pair.append(o / den)
            outs.append(jnp.where(masks[0], pair[0], pair[1]))
        o_ref[0] = jnp.concatenate(outs, axis=-1).astype(o_ref.dtype)

    @pl.when(i < n_ctx_q)
    def _():
        run(False)

    @pl.when(i >= n_ctx_q)
    def _():
        run(True)


def _sw_call(sinks, q, k, v, ctx_len):
    bsz, s, w = q.shape
    seq = s - ctx_len
    assert seq >= SW_KBLK and seq % SW_QBLK == 0 and ctx_len % SW_QBLK == 0
    whole = lambda b, i: (b, 0, 0)
    return pl.pallas_call(
        functools.partial(_sw_kernel, ctx_len=ctx_len, seq=seq),
        grid=(bsz, s // SW_QBLK),
        in_specs=[pl.BlockSpec(memory_space=pltpu.SMEM),
                  pl.BlockSpec((1, SW_QBLK, w), lambda b, i: (b, i, 0)),
                  pl.BlockSpec((1, s, SW_KV_W), whole),
                  pl.BlockSpec((1, s, SW_KV_W), whole)],
        out_specs=pl.BlockSpec((1, SW_QBLK, w), lambda b, i: (b, i, 0)),
        out_shape=jax.ShapeDtypeStruct((bsz, s, w), q.dtype),
        compiler_params=_params("parallel", "arbitrary"),
        name="sw_attn",
    )(sinks.astype(F32), q, k, v)


def _s5_kernel(u_ref, m_ref, wsr_ref, wsi_ref, wor_ref, woi_ref, lr_ref, li_ref, o_ref,
               sre, sim, xre, xim, *, n_ctx_chunks):
    d = pl.program_id(1)
    bsz, nc, _ = u_ref.shape
    for b in range(bsz):
        ub = u_ref[b]
        sre[b] = _dot(ub, wsr_ref[0, 0])
        sim[b] = _dot(ub, wsi_ref[0, 0])
    lr = lr_ref[0, 0]
    li = li_ref[0, 0]

    def step(c, carry):
        new = []
        for b in range(bsz):
            xr, xi = carry[2 * b], carry[2 * b + 1]
            xre[b, pl.ds(c, 1), :] = xr
            xim[b, pl.ds(c, 1), :] = xi
            sr = sre[b, pl.ds(c, 1), :]
            si = sim[b, pl.ds(c, 1), :]
            new.append(lr * xr - li * xi + sr)
            new.append(lr * xi + li * xr + si)
        return tuple(new)

    zero = tuple(jnp.zeros((1, LANES), F32) for _ in range(2 * bsz))

    @pl.when(d == 0)
    def _():
        lax.fori_loop(0, nc, step, zero)

    @pl.when(d == 1)
    def _():
        carry = lax.fori_loop(0, n_ctx_chunks, lambda k, cr: step(n_ctx_chunks - 1 - k, cr), zero)
        lax.fori_loop(0, nc - n_ctx_chunks, lambda k, cr: step(nc - 1 - k, cr), carry)

    half = u_ref.shape[2] // 2
    for b in range(bsz):
        ub = u_ref[b]
        y_intra = jnp.concatenate([_dot(ub[:, :half], m_ref[0, 0]), _dot(ub[:, half:], m_ref[0, 1])], axis=-1)
        y = (y_intra + _dot(xre[b].astype(MXU_DTYPE), wor_ref[0, 0])
             + _dot(xim[b].astype(MXU_DTYPE), woi_ref[0, 0]))

        @pl.when(d == 0)
        def _():
            o_ref[b] = y

        @pl.when(d == 1)
        def _():
            o_ref[b] = o_ref[b] + y


def _s5_call(u_t, wts, n_ctx_chunks):
    m, wsr, wsi, wor, woi, lr, li = wts
    bsz, nc, width = u_t.shape
    pw = 2 * S5_CHUNK * S5_GROUP_CH
    n_pairs = width // pw
    blk = lambda shp: pl.BlockSpec((1, 1) + shp, lambda j, d: (d, j, 0, 0))
    return pl.pallas_call(
        functools.partial(_s5_kernel, n_ctx_chunks=n_ctx_chunks),
        grid=(n_pairs, 2),
        in_specs=[pl.BlockSpec((bsz, nc, pw), lambda j, d: (0, 0, j)),
                  pl.BlockSpec((1, 2, pw // 2, pw // 2), lambda j, d: (d, j, 0, 0)),
                  blk((pw, LANES)), blk((pw, LANES)), blk((LANES, pw)), blk((LANES, pw)),
                  blk((1, LANES)), blk((1, LANES))],
        out_specs=pl.BlockSpec((bsz, nc, pw), lambda j, d: (0, 0, j)),
        out_shape=jax.ShapeDtypeStruct((bsz, nc, width), F32),
        scratch_shapes=[pltpu.VMEM((bsz, nc, LANES), F32) for _ in range(4)],
        compiler_params=_params("parallel", "arbitrary"),
        name="s5_scan",
    )(u_t, m, wsr, wsi, wor, woi, lr, li)


def _s5_weights(a_re, a_im, log_step, b_re, b_im, c_re, c_im):
    lc, g, p, h = S5_CHUNK, S5_GROUPS, S5_STATE, S5_GROUP_CH
    lam = lax.complex(jnp.minimum(a_re.astype(F32), S5_EIG_MAX), a_im.astype(F32))
    step = jnp.exp(log_step.astype(F32))[..., None]
    lam_bar = jnp.exp(lam * step)
    b_bar = ((lam_bar - 1.0) / lam)[..., None] * lax.complex(b_re.astype(F32), b_im.astype(F32))
    cc = lax.complex(c_re.astype(F32), c_im.astype(F32))
    dd = jnp.arange(lc + 1, dtype=F32)
    pw = jnp.exp((lam * step)[..., None] * dd)
    kern = jnp.real(jnp.einsum('zgop,zgpd,zgpi->zgdoi', cc, pw[..., :lc], b_bar))
    jj = np.arange(lc)[:, None]
    ii = np.arange(lc)[None, :]
    mats, wst, wout = [], [], []
    for z in range(2):
        lag = (ii - jj) if z == 0 else (jj - ii)
        ok = lag >= 0
        kz = kern[z][:, np.where(ok, lag, 0)]
        kz = jnp.where(ok[None, :, :, None, None], kz, 0.0)
        mats.append(kz.transpose(0, 1, 4, 2, 3).reshape(g, lc * h, lc * h))
        d_state = (lc - 1 - np.arange(lc)) if z == 0 else np.arange(lc)
        ws = pw[z][:, :, d_state][..., None] * b_bar[z][:, :, None, :]
        wst.append(ws.transpose(0, 2, 3, 1).reshape(g, lc * h, p))
        d_out = (np.arange(lc) + 1) if z == 0 else (lc - np.arange(lc))
        wo = cc[z][:, :, :, None] * pw[z][:, None, :, :][..., d_out]
        wout.append(wo.transpose(0, 2, 3, 1).reshape(g, p, lc * h))
    mats = jnp.stack(mats)
    wst = jnp.stack(wst)
    wout = jnp.stack(wout)

    def pair_rows(w):
        w = w.reshape(2, g // 2, 2, lc * h, p)
        z0 = jnp.zeros_like(w[:, :, 0])
        top = jnp.concatenate([w[:, :, 0], z0], axis=-1)
        bot = jnp.concatenate([z0, w[:, :, 1]], axis=-1)
        return jnp.concatenate([top, bot], axis=-2)

    def pair_cols(w):
        w = w.reshape(2, g // 2, 2, p, lc * h)
        z0 = jnp.zeros_like(w[:, :, 0])
        top = jnp.concatenate([w[:, :, 0], z0], axis=-1)
        bot = jnp.concatenate([z0, w[:, :, 1]], axis=-1)
        return jnp.concatenate([top, bot], axis=-2)

    lam_c = pw[..., lc].reshape(2, g // 2, 1, 2 * p)
    cast = lambda w: w.astype(MXU_DTYPE)
    return (cast(mats), cast(pair_rows(jnp.real(wst))), cast(pair_rows(jnp.imag(wst))),
            cast(pair_cols(jnp.real(wout))), cast(pair_cols(-jnp.imag(wout))),
            jnp.real(lam_c), jnp.imag(lam_c))


def _gelu_tanh(x):
    cdf = 0.5 * (1.0 + jnp.tanh(math.sqrt(2.0 / math.pi) * (x + 0.044715 * (x * x * x))))
    return x * cdf


def _outproj_kernel(x_ref, ya_ref, ys_ref, su_ref, yc_ref, mod_ref, woa_ref, wob_ref, woc_ref,
                    wglu_ref, bglu_ref, dsk_ref, g_ref, wr_ref, br_ref,
                    xo_ref, h_ref, topi_ref, topw_ref):
    m = mod_ref[0]
    y = dsk_ref[...] * su_ref[0] + ys_ref[0]
    gl = _gelu_tanh(y)
    yb = gl * jax.nn.sigmoid(_dot(gl.astype(MXU_DTYPE), wglu_ref[...]) + bglu_ref[...])
    mix = (_dot(ya_ref[0], woa_ref[...]) + _dot(yb.astype(MXU_DTYPE), wob_ref[...])
           + _dot(yc_ref[0], woc_ref[...]))
    x = x_ref[0] + m[2:3] * mix
    xo_ref[0] = x
    h = _rms_mod(x, g_ref[...], m[3:4], m[4:5])
    h_ref[0] = h
    logits = _dot3(wr_ref[...], h, nt=True) + br_ref[...]
    n_e, tm = logits.shape
    e_iota = lax.broadcasted_iota(jnp.int32, (n_e, tm), 0)
    vals, idxs = [], []
    for _ in range(TOP_K):
        mx = jnp.max(logits, axis=0, keepdims=True)
        ix = jnp.min(jnp.where(logits == mx, e_iota, n_e), axis=0, keepdims=True)
        vals.append(mx)
        idxs.append(ix)
        logits = jnp.where(e_iota == ix, -jnp.inf, logits)
    ex = [jnp.exp(v - vals[0]) for v in vals]
    den = ex[0] + ex[1] + ex[2] + ex[3]
    topi_ref[0] = jnp.concatenate(idxs, axis=0)
    w8 = jnp.concatenate([e / den for e in ex] + [jnp.zeros((MOD_ROWS - TOP_K, tm), F32)], axis=0)
    topw_ref[0] = w8.T


def _outproj_call(xs, ya, ys, su, yc, mod_l, wts, ctx_len):
    bsz, s, d = xs.shape
    tm = ROW_TILE
    row = lambda b, j: (b, j, 0)
    const = lambda b, j: (0, 0)
    full = lambda a: pl.BlockSpec(a.shape, const)
    return pl.pallas_call(
        _outproj_kernel,
        grid=(bsz, s // tm),
        in_specs=[pl.BlockSpec((1, tm, d), row),
                  pl.BlockSpec((1, tm, NA_W), row),
                  pl.BlockSpec((1, tm, S5_CH), row),
                  pl.BlockSpec((1, tm, S5_CH), row),
                  pl.BlockSpec((1, tm, SW_W), row),
                  pl.BlockSpec((1, 6, d), _mod_index(ctx_len // tm, bsz))] + [full(a) for a in wts],
        out_specs=[pl.BlockSpec((1, tm, d), row),
                   pl.BlockSpec((1, tm, d), row),
                   pl.BlockSpec((1, TOP_K, tm), lambda b, j: (b, 0, j)),
                   pl.BlockSpec((1, tm, MOD_ROWS), row)],
        out_shape=[jax.ShapeDtypeStruct((bsz, s, d), F32),
                   jax.ShapeDtypeStruct((bsz, s, d), F32),
                   jax.ShapeDtypeStruct((bsz, TOP_K, s), jnp.int32),
                   jax.ShapeDtypeStruct((bsz, s, MOD_ROWS), F32)],
        compiler_params=_params("parallel", "parallel"),
        name="outproj",
    )(xs, ya, ys, su, yc, mod_l, *wts)


def _route_kernel(topi_ref, dest_ref, blk_ref, ends_ref, carry, ranks, *, n_tiles):
    phase = pl.program_id(0)
    t = pl.program_id(1) * n_tiles + pl.program_id(2)
    idx = topi_ref[0]
    tm = idx.shape[1]
    e_iota = lax.broadcasted_iota(jnp.int32, (N_EXPERTS, tm), 0)
    sel = [idx[k:k + 1] == e_iota for k in range(TOP_K)]

    @pl.when(phase == 0)
    def _():
        @pl.when(t == 0)
        def _():
            carry[...] = jnp.zeros_like(carry)

        chosen = sel[0] | sel[1] | sel[2] | sel[3]
        onehot = jnp.where(chosen, 1.0, 0.0)
        before = (lax.broadcasted_iota(jnp.int32, (tm, tm), 0) < lax.broadcasted_iota(jnp.int32, (tm, tm), 1))
        pfx = _dot(onehot.astype(MXU_DTYPE), jnp.where(before, 1.0, 0.0).astype(MXU_DTYPE)) + carry[:, 0:1]
        ranks[t] = jnp.concatenate(
            [jnp.sum(jnp.where(sel[k], pfx, 0.0), axis=0, keepdims=True) for k in range(TOP_K)], axis=0)
        carry[...] = carry[...] + jnp.sum(onehot, axis=1, keepdims=True)

    @pl.when(phase == 1)
    def _():
        counts = carry[:, 0:1]
        padded = jnp.ceil(counts * (1.0 / MOE_BLK)) * MOE_BLK
        r_i = lax.broadcasted_iota(jnp.int32, (N_EXPERTS, N_EXPERTS), 0)
        c_i = lax.broadcasted_iota(jnp.int32, (N_EXPERTS, N_EXPERTS), 1)
        padded_row = jnp.sum(jnp.where(r_i == c_i, padded, 0.0), axis=0, keepdims=True)
        pstart = jnp.sum(jnp.where(c_i < r_i, padded_row, 0.0), axis=1, keepdims=True)
        ends = pstart + padded
        rk = ranks[t]
        dest_ref[0] = jnp.concatenate(
            [jnp.sum(jnp.where(sel[k], pstart, 0.0), axis=0, keepdims=True) + rk[k:k + 1] for k in range(TOP_K)],
            axis=0).astype(jnp.int32)
        nb = blk_ref.shape[1]
        blk_start = (lax.broadcasted_iota(jnp.int32, (N_EXPERTS, nb), 1) * MOE_BLK).astype(F32)
        owner = jnp.sum(jnp.where(ends <= blk_start, 1.0, 0.0), axis=0, keepdims=True)
        blk_ref[...] = jnp.minimum(owner, N_EXPERTS - 1.0).astype(jnp.int32)
        ends_ref[...] = jnp.broadcast_to(ends, ends_ref.shape).astype(jnp.int32)


def _route_call(topi, n_blocks):
    bsz, _, s = topi.shape
    tm = ROW_TILE
    n_tiles = s // tm
    nb_pad = -(-n_blocks // LANES) * LANES
    return pl.pallas_call(
        functools.partial(_route_kernel, n_tiles=n_tiles),
        grid=(2, bsz, n_tiles),
        in_specs=[pl.BlockSpec((1, TOP_K, tm), lambda p, b, j: (b, 0, j))],
        out_specs=[pl.BlockSpec((1, TOP_K, tm), lambda p, b, j: (b * p, 0, j * p)),
                   pl.BlockSpec((1, nb_pad), lambda p, b, j: (0, 0)),
                   pl.BlockSpec((N_EXPERTS, LANES), lambda p, b, j: (0, 0))],
        out_shape=[jax.ShapeDtypeStruct((bsz, TOP_K, s), jnp.int32),
                   jax.ShapeDtypeStruct((1, nb_pad), jnp.int32),
                   jax.ShapeDtypeStruct((N_EXPERTS, LANES), jnp.int32)],
        scratch_shapes=[pltpu.VMEM((N_EXPERTS, LANES), F32),
                        pltpu.VMEM((bsz * n_tiles, TOP_K, tm), F32)],
        compiler_params=_params("arbitrary", "arbitrary", "arbitrary"),
        name="route_rank",
    )(topi)


def _dispatch_kernel(ends_ref, dest_ref, h_ref, xg_ref, zbuf, sem, zsem, *, n_blocks):
    tm = h_ref.shape[1]

    @pl.when((pl.program_id(0) == 0) & (pl.program_id(1) == 0))
    def _():
        zbuf[...] = jnp.zeros_like(zbuf)

        def fill(row):
            return pltpu.make_async_copy(zbuf, xg_ref.at[pl.ds(pl.multiple_of(row, MOE_BLK), MOE_BLK)], zsem)

        def each(fn):
            for e in range(N_EXPERTS):
                begin = ends_ref[e - 1] if e else 0

                @pl.when(ends_ref[e] > begin)
                def _():
                    fn(fill(ends_ref[e] - MOE_BLK))

            def dead(i, c):
                fn(fill(i * MOE_BLK))
                return c

            lax.fori_loop(ends_ref[N_EXPERTS - 1] // MOE_BLK, n_blocks, dead, 0)

        each(lambda cp: cp.start())
        each(lambda cp: cp.wait())

    def body(t, c):
        for k in range(TOP_K):
            pltpu.make_async_copy(h_ref.at[0, pl.ds(t, 1)], xg_ref.at[pl.ds(dest_ref[0, k, t], 1)],
                                  sem).start(priority=k % 2)
        return c

    lax.fori_loop(0, tm, body, 0, unroll=8)
    pltpu.make_async_copy(xg_ref.at[pl.ds(0, TOP_K * tm)], xg_ref.at[pl.ds(0, TOP_K * tm)], sem).wait()


def _dispatch_call(ends, dest, h, n_blocks):
    bsz, s, d = h.shape
    tm = ROW_TILE
    grid_spec = pltpu.PrefetchScalarGridSpec(
        num_scalar_prefetch=1,
        grid=(bsz, s // tm),
        in_specs=[pl.BlockSpec((1, TOP_K, tm), lambda b, j, en: (b, 0, j), memory_space=pltpu.SMEM),
                  pl.BlockSpec((1, tm, d), lambda b, j, en: (b, j, 0))],
        out_specs=pl.BlockSpec(memory_space=pl.ANY),
        scratch_shapes=[pltpu.VMEM((MOE_BLK, d), h.dtype), pltpu.SemaphoreType.DMA(()),
                        pltpu.SemaphoreType.DMA(())],
    )
    return pl.pallas_call(
        functools.partial(_dispatch_kernel, n_blocks=n_blocks),
        grid_spec=grid_spec,
        out_shape=jax.ShapeDtypeStruct((n_blocks * MOE_BLK, d), h.dtype),
        compiler_params=_params("arbitrary", "arbitrary"),
        name="moe_dispatch",
    )(ends, dest, h)


def _moe_kernel(blk_exp_ref, nact_ref, x_ref, wgu_ref, bgu_ref, wd_ref, bd_ref, y_ref, wgu_c, wd_c):
    i = pl.program_id(0)
    e = blk_exp_ref[i]
    prev = blk_exp_ref[jnp.maximum(i - 1, 0)]
    d, f2 = wgu_c.shape
    f = f2 // 2
    rows = 128

    @pl.when((i == 0) | (e != prev))
    def _():
        def cv(r, c):
            sl = pl.ds(pl.multiple_of(r * rows, rows), rows)
            wgu_c[sl, :] = wgu_ref[0, 0, sl, :].astype(wgu_c.dtype)
            return c
        lax.fori_loop(0, d // rows, cv, 0)

        def cv2(r, c):
            sl = pl.ds(pl.multiple_of(r * rows, rows), rows)
            wd_c[sl, :] = wd_ref[0, 0, sl, :].astype(wd_c.dtype)
            return c
        lax.fori_loop(0, f // rows, cv2, 0)

    @pl.when(i < nact_ref[0])
    def _():
        gu = _dot(x_ref[...].astype(MXU_DTYPE), wgu_c[...]) + bgu_ref[0, 0]
        gate = jnp.minimum(gu[:, :f], SWIGLU_LIMIT)
        up = jnp.clip(gu[:, f:], -SWIGLU_LIMIT, SWIGLU_LIMIT)
        act = gate * jax.nn.sigmoid(SWIGLU_ALPHA * gate) * (up + 1.0)
        y_ref[...] = _dot(act.astype(MXU_DTYPE), wd_c[...]) + bd_ref[0, 0]

    @pl.when(i >= nact_ref[0])
    def _():
        y_ref[...] = jnp.zeros_like(y_ref)


def _moe_call(layer, blk_exp, n_active, xg, w_gate_up, b_gate_up, w_down, b_down):
    n_rows, d = xg.shape
    depth, n_e, _, f2 = w_gate_up.shape
    f = f2 // 2
    n_blocks = n_rows // MOE_BLK
    wsel = lambda i, be, na: (layer, be[i], 0, 0)
    grid_spec = pltpu.PrefetchScalarGridSpec(
        num_scalar_prefetch=2,
        grid=(n_blocks,),
        in_specs=[pl.BlockSpec((MOE_BLK, d), lambda i, be, na: (i, 0)),
                  pl.BlockSpec((1, 1, d, f2), wsel),
                  pl.BlockSpec((1, 1, 1, f2), wsel),
                  pl.BlockSpec((1, 1, f, d), wsel),
                  pl.BlockSpec((1, 1, 1, d), wsel)],
        out_specs=pl.BlockSpec((MOE_BLK, d), lambda i, be, na: (i, 0)),
        scratch_shapes=[pltpu.VMEM((d, f2), MXU_DTYPE), pltpu.VMEM((f, d), MXU_DTYPE)],
    )
    return pl.pallas_call(
        _moe_kernel,
        grid_spec=grid_spec,
        out_shape=jax.ShapeDtypeStruct((n_rows, d), F32),
        compiler_params=_params("arbitrary"),
        name="moe_experts",
    )(blk_exp, n_active, xg, w_gate_up, b_gate_up.reshape(depth, n_e, 1, f2),
      w_down, b_down.reshape(depth, n_e, 1, d))


def _combine_kernel(dest_ref, w_ref, x_ref, mod_ref, yg_ref, o_ref, gbuf, sem):
    tm = x_ref.shape[1]

    def body(t, c):
        for k in range(TOP_K):
            pltpu.make_async_copy(yg_ref.at[pl.ds(dest_ref[0, k, t], 1)], gbuf.at[k, pl.ds(t, 1)],
                                  sem).start(priority=k % 2)
        return c

    lax.fori_loop(0, tm, body, 0, unroll=8)
    pltpu.make_async_copy(gbuf, gbuf, sem).wait()
    w = w_ref[0]
    f = gbuf[0] * w[:, 0:1]
    for k in range(1, TOP_K):
        f = f + gbuf[k] * w[:, k:k + 1]
    o_ref[0] = x_ref[0] + mod_ref[0][5:6] * f


def _combine_call(dest, topw, xs, mod_l, yg, ctx_len):
    bsz, s, d = xs.shape
    tm = ROW_TILE
    row = lambda b, j: (b, j, 0)
    return pl.pallas_call(
        _combine_kernel,
        grid=(bsz, s // tm),
        in_specs=[pl.BlockSpec((1, TOP_K, tm), lambda b, j: (b, 0, j), memory_space=pltpu.SMEM),
                  pl.BlockSpec((1, tm, MOD_ROWS), row),
                  pl.BlockSpec((1, tm, d), row),
                  pl.BlockSpec((1, 6, d), _mod_index(ctx_len // tm, bsz)),
                  pl.BlockSpec(memory_space=pl.ANY)],
        out_specs=pl.BlockSpec((1, tm, d), row),
        out_shape=jax.ShapeDtypeStruct((bsz, s, d), F32),
        scratch_shapes=[pltpu.VMEM((TOP_K, tm, d), F32), pltpu.SemaphoreType.DMA(())],
        compiler_params=_params("arbitrary", "arbitrary"),
        name="moe_combine",
    )(dest, topw, xs, mod_l, yg)


def _final_kernel(x_ref, g_ref, o_ref):
    x = x_ref[0]
    o_ref[0] = x * lax.rsqrt(jnp.mean(x * x, axis=-1, keepdims=True) + RMS_EPS) * g_ref[...]


def _final_call(xs, g, ctx_len):
    bsz, s, d = xs.shape
    tm = ROW_TILE
    off = ctx_len // tm
    return pl.pallas_call(
        _final_kernel,
        grid=(bsz, (s - ctx_len) // tm),
        in_specs=[pl.BlockSpec((1, tm, d), lambda b, j: (b, j + off, 0)),
                  pl.BlockSpec((1, d), lambda b, j: (0, 0))],
        out_specs=pl.BlockSpec((1, tm, d), lambda b, j: (b, j, 0)),
        out_shape=jax.ShapeDtypeStruct((bsz, s - ctx_len, d), F32),
        compiler_params=_params("parallel", "parallel"),
        name="final_norm",
    )(xs, g)


def _rope_tables(seq, ctx_len):
    t = jnp.arange(seq)
    row = (t // GRID_W).astype(F32)
    col = (t % GRID_W).astype(F32)
    nf = HEAD_DIM // 4
    inv = ROPE_BASE ** (-jnp.arange(nf, dtype=F32) / nf)
    ar = row[:, None] * inv
    ac = col[:, None] * inv
    ang = jnp.concatenate([ar, ar, ac, ac], axis=-1)
    cos = jnp.concatenate([jnp.ones((ctx_len, HEAD_DIM), F32), jnp.cos(ang)], axis=0)
    sin = jnp.concatenate([jnp.zeros((ctx_len, HEAD_DIM), F32), jnp.sin(ang)], axis=0)
    reps = LANES // HEAD_DIM
    return jnp.tile(cos, (1, reps)), jnp.tile(sin, (1, reps))


def _rot_cols(w):
    q = HEAD_DIM // 4
    j = np.arange(HEAD_DIM)
    first = (j % (2 * q)) < q
    src = np.where(first, j + q, j - q)
    sign = np.where(first, -1.0, 1.0).astype(np.float32)
    n_heads = w.shape[1] // HEAD_DIM
    src_all = (np.arange(n_heads)[:, None] * HEAD_DIM + src[None, :]).reshape(-1)
    return w[:, src_all] * jnp.asarray(np.tile(sign, n_heads))


def _head_perm_cols(order):
    return (np.asarray(order)[:, None] * HEAD_DIM + np.arange(HEAD_DIM)[None, :]).reshape(-1)


def _inproj_weight(w_in_l):
    aq, ak, av, su, sq, sk, sv = jnp.split(
        w_in_l, np.cumsum([NA_W, NA_W, NA_W, S5_CH, SW_W, SW_KV_W])[:6].tolist(), axis=1)
    sq = sq[:, _head_perm_cols(SW_HEAD_ORDER)]
    return jnp.concatenate([aq, ak, av, sq, _rot_cols(sq), sk, _rot_cols(sk), sv, su], axis=1).astype(MXU_DTYPE)


def _chunk_major(su):
    bsz, s, _ = su.shape
    u = su.reshape(bsz, s // S5_CHUNK, S5_CHUNK, S5_GROUPS, S5_GROUP_CH).transpose(0, 1, 3, 2, 4)
    return u.reshape(bsz, s // S5_CHUNK, S5_GROUPS * S5_CHUNK * S5_GROUP_CH)


def _token_major(y_t):
    bsz, nc, _ = y_t.shape
    y = y_t.reshape(bsz, nc, S5_GROUPS, S5_CHUNK, S5_GROUP_CH).transpose(0, 1, 3, 2, 4)
    return y.reshape(bsz, nc * S5_CHUNK, S5_CH)


def kernel(x, c, ctx, c_ctx, w_mod, b_mod, g_mix, w_in, w_out, na_rpb, s5_a_re, s5_a_im, s5_log_step,
           s5_b_re, s5_b_im, s5_c_re, s5_c_im, s5_d, s5_w_glu, s5_b_glu, sw_sinks, g_ffn, w_router, b_router,
           w_gate_up, b_gate_up, w_down, b_down, g_final):
    bsz, seq, d = x.shape
    ctx_len = ctx.shape[1]
    depth = w_mod.shape[0]
    s = ctx_len + seq
    assert bsz + 1 <= MOD_ROWS and ctx_len % ROW_TILE == 0 and seq % ROW_TILE == 0
    assert seq % GRID_W == 0 and ctx_len % S5_CHUNK == 0

    xs = jnp.concatenate([ctx, x], axis=1)
    cond = jnp.zeros((MOD_ROWS, d), F32).at[:bsz].set(c).at[bsz].set(c_ctx)
    mod = _mod_call(cond, w_mod, b_mod).reshape(depth, MOD_ROWS, 6, d)
    cos2, sin2 = _rope_tables(seq, ctx_len)

    n_assign = bsz * s * TOP_K
    n_blocks = -(-(n_assign + N_EXPERTS * (MOE_BLK - 1)) // MOE_BLK)
    sw_rows = _head_perm_cols(SW_HEAD_ORDER)

    for l in range(depth):
        mod_l = mod[l]
        naq, nak, nav, swq, swk, swv, su = _inproj_call(
            xs, mod_l, g_mix[l].reshape(1, d), _inproj_weight(w_in[l]), cos2, sin2, ctx_len)
        ya = _na_call(naq, nak, nav, _na_bias_table(na_rpb[l]), ctx_len)
        yc = _sw_call(sw_sinks[l], swq, swk, swv, ctx_len)
        s5w = _s5_weights(s5_a_re[l], s5_a_im[l], s5_log_step[l], s5_b_re[l], s5_b_im[l], s5_c_re[l], s5_c_im[l])
        ys = _token_major(_s5_call(_chunk_major(su).astype(MXU_DTYPE), s5w, ctx_len // S5_CHUNK))

        wo = w_out[l]
        out_wts = (wo[:NA_W].astype(MXU_DTYPE),
                   wo[NA_W:NA_W + S5_CH].astype(MXU_DTYPE),
                   wo[NA_W + S5_CH:][sw_rows].astype(MXU_DTYPE),
                   s5_w_glu[l].astype(MXU_DTYPE), s5_b_glu[l].reshape(1, S5_CH).astype(F32),
                   s5_d[l].reshape(1, S5_CH).astype(F32), g_ffn[l].reshape(1, d),
                   w_router[l].T.astype(F32), b_router[l].reshape(N_EXPERTS, 1).astype(F32))
        xs, h, topi, topw = _outproj_call(xs, ya, ys, su, yc, mod_l, out_wts, ctx_len)

        dest, blk, ends = _route_call(topi, n_blocks)
        ends = ends[:, 0]
        xg = _dispatch_call(ends, dest, h, n_blocks)
        yg = _moe_call(l, blk[0, :n_blocks], ends[-1:] // MOE_BLK, xg, w_gate_up, b_gate_up, w_down, b_down)
        xs = _combine_call(dest, topw, xs, mod_l, yg, ctx_len)

    return _final_call(xs, g_final.reshape(1, d), ctx_len)
```

```python
import functools
import math

import numpy as np
import jax
import jax.numpy as jnp
from jax import lax
from jax.experimental import pallas as pl
from jax.experimental.pallas import tpu as pltpu

F32 = jnp.float32
MXU_DTYPE = jnp.bfloat16

GRID_W = 64
HEAD_DIM = 64
NA_HEADS = 6
NA_W = NA_HEADS * HEAD_DIM
NA_ROWS = 8
NA_COLS = 16
S5_GROUP_CH = 16
S5_CH = 256
S5_GROUPS = S5_CH // S5_GROUP_CH
S5_STATE = 64
S5_EIG_MAX = -1e-4
SW_HEADS = 6
SW_KV_HEADS = 2
SW_GRP = SW_HEADS // SW_KV_HEADS
SW_W = SW_HEADS * HEAD_DIM
SW_KV_W = SW_KV_HEADS * HEAD_DIM
SW_WINDOW = 128
SW_BLK = 128
ROPE_BASE = 10000.0
N_EXPERTS = 32
TOP_K = 4
MOE_BLK = 512
SWIGLU_LIMIT = 7.0
SWIGLU_ALPHA = 1.702
RMS_EPS = 1e-6
NEG_INF = -1e30

LANES = 128
ROW_TILE = 256
S5_CHUNK = 16
NA_QROWS = 4
NA_KROWS = NA_QROWS + NA_ROWS
SW_QBLK = 2 * SW_BLK
SW_KBLK = SW_QBLK + 2 * SW_WINDOW
MOD_ROWS = 8
VMEM_LIMIT = 56 << 20

SW_HEAD_ORDER = tuple(g * SW_GRP + t for t in range(SW_GRP) for g in range(SW_KV_HEADS))


def _params(*sem):
    return pltpu.CompilerParams(dimension_semantics=sem, vmem_limit_bytes=VMEM_LIMIT)


def _dot(a, b):
    return jnp.dot(a, b, preferred_element_type=F32)


def _dot_nt(a, b):
    return lax.dot_general(a, b, (((1,), (1,)), ((), ())), preferred_element_type=F32)


def _split(a):
    hi = a.astype(MXU_DTYPE)
    lo = (a - hi.astype(F32)).astype(MXU_DTYPE)
    return hi, lo


def _dot3(a, b, nt=False):
    f = _dot_nt if nt else _dot
    ah, al = _split(a)
    bh, bl = _split(b)
    return f(ah, bh) + (f(ah, bl) + f(al, bh))


def _mod_kernel(cond_ref, w_ref, b_ref, o_ref):
    c = cond_ref[...]
    a = c * jax.nn.sigmoid(c)
    o_ref[0] = _dot3(a, w_ref[0]) + b_ref[0]


def _mod_call(cond, w_mod, b_mod):
    depth, d, n = w_mod.shape
    tn = n // 6
    return pl.pallas_call(
        _mod_kernel,
        grid=(depth, n // tn),
        in_specs=[pl.BlockSpec((MOD_ROWS, d), lambda l, j: (0, 0)),
                  pl.BlockSpec((1, d, tn), lambda l, j: (l, 0, j)),
                  pl.BlockSpec((1, 1, tn), lambda l, j: (l, 0, j))],
        out_specs=pl.BlockSpec((1, MOD_ROWS, tn), lambda l, j: (l, 0, j)),
        out_shape=jax.ShapeDtypeStruct((depth, MOD_ROWS, n), F32),
        compiler_params=_params("parallel", "parallel"),
        name="mod",
    )(cond, w_mod, b_mod.reshape(depth, 1, n))


C_AQ = 0
C_AK = C_AQ + NA_W
C_AV = C_AK + NA_W
C_SQ = C_AV + NA_W
C_SQR = C_SQ + SW_W
C_SK = C_SQR + SW_W
C_SKR = C_SK + SW_KV_W
C_SV = C_SKR + SW_KV_W
C_SU = C_SV + SW_KV_W
C_END = C_SU + S5_CH


def _rms_mod(x, g, shift, scale):
    y = x * lax.rsqrt(jnp.mean(x * x, axis=-1, keepdims=True) + RMS_EPS) * g
    return y * (1.0 + scale) + shift


def _inproj_kernel(x_ref, mod_ref, g_ref, w_ref, cos_ref, sin_ref,
                   naq_ref, nak_ref, nav_ref, swq_ref, swk_ref, swv_ref, su_ref):
    m = mod_ref[0]
    h = _rms_mod(x_ref[0], g_ref[...], m[0:1], m[1:2]).astype(MXU_DTYPE)
    p = _dot(h, w_ref[...])
    cos = cos_ref[...]
    sin = sin_ref[...]
    cos3 = jnp.concatenate([cos] * (SW_W // LANES), axis=1)
    sin3 = jnp.concatenate([sin] * (SW_W // LANES), axis=1)
    qk_scale = HEAD_DIM ** -0.5
    naq_ref[0] = (p[:, C_AQ:C_AK] * qk_scale).astype(naq_ref.dtype)
    nak_ref[0] = p[:, C_AK:C_AV].astype(nak_ref.dtype)
    nav_ref[0] = p[:, C_AV:C_SQ].astype(nav_ref.dtype)
    swq_ref[0] = ((p[:, C_SQ:C_SQR] * cos3 + p[:, C_SQR:C_SK] * sin3) * qk_scale).astype(swq_ref.dtype)
    swk_ref[0] = (p[:, C_SK:C_SKR] * cos + p[:, C_SKR:C_SV] * sin).astype(swk_ref.dtype)
    swv_ref[0] = p[:, C_SV:C_SU].astype(swv_ref.dtype)
    su_ref[0] = p[:, C_SU:C_END]


def _mod_index(n_ctx_tiles, ctx_row):
    return lambda b, j: (jnp.where(j < n_ctx_tiles, ctx_row, b), 0, 0)


def _inproj_call(xs, mod_l, g, w_cat, cos2, sin2, ctx_len):
    bsz, s, d = xs.shape
    tm = ROW_TILE
    row = lambda b, j: (b, j, 0)
    const = lambda b, j: (0, 0)
    widths = (NA_W, NA_W, NA_W, SW_W, SW_KV_W, SW_KV_W, S5_CH)
    dtypes = (MXU_DTYPE,) * 6 + (F32,)
    return pl.pallas_call(
        _inproj_kernel,
        grid=(bsz, s // tm),
        in_specs=[pl.BlockSpec((1, tm, d), row),
                  pl.BlockSpec((1, 6, d), _mod_index(ctx_len // tm, bsz)),
                  pl.BlockSpec((1, d), const),
                  pl.BlockSpec((d, C_END), const),
                  pl.BlockSpec((tm, LANES), lambda b, j: (j, 0)),
                  pl.BlockSpec((tm, LANES), lambda b, j: (j, 0))],
        out_specs=[pl.BlockSpec((1, tm, w), row) for w in widths],
        out_shape=[jax.ShapeDtypeStruct((bsz, s, w), t) for w, t in zip(widths, dtypes)],
        compiler_params=_params("parallel", "parallel"),
        name="inproj",
    )(xs, mod_l, g, w_cat, cos2, sin2)


def _half_masks():
    lane = lax.broadcasted_iota(jnp.int32, (1, LANES), 1)
    return lane < HEAD_DIM, lane >= HEAD_DIM


def _na_kernel(q_ref, k_ref, v_ref, bias_ref, o_ref, *, ctx_len, rows):
    i = pl.program_id(1)
    tq = NA_QROWS * GRID_W
    n_ctx_q = ctx_len // tq
    masks = _half_masks()
    nk = NA_KROWS * GRID_W

    def run(local):
        q = q_ref[0]
        if local:
            r0 = (i - n_ctx_q) * NA_QROWS
            start0 = jnp.clip(r0 - NA_ROWS // 2, 0, rows - NA_KROWS)
            start = pl.multiple_of(ctx_len + start0 * GRID_W, GRID_W)
            tab_idx, row_mask = {}, {}
            for a in range(NA_QROWS):
                r = r0 + a
                s_r = jnp.clip(r - NA_ROWS // 2, 0, rows - NA_ROWS)
                for p in range(NA_KROWS // 2):
                    kr = start0 + 2 * p
                    ok_lo = (kr >= s_r) & (kr < s_r + NA_ROWS)
                    ok_hi = (kr + 1 >= s_r) & (kr + 1 < s_r + NA_ROWS)
                    tab_idx[a, p] = jnp.clip(kr - r + NA_ROWS, 0, 2 * NA_ROWS - 1)
                    row_mask[a, p] = jnp.where(masks[0], jnp.where(ok_lo, 0.0, NEG_INF),
                                               jnp.where(ok_hi, 0.0, NEG_INF))
        outs = []
        for t in range(NA_W // LANES):
            sl = slice(LANES * t, LANES * (t + 1))
            qt = q[:, sl]
            zero = jnp.zeros_like(qt)
            qm = jnp.concatenate([jnp.where(masks[0], qt, zero), jnp.where(masks[1], qt, zero)], axis=0)
            kc = k_ref[0, 0:ctx_len, sl]
            vc = v_ref[0, 0:ctx_len, sl]
            s_cx = _dot_nt(qm, kc)
            m = jnp.max(s_cx, axis=-1, keepdims=True)
            if local:
                kw = k_ref[0, pl.ds(start, nk), sl]
                vw = v_ref[0, pl.ds(start, nk), sl]
                bias = jnp.concatenate(
                    [jnp.concatenate([bias_ref[2 * t + hh, pl.ds(tab_idx[a, p], 1)][0] + row_mask[a, p]
                                      for p in range(NA_KROWS // 2)], axis=-1)
                     for hh in range(2) for a in range(NA_QROWS)], axis=0)
                s_nb = _dot_nt(qm, kw) + bias
                m = jnp.maximum(m, jnp.max(s_nb, axis=-1, keepdims=True))
                p_nb = jnp.exp(s_nb - m)
            p_cx = jnp.exp(s_cx - m)
            den = jnp.sum(p_cx, axis=-1, keepdims=True)
            o = _dot(p_cx.astype(MXU_DTYPE), vc)
            if local:
                den = den + jnp.sum(p_nb, axis=-1, keepdims=True)
                o = o + _dot(p_nb.astype(MXU_DTYPE), vw)
            o = o / den
            outs.append(jnp.where(masks[0], o[:tq], o[tq:]))
        o_ref[0] = jnp.concatenate(outs, axis=-1).astype(o_ref.dtype)

    @pl.when(i < n_ctx_q)
    def _():
        run(False)

    @pl.when(i >= n_ctx_q)
    def _():
        run(True)


def _na_call(q, k, v, bias_tab, ctx_len):
    bsz, s, w = q.shape
    rows = (s - ctx_len) // GRID_W
    tq = NA_QROWS * GRID_W
    assert rows >= NA_KROWS and rows % NA_QROWS == 0 and ctx_len % tq == 0
    whole = lambda b, i: (b, 0, 0)
    return pl.pallas_call(
        functools.partial(_na_kernel, ctx_len=ctx_len, rows=rows),
        grid=(bsz, s // tq),
        in_specs=[pl.BlockSpec((1, tq, w), lambda b, i: (b, i, 0)),
                  pl.BlockSpec((1, s, w), whole),
                  pl.BlockSpec((1, s, w), whole),
                  pl.BlockSpec(bias_tab.shape, lambda b, i: (0, 0, 0, 0))],
        out_specs=pl.BlockSpec((1, tq, w), lambda b, i: (b, i, 0)),
        out_shape=jax.ShapeDtypeStruct((bsz, s, w), q.dtype),
        compiler_params=_params("parallel", "arbitrary"),
        name="na_attn",
    )(q, k, v, bias_tab)


def _na_bias_table(rpb):
    qcol = np.arange(GRID_W)[:, None]
    kcol = np.arange(GRID_W)[None, :]
    ws = np.clip(qcol - NA_COLS // 2, 0, GRID_W - NA_COLS)
    valid = (kcol >= ws) & (kcol < ws + NA_COLS)
    dc = np.clip(kcol - qcol + NA_COLS - 1, 0, 2 * NA_COLS - 2)
    full = jnp.where(valid[None, None], rpb[:, :, dc].astype(F32), NEG_INF)
    edge = jnp.full_like(full[:, :1], NEG_INF)
    full = jnp.concatenate([edge, full, edge], axis=1)
    return jnp.concatenate([full[:, :-1], full[:, 1:]], axis=-1)


def _sw_kernel(sink_ref, q_ref, k_ref, v_ref, o_ref, *, ctx_len, seq):
    i = pl.program_id(1)
    tq = SW_QBLK
    n_ctx_q = ctx_len // tq
    masks = _half_masks()
    nk = SW_KBLK
    first_head = lax.broadcasted_iota(jnp.int32, (2 * tq, 1), 0) < tq

    def run(local):
        q = q_ref[0]
        kc = k_ref[0, 0:ctx_len, :]
        vc = v_ref[0, 0:ctx_len, :]
        if local:
            n = i - n_ctx_q
            start_lat = jnp.clip(n * tq - SW_WINDOW, 0, seq - nk)
            start = pl.multiple_of(ctx_len + start_lat, SW_BLK)
            kw = k_ref[0, pl.ds(start, nk), :]
            vw = v_ref[0, pl.ds(start, nk), :]
            row = lax.broadcasted_iota(jnp.int32, (2 * tq, 1), 0)
            qpos = n * tq + jnp.where(first_head, row, row - tq)
            kpos = start_lat + lax.broadcasted_iota(jnp.int32, (1, nk), 1)
            valid = jnp.abs(qpos - kpos) <= SW_WINDOW
        outs = []
        for t in range(SW_W // LANES):
            qt = q[:, LANES * t:LANES * (t + 1)]
            zero = jnp.zeros_like(qt)
            qm = jnp.concatenate([jnp.where(masks[0], qt, zero), jnp.where(masks[1], qt, zero)], axis=0)
            sink = jnp.where(first_head, sink_ref[SW_HEAD_ORDER[2 * t]], sink_ref[SW_HEAD_ORDER[2 * t + 1]])
            s_cx = _dot_nt(qm, kc)
            m = jnp.maximum(jnp.max(s_cx, axis=-1, keepdims=True), sink)
            if local:
                s_loc = jnp.where(valid, _dot_nt(qm, kw), NEG_INF)
                m = jnp.maximum(m, jnp.max(s_loc, axis=-1, keepdims=True))
                p_loc = jnp.exp(s_loc - m)
            p_cx = jnp.exp(s_cx - m)
            den = jnp.sum(p_cx, axis=-1, keepdims=True) + jnp.exp(sink - m)
            o = _dot(p_cx.astype(MXU_DTYPE), vc)
            if local:
                den = den + jnp.sum(p_loc, axis=-1, keepdims=True)
                o = o + _dot(p_loc.astype(MXU_DTYPE), vw)
            o = o / den
            outs.append(jnp.where(masks[0], o[:tq], o[tq:]))
        o_ref[0] = jnp.concatenate(outs, axis=-1).astype(o_ref.dtype)

    @pl.when(i < n_ctx_q)
    def _():
        run(False)

    @pl.when(i >= n_ctx_q)
    def _():
        run(True)


def _sw_call(sinks, q, k, v, ctx_len):
    bsz, s, w = q.shape
    seq = s - ctx_len
    assert seq >= SW_KBLK and seq % SW_QBLK == 0 and ctx_len % SW_QBLK == 0
    whole = lambda b, i: (b, 0, 0)
    return pl.pallas_call(
        functools.partial(_sw_kernel, ctx_len=ctx_len, seq=seq),
        grid=(bsz, s // SW_QBLK),
        in_specs=[pl.BlockSpec(memory_space=pltpu.SMEM),
                  pl.BlockSpec((1, SW_QBLK, w), lambda b, i: (b, i, 0)),
                  pl.BlockSpec((1, s, SW_KV_W), whole),
                  pl.BlockSpec((1, s, SW_KV_W), whole)],
        out_specs=pl.BlockSpec((1, SW_QBLK, w), lambda b, i: (b, i, 0)),
        out_shape=jax.ShapeDtypeStruct((bsz, s, w), q.dtype),
        compiler_params=_params("parallel", "arbitrary"),
        name="sw_attn",
    )(sinks.astype(F32), q, k, v)


def _s5_kernel(u_ref, m_ref, wsr_ref, wsi_ref, wor_ref, woi_ref, lr_ref, li_ref, o_ref,
               sre, sim, xre, xim, *, n_ctx_chunks):
    d = pl.program_id(1)
    bsz, nc, _ = u_ref.shape
    for b in range(bsz):
        ub = u_ref[b]
        sre[b] = _dot(ub, wsr_ref[0, 0])
        sim[b] = _dot(ub, wsi_ref[0, 0])
    lr = lr_ref[0, 0]
    li = li_ref[0, 0]

    def step(c, carry):
        new = []
        for b in range(bsz):
            xr, xi = carry[2 * b], carry[2 * b + 1]
            xre[b, pl.ds(c, 1), :] = xr
            xim[b, pl.ds(c, 1), :] = xi
            sr = sre[b, pl.ds(c, 1), :]
            si = sim[b, pl.ds(c, 1), :]
            new.append(lr * xr - li * xi + sr)
            new.append(lr * xi + li * xr + si)
        return tuple(new)

    zero = tuple(jnp.zeros((1, LANES), F32) for _ in range(2 * bsz))

    @pl.when(d == 0)
    def _():
        lax.fori_loop(0, nc, step, zero)

    @pl.when(d == 1)
    def _():
        carry = lax.fori_loop(0, n_ctx_chunks, lambda k, cr: step(n_ctx_chunks - 1 - k, cr), zero)
        lax.fori_loop(0, nc - n_ctx_chunks, lambda k, cr: step(nc - 1 - k, cr), carry)

    half = u_ref.shape[2] // 2
    for b in range(bsz):
        ub = u_ref[b]
        y_intra = jnp.concatenate([_dot(ub[:, :half], m_ref[0, 0]), _dot(ub[:, half:], m_ref[0, 1])], axis=-1)
        y = (y_intra + _dot(xre[b].astype(MXU_DTYPE), wor_ref[0, 0])
             + _dot(xim[b].astype(MXU_DTYPE), woi_ref[0, 0]))

        @pl.when(d == 0)
        def _():
            o_ref[b] = y

        @pl.when(d == 1)
        def _():
            o_ref[b] = o_ref[b] + y


def _s5_call(u_t, wts, n_ctx_chunks):
    m, wsr, wsi, wor, woi, lr, li = wts
    bsz, nc, width = u_t.shape
    pw = 2 * S5_CHUNK * S5_GROUP_CH
    n_pairs = width // pw
    blk = lambda shp: pl.BlockSpec((1, 1) + shp, lambda j, d: (d, j, 0, 0))
    return pl.pallas_call(
        functools.partial(_s5_kernel, n_ctx_chunks=n_ctx_chunks),
        grid=(n_pairs, 2),
        in_specs=[pl.BlockSpec((bsz, nc, pw), lambda j, d: (0, 0, j)),
                  pl.BlockSpec((1, 2, pw // 2, pw // 2), lambda j, d: (d, j, 0, 0)),
                  blk((pw, LANES)), blk((pw, LANES)), blk((LANES, pw)), blk((LANES, pw)),
                  blk((1, LANES)), blk((1, LANES))],
        out_specs=pl.BlockSpec((bsz, nc, pw), lambda j, d: (0, 0, j)),
        out_shape=jax.ShapeDtypeStruct((bsz, nc, width), F32),
        scratch_shapes=[pltpu.VMEM((bsz, nc, LANES), F32) for _ in range(4)],
        compiler_params=_params("parallel", "arbitrary"),
        name="s5_scan",
    )(u_t, m, wsr, wsi, wor, woi, lr, li)


def _s5_weights(a_re, a_im, log_step, b_re, b_im, c_re, c_im):
    lc, g, p, h = S5_CHUNK, S5_GROUPS, S5_STATE, S5_GROUP_CH
    lam = lax.complex(jnp.minimum(a_re.astype(F32), S5_EIG_MAX), a_im.astype(F32))
    step = jnp.exp(log_step.astype(F32))[..., None]
    lam_bar = jnp.exp(lam * step)
    b_bar = ((lam_bar - 1.0) / lam)[..., None] * lax.complex(b_re.astype(F32), b_im.astype(F32))
    cc = lax.complex(c_re.astype(F32), c_im.astype(F32))
    dd = jnp.arange(lc + 1, dtype=F32)
    pw = jnp.exp((lam * step)[..., None] * dd)
    kern = jnp.real(jnp.einsum('zgop,zgpd,zgpi->zgdoi', cc, pw[..., :lc], b_bar))
    jj = np.arange(lc)[:, None]
    ii = np.arange(lc)[None, :]
    mats, wst, wout = [], [], []
    for z in range(2):
        lag = (ii - jj) if z == 0 else (jj - ii)
        ok = lag >= 0
        kz = kern[z][:, np.where(ok, lag, 0)]
        kz = jnp.where(ok[None, :, :, None, None], kz, 0.0)
        mats.append(kz.transpose(0, 1, 4, 2, 3).reshape(g, lc * h, lc * h))
        d_state = (lc - 1 - np.arange(lc)) if z == 0 else np.arange(lc)
        ws = pw[z][:, :, d_state][..., None] * b_bar[z][:, :, None, :]
        wst.append(ws.transpose(0, 2, 3, 1).reshape(g, lc * h, p))
        d_out = (np.arange(lc) + 1) if z == 0 else (lc - np.arange(lc))
        wo = cc[z][:, :, :, None] * pw[z][:, None, :, :][..., d_out]
        wout.append(wo.transpose(0, 2, 3, 1).reshape(g, p, lc * h))
    mats = jnp.stack(mats)
    wst = jnp.stack(wst)
    wout = jnp.stack(wout)

    def pair_rows(w):
        w = w.reshape(2, g // 2, 2, lc * h, p)
        z0 = jnp.zeros_like(w[:, :, 0])
        top = jnp.concatenate([w[:, :, 0], z0], axis=-1)
        bot = jnp.concatenate([z0, w[:, :, 1]], axis=-1)
        return jnp.concatenate([top, bot], axis=-2)

    def pair_cols(w):
        w = w.reshape(2, g // 2, 2, p, lc * h)
        z0 = jnp.zeros_like(w[:, :, 0])
        top = jnp.concatenate([w[:, :, 0], z0], axis=-1)
        bot = jnp.concatenate([z0, w[:, :, 1]], axis=-1)
        return jnp.concatenate([top, bot], axis=-2)

    lam_c = pw[..., lc].reshape(2, g // 2, 1, 2 * p)
    cast = lambda w: w.astype(MXU_DTYPE)
    return (cast(mats), cast(pair_rows(jnp.real(wst))), cast(pair_rows(jnp.imag(wst))),
            cast(pair_cols(jnp.real(wout))), cast(pair_cols(-jnp.imag(wout))),
            jnp.real(lam_c), jnp.imag(lam_c))


def _gelu_tanh(x):
    cdf = 0.5 * (1.0 + jnp.tanh(math.sqrt(2.0 / math.pi) * (x + 0.044715 * (x * x * x))))
    return x * cdf


def _outproj_kernel(x_ref, ya_ref, ys_ref, su_ref, yc_ref, mod_ref, woa_ref, wob_ref, woc_ref,
                    wglu_ref, bglu_ref, dsk_ref, g_ref, wr_ref, br_ref,
                    xo_ref, h_ref, topi_ref, topw_ref):
    m = mod_ref[0]
    y = dsk_ref[...] * su_ref[0] + ys_ref[0]
    gl = _gelu_tanh(y)
    yb = gl * jax.nn.sigmoid(_dot(gl.astype(MXU_DTYPE), wglu_ref[...]) + bglu_ref[...])
    mix = (_dot(ya_ref[0], woa_ref[...]) + _dot(yb.astype(MXU_DTYPE), wob_ref[...])
           + _dot(yc_ref[0], woc_ref[...]))
    x = x_ref[0] + m[2:3] * mix
    xo_ref[0] = x
    h = _rms_mod(x, g_ref[...], m[3:4], m[4:5])
    h_ref[0] = h
    logits = _dot3(wr_ref[...], h, nt=True) + br_ref[...]
    n_e, tm = logits.shape
    e_iota = lax.broadcasted_iota(jnp.int32, (n_e, tm), 0)
    vals, idxs = [], []
    for _ in range(TOP_K):
        mx = jnp.max(logits, axis=0, keepdims=True)
        ix = jnp.min(jnp.where(logits == mx, e_iota, n_e), axis=0, keepdims=True)
        vals.append(mx)
        idxs.append(ix)
        logits = jnp.where(e_iota == ix, -jnp.inf, logits)
    ex = [jnp.exp(v - vals[0]) for v in vals]
    den = ex[0] + ex[1] + ex[2] + ex[3]
    topi_ref[0] = jnp.concatenate(idxs, axis=0)
    w8 = jnp.concatenate([e / den for e in ex] + [jnp.zeros((MOD_ROWS - TOP_K, tm), F32)], axis=0)
    topw_ref[0] = w8.T


def _outproj_call(xs, ya, ys, su, yc, mod_l, wts, ctx_len):
    bsz, s, d = xs.shape
    tm = ROW_TILE
    row = lambda b, j: (b, j, 0)
    const = lambda b, j: (0, 0)
    full = lambda a: pl.BlockSpec(a.shape, const)
    return pl.pallas_call(
        _outproj_kernel,
        grid=(bsz, s // tm),
        in_specs=[pl.BlockSpec((1, tm, d), row),
                  pl.BlockSpec((1, tm, NA_W), row),
                  pl.BlockSpec((1, tm, S5_CH), row),
                  pl.BlockSpec((1, tm, S5_CH), row),
                  pl.BlockSpec((1, tm, SW_W), row),
                  pl.BlockSpec((1, 6, d), _mod_index(ctx_len // tm, bsz))] + [full(a) for a in wts],
        out_specs=[pl.BlockSpec((1, tm, d), row),
                   pl.BlockSpec((1, tm, d), row),
                   pl.BlockSpec((1, TOP_K, tm), lambda b, j: (b, 0, j)),
                   pl.BlockSpec((1, tm, MOD_ROWS), row)],
        out_shape=[jax.ShapeDtypeStruct((bsz, s, d), F32),
                   jax.ShapeDtypeStruct((bsz, s, d), F32),
                   jax.ShapeDtypeStruct((bsz, TOP_K, s), jnp.int32),
                   jax.ShapeDtypeStruct((bsz, s, MOD_ROWS), F32)],
        compiler_params=_params("parallel", "parallel"),
        name="outproj",
    )(xs, ya, ys, su, yc, mod_l, *wts)


def _route_kernel(topi_ref, dest_ref, blk_ref, ends_ref, carry, ranks, *, n_tiles):
    phase = pl.program_id(0)
    t = pl.program_id(1) * n_tiles + pl.program_id(2)
    idx = topi_ref[0]
    tm = idx.shape[1]
    e_iota = lax.broadcasted_iota(jnp.int32, (N_EXPERTS, tm), 0)
    sel = [idx[k:k + 1] == e_iota for k in range(TOP_K)]

    @pl.when(phase == 0)
    def _():
        @pl.when(t == 0)
        def _():
            carry[...] = jnp.zeros_like(carry)

        chosen = sel[0] | sel[1] | sel[2] | sel[3]
        onehot = jnp.where(chosen, 1.0, 0.0)
        before = (lax.broadcasted_iota(jnp.int32, (tm, tm), 0) < lax.broadcasted_iota(jnp.int32, (tm, tm), 1))
        pfx = _dot(onehot.astype(MXU_DTYPE), jnp.where(before, 1.0, 0.0).astype(MXU_DTYPE)) + carry[:, 0:1]
        ranks[t] = jnp.concatenate(
            [jnp.sum(jnp.where(sel[k], pfx, 0.0), axis=0, keepdims=True) for k in range(TOP_K)], axis=0)
        carry[...] = carry[...] + jnp.sum(onehot, axis=1, keepdims=True)

    @pl.when(phase == 1)
    def _():
        counts = carry[:, 0:1]
        padded = jnp.ceil(counts * (1.0 / MOE_BLK)) * MOE_BLK
        r_i = lax.broadcasted_iota(jnp.int32, (N_EXPERTS, N_EXPERTS), 0)
        c_i = lax.broadcasted_iota(jnp.int32, (N_EXPERTS, N_EXPERTS), 1)
        padded_row = jnp.sum(jnp.where(r_i == c_i, padded, 0.0), axis=0, keepdims=True)
        pstart = jnp.sum(jnp.where(c_i < r_i, padded_row, 0.0), axis=1, keepdims=True)
        ends = pstart + padded
        rk = ranks[t]
        dest_ref[0] = jnp.concatenate(
            [jnp.sum(jnp.where(sel[k], pstart, 0.0), axis=0, keepdims=True) + rk[k:k + 1] for k in range(TOP_K)],
            axis=1).astype(jnp.int32)
        nb = blk_ref.shape[1]
        blk_start = (lax.broadcasted_iota(jnp.int32, (N_EXPERTS, nb), 1) * MOE_BLK).astype(F32)
        owner = jnp.sum(jnp.where(ends <= blk_start, 1.0, 0.0), axis=0, keepdims=True)
        blk_ref[...] = jnp.minimum(owner, N_EXPERTS - 1.0).astype(jnp.int32)
        ends_ref[...] = jnp.broadcast_to(ends, ends_ref.shape).astype(jnp.int32)


def _route_call(topi, n_blocks):
    bsz, _, s = topi.shape
    tm = ROW_TILE
    n_tiles = s // tm
    nb_pad = -(-n_blocks // LANES) * LANES
    return pl.pallas_call(
        functools.partial(_route_kernel, n_tiles=n_tiles),
        grid=(2, bsz, n_tiles),
        in_specs=[pl.BlockSpec((1, TOP_K, tm), lambda p, b, j: (b, 0, j))],
        out_specs=[pl.BlockSpec((1, 1, TOP_K * tm), lambda p, b, j: ((b * n_tiles + j) * p, 0, 0)),
                   pl.BlockSpec((1, nb_pad), lambda p, b, j: (0, 0)),
                   pl.BlockSpec((N_EXPERTS, LANES), lambda p, b, j: (0, 0))],
        out_shape=[jax.ShapeDtypeStruct((bsz * n_tiles, 1, TOP_K * tm), jnp.int32),
                   jax.ShapeDtypeStruct((1, nb_pad), jnp.int32),
                   jax.ShapeDtypeStruct((N_EXPERTS, LANES), jnp.int32)],
        scratch_shapes=[pltpu.VMEM((N_EXPERTS, LANES), F32),
                        pltpu.VMEM((bsz * n_tiles, TOP_K, tm), F32)],
        compiler_params=_params("arbitrary", "arbitrary", "arbitrary"),
        name="route_rank",
    )(topi)


def _dispatch_kernel(ends_ref, dest_ref, h_ref, xg_ref, zbuf, stage, sem, zsem, *, n_blocks):
    tm = h_ref.shape[1]

    @pl.when((pl.program_id(0) == 0) & (pl.program_id(1) == 0))
    def _():
        zbuf[...] = jnp.zeros_like(zbuf)

        def fill(row):
            return pltpu.make_async_copy(zbuf, xg_ref.at[pl.ds(pl.multiple_of(row, MOE_BLK), MOE_BLK)], zsem)

        def each(fn):
            for e in range(N_EXPERTS):
                begin = ends_ref[e - 1] if e else 0

                @pl.when(ends_ref[e] > begin)
                def _():
                    fn(fill(ends_ref[e] - MOE_BLK))

            def dead(i, c):
                fn(fill(i * MOE_BLK))
                return c

            lax.fori_loop(ends_ref[N_EXPERTS - 1] // MOE_BLK, n_blocks, dead, 0)

        each(lambda cp: cp.start())
        each(lambda cp: cp.wait())

    step = pl.program_id(0) * pl.num_programs(1) + pl.program_id(1)
    n_steps = pl.num_programs(0) * pl.num_programs(1)
    slot = lax.rem(step, 2)
    stage[slot] = h_ref[0]

    def body(t, c):
        for k in range(TOP_K):
            pltpu.make_async_copy(stage.at[slot, pl.ds(t, 1)], xg_ref.at[pl.ds(dest_ref[0, 0, k * tm + t], 1)],
                                  sem.at[slot]).start(priority=k % 2)
        return c

    lax.fori_loop(0, tm, body, 0, unroll=8)

    def wait_tile(sl):
        pltpu.make_async_copy(xg_ref.at[pl.ds(0, TOP_K * tm)], xg_ref.at[pl.ds(0, TOP_K * tm)], sem.at[sl]).wait()

    @pl.when(step > 0)
    def _():
        wait_tile(1 - slot)

    @pl.when(step == n_steps - 1)
    def _():
        wait_tile(slot)


def _dispatch_call(ends, dest, h, n_blocks):
    bsz, s, d = h.shape
    tm = ROW_TILE
    n_tiles = s // tm
    grid_spec = pltpu.PrefetchScalarGridSpec(
        num_scalar_prefetch=1,
        grid=(bsz, n_tiles),
        in_specs=[pl.BlockSpec((1, 1, TOP_K * tm), lambda b, j, en: (b * n_tiles + j, 0, 0), memory_space=pltpu.SMEM),
                  pl.BlockSpec((1, tm, d), lambda b, j, en: (b, j, 0))],
        out_specs=pl.BlockSpec(memory_space=pl.ANY),
        scratch_shapes=[pltpu.VMEM((MOE_BLK, d), h.dtype), pltpu.VMEM((2, tm, d), h.dtype),
                        pltpu.SemaphoreType.DMA((2,)),
                        pltpu.SemaphoreType.DMA(())],
    )
    return pl.pallas_call(
        functools.partial(_dispatch_kernel, n_blocks=n_blocks),
        grid_spec=grid_spec,
        out_shape=jax.ShapeDtypeStruct((n_blocks * MOE_BLK, d), h.dtype),
        compiler_params=_params("arbitrary", "arbitrary"),
        name="moe_dispatch",
    )(ends, dest, h)


def _moe_kernel(blk_exp_ref, nact_ref, x_ref, wgu_ref, bgu_ref, wd_ref, bd_ref, y_ref, wgu_c, wd_c):
    i = pl.program_id(0)
    e = blk_exp_ref[i]
    prev = blk_exp_ref[jnp.maximum(i - 1, 0)]
    d, f2 = wgu_c.shape
    f = f2 // 2
    rows = 128

    @pl.when((i == 0) | (e != prev))
    def _():
        def cv(r, c):
            sl = pl.ds(pl.multiple_of(r * rows, rows), rows)
            wgu_c[sl, :] = wgu_ref[0, 0, sl, :].astype(wgu_c.dtype)
            return c
        lax.fori_loop(0, d // rows, cv, 0)

        def cv2(r, c):
            sl = pl.ds(pl.multiple_of(r * rows, rows), rows)
            wd_c[sl, :] = wd_ref[0, 0, sl, :].astype(wd_c.dtype)
            return c
        lax.fori_loop(0, f // rows, cv2, 0)

    @pl.when(i < nact_ref[0])
    def _():
        gu = _dot(x_ref[...].astype(MXU_DTYPE), wgu_c[...]) + bgu_ref[0, 0]
        gate = jnp.minimum(gu[:, :f], SWIGLU_LIMIT)
        up = jnp.clip(gu[:, f:], -SWIGLU_LIMIT, SWIGLU_LIMIT)
        act = gate * jax.nn.sigmoid(SWIGLU_ALPHA * gate) * (up + 1.0)
        y_ref[...] = _dot(act.astype(MXU_DTYPE), wd_c[...]) + bd_ref[0, 0]

    @pl.when(i >= nact_ref[0])
    def _():
        y_ref[...] = jnp.zeros_like(y_ref)


def _moe_call(layer, blk_exp, n_active, xg, w_gate_up, b_gate_up, w_down, b_down):
    n_rows, d = xg.shape
    depth, n_e, _, f2 = w_gate_up.shape
    f = f2 // 2
    n_blocks = n_rows // MOE_BLK
    wsel = lambda i, be, na: (layer, be[i], 0, 0)
    grid_spec = pltpu.PrefetchScalarGridSpec(
        num_scalar_prefetch=2,
        grid=(n_blocks,),
        in_specs=[pl.BlockSpec((MOE_BLK, d), lambda i, be, na: (i, 0)),
                  pl.BlockSpec((1, 1, d, f2), wsel),
                  pl.BlockSpec((1, 1, 1, f2), wsel),
                  pl.BlockSpec((1, 1, f, d), wsel),
                  pl.BlockSpec((1, 1, 1, d), wsel)],
        out_specs=pl.BlockSpec((MOE_BLK, d), lambda i, be, na: (i, 0)),
        scratch_shapes=[pltpu.VMEM((d, f2), MXU_DTYPE), pltpu.VMEM((f, d), MXU_DTYPE)],
    )
    return pl.pallas_call(
        _moe_kernel,
        grid_spec=grid_spec,
        out_shape=jax.ShapeDtypeStruct((n_rows, d), F32),
        compiler_params=_params("arbitrary"),
        name="moe_experts",
    )(blk_exp, n_active, xg, w_gate_up, b_gate_up.reshape(depth, n_e, 1, f2),
      w_down, b_down.reshape(depth, n_e, 1, d))


def _combine_kernel(dest_ref, dnext_ref, w_ref, x_ref, mod_ref, yg_ref, o_ref, gbuf, sem):
    tm = x_ref.shape[1]
    step = pl.program_id(0) * pl.num_programs(1) + pl.program_id(1)
    n_steps = pl.num_programs(0) * pl.num_programs(1)
    slot = lax.rem(step, 2)

    def gather(dref, sl):
        def body(t, c):
            for k in range(TOP_K):
                pltpu.make_async_copy(yg_ref.at[pl.ds(dref[0, 0, k * tm + t], 1)], gbuf.at[sl, k, pl.ds(t, 1)],
                                      sem.at[sl]).start(priority=k % 2)
            return c

        lax.fori_loop(0, tm, body, 0, unroll=8)

    @pl.when(step == 0)
    def _():
        gather(dest_ref, 0)

    @pl.when(step + 1 < n_steps)
    def _():
        gather(dnext_ref, 1 - slot)

    pltpu.make_async_copy(gbuf.at[slot], gbuf.at[slot], sem.at[slot]).wait()
    w = w_ref[0]
    f = gbuf[slot, 0] * w[:, 0:1]
    for k in range(1, TOP_K):
        f = f + gbuf[slot, k] * w[:, k:k + 1]
    o_ref[0] = x_ref[0] + mod_ref[0][5:6] * f


def _combine_call(dest, topw, xs, mod_l, yg, ctx_len):
    bsz, s, d = xs.shape
    tm = ROW_TILE
    n_tiles = s // tm
    last = bsz * n_tiles - 1
    row = lambda b, j: (b, j, 0)
    idx_spec = lambda ahead: pl.BlockSpec(
        (1, 1, TOP_K * tm), lambda b, j: (jnp.minimum(b * n_tiles + j + ahead, last), 0, 0), memory_space=pltpu.SMEM)
    return pl.pallas_call(
        _combine_kernel,
        grid=(bsz, n_tiles),
        in_specs=[idx_spec(0), idx_spec(1),
                  pl.BlockSpec((1, tm, MOD_ROWS), row),
                  pl.BlockSpec((1, tm, d), row),
                  pl.BlockSpec((1, 6, d), _mod_index(ctx_len // tm, bsz)),
                  pl.BlockSpec(memory_space=pl.ANY)],
        out_specs=pl.BlockSpec((1, tm, d), row),
        out_shape=jax.ShapeDtypeStruct((bsz, s, d), F32),
        scratch_shapes=[pltpu.VMEM((2, TOP_K, tm, d), F32), pltpu.SemaphoreType.DMA((2,))],
        compiler_params=_params("arbitrary", "arbitrary"),
        name="moe_combine",
    )(dest, dest, topw, xs, mod_l, yg)


def _final_kernel(x_ref, g_ref, o_ref):
    x = x_ref[0]
    o_ref[0] = x * lax.rsqrt(jnp.mean(x * x, axis=-1, keepdims=True) + RMS_EPS) * g_ref[...]


def _final_call(xs, g, ctx_len):
    bsz, s, d = xs.shape
    tm = ROW_TILE
    off = ctx_len // tm
    return pl.pallas_call(
        _final_kernel,
        grid=(bsz, (s - ctx_len) // tm),
        in_specs=[pl.BlockSpec((1, tm, d), lambda b, j: (b, j + off, 0)),
                  pl.BlockSpec((1, d), lambda b, j: (0, 0))],
        out_specs=pl.BlockSpec((1, tm, d), lambda b, j: (b, j, 0)),
        out_shape=jax.ShapeDtypeStruct((bsz, s - ctx_len, d), F32),
        compiler_params=_params("parallel", "parallel"),
        name="final_norm",
    )(xs, g)


def _rope_tables(seq, ctx_len):
    t = jnp.arange(seq)
    row = (t // GRID_W).astype(F32)
    col = (t % GRID_W).astype(F32)
    nf = HEAD_DIM // 4
    inv = ROPE_BASE ** (-jnp.arange(nf, dtype=F32) / nf)
    ar = row[:, None] * inv
    ac = col[:, None] * inv
    ang = jnp.concatenate([ar, ar, ac, ac], axis=-1)
    cos = jnp.concatenate([jnp.ones((ctx_len, HEAD_DIM), F32), jnp.cos(ang)], axis=0)
    sin = jnp.concatenate([jnp.zeros((ctx_len, HEAD_DIM), F32), jnp.sin(ang)], axis=0)
    reps = LANES // HEAD_DIM
    return jnp.tile(cos, (1, reps)), jnp.tile(sin, (1, reps))


def _rot_cols(w):
    q = HEAD_DIM // 4
    j = np.arange(HEAD_DIM)
    first = (j % (2 * q)) < q
    src = np.where(first, j + q, j - q)
    sign = np.where(first, -1.0, 1.0).astype(np.float32)
    n_heads = w.shape[1] // HEAD_DIM
    src_all = (np.arange(n_heads)[:, None] * HEAD_DIM + src[None, :]).reshape(-1)
    return w[:, src_all] * jnp.asarray(np.tile(sign, n_heads))


def _head_perm_cols(order):
    return (np.asarray(order)[:, None] * HEAD_DIM + np.arange(HEAD_DIM)[None, :]).reshape(-1)


def _inproj_weight(w_in_l):
    aq, ak, av, su, sq, sk, sv = jnp.split(
        w_in_l, np.cumsum([NA_W, NA_W, NA_W, S5_CH, SW_W, SW_KV_W])[:6].tolist(), axis=1)
    sq = sq[:, _head_perm_cols(SW_HEAD_ORDER)]
    return jnp.concatenate([aq, ak, av, sq, _rot_cols(sq), sk, _rot_cols(sk), sv, su], axis=1).astype(MXU_DTYPE)


def _chunk_major(su):
    bsz, s, _ = su.shape
    u = su.reshape(bsz, s // S5_CHUNK, S5_CHUNK, S5_GROUPS, S5_GROUP_CH).transpose(0, 1, 3, 2, 4)
    return u.reshape(bsz, s // S5_CHUNK, S5_GROUPS * S5_CHUNK * S5_GROUP_CH)


def _token_major(y_t):
    bsz, nc, _ = y_t.shape
    y = y_t.reshape(bsz, nc, S5_GROUPS, S5_CHUNK, S5_GROUP_CH).transpose(0, 1, 3, 2, 4)
    return y.reshape(bsz, nc * S5_CHUNK, S5_CH)


def kernel(x, c, ctx, c_ctx, w_mod, b_mod, g_mix, w_in, w_out, na_rpb, s5_a_re, s5_a_im, s5_log_step,
           s5_b_re, s5_b_im, s5_c_re, s5_c_im, s5_d, s5_w_glu, s5_b_glu, sw_sinks, g_ffn, w_router, b_router,
           w_gate_up, b_gate_up, w_down, b_down, g_final):
    bsz, seq, d = x.shape
    ctx_len = ctx.shape[1]
    depth = w_mod.shape[0]
    s = ctx_len + seq
    assert bsz + 1 <= MOD_ROWS and ctx_len % ROW_TILE == 0 and seq % ROW_TILE == 0
    assert seq % GRID_W == 0 and ctx_len % S5_CHUNK == 0

    xs = jnp.concatenate([ctx, x], axis=1)
    cond = jnp.zeros((MOD_ROWS, d), F32).at[:bsz].set(c).at[bsz].set(c_ctx)
    mod = _mod_call(cond, w_mod, b_mod).reshape(depth, MOD_ROWS, 6, d)
    cos2, sin2 = _rope_tables(seq, ctx_len)

    n_assign = bsz * s * TOP_K
    n_blocks = -(-(n_assign + N_EXPERTS * (MOE_BLK - 1)) // MOE_BLK)
    sw_rows = _head_perm_cols(SW_HEAD_ORDER)

    for l in range(depth):
        mod_l = mod[l]
        naq, nak, nav, swq, swk, swv, su = _inproj_call(
            xs, mod_l, g_mix[l].reshape(1, d), _inproj_weight(w_in[l]), cos2, sin2, ctx_len)
        ya = _na_call(naq, nak, nav, _na_bias_table(na_rpb[l]), ctx_len)
        yc = _sw_call(sw_sinks[l], swq, swk, swv, ctx_len)
        s5w = _s5_weights(s5_a_re[l], s5_a_im[l], s5_log_step[l], s5_b_re[l], s5_b_im[l], s5_c_re[l], s5_c_im[l])
        ys = _token_major(_s5_call(_chunk_major(su).astype(MXU_DTYPE), s5w, ctx_len // S5_CHUNK))

        wo = w_out[l]
        out_wts = (wo[:NA_W].astype(MXU_DTYPE),
                   wo[NA_W:NA_W + S5_CH].astype(MXU_DTYPE),
                   wo[NA_W + S5_CH:][sw_rows].astype(MXU_DTYPE),
                   s5_w_glu[l].astype(MXU_DTYPE), s5_b_glu[l].reshape(1, S5_CH).astype(F32),
                   s5_d[l].reshape(1, S5_CH).astype(F32), g_ffn[l].reshape(1, d),
                   w_router[l].T.astype(F32), b_router[l].reshape(N_EXPERTS, 1).astype(F32))
        xs, h, topi, topw = _outproj_call(xs, ya, ys, su, yc, mod_l, out_wts, ctx_len)

        dest, blk, ends = _route_call(topi, n_blocks)
        ends = ends[:, 0]
        xg = _dispatch_call(ends, dest, h, n_blocks)
        yg = _moe_call(l, blk[0, :n_blocks], ends[-1:] // MOE_BLK, xg, w_gate_up, b_gate_up, w_down, b_down)
        xs = _combine_call(dest, topw, xs, mod_l, yg, ctx_len)

    return _final_call(xs, g_final.reshape(1, d), ctx_len)
```

```python
import functools
import math

import numpy as np
import jax
import jax.numpy as jnp
from jax import lax
from jax.experimental import pallas as pl
from jax.experimental.pallas import tpu as pltpu

F32 = jnp.float32
MXU_DTYPE = jnp.bfloat16

GRID_W = 64
HEAD_DIM = 64
NA_HEADS = 6
NA_W = NA_HEADS * HEAD_DIM
NA_ROWS = 8
NA_COLS = 16
S5_GROUP_CH = 16
S5_CH = 256
S5_GROUPS = S5_CH // S5_GROUP_CH
S5_STATE = 64
S5_EIG_MAX = -1e-4
SW_HEADS = 6
SW_KV_HEADS = 2
SW_GRP = SW_HEADS // SW_KV_HEADS
SW_W = SW_HEADS * HEAD_DIM
SW_KV_W = SW_KV_HEADS * HEAD_DIM
SW_WINDOW = 128
SW_BLK = 128
ROPE_BASE = 10000.0
N_EXPERTS = 32
TOP_K = 4
MOE_BLK = 512
SWIGLU_LIMIT = 7.0
SWIGLU_ALPHA = 1.702
RMS_EPS = 1e-6
NEG_INF = -1e30

LANES = 128
ROW_TILE = 768
MOE_TILE = 256
S5_CHUNK = 16
NA_QROWS = 4
NA_KROWS = NA_QROWS + NA_ROWS
SW_QBLK = 2 * SW_BLK
SW_KBLK = SW_QBLK + 2 * SW_WINDOW
MOD_ROWS = 8
VMEM_LIMIT = 56 << 20
SUBLANES = 8

SW_HEAD_ORDER = tuple(g * SW_GRP + t for t in range(SW_GRP) for g in range(SW_KV_HEADS))


def _params(*sem):
    return pltpu.CompilerParams(dimension_semantics=sem, vmem_limit_bytes=VMEM_LIMIT)


def _dot(a, b):
    return jnp.dot(a, b, preferred_element_type=F32)


def _dot_nt(a, b):
    return lax.dot_general(a, b, (((1,), (1,)), ((), ())), preferred_element_type=F32)


def _split(a):
    hi = a.astype(MXU_DTYPE)
    lo = (a - hi.astype(F32)).astype(MXU_DTYPE)
    return hi, lo


def _dot3(a, b, nt=False):
    f = _dot_nt if nt else _dot
    ah, al = _split(a)
    bh, bl = _split(b)
    return f(ah, bh) + (f(ah, bl) + f(al, bh))


def _mod_kernel(cond_ref, w_ref, b_ref, o_ref):
    c = cond_ref[...]
    a = c * jax.nn.sigmoid(c)
    o_ref[0] = _dot3(a, w_ref[0]) + b_ref[0]


def _mod_call(cond, w_mod, b_mod):
    depth, d, n = w_mod.shape
    tn = n // 6
    return pl.pallas_call(
        _mod_kernel,
        grid=(depth, n // tn),
        in_specs=[pl.BlockSpec((MOD_ROWS, d), lambda l, j: (0, 0)),
                  pl.BlockSpec((1, d, tn), lambda l, j: (l, 0, j)),
                  pl.BlockSpec((1, 1, tn), lambda l, j: (l, 0, j))],
        out_specs=pl.BlockSpec((1, MOD_ROWS, tn), lambda l, j: (l, 0, j)),
        out_shape=jax.ShapeDtypeStruct((depth, MOD_ROWS, n), F32),
        compiler_params=_params("parallel", "parallel"),
        name="mod",
    )(cond, w_mod, b_mod.reshape(depth, 1, n))


C_AQ = 0
C_AK = C_AQ + NA_W
C_AV = C_AK + NA_W
C_SQ = C_AV + NA_W
C_SQR = C_SQ + SW_W
C_SK = C_SQR + SW_W
C_SKR = C_SK + SW_KV_W
C_SV = C_SKR + SW_KV_W
C_SU = C_SV + SW_KV_W
C_END = C_SU + S5_CH


def _rms_mod(x, g, shift, scale):
    y = x * lax.rsqrt(jnp.mean(x * x, axis=-1, keepdims=True) + RMS_EPS) * g
    return y * (1.0 + scale) + shift


def _mod_vectors(modb_ref, modc_ref, ctx_len, which):
    tm = ROW_TILE
    row = pl.program_id(1) * tm + lax.broadcasted_iota(jnp.int32, (tm, 1), 0)
    is_ctx = row < ctx_len
    mb = modb_ref[0]
    mc = modc_ref[0]
    return [jnp.where(is_ctx, mc[i:i + 1], mb[i:i + 1]) for i in which]


def _inproj_kernel(x_ref, modb_ref, modc_ref, g_ref, w_ref, cos_ref, sin_ref,
                   naq_ref, nak_ref, nav_ref, swq_ref, swk_ref, swv_ref, su_ref, *, ctx_len):
    shift, scale = _mod_vectors(modb_ref, modc_ref, ctx_len, (0, 1))
    h = _rms_mod(x_ref[0], g_ref[...], shift, scale).astype(MXU_DTYPE)
    p = _dot(h, w_ref[...])
    cos = cos_ref[...]
    sin = sin_ref[...]
    cos3 = jnp.concatenate([cos] * (SW_W // LANES), axis=1)
    sin3 = jnp.concatenate([sin] * (SW_W // LANES), axis=1)
    qk_scale = HEAD_DIM ** -0.5
    naq_ref[0] = (p[:, C_AQ:C_AK] * qk_scale).astype(naq_ref.dtype)
    nak_ref[0] = p[:, C_AK:C_AV].astype(nak_ref.dtype)
    nav_ref[0] = p[:, C_AV:C_SQ].astype(nav_ref.dtype)
    swq_ref[0] = ((p[:, C_SQ:C_SQR] * cos3 + p[:, C_SQR:C_SK] * sin3) * qk_scale).astype(swq_ref.dtype)
    swk_ref[0] = (p[:, C_SK:C_SKR] * cos + p[:, C_SKR:C_SV] * sin).astype(swk_ref.dtype)
    swv_ref[0] = p[:, C_SV:C_SU].astype(swv_ref.dtype)
    su_ref[0] = p[:, C_SU:C_END]


def _mod_index(n_ctx_tiles, ctx_row):
    return lambda b, j: (jnp.where(j < n_ctx_tiles, ctx_row, b), 0, 0)


def _inproj_call(xs, mod_l, g, w_cat, cos2, sin2, ctx_len):
    bsz, s, d = xs.shape
    tm = ROW_TILE
    row = lambda b, j: (b, j, 0)
    const = lambda b, j: (0, 0)
    widths = (NA_W, NA_W, NA_W, SW_W, SW_KV_W, SW_KV_W, S5_CH)
    dtypes = (MXU_DTYPE,) * 6 + (F32,)
    return pl.pallas_call(
        functools.partial(_inproj_kernel, ctx_len=ctx_len),
        grid=(bsz, s // tm),
        in_specs=[pl.BlockSpec((1, tm, d), row),
                  pl.BlockSpec((1, 6, d), lambda b, j: (b, 0, 0)),
                  pl.BlockSpec((1, 6, d), lambda b, j: (bsz, 0, 0)),
                  pl.BlockSpec((1, d), const),
                  pl.BlockSpec((d, C_END), const),
                  pl.BlockSpec((tm, LANES), lambda b, j: (j, 0)),
                  pl.BlockSpec((tm, LANES), lambda b, j: (j, 0))],
        out_specs=[pl.BlockSpec((1, tm, w), row) for w in widths],
        out_shape=[jax.ShapeDtypeStruct((bsz, s, w), t) for w, t in zip(widths, dtypes)],
        compiler_params=_params("parallel", "parallel"),
        name="inproj",
    )(xs, mod_l, mod_l, g, w_cat, cos2, sin2)


def _half_masks():
    lane = lax.broadcasted_iota(jnp.int32, (1, LANES), 1)
    return lane < HEAD_DIM, lane >= HEAD_DIM


def _na_kernel(q_ref, k_ref, v_ref, bias_ref, o_ref, *, ctx_len, rows):
    i = pl.program_id(1)
    tq = NA_QROWS * GRID_W
    n_ctx_q = ctx_len // tq
    masks = _half_masks()
    nk = NA_KROWS * GRID_W

    def run(local):
        q = q_ref[0]
        if local:
            r0 = (i - n_ctx_q) * NA_QROWS
            start0 = jnp.clip(r0 - NA_ROWS // 2, 0, rows - NA_KROWS)
            start = pl.multiple_of(ctx_len + start0 * GRID_W, GRID_W)
            tab_idx, row_mask = {}, {}
            for a in range(NA_QROWS):
                r = r0 + a
                s_r = jnp.clip(r - NA_ROWS // 2, 0, rows - NA_ROWS)
                for p in range(NA_KROWS // 2):
                    kr = start0 + 2 * p
                    ok_lo = (kr >= s_r) & (kr < s_r + NA_ROWS)
                    ok_hi = (kr + 1 >= s_r) & (kr + 1 < s_r + NA_ROWS)
                    tab_idx[a, p] = jnp.clip(kr - r + NA_ROWS, 0, 2 * NA_ROWS - 1)
                    row_mask[a, p] = jnp.where(masks[0], jnp.where(ok_lo, 0.0, NEG_INF),
                                               jnp.where(ok_hi, 0.0, NEG_INF))
        outs = []
        for t in range(NA_W // LANES):
            sl = slice(LANES * t, LANES * (t + 1))
            qt = q[:, sl]
            zero = jnp.zeros_like(qt)
            qm = jnp.concatenate([jnp.where(masks[0], qt, zero), jnp.where(masks[1], qt, zero)], axis=0)
            kc = k_ref[0, 0:ctx_len, sl]
            vc = v_ref[0, 0:ctx_len, sl]
            s_cx = _dot_nt(qm, kc)
            m = jnp.max(s_cx, axis=-1, keepdims=True)
            if local:
                kw = k_ref[0, pl.ds(start, nk), sl]
                vw = v_ref[0, pl.ds(start, nk), sl]
                bias = jnp.concatenate(
                    [jnp.concatenate([bias_ref[2 * t + hh, pl.ds(tab_idx[a, p], 1)][0] + row_mask[a, p]
                                      for p in range(NA_KROWS // 2)], axis=-1)
                     for hh in range(2) for a in range(NA_QROWS)], axis=0)
                s_nb = _dot_nt(qm, kw) + bias
                m = jnp.maximum(m, jnp.max(s_nb, axis=-1, keepdims=True))
                p_nb = jnp.exp(s_nb - m)
            p_cx = jnp.exp(s_cx - m)
            den = jnp.sum(p_cx, axis=-1, keepdims=True)
            o = _dot(p_cx.astype(MXU_DTYPE), vc)
            if local:
                den = den + jnp.sum(p_nb, axis=-1, keepdims=True)
                o = o + _dot(p_nb.astype(MXU_DTYPE), vw)
            o = o / den
            outs.append(jnp.where(masks[0], o[:tq], o[tq:]))
        o_ref[0] = jnp.concatenate(outs, axis=-1).astype(o_ref.dtype)

    @pl.when(i < n_ctx_q)
    def _():
        run(False)

    @pl.when(i >= n_ctx_q)
    def _():
        run(True)


def _na_call(q, k, v, bias_tab, ctx_len):
    bsz, s, w = q.shape
    rows = (s - ctx_len) // GRID_W
    tq = NA_QROWS * GRID_W
    assert rows >= NA_KROWS and rows % NA_QROWS == 0 and ctx_len % tq == 0
    whole = lambda b, i: (b, 0, 0)
    return pl.pallas_call(
        functools.partial(_na_kernel, ctx_len=ctx_len, rows=rows),
        grid=(bsz, s // tq),
        in_specs=[pl.BlockSpec((1, tq, w), lambda b, i: (b, i, 0)),
                  pl.BlockSpec((1, s, w), whole),
                  pl.BlockSpec((1, s, w), whole),
                  pl.BlockSpec(bias_tab.shape, lambda b, i: (0, 0, 0, 0))],
        out_specs=pl.BlockSpec((1, tq, w), lambda b, i: (b, i, 0)),
        out_shape=jax.ShapeDtypeStruct((bsz, s, w), q.dtype),
        compiler_params=_params("parallel", "arbitrary"),
        name="na_attn",
    )(q, k, v, bias_tab)


def _na_bias_table(rpb):
    qcol = np.arange(GRID_W)[:, None]
    kcol = np.arange(GRID_W)[None, :]
    ws = np.clip(qcol - NA_COLS // 2, 0, GRID_W - NA_COLS)
    valid = (kcol >= ws) & (kcol < ws + NA_COLS)
    dc = np.clip(kcol - qcol + NA_COLS - 1, 0, 2 * NA_COLS - 2)
    full = jnp.where(valid[None, None], rpb[:, :, dc].astype(F32), NEG_INF)
    edge = jnp.full_like(full[:, :1], NEG_INF)
    full = jnp.concatenate([edge, full, edge], axis=1)
    return jnp.concatenate([full[:, :-1], full[:, 1:]], axis=-1)


def _sw_kernel(sink_ref, q_ref, k_ref, v_ref, o_ref, *, ctx_len, seq):
    i = pl.program_id(1)
    tq = SW_QBLK
    n_ctx_q = ctx_len // tq
    masks = _half_masks()
    nk = SW_KBLK
    first_head = lax.broadcasted_iota(jnp.int32, (2 * tq, 1), 0) < tq

    def run(local):
        q = q_ref[0]
        kc = k_ref[0, 0:ctx_len, :]
        vc = v_ref[0, 0:ctx_len, :]
        if local:
            n = i - n_ctx_q
            start_lat = jnp.clip(n * tq - SW_WINDOW, 0, seq - nk)
            start = pl.multiple_of(ctx_len + start_lat, SW_BLK)
            kw = k_ref[0, pl.ds(start, nk), :]
            vw = v_ref[0, pl.ds(start, nk), :]
            row = lax.broadcasted_iota(jnp.int32, (2 * tq, 1), 0)
            qpos = n * tq + jnp.where(first_head, row, row - tq)
            kpos = start_lat + lax.broadcasted_iota(jnp.int32, (1, nk), 1)
            valid = jnp.abs(qpos - kpos) <= SW_WINDOW
        outs = []
        for t in range(SW_W // LANES):
            qt = q[:, LANES * t:LANES * (t + 1)]
            zero = jnp.zeros_like(qt)
            qm = jnp.concatenate([jnp.where(masks[0], qt, zero), jnp.where(masks[1], qt, zero)], axis=0)
            sink = jnp.where(first_head, sink_ref[SW_HEAD_ORDER[2 * t]], sink_ref[SW_HEAD_ORDER[2 * t + 1]])
            s_cx = _dot_nt(qm, kc)
            m = jnp.maximum(jnp.max(s_cx, axis=-1, keepdims=True), sink)
            if local:
                s_loc = jnp.where(valid, _dot_nt(qm, kw), NEG_INF)
                m = jnp.maximum(m, jnp.max(s_loc, axis=-1, keepdims=True))
                p_loc = jnp.exp(s_loc - m)
            p_cx = jnp.exp(s_cx - m)
            den = jnp.sum(p_cx, axis=-1, keepdims=True) + jnp.exp(sink - m)
            o = _dot(p_cx.astype(MXU_DTYPE), vc)
            if local:
                den = den + jnp.sum(p_loc, axis=-1, keepdims=True)
                o = o + _dot(p_loc.astype(MXU_DTYPE), vw)
            o = o / den
            outs.append(jnp.where(masks[0], o[:tq], o[tq:]))
        o_ref[0] = jnp.concatenate(outs, axis=-1).astype(o_ref.dtype)

    @pl.when(i < n_ctx_q)
    def _():
        run(False)

    @pl.when(i >= n_ctx_q)
    def _():
        run(True)


def _sw_call(sinks, q, k, v, ctx_len):
    bsz, s, w = q.shape
    seq = s - ctx_len
    assert seq >= SW_KBLK and seq % SW_QBLK == 0 and ctx_len % SW_QBLK == 0
    whole = lambda b, i: (b, 0, 0)
    return pl.pallas_call(
        functools.partial(_sw_kernel, ctx_len=ctx_len, seq=seq),
        grid=(bsz, s // SW_QBLK),
        in_specs=[pl.BlockSpec(memory_space=pltpu.SMEM),
                  pl.BlockSpec((1, SW_QBLK, w), lambda b, i: (b, i, 0)),
                  pl.BlockSpec((1, s, SW_KV_W), whole),
                  pl.BlockSpec((1, s, SW_KV_W), whole)],
        out_specs=pl.BlockSpec((1, SW_QBLK, w), lambda b, i: (b, i, 0)),
        out_shape=jax.ShapeDtypeStruct((bsz, s, w), q.dtype),
        compiler_params=_params("parallel", "arbitrary"),
        name="sw_attn",
    )(sinks.astype(F32), q, k, v)


def _s5_kernel(u_ref, m_ref, wsr_ref, wsi_ref, wor_ref, woi_ref, lr_ref, li_ref, o_ref,
               sre, sim, xre, xim, acc, *, n_ctx_chunks):
    d = pl.program_id(1)
    bsz, nc, _ = u_ref.shape
    for b in range(bsz):
        ub = u_ref[b]
        sre[b] = _dot(ub, wsr_ref[0, 0])
        sim[b] = _dot(ub, wsi_ref[0, 0])
    lr = lr_ref[0, 0]
    li = li_ref[0, 0]

    def step(c, carry):
        new = []
        for b in range(bsz):
            xr, xi = carry[2 * b], carry[2 * b + 1]
            xre[b, pl.ds(c, 1), :] = xr
            xim[b, pl.ds(c, 1), :] = xi
            sr = sre[b, pl.ds(c, 1), :]
            si = sim[b, pl.ds(c, 1), :]
            new.append(lr * xr - li * xi + sr)
            new.append(lr * xi + li * xr + si)
        return tuple(new)

    zero = tuple(jnp.zeros((1, LANES), F32) for _ in range(2 * bsz))

    @pl.when(d == 0)
    def _():
        lax.fori_loop(0, nc, step, zero)

    @pl.when(d == 1)
    def _():
        carry = lax.fori_loop(0, n_ctx_chunks, lambda k, cr: step(n_ctx_chunks - 1 - k, cr), zero)
        lax.fori_loop(0, nc - n_ctx_chunks, lambda k, cr: step(nc - 1 - k, cr), carry)

    half = u_ref.shape[2] // 2
    for b in range(bsz):
        ub = u_ref[b]
        y_intra = jnp.concatenate([_dot(ub[:, :half], m_ref[0, 0]), _dot(ub[:, half:], m_ref[0, 1])], axis=-1)
        y = (y_intra + _dot(xre[b].astype(MXU_DTYPE), wor_ref[0, 0])
             + _dot(xim[b].astype(MXU_DTYPE), woi_ref[0, 0]))

        @pl.when(d == 0)
        def _():
            acc[b] = y

        @pl.when(d == 1)
        def _():
            o_ref[b] = (acc[b] + y).astype(o_ref.dtype)


def _s5_call(u_t, wts, n_ctx_chunks):
    m, wsr, wsi, wor, woi, lr, li = wts
    bsz, nc, width = u_t.shape
    pw = 2 * S5_CHUNK * S5_GROUP_CH
    n_pairs = width // pw
    blk = lambda shp: pl.BlockSpec((1, 1) + shp, lambda j, d: (d, j, 0, 0))
    return pl.pallas_call(
        functools.partial(_s5_kernel, n_ctx_chunks=n_ctx_chunks),
        grid=(n_pairs, 2),
        in_specs=[pl.BlockSpec((bsz, nc, pw), lambda j, d: (0, 0, j)),
                  pl.BlockSpec((1, 2, pw // 2, pw // 2), lambda j, d: (d, j, 0, 0)),
                  blk((pw, LANES)), blk((pw, LANES)), blk((LANES, pw)), blk((LANES, pw)),
                  blk((1, LANES)), blk((1, LANES))],
        out_specs=pl.BlockSpec((bsz, nc, pw), lambda j, d: (0, 0, j)),
        out_shape=jax.ShapeDtypeStruct((bsz, nc, width), MXU_DTYPE),
        scratch_shapes=[pltpu.VMEM((bsz, nc, LANES), F32) for _ in range(4)] + [pltpu.VMEM((bsz, nc, pw), F32)],
        compiler_params=_params("parallel", "arbitrary"),
        name="s5_scan",
    )(u_t, m, wsr, wsi, wor, woi, lr, li)


def _s5_weights(a_re, a_im, log_step, b_re, b_im, c_re, c_im):
    lc, g, p, h = S5_CHUNK, S5_GROUPS, S5_STATE, S5_GROUP_CH
    lam = lax.complex(jnp.minimum(a_re.astype(F32), S5_EIG_MAX), a_im.astype(F32))
    step = jnp.exp(log_step.astype(F32))[..., None]
    lam_bar = jnp.exp(lam * step)
    b_bar = ((lam_bar - 1.0) / lam)[..., None] * lax.complex(b_re.astype(F32), b_im.astype(F32))
    cc = lax.complex(c_re.astype(F32), c_im.astype(F32))
    dd = jnp.arange(lc + 1, dtype=F32)
    pw = jnp.exp((lam * step)[..., None] * dd)
    kern = jnp.real(jnp.einsum('zgop,zgpd,zgpi->zgdoi', cc, pw[..., :lc], b_bar))
    jj = np.arange(lc)[:, None]
    ii = np.arange(lc)[None, :]
    mats, wst, wout = [], [], []
    for z in range(2):
        lag = (ii - jj) if z == 0 else (jj - ii)
        ok = lag >= 0
        kz = kern[z][:, np.where(ok, lag, 0)]
        kz = jnp.where(ok[None, :, :, None, None], kz, 0.0)
        mats.append(kz.transpose(0, 1, 4, 2, 3).reshape(g, lc * h, lc * h))
        d_state = (lc - 1 - np.arange(lc)) if z == 0 else np.arange(lc)
        ws = pw[z][:, :, d_state][..., None] * b_bar[z][:, :, None, :]
        wst.append(ws.transpose(0, 2, 3, 1).reshape(g, lc * h, p))
        d_out = (np.arange(lc) + 1) if z == 0 else (lc - np.arange(lc))
        wo = cc[z][:, :, :, None] * pw[z][:, None, :, :][..., d_out]
        wout.append(wo.transpose(0, 2, 3, 1).reshape(g, p, lc * h))
    mats = jnp.stack(mats)
    wst = jnp.stack(wst)
    wout = jnp.stack(wout)

    def pair_rows(w):
        w = w.reshape(2, g // 2, 2, lc * h, p)
        z0 = jnp.zeros_like(w[:, :, 0])
        top = jnp.concatenate([w[:, :, 0], z0], axis=-1)
        bot = jnp.concatenate([z0, w[:, :, 1]], axis=-1)
        return jnp.concatenate([top, bot], axis=-2)

    def pair_cols(w):
        w = w.reshape(2, g // 2, 2, p, lc * h)
        z0 = jnp.zeros_like(w[:, :, 0])
        top = jnp.concatenate([w[:, :, 0], z0], axis=-1)
        bot = jnp.concatenate([z0, w[:, :, 1]], axis=-1)
        return jnp.concatenate([top, bot], axis=-2)

    lam_c = pw[..., lc].reshape(2, g // 2, 1, 2 * p)
    cast = lambda w: w.astype(MXU_DTYPE)
    return (cast(mats), cast(pair_rows(jnp.real(wst))), cast(pair_rows(jnp.imag(wst))),
            cast(pair_cols(jnp.real(wout))), cast(pair_cols(-jnp.imag(wout))),
            jnp.real(lam_c), jnp.imag(lam_c))


def _gelu_tanh(x):
    cdf = 0.5 * (1.0 + jnp.tanh(math.sqrt(2.0 / math.pi) * (x + 0.044715 * (x * x * x))))
    return x * cdf


def _outproj_kernel(x_ref, ya_ref, ys_ref, su_ref, yc_ref, modb_ref, modc_ref, woa_ref, wob_ref, woc_ref,
                    wglu_ref, bglu_ref, dsk_ref, g_ref, wr_ref, br_ref,
                    xo_ref, h_ref, topi_ref, topw_ref, *, ctx_len):
    gate, shift, scale = _mod_vectors(modb_ref, modc_ref, ctx_len, (2, 3, 4))
    y = dsk_ref[...] * su_ref[0] + ys_ref[0].astype(F32)
    gl = _gelu_tanh(y)
    yb = gl * jax.nn.sigmoid(_dot(gl.astype(MXU_DTYPE), wglu_ref[...]) + bglu_ref[...])
    mix = (_dot(ya_ref[0], woa_ref[...]) + _dot(yb.astype(MXU_DTYPE), wob_ref[...])
           + _dot(yc_ref[0], woc_ref[...]))
    x = x_ref[0] + gate * mix
    xo_ref[0] = x
    h = _rms_mod(x, g_ref[...], shift, scale)
    dl = h_ref.shape[-1]
    for sl in range(SUBLANES):
        h_ref[0, :, sl, :] = h[:, dl * sl:dl * (sl + 1)]
    logits = _dot3(wr_ref[...], h, nt=True) + br_ref[...]
    n_e, tm = logits.shape
    e_iota = lax.broadcasted_iota(jnp.int32, (n_e, tm), 0)
    vals, idxs = [], []
    for _ in range(TOP_K):
        mx = jnp.max(logits, axis=0, keepdims=True)
        ix = jnp.min(jnp.where(logits == mx, e_iota, n_e), axis=0, keepdims=True)
        vals.append(mx)
        idxs.append(ix)
        logits = jnp.where(e_iota == ix, -jnp.inf, logits)
    ex = [jnp.exp(v - vals[0]) for v in vals]
    den = ex[0] + ex[1] + ex[2] + ex[3]
    topi_ref[0] = jnp.concatenate(idxs, axis=0)
    w8 = jnp.concatenate([e / den for e in ex] + [jnp.zeros((MOD_ROWS - TOP_K, tm), F32)], axis=0)
    topw_ref[0] = w8.T


def _outproj_call(xs, ya, ys, su, yc, mod_l, wts, ctx_len):
    bsz, s, d = xs.shape
    tm = ROW_TILE
    row = lambda b, j: (b, j, 0)
    const = lambda b, j: (0, 0)
    full = lambda a: pl.BlockSpec(a.shape, const)
    return pl.pallas_call(
        functools.partial(_outproj_kernel, ctx_len=ctx_len),
        grid=(bsz, s // tm),
        in_specs=[pl.BlockSpec((1, tm, d), row),
                  pl.BlockSpec((1, tm, NA_W), row),
                  pl.BlockSpec((1, tm, S5_CH), row),
                  pl.BlockSpec((1, tm, S5_CH), row),
                  pl.BlockSpec((1, tm, SW_W), row),
                  pl.BlockSpec((1, 6, d), lambda b, j: (b, 0, 0)),
                  pl.BlockSpec((1, 6, d), lambda b, j: (bsz, 0, 0))] + [full(a) for a in wts],
        out_specs=[pl.BlockSpec((1, tm, d), row),
                   pl.BlockSpec((1, tm, SUBLANES, d // SUBLANES), lambda b, j: (b, j, 0, 0)),
                   pl.BlockSpec((1, TOP_K, tm), lambda b, j: (b, 0, j)),
                   pl.BlockSpec((1, tm, MOD_ROWS), row)],
        out_shape=[jax.ShapeDtypeStruct((bsz, s, d), F32),
                   jax.ShapeDtypeStruct((bsz, s, SUBLANES, d // SUBLANES), F32),
                   jax.ShapeDtypeStruct((bsz, TOP_K, s), jnp.int32),
                   jax.ShapeDtypeStruct((bsz, s, MOD_ROWS), F32)],
        compiler_params=_params("parallel", "parallel"),
        name="outproj",
    )(xs, ya, ys, su, yc, mod_l, mod_l, *wts)


def _route_kernel(topi_ref, dest_ref, blk_ref, ends_ref, carry, ranks, *, n_tiles):
    phase = pl.program_id(0)
    t = pl.program_id(1) * n_tiles + pl.program_id(2)
    idx = topi_ref[0]
    tm = idx.shape[1]
    e_iota = lax.broadcasted_iota(jnp.int32, (N_EXPERTS, tm), 0)
    sel = [idx[k:k + 1] == e_iota for k in range(TOP_K)]

    @pl.when(phase == 0)
    def _():
        @pl.when(t == 0)
        def _():
            carry[...] = jnp.zeros_like(carry)

        chosen = sel[0] | sel[1] | sel[2] | sel[3]
        onehot = jnp.where(chosen, 1.0, 0.0)
        before = (lax.broadcasted_iota(jnp.int32, (tm, tm), 0) < lax.broadcasted_iota(jnp.int32, (tm, tm), 1))
        pfx = _dot(onehot.astype(MXU_DTYPE), jnp.where(before, 1.0, 0.0).astype(MXU_DTYPE)) + carry[:, 0:1]
        ranks[t] = jnp.concatenate(
            [jnp.sum(jnp.where(sel[k], pfx, 0.0), axis=0, keepdims=True) for k in range(TOP_K)], axis=0)
        carry[...] = carry[...] + jnp.sum(onehot, axis=1, keepdims=True)

    @pl.when(phase == 1)
    def _():
        counts = carry[:, 0:1]
        padded = jnp.ceil(counts * (1.0 / MOE_BLK)) * MOE_BLK
        r_i = lax.broadcasted_iota(jnp.int32, (N_EXPERTS, N_EXPERTS), 0)
        c_i = lax.broadcasted_iota(jnp.int32, (N_EXPERTS, N_EXPERTS), 1)
        padded_row = jnp.sum(jnp.where(r_i == c_i, padded, 0.0), axis=0, keepdims=True)
        pstart = jnp.sum(jnp.where(c_i < r_i, padded_row, 0.0), axis=1, keepdims=True)
        ends = pstart + padded
        rk = ranks[t]
        dest_ref[0] = jnp.concatenate(
            [jnp.sum(jnp.where(sel[k], pstart, 0.0), axis=0, keepdims=True) + rk[k:k + 1] for k in range(TOP_K)],
            axis=1).astype(jnp.int32)
        nb = blk_ref.shape[1]
        blk_start = (lax.broadcasted_iota(jnp.int32, (N_EXPERTS, nb), 1) * MOE_BLK).astype(F32)
        owner = jnp.sum(jnp.where(ends <= blk_start, 1.0, 0.0), axis=0, keepdims=True)
        blk_ref[...] = jnp.minimum(owner, N_EXPERTS - 1.0).astype(jnp.int32)
        ends_ref[...] = jnp.broadcast_to(ends, ends_ref.shape).astype(jnp.int32)


def _route_call(topi, n_blocks):
    bsz, _, s = topi.shape
    tm = MOE_TILE
    n_tiles = s // tm
    nb_pad = -(-n_blocks // LANES) * LANES
    return pl.pallas_call(
        functools.partial(_route_kernel, n_tiles=n_tiles),
        grid=(2, bsz, n_tiles),
        in_specs=[pl.BlockSpec((1, TOP_K, tm), lambda p, b, j: (b, 0, j))],
        out_specs=[pl.BlockSpec((1, 1, TOP_K * tm), lambda p, b, j: ((b * n_tiles + j) * p, 0, 0)),
                   pl.BlockSpec((1, nb_pad), lambda p, b, j: (0, 0)),
                   pl.BlockSpec((N_EXPERTS, LANES), lambda p, b, j: (0, 0))],
        out_shape=[jax.ShapeDtypeStruct((bsz * n_tiles, 1, TOP_K * tm), jnp.int32),
                   jax.ShapeDtypeStruct((1, nb_pad), jnp.int32),
                   jax.ShapeDtypeStruct((N_EXPERTS, LANES), jnp.int32)],
        scratch_shapes=[pltpu.VMEM((N_EXPERTS, LANES), F32),
                        pltpu.VMEM((bsz * n_tiles, TOP_K, tm), F32)],
        compiler_params=_params("arbitrary", "arbitrary", "arbitrary"),
        name="route_rank",
    )(topi)


def _dispatch_kernel(ends_ref, dest_ref, h_ref, xg_ref, zbuf, stage, sem, zsem, *, n_blocks):
    tm = h_ref.shape[1]

    @pl.when((pl.program_id(0) == 0) & (pl.program_id(1) == 0))
    def _():
        zbuf[...] = jnp.zeros_like(zbuf)

        def fill(row):
            return pltpu.make_async_copy(zbuf, xg_ref.at[pl.ds(pl.multiple_of(row, MOE_BLK), MOE_BLK)], zsem)

        def each(fn):
            for e in range(N_EXPERTS):
                begin = ends_ref[e - 1] if e else 0

                @pl.when(ends_ref[e] > begin)
                def _():
                    fn(fill(ends_ref[e] - MOE_BLK))

            def dead(i, c):
                fn(fill(i * MOE_BLK))
                return c

            lax.fori_loop(ends_ref[N_EXPERTS - 1] // MOE_BLK, n_blocks, dead, 0)

        each(lambda cp: cp.start())
        each(lambda cp: cp.wait())

    step = pl.program_id(0) * pl.num_programs(1) + pl.program_id(1)
    n_steps = pl.num_programs(0) * pl.num_programs(1)
    slot = lax.rem(step, 2)
    stage[slot] = h_ref[0]

    def body(t, c):
        for k in range(TOP_K):
            pltpu.make_async_copy(stage.at[slot, t], xg_ref.at[dest_ref[0, 0, k * tm + t]],
                                  sem.at[slot]).start(priority=k % 2)
        return c

    lax.fori_loop(0, tm, body, 0, unroll=8)

    def wait_tile(sl):
        pltpu.make_async_copy(xg_ref.at[pl.ds(0, TOP_K * tm)], xg_ref.at[pl.ds(0, TOP_K * tm)], sem.at[sl]).wait()

    @pl.when(step > 0)
    def _():
        wait_tile(1 - slot)

    @pl.when(step == n_steps - 1)
    def _():
        wait_tile(slot)


def _dispatch_call(ends, dest, h, n_blocks):
    bsz, s, _, dl = h.shape
    tm = MOE_TILE
    n_tiles = s // tm
    grid_spec = pltpu.PrefetchScalarGridSpec(
        num_scalar_prefetch=1,
        grid=(bsz, n_tiles),
        in_specs=[pl.BlockSpec((1, 1, TOP_K * tm), lambda b, j, en: (b * n_tiles + j, 0, 0), memory_space=pltpu.SMEM),
                  pl.BlockSpec((1, tm, SUBLANES, dl), lambda b, j, en: (b, j, 0, 0))],
        out_specs=pl.BlockSpec(memory_space=pl.ANY),
        scratch_shapes=[pltpu.VMEM((MOE_BLK, SUBLANES, dl), h.dtype), pltpu.VMEM((2, tm, SUBLANES, dl), h.dtype),
                        pltpu.SemaphoreType.DMA((2,)),
                        pltpu.SemaphoreType.DMA(())],
    )
    return pl.pallas_call(
        functools.partial(_dispatch_kernel, n_blocks=n_blocks),
        grid_spec=grid_spec,
        out_shape=jax.ShapeDtypeStruct((n_blocks * MOE_BLK, SUBLANES, dl), h.dtype),
        compiler_params=_params("arbitrary", "arbitrary"),
        name="moe_dispatch",
    )(ends, dest, h)


def _moe_kernel(blk_exp_ref, nact_ref, x_hbm, wgu_ref, bgu_ref, wd_ref, bd_ref, y_ref, wgu_c, wd_c, xbuf, xsem):
    i = pl.program_id(0)
    slot = lax.rem(i, 2)
    dl = x_hbm.shape[-1]

    def fetch(blk, sl_):
        for s in range(SUBLANES):
            pltpu.make_async_copy(x_hbm.at[pl.ds(blk * MOE_BLK, MOE_BLK), s, :],
                                  xbuf.at[sl_, :, pl.ds(dl * s, dl)], xsem.at[sl_]).start()

    @pl.when(i == 0)
    def _():
        fetch(0, 0)

    @pl.when(i + 1 < pl.num_programs(0))
    def _():
        fetch(i + 1, 1 - slot)

    e = blk_exp_ref[i]
    prev = blk_exp_ref[jnp.maximum(i - 1, 0)]
    d, f2 = wgu_c.shape
    f = f2 // 2
    rows = 128

    @pl.when((i == 0) | (e != prev))
    def _():
        def cv(r, c):
            sl = pl.ds(pl.multiple_of(r * rows, rows), rows)
            wgu_c[sl, :] = wgu_ref[0, 0, sl, :].astype(wgu_c.dtype)
            return c
        lax.fori_loop(0, d // rows, cv, 0)

        def cv2(r, c):
            sl = pl.ds(pl.multiple_of(r * rows, rows), rows)
            wd_c[sl, :] = wd_ref[0, 0, sl, :].astype(wd_c.dtype)
            return c
        lax.fori_loop(0, f // rows, cv2, 0)

    pltpu.make_async_copy(xbuf.at[slot], xbuf.at[slot], xsem.at[slot]).wait()

    @pl.when(i < nact_ref[0])
    def _():
        gu = _dot(xbuf[slot].astype(MXU_DTYPE), wgu_c[...]) + bgu_ref[0, 0]
        gate = jnp.minimum(gu[:, :f], SWIGLU_LIMIT)
        up = jnp.clip(gu[:, f:], -SWIGLU_LIMIT, SWIGLU_LIMIT)
        act = gate * jax.nn.sigmoid(SWIGLU_ALPHA * gate) * (up + 1.0)
        y_ref[...] = _dot(act.astype(MXU_DTYPE), wd_c[...]) + bd_ref[0, 0]

    @pl.when(i >= nact_ref[0])
    def _():
        y_ref[...] = jnp.zeros_like(y_ref)


def _moe_call(layer, blk_exp, n_active, xg, w_gate_up, b_gate_up, w_down, b_down):
    n_rows, _, dl = xg.shape
    depth, n_e, d, f2 = w_gate_up.shape
    f = f2 // 2
    n_blocks = n_rows // MOE_BLK
    wsel = lambda i, be, na: (layer, be[i], 0, 0)
    grid_spec = pltpu.PrefetchScalarGridSpec(
        num_scalar_prefetch=2,
        grid=(n_blocks,),
        in_specs=[pl.BlockSpec(memory_space=pl.ANY),
                  pl.BlockSpec((1, 1, d, f2), wsel),
                  pl.BlockSpec((1, 1, 1, f2), wsel),
                  pl.BlockSpec((1, 1, f, d), wsel),
                  pl.BlockSpec((1, 1, 1, d), wsel)],
        out_specs=pl.BlockSpec((MOE_BLK, d), lambda i, be, na: (i, 0)),
        scratch_shapes=[pltpu.VMEM((d, f2), MXU_DTYPE), pltpu.VMEM((f, d), MXU_DTYPE),
                        pltpu.VMEM((2, MOE_BLK, d), F32), pltpu.SemaphoreType.DMA((2,))],
    )
    return pl.pallas_call(
        _moe_kernel,
        grid_spec=grid_spec,
        out_shape=jax.ShapeDtypeStruct((n_rows, d), F32),
        compiler_params=_params("arbitrary"),
        name="moe_experts",
    )(blk_exp, n_active, xg, w_gate_up, b_gate_up.reshape(depth, n_e, 1, f2),
      w_down, b_down.reshape(depth, n_e, 1, d))


def _combine_kernel(dest_ref, dnext_ref, w_ref, x_ref, mod_ref, yg_ref, o_ref, gbuf, sem):
    tm = x_ref.shape[1]
    step = pl.program_id(0) * pl.num_programs(1) + pl.program_id(1)
    n_steps = pl.num_programs(0) * pl.num_programs(1)
    slot = lax.rem(step, 2)

    def gather(dref, sl):
        def body(t, c):
            for k in range(TOP_K):
                pltpu.make_async_copy(yg_ref.at[pl.ds(dref[0, 0, k * tm + t], 1)], gbuf.at[sl, k, pl.ds(t, 1)],
                                      sem.at[sl]).start(priority=k % 2)
            return c

        lax.fori_loop(0, tm, body, 0, unroll=8)

    @pl.when(step == 0)
    def _():
        gather(dest_ref, 0)

    @pl.when(step + 1 < n_steps)
    def _():
        gather(dnext_ref, 1 - slot)

    pltpu.make_async_copy(gbuf.at[slot], gbuf.at[slot], sem.at[slot]).wait()
    w = w_ref[0]
    f = gbuf[slot, 0] * w[:, 0:1]
    for k in range(1, TOP_K):
        f = f + gbuf[slot, k] * w[:, k:k + 1]
    o_ref[0] = x_ref[0] + mod_ref[0][5:6] * f


def _combine_call(dest, topw, xs, mod_l, yg, ctx_len):
    bsz, s, d = xs.shape
    tm = MOE_TILE
    n_tiles = s // tm
    last = bsz * n_tiles - 1
    row = lambda b, j: (b, j, 0)
    idx_spec = lambda ahead: pl.BlockSpec(
        (1, 1, TOP_K * tm), lambda b, j: (jnp.minimum(b * n_tiles + j + ahead, last), 0, 0), memory_space=pltpu.SMEM)
    return pl.pallas_call(
        _combine_kernel,
        grid=(bsz, n_tiles),
        in_specs=[idx_spec(0), idx_spec(1),
                  pl.BlockSpec((1, tm, MOD_ROWS), row),
                  pl.BlockSpec((1, tm, d), row),
                  pl.BlockSpec((1, 6, d), _mod_index(ctx_len // tm, bsz)),
                  pl.BlockSpec(memory_space=pl.ANY)],
        out_specs=pl.BlockSpec((1, tm, d), row),
        out_shape=jax.ShapeDtypeStruct((bsz, s, d), F32),
        scratch_shapes=[pltpu.VMEM((2, TOP_K, tm, d), F32), pltpu.SemaphoreType.DMA((2,))],
        compiler_params=_params("arbitrary", "arbitrary"),
        name="moe_combine",
    )(dest, dest, topw, xs, mod_l, yg)


def _final_kernel(x_ref, g_ref, o_ref):
    x = x_ref[0]
    o_ref[0] = x * lax.rsqrt(jnp.mean(x * x, axis=-1, keepdims=True) + RMS_EPS) * g_ref[...]


def _final_call(xs, g, ctx_len):
    bsz, s, d = xs.shape
    tm = MOE_TILE
    off = ctx_len // tm
    return pl.pallas_call(
        _final_kernel,
        grid=(bsz, (s - ctx_len) // tm),
        in_specs=[pl.BlockSpec((1, tm, d), lambda b, j: (b, j + off, 0)),
                  pl.BlockSpec((1, d), lambda b, j: (0, 0))],
        out_specs=pl.BlockSpec((1, tm, d), lambda b, j: (b, j, 0)),
        out_shape=jax.ShapeDtypeStruct((bsz, s - ctx_len, d), F32),
        compiler_params=_params("parallel", "parallel"),
        name="final_norm",
    )(xs, g)


def _rope_tables(seq, ctx_len):
    t = jnp.arange(seq)
    row = (t // GRID_W).astype(F32)
    col = (t % GRID_W).astype(F32)
    nf = HEAD_DIM // 4
    inv = ROPE_BASE ** (-jnp.arange(nf, dtype=F32) / nf)
    ar = row[:, None] * inv
    ac = col[:, None] * inv
    ang = jnp.concatenate([ar, ar, ac, ac], axis=-1)
    cos = jnp.concatenate([jnp.ones((ctx_len, HEAD_DIM), F32), jnp.cos(ang)], axis=0)
    sin = jnp.concatenate([jnp.zeros((ctx_len, HEAD_DIM), F32), jnp.sin(ang)], axis=0)
    reps = LANES // HEAD_DIM
    return jnp.tile(cos, (1, reps)), jnp.tile(sin, (1, reps))


def _rot_cols(w):
    q = HEAD_DIM // 4
    j = np.arange(HEAD_DIM)
    first = (j % (2 * q)) < q
    src = np.where(first, j + q, j - q)
    sign = np.where(first, -1.0, 1.0).astype(np.float32)
    n_heads = w.shape[1] // HEAD_DIM
    src_all = (np.arange(n_heads)[:, None] * HEAD_DIM + src[None, :]).reshape(-1)
    return w[:, src_all] * jnp.asarray(np.tile(sign, n_heads))


def _head_perm_cols(order):
    return (np.asarray(order)[:, None] * HEAD_DIM + np.arange(HEAD_DIM)[None, :]).reshape(-1)


def _inproj_weight(w_in_l):
    aq, ak, av, su, sq, sk, sv = jnp.split(
        w_in_l, np.cumsum([NA_W, NA_W, NA_W, S5_CH, SW_W, SW_KV_W])[:6].tolist(), axis=1)
    sq = sq[:, _head_perm_cols(SW_HEAD_ORDER)]
    return jnp.concatenate([aq, ak, av, sq, _rot_cols(sq), sk, _rot_cols(sk), sv, su], axis=1).astype(MXU_DTYPE)


def _chunk_major(su):
    bsz, s, _ = su.shape
    u = su.reshape(bsz, s // S5_CHUNK, S5_CHUNK, S5_GROUPS, S5_GROUP_CH).transpose(0, 1, 3, 2, 4)
    return u.reshape(bsz, s // S5_CHUNK, S5_GROUPS * S5_CHUNK * S5_GROUP_CH)


def _token_major(y_t):
    bsz, nc, _ = y_t.shape
    y = y_t.reshape(bsz, nc, S5_GROUPS, S5_CHUNK, S5_GROUP_CH).transpose(0, 1, 3, 2, 4)
    return y.reshape(bsz, nc * S5_CHUNK, S5_CH)


def kernel(x, c, ctx, c_ctx, w_mod, b_mod, g_mix, w_in, w_out, na_rpb, s5_a_re, s5_a_im, s5_log_step,
           s5_b_re, s5_b_im, s5_c_re, s5_c_im, s5_d, s5_w_glu, s5_b_glu, sw_sinks, g_ffn, w_router, b_router,
           w_gate_up, b_gate_up, w_down, b_down, g_final):
    bsz, seq, d = x.shape
    ctx_len = ctx.shape[1]
    depth = w_mod.shape[0]
    s = ctx_len + seq
    assert bsz + 1 <= MOD_ROWS and s % ROW_TILE == 0 and ctx_len % MOE_TILE == 0 and seq % MOE_TILE == 0
    assert seq % GRID_W == 0 and ctx_len % S5_CHUNK == 0

    xs = jnp.concatenate([ctx, x], axis=1)
    cond = jnp.zeros((MOD_ROWS, d), F32).at[:bsz].set(c).at[bsz].set(c_ctx)
    mod = _mod_call(cond, w_mod, b_mod).reshape(depth, MOD_ROWS, 6, d)
    cos2, sin2 = _rope_tables(seq, ctx_len)

    n_assign = bsz * s * TOP_K
    n_blocks = -(-(n_assign + N_EXPERTS * (MOE_BLK - 1)) // MOE_BLK)
    sw_rows = _head_perm_cols(SW_HEAD_ORDER)

    for l in range(depth):
        mod_l = mod[l]
        naq, nak, nav, swq, swk, swv, su = _inproj_call(
            xs, mod_l, g_mix[l].reshape(1, d), _inproj_weight(w_in[l]), cos2, sin2, ctx_len)
        ya = _na_call(naq, nak, nav, _na_bias_table(na_rpb[l]), ctx_len)
        yc = _sw_call(sw_sinks[l], swq, swk, swv, ctx_len)
        s5w = _s5_weights(s5_a_re[l], s5_a_im[l], s5_log_step[l], s5_b_re[l], s5_b_im[l], s5_c_re[l], s5_c_im[l])
        ys = _token_major(_s5_call(_chunk_major(su).astype(MXU_DTYPE), s5w, ctx_len // S5_CHUNK))

        wo = w_out[l]
        out_wts = (wo[:NA_W].astype(MXU_DTYPE),
                   wo[NA_W:NA_W + S5_CH].astype(MXU_DTYPE),
                   wo[NA_W + S5_CH:][sw_rows].astype(MXU_DTYPE),
                   s5_w_glu[l].astype(MXU_DTYPE), s5_b_glu[l].reshape(1, S5_CH).astype(F32),
                   s5_d[l].reshape(1, S5_CH).astype(F32), g_ffn[l].reshape(1, d),
                   w_router[l].T.astype(F32), b_router[l].reshape(N_EXPERTS, 1).astype(F32))
        xs, h, topi, topw = _outproj_call(xs, ya, ys, su, yc, mod_l, out_wts, ctx_len)

        dest, blk, ends = _route_call(topi, n_blocks)
        ends = ends[:, 0]
        xg = _dispatch_call(ends, dest, h, n_blocks)
        yg = _moe_call(l, blk[0, :n_blocks], ends[-1:] // MOE_BLK, xg, w_gate_up, b_gate_up, w_down, b_down)
        xs = _combine_call(dest, topw, xs, mod_l, yg, ctx_len)

    return _final_call(xs, g_final.reshape(1, d), ctx_len)
```

```python
import functools
import math

import numpy as np
import jax
import jax.numpy as jnp
from jax import lax
from jax.experimental import pallas as pl
from jax.experimental.pallas import tpu as pltpu

F32 = jnp.float32
MXU_DTYPE = jnp.bfloat16

GRID_W = 64
HEAD_DIM = 64
NA_HEADS = 6
NA_W = NA_HEADS * HEAD_DIM
NA_ROWS = 8
NA_COLS = 16
S5_GROUP_CH = 16
S5_CH = 256
S5_GROUPS = S5_CH // S5_GROUP_CH
S5_STATE = 64
S5_EIG_MAX = -1e-4
SW_HEADS = 6
SW_KV_HEADS = 2
SW_GRP = SW_HEADS // SW_KV_HEADS
SW_W = SW_HEADS * HEAD_DIM
SW_KV_W = SW_KV_HEADS * HEAD_DIM
SW_WINDOW = 128
SW_BLK = 128
ROPE_BASE = 10000.0
N_EXPERTS = 32
TOP_K = 4
MOE_BLK = 512
SWIGLU_LIMIT = 7.0
SWIGLU_ALPHA = 1.702
RMS_EPS = 1e-6
NEG_INF = -1e30

LANES = 128
ROW_TILE = 768
MOE_TILE = 256
S5_CHUNK = 16
NA_QROWS = 4
NA_KROWS = NA_QROWS + NA_ROWS
SW_QBLK = 2 * SW_BLK
SW_KBLK = SW_QBLK + 2 * SW_WINDOW
MOD_ROWS = 8
VMEM_LIMIT = 56 << 20
SUBLANES = 8

SW_HEAD_ORDER = tuple(g * SW_GRP + t for t in range(SW_GRP) for g in range(SW_KV_HEADS))


def _params(*sem):
    return pltpu.CompilerParams(dimension_semantics=sem, vmem_limit_bytes=VMEM_LIMIT)


def _dot(a, b):
    return jnp.dot(a, b, preferred_element_type=F32)


def _dot_nt(a, b):
    return lax.dot_general(a, b, (((1,), (1,)), ((), ())), preferred_element_type=F32)


def _split(a):
    hi = a.astype(MXU_DTYPE)
    lo = (a - hi.astype(F32)).astype(MXU_DTYPE)
    return hi, lo


def _dot3(a, b, nt=False):
    f = _dot_nt if nt else _dot
    ah, al = _split(a)
    bh, bl = _split(b)
    return f(ah, bh) + (f(ah, bl) + f(al, bh))


def _mod_kernel(cond_ref, w_ref, b_ref, o_ref):
    c = cond_ref[...]
    a = c * jax.nn.sigmoid(c)
    o_ref[0] = _dot3(a, w_ref[0]) + b_ref[0]


def _mod_call(cond, w_mod, b_mod):
    depth, d, n = w_mod.shape
    tn = n // 6
    return pl.pallas_call(
        _mod_kernel,
        grid=(depth, n // tn),
        in_specs=[pl.BlockSpec((MOD_ROWS, d), lambda l, j: (0, 0)),
                  pl.BlockSpec((1, d, tn), lambda l, j: (l, 0, j)),
                  pl.BlockSpec((1, 1, tn), lambda l, j: (l, 0, j))],
        out_specs=pl.BlockSpec((1, MOD_ROWS, tn), lambda l, j: (l, 0, j)),
        out_shape=jax.ShapeDtypeStruct((depth, MOD_ROWS, n), F32),
        compiler_params=_params("parallel", "parallel"),
        name="mod",
    )(cond, w_mod, b_mod.reshape(depth, 1, n))


C_AQ = 0
C_AK = C_AQ + NA_W
C_AV = C_AK + NA_W
C_SQ = C_AV + NA_W
C_SQR = C_SQ + SW_W
C_SK = C_SQR + SW_W
C_SKR = C_SK + SW_KV_W
C_SV = C_SKR + SW_KV_W
C_SU = C_SV + SW_KV_W
C_END = C_SU + S5_CH


def _rms_mod(x, g, shift, scale):
    y = x * lax.rsqrt(jnp.mean(x * x, axis=-1, keepdims=True) + RMS_EPS) * g
    return y * (1.0 + scale) + shift


def _mod_vectors(modb_ref, modc_ref, ctx_len, which):
    tm = ROW_TILE
    row = pl.program_id(1) * tm + lax.broadcasted_iota(jnp.int32, (tm, 1), 0)
    is_ctx = row < ctx_len
    mb = modb_ref[0]
    mc = modc_ref[0]
    return [jnp.where(is_ctx, mc[i:i + 1], mb[i:i + 1]) for i in which]


def _inproj_kernel(x_ref, modb_ref, modc_ref, g_ref, w_ref, cos_ref, sin_ref,
                   naq_ref, nak_ref, nav_ref, swq_ref, swk_ref, swv_ref, su_ref, *, ctx_len):
    shift, scale = _mod_vectors(modb_ref, modc_ref, ctx_len, (0, 1))
    h = _rms_mod(x_ref[0], g_ref[...], shift, scale).astype(MXU_DTYPE)
    p = _dot(h, w_ref[...])
    cos = cos_ref[...]
    sin = sin_ref[...]
    cos3 = jnp.concatenate([cos] * (SW_W // LANES), axis=1)
    sin3 = jnp.concatenate([sin] * (SW_W // LANES), axis=1)
    qk_scale = HEAD_DIM ** -0.5
    naq_ref[0] = (p[:, C_AQ:C_AK] * qk_scale).astype(naq_ref.dtype)
    nak_ref[0] = p[:, C_AK:C_AV].astype(nak_ref.dtype)
    nav_ref[0] = p[:, C_AV:C_SQ].astype(nav_ref.dtype)
    swq_ref[0] = ((p[:, C_SQ:C_SQR] * cos3 + p[:, C_SQR:C_SK] * sin3) * qk_scale).astype(swq_ref.dtype)
    swk_ref[0] = (p[:, C_SK:C_SKR] * cos + p[:, C_SKR:C_SV] * sin).astype(swk_ref.dtype)
    swv_ref[0] = p[:, C_SV:C_SU].astype(swv_ref.dtype)
    su_ref[0] = p[:, C_SU:C_END]


def _mod_index(n_ctx_tiles, ctx_row):
    return lambda b, j: (jnp.where(j < n_ctx_tiles, ctx_row, b), 0, 0)


def _inproj_call(xs, mod_l, g, w_cat, cos2, sin2, ctx_len):
    bsz, s, d = xs.shape
    tm = ROW_TILE
    row = lambda b, j: (b, j, 0)
    const = lambda b, j: (0, 0)
    widths = (NA_W, NA_W, NA_W, SW_W, SW_KV_W, SW_KV_W, S5_CH)
    dtypes = (MXU_DTYPE,) * 6 + (F32,)
    return pl.pallas_call(
        functools.partial(_inproj_kernel, ctx_len=ctx_len),
        grid=(bsz, s // tm),
        in_specs=[pl.BlockSpec((1, tm, d), row),
                  pl.BlockSpec((1, 6, d), lambda b, j: (b, 0, 0)),
                  pl.BlockSpec((1, 6, d), lambda b, j: (bsz, 0, 0)),
                  pl.BlockSpec((1, d), const),
                  pl.BlockSpec((d, C_END), const),
                  pl.BlockSpec((tm, LANES), lambda b, j: (j, 0)),
                  pl.BlockSpec((tm, LANES), lambda b, j: (j, 0))],
        out_specs=[pl.BlockSpec((1, tm, w), row) for w in widths],
        out_shape=[jax.ShapeDtypeStruct((bsz, s, w), t) for w, t in zip(widths, dtypes)],
        compiler_params=_params("parallel", "parallel"),
        name="inproj",
    )(xs, mod_l, mod_l, g, w_cat, cos2, sin2)


def _half_masks():
    lane = lax.broadcasted_iota(jnp.int32, (1, LANES), 1)
    return lane < HEAD_DIM, lane >= HEAD_DIM


def _na_kernel(q_ref, k_ref, v_ref, bias_ref, o_ref, *, ctx_len, rows):
    i = pl.program_id(1)
    tq = NA_QROWS * GRID_W
    n_ctx_q = ctx_len // tq
    masks = _half_masks()
    nk = NA_KROWS * GRID_W

    def run(local):
        q = q_ref[0]
        if local:
            r0 = (i - n_ctx_q) * NA_QROWS
            start0 = jnp.clip(r0 - NA_ROWS // 2, 0, rows - NA_KROWS)
            start = pl.multiple_of(ctx_len + start0 * GRID_W, GRID_W)
            tab_idx, row_mask = {}, {}
            for a in range(NA_QROWS):
                r = r0 + a
                s_r = jnp.clip(r - NA_ROWS // 2, 0, rows - NA_ROWS)
                for p in range(NA_KROWS // 2):
                    kr = start0 + 2 * p
                    ok_lo = (kr >= s_r) & (kr < s_r + NA_ROWS)
                    ok_hi = (kr + 1 >= s_r) & (kr + 1 < s_r + NA_ROWS)
                    tab_idx[a, p] = jnp.clip(kr - r + NA_ROWS, 0, 2 * NA_ROWS - 1)
                    row_mask[a, p] = jnp.where(masks[0], jnp.where(ok_lo, 0.0, NEG_INF),
                                               jnp.where(ok_hi, 0.0, NEG_INF))
        outs = []
        for t in range(NA_W // LANES):
            sl = slice(LANES * t, LANES * (t + 1))
            qt = q[:, sl]
            zero = jnp.zeros_like(qt)
            qm = jnp.concatenate([jnp.where(masks[0], qt, zero), jnp.where(masks[1], qt, zero)], axis=0)
            kc = k_ref[0, 0:ctx_len, sl]
            vc = v_ref[0, 0:ctx_len, sl]
            s_cx = _dot_nt(qm, kc)
            m = jnp.max(s_cx, axis=-1, keepdims=True)
            if local:
                kw = k_ref[0, pl.ds(start, nk), sl]
                vw = v_ref[0, pl.ds(start, nk), sl]
                bias = jnp.concatenate(
                    [jnp.concatenate([bias_ref[2 * t + hh, pl.ds(tab_idx[a, p], 1)][0] + row_mask[a, p]
                                      for p in range(NA_KROWS // 2)], axis=-1)
                     for hh in range(2) for a in range(NA_QROWS)], axis=0)
                s_nb = _dot_nt(qm, kw) + bias
                m = jnp.maximum(m, jnp.max(s_nb, axis=-1, keepdims=True))
                p_nb = jnp.exp(s_nb - m)
            p_cx = jnp.exp(s_cx - m)
            den = jnp.sum(p_cx, axis=-1, keepdims=True)
            o = _dot(p_cx.astype(MXU_DTYPE), vc)
            if local:
                den = den + jnp.sum(p_nb, axis=-1, keepdims=True)
                o = o + _dot(p_nb.astype(MXU_DTYPE), vw)
            o = o / den
            outs.append(jnp.where(masks[0], o[:tq], o[tq:]))
        o_ref[0] = jnp.concatenate(outs, axis=-1).astype(o_ref.dtype)

    @pl.when(i < n_ctx_q)
    def _():
        run(False)

    @pl.when(i >= n_ctx_q)
    def _():
        run(True)


def _na_call(q, k, v, bias_tab, ctx_len):
    bsz, s, w = q.shape
    rows = (s - ctx_len) // GRID_W
    tq = NA_QROWS * GRID_W
    assert rows >= NA_KROWS and rows % NA_QROWS == 0 and ctx_len % tq == 0
    whole = lambda b, i: (b, 0, 0)
    return pl.pallas_call(
        functools.partial(_na_kernel, ctx_len=ctx_len, rows=rows),
        grid=(bsz, s // tq),
        in_specs=[pl.BlockSpec((1, tq, w), lambda b, i: (b, i, 0)),
                  pl.BlockSpec((1, s, w), whole),
                  pl.BlockSpec((1, s, w), whole),
                  pl.BlockSpec(bias_tab.shape, lambda b, i: (0, 0, 0, 0))],
        out_specs=pl.BlockSpec((1, tq, w), lambda b, i: (b, i, 0)),
        out_shape=jax.ShapeDtypeStruct((bsz, s, w), q.dtype),
        compiler_params=_params("parallel", "arbitrary"),
        name="na_attn",
    )(q, k, v, bias_tab)


def _na_bias_table(rpb):
    qcol = np.arange(GRID_W)[:, None]
    kcol = np.arange(GRID_W)[None, :]
    ws = np.clip(qcol - NA_COLS // 2, 0, GRID_W - NA_COLS)
    valid = (kcol >= ws) & (kcol < ws + NA_COLS)
    dc = np.clip(kcol - qcol + NA_COLS - 1, 0, 2 * NA_COLS - 2)
    full = jnp.where(valid[None, None], rpb[:, :, dc].astype(F32), NEG_INF)
    edge = jnp.full_like(full[:, :1], NEG_INF)
    full = jnp.concatenate([edge, full, edge], axis=1)
    return jnp.concatenate([full[:, :-1], full[:, 1:]], axis=-1)


def _sw_kernel(sink_ref, q_ref, k_ref, v_ref, o_ref, *, ctx_len, seq):
    i = pl.program_id(1)
    tq = SW_QBLK
    n_ctx_q = ctx_len // tq
    masks = _half_masks()
    nk = SW_KBLK
    first_head = lax.broadcasted_iota(jnp.int32, (2 * tq, 1), 0) < tq

    def run(local):
        q = q_ref[0]
        kc = k_ref[0, 0:ctx_len, :]
        vc = v_ref[0, 0:ctx_len, :]
        if local:
            n = i - n_ctx_q
            start_lat = jnp.clip(n * tq - SW_WINDOW, 0, seq - nk)
            start = pl.multiple_of(ctx_len + start_lat, SW_BLK)
            kw = k_ref[0, pl.ds(start, nk), :]
            vw = v_ref[0, pl.ds(start, nk), :]
            row = lax.broadcasted_iota(jnp.int32, (2 * tq, 1), 0)
            qpos = n * tq + jnp.where(first_head, row, row - tq)
            kpos = start_lat + lax.broadcasted_iota(jnp.int32, (1, nk), 1)
            valid = jnp.abs(qpos - kpos) <= SW_WINDOW
        outs = []
        for t in range(SW_W // LANES):
            qt = q[:, LANES * t:LANES * (t + 1)]
            zero = jnp.zeros_like(qt)
            qm = jnp.concatenate([jnp.where(masks[0], qt, zero), jnp.where(masks[1], qt, zero)], axis=0)
            sink = jnp.where(first_head, sink_ref[SW_HEAD_ORDER[2 * t]], sink_ref[SW_HEAD_ORDER[2 * t + 1]])
            s_cx = _dot_nt(qm, kc)
            m = jnp.maximum(jnp.max(s_cx, axis=-1, keepdims=True), sink)
            if local:
                s_loc = jnp.where(valid, _dot_nt(qm, kw), NEG_INF)
                m = jnp.maximum(m, jnp.max(s_loc, axis=-1, keepdims=True))
                p_loc = jnp.exp(s_loc - m)
            p_cx = jnp.exp(s_cx - m)
            den = jnp.sum(p_cx, axis=-1, keepdims=True) + jnp.exp(sink - m)
            o = _dot(p_cx.astype(MXU_DTYPE), vc)
            if local:
                den = den + jnp.sum(p_loc, axis=-1, keepdims=True)
                o = o + _dot(p_loc.astype(MXU_DTYPE), vw)
            o = o / den
            outs.append(jnp.where(masks[0], o[:tq], o[tq:]))
        o_ref[0] = jnp.concatenate(outs, axis=-1).astype(o_ref.dtype)

    @pl.when(i < n_ctx_q)
    def _():
        run(False)

    @pl.when(i >= n_ctx_q)
    def _():
        run(True)


def _sw_call(sinks, q, k, v, ctx_len):
    bsz, s, w = q.shape
    seq = s - ctx_len
    assert seq >= SW_KBLK and seq % SW_QBLK == 0 and ctx_len % SW_QBLK == 0
    whole = lambda b, i: (b, 0, 0)
    return pl.pallas_call(
        functools.partial(_sw_kernel, ctx_len=ctx_len, seq=seq),
        grid=(bsz, s // SW_QBLK),
        in_specs=[pl.BlockSpec(memory_space=pltpu.SMEM),
                  pl.BlockSpec((1, SW_QBLK, w), lambda b, i: (b, i, 0)),
                  pl.BlockSpec((1, s, SW_KV_W), whole),
                  pl.BlockSpec((1, s, SW_KV_W), whole)],
        out_specs=pl.BlockSpec((1, SW_QBLK, w), lambda b, i: (b, i, 0)),
        out_shape=jax.ShapeDtypeStruct((bsz, s, w), q.dtype),
        compiler_params=_params("parallel", "arbitrary"),
        name="sw_attn",
    )(sinks.astype(F32), q, k, v)


def _s5_kernel(u_ref, m_ref, wsr_ref, wsi_ref, wor_ref, woi_ref, lr_ref, li_ref, o_ref,
               sre, sim, xre, xim, acc, *, n_ctx_chunks):
    d = pl.program_id(1)
    bsz, nc, _ = u_ref.shape
    for b in range(bsz):
        ub = u_ref[b]
        sre[b] = _dot(ub, wsr_ref[0, 0])
        sim[b] = _dot(ub, wsi_ref[0, 0])
    lr = lr_ref[0, 0]
    li = li_ref[0, 0]

    def step(c, carry):
        new = []
        for b in range(bsz):
            xr, xi = carry[2 * b], carry[2 * b + 1]
            xre[b, pl.ds(c, 1), :] = xr
            xim[b, pl.ds(c, 1), :] = xi
            sr = sre[b, pl.ds(c, 1), :]
            si = sim[b, pl.ds(c, 1), :]
            new.append(lr * xr - li * xi + sr)
            new.append(lr * xi + li * xr + si)
        return tuple(new)

    zero = tuple(jnp.zeros((1, LANES), F32) for _ in range(2 * bsz))

    @pl.when(d == 0)
    def _():
        lax.fori_loop(0, nc, step, zero)

    @pl.when(d == 1)
    def _():
        carry = lax.fori_loop(0, n_ctx_chunks, lambda k, cr: step(n_ctx_chunks - 1 - k, cr), zero)
        lax.fori_loop(0, nc - n_ctx_chunks, lambda k, cr: step(nc - 1 - k, cr), carry)

    half = u_ref.shape[2] // 2
    for b in range(bsz):
        ub = u_ref[b]
        y_intra = jnp.concatenate([_dot(ub[:, :half], m_ref[0, 0]), _dot(ub[:, half:], m_ref[0, 1])], axis=-1)
        y = (y_intra + _dot(xre[b].astype(MXU_DTYPE), wor_ref[0, 0])
             + _dot(xim[b].astype(MXU_DTYPE), woi_ref[0, 0]))

        @pl.when(d == 0)
        def _():
            acc[b] = y

        @pl.when(d == 1)
        def _():
            o_ref[b] = (acc[b] + y).astype(o_ref.dtype)


def _s5_call(u_t, wts, n_ctx_chunks):
    m, wsr, wsi, wor, woi, lr, li = wts
    bsz, nc, width = u_t.shape
    pw = 2 * S5_CHUNK * S5_GROUP_CH
    n_pairs = width // pw
    blk = lambda shp: pl.BlockSpec((1, 1) + shp, lambda j, d: (d, j, 0, 0))
    return pl.pallas_call(
        functools.partial(_s5_kernel, n_ctx_chunks=n_ctx_chunks),
        grid=(n_pairs, 2),
        in_specs=[pl.BlockSpec((bsz, nc, pw), lambda j, d: (0, 0, j)),
                  pl.BlockSpec((1, 2, pw // 2, pw // 2), lambda j, d: (d, j, 0, 0)),
                  blk((pw, LANES)), blk((pw, LANES)), blk((LANES, pw)), blk((LANES, pw)),
                  blk((1, LANES)), blk((1, LANES))],
        out_specs=pl.BlockSpec((bsz, nc, pw), lambda j, d: (0, 0, j)),
        out_shape=jax.ShapeDtypeStruct((bsz, nc, width), MXU_DTYPE),
        scratch_shapes=[pltpu.VMEM((bsz, nc, LANES), F32) for _ in range(4)] + [pltpu.VMEM((bsz, nc, pw), F32)],
        compiler_params=_params("parallel", "arbitrary"),
        name="s5_scan",
    )(u_t, m, wsr, wsi, wor, woi, lr, li)


def _s5_weights(a_re, a_im, log_step, b_re, b_im, c_re, c_im):
    lc, g, p, h = S5_CHUNK, S5_GROUPS, S5_STATE, S5_GROUP_CH
    lam = lax.complex(jnp.minimum(a_re.astype(F32), S5_EIG_MAX), a_im.astype(F32))
    step = jnp.exp(log_step.astype(F32))[..., None]
    lam_bar = jnp.exp(lam * step)
    b_bar = ((lam_bar - 1.0) / lam)[..., None] * lax.complex(b_re.astype(F32), b_im.astype(F32))
    cc = lax.complex(c_re.astype(F32), c_im.astype(F32))
    dd = jnp.arange(lc + 1, dtype=F32)
    pw = jnp.exp((lam * step)[..., None] * dd)
    kern = jnp.real(jnp.einsum('zgop,zgpd,zgpi->zgdoi', cc, pw[..., :lc], b_bar))
    jj = np.arange(lc)[:, None]
    ii = np.arange(lc)[None, :]
    mats, wst, wout = [], [], []
    for z in range(2):
        lag = (ii - jj) if z == 0 else (jj - ii)
        ok = lag >= 0
        kz = kern[z][:, np.where(ok, lag, 0)]
        kz = jnp.where(ok[None, :, :, None, None], kz, 0.0)
        mats.append(kz.transpose(0, 1, 4, 2, 3).reshape(g, lc * h, lc * h))
        d_state = (lc - 1 - np.arange(lc)) if z == 0 else np.arange(lc)
        ws = pw[z][:, :, d_state][..., None] * b_bar[z][:, :, None, :]
        wst.append(ws.transpose(0, 2, 3, 1).reshape(g, lc * h, p))
        d_out = (np.arange(lc) + 1) if z == 0 else (lc - np.arange(lc))
        wo = cc[z][:, :, :, None] * pw[z][:, None, :, :][..., d_out]
        wout.append(wo.transpose(0, 2, 3, 1).reshape(g, p, lc * h))
    mats = jnp.stack(mats)
    wst = jnp.stack(wst)
    wout = jnp.stack(wout)

    def pair_rows(w):
        w = w.reshape(2, g // 2, 2, lc * h, p)
        z0 = jnp.zeros_like(w[:, :, 0])
        top = jnp.concatenate([w[:, :, 0], z0], axis=-1)
        bot = jnp.concatenate([z0, w[:, :, 1]], axis=-1)
        return jnp.concatenate([top, bot], axis=-2)

    def pair_cols(w):
        w = w.reshape(2, g // 2, 2, p, lc * h)
        z0 = jnp.zeros_like(w[:, :, 0])
        top = jnp.concatenate([w[:, :, 0], z0], axis=-1)
        bot = jnp.concatenate([z0, w[:, :, 1]], axis=-1)
        return jnp.concatenate([top, bot], axis=-2)

    lam_c = pw[..., lc].reshape(2, g // 2, 1, 2 * p)
    cast = lambda w: w.astype(MXU_DTYPE)
    return (cast(mats), cast(pair_rows(jnp.real(wst))), cast(pair_rows(jnp.imag(wst))),
            cast(pair_cols(jnp.real(wout))), cast(pair_cols(-jnp.imag(wout))),
            jnp.real(lam_c), jnp.imag(lam_c))


def _gelu_tanh(x):
    cdf = 0.5 * (1.0 + jnp.tanh(math.sqrt(2.0 / math.pi) * (x + 0.044715 * (x * x * x))))
    return x * cdf


def _outproj_kernel(x_ref, ya_ref, ys_ref, su_ref, yc_ref, modb_ref, modc_ref, woa_ref, wob_ref, woc_ref,
                    wglu_ref, bglu_ref, dsk_ref, g_ref, wr_ref, br_ref,
                    xo_ref, h_ref, topi_ref, topw_ref, *, ctx_len):
    gate, shift, scale = _mod_vectors(modb_ref, modc_ref, ctx_len, (2, 3, 4))
    y = dsk_ref[...] * su_ref[0] + ys_ref[0].astype(F32)
    gl = _gelu_tanh(y)
    yb = gl * jax.nn.sigmoid(_dot(gl.astype(MXU_DTYPE), wglu_ref[...]) + bglu_ref[...])
    mix = (_dot(ya_ref[0], woa_ref[...]) + _dot(yb.astype(MXU_DTYPE), wob_ref[...])
           + _dot(yc_ref[0], woc_ref[...]))
    x = x_ref[0] + gate * mix
    xo_ref[0] = x
    h = _rms_mod(x, g_ref[...], shift, scale)
    dl = h_ref.shape[-1]
    for sl in range(SUBLANES):
        h_ref[0, :, sl, :] = h[:, dl * sl:dl * (sl + 1)]
    logits = _dot3(wr_ref[...], h, nt=True) + br_ref[...]
    n_e, tm = logits.shape
    e_iota = lax.broadcasted_iota(jnp.int32, (n_e, tm), 0)
    vals, idxs = [], []
    for _ in range(TOP_K):
        mx = jnp.max(logits, axis=0, keepdims=True)
        ix = jnp.min(jnp.where(logits == mx, e_iota, n_e), axis=0, keepdims=True)
        vals.append(mx)
        idxs.append(ix)
        logits = jnp.where(e_iota == ix, -jnp.inf, logits)
    ex = [jnp.exp(v - vals[0]) for v in vals]
    den = ex[0] + ex[1] + ex[2] + ex[3]
    topi_ref[0] = jnp.concatenate(idxs, axis=0)
    topw_ref[0] = jnp.concatenate([e / den for e in ex], axis=0)


def _outproj_call(xs, ya, ys, su, yc, mod_l, wts, ctx_len):
    bsz, s, d = xs.shape
    tm = ROW_TILE
    row = lambda b, j: (b, j, 0)
    const = lambda b, j: (0, 0)
    full = lambda a: pl.BlockSpec(a.shape, const)
    return pl.pallas_call(
        functools.partial(_outproj_kernel, ctx_len=ctx_len),
        grid=(bsz, s // tm),
        in_specs=[pl.BlockSpec((1, tm, d), row),
                  pl.BlockSpec((1, tm, NA_W), row),
                  pl.BlockSpec((1, tm, S5_CH), row),
                  pl.BlockSpec((1, tm, S5_CH), row),
                  pl.BlockSpec((1, tm, SW_W), row),
                  pl.BlockSpec((1, 6, d), lambda b, j: (b, 0, 0)),
                  pl.BlockSpec((1, 6, d), lambda b, j: (bsz, 0, 0))] + [full(a) for a in wts],
        out_specs=[pl.BlockSpec((1, tm, d), row),
                   pl.BlockSpec((1, tm, SUBLANES, d // SUBLANES), lambda b, j: (b, j, 0, 0)),
                   pl.BlockSpec((1, TOP_K, tm), lambda b, j: (b, 0, j)),
                   pl.BlockSpec((1, TOP_K, tm), lambda b, j: (b, 0, j))],
        out_shape=[jax.ShapeDtypeStruct((bsz, s, d), F32),
                   jax.ShapeDtypeStruct((bsz, s, SUBLANES, d // SUBLANES), F32),
                   jax.ShapeDtypeStruct((bsz, TOP_K, s), jnp.int32),
                   jax.ShapeDtypeStruct((bsz, TOP_K, s), F32)],
        compiler_params=_params("parallel", "parallel"),
        name="outproj",
    )(xs, ya, ys, su, yc, mod_l, mod_l, *wts)


def _route_kernel(topi_ref, dest_ref, blk_ref, ends_ref, carry, ranks, *, n_tiles):
    phase = pl.program_id(0)
    t = pl.program_id(1) * n_tiles + pl.program_id(2)
    idx = topi_ref[0]
    tm = idx.shape[1]
    e_iota = lax.broadcasted_iota(jnp.int32, (N_EXPERTS, tm), 0)
    sel = [idx[k:k + 1] == e_iota for k in range(TOP_K)]

    @pl.when(phase == 0)
    def _():
        @pl.when(t == 0)
        def _():
            carry[...] = jnp.zeros_like(carry)

        chosen = sel[0] | sel[1] | sel[2] | sel[3]
        onehot = jnp.where(chosen, 1.0, 0.0)
        before = (lax.broadcasted_iota(jnp.int32, (tm, tm), 0) < lax.broadcasted_iota(jnp.int32, (tm, tm), 1))
        pfx = _dot(onehot.astype(MXU_DTYPE), jnp.where(before, 1.0, 0.0).astype(MXU_DTYPE)) + carry[:, 0:1]
        ranks[t] = jnp.concatenate(
            [jnp.sum(jnp.where(sel[k], pfx, 0.0), axis=0, keepdims=True) for k in range(TOP_K)], axis=0)
        carry[...] = carry[...] + jnp.sum(onehot, axis=1, keepdims=True)

    @pl.when(phase == 1)
    def _():
        counts = carry[:, 0:1]
        padded = jnp.ceil(counts * (1.0 / MOE_BLK)) * MOE_BLK
        r_i = lax.broadcasted_iota(jnp.int32, (N_EXPERTS, N_EXPERTS), 0)
        c_i = lax.broadcasted_iota(jnp.int32, (N_EXPERTS, N_EXPERTS), 1)
        padded_row = jnp.sum(jnp.where(r_i == c_i, padded, 0.0), axis=0, keepdims=True)
        pstart = jnp.sum(jnp.where(c_i < r_i, padded_row, 0.0), axis=1, keepdims=True)
        ends = pstart + padded
        rk = ranks[t]
        dest_ref[0] = jnp.concatenate(
            [jnp.sum(jnp.where(sel[k], pstart, 0.0), axis=0, keepdims=True) + rk[k:k + 1] for k in range(TOP_K)],
            axis=1).astype(jnp.int32)
        nb = blk_ref.shape[1]
        blk_start = (lax.broadcasted_iota(jnp.int32, (N_EXPERTS, nb), 1) * MOE_BLK).astype(F32)
        owner = jnp.sum(jnp.where(ends <= blk_start, 1.0, 0.0), axis=0, keepdims=True)
        blk_ref[...] = jnp.minimum(owner, N_EXPERTS - 1.0).astype(jnp.int32)
        ends_ref[...] = jnp.broadcast_to(ends, ends_ref.shape).astype(jnp.int32)


def _route_call(topi, n_blocks):
    bsz, _, s = topi.shape
    tm = MOE_TILE
    n_tiles = s // tm
    nb_pad = -(-n_blocks // LANES) * LANES
    return pl.pallas_call(
        functools.partial(_route_kernel, n_tiles=n_tiles),
        grid=(2, bsz, n_tiles),
        in_specs=[pl.BlockSpec((1, TOP_K, tm), lambda p, b, j: (b, 0, j))],
        out_specs=[pl.BlockSpec((1, 1, TOP_K * tm), lambda p, b, j: ((b * n_tiles + j) * p, 0, 0)),
                   pl.BlockSpec((1, nb_pad), lambda p, b, j: (0, 0)),
                   pl.BlockSpec((N_EXPERTS, LANES), lambda p, b, j: (0, 0))],
        out_shape=[jax.ShapeDtypeStruct((bsz * n_tiles, 1, TOP_K * tm), jnp.int32),
                   jax.ShapeDtypeStruct((1, nb_pad), jnp.int32),
                   jax.ShapeDtypeStruct((N_EXPERTS, LANES), jnp.int32)],
        scratch_shapes=[pltpu.VMEM((N_EXPERTS, LANES), F32),
                        pltpu.VMEM((bsz * n_tiles, TOP_K, tm), F32)],
        compiler_params=_params("arbitrary", "arbitrary", "arbitrary"),
        name="route_rank",
    )(topi)


def _dispatch_kernel(ends_ref, dest_ref, h_ref, xg_ref, zbuf, stage, sem, zsem, *, n_blocks):
    tm = h_ref.shape[1]

    @pl.when((pl.program_id(0) == 0) & (pl.program_id(1) == 0))
    def _():
        zbuf[...] = jnp.zeros_like(zbuf)

        def fill(row):
            return pltpu.make_async_copy(zbuf, xg_ref.at[pl.ds(pl.multiple_of(row, MOE_BLK), MOE_BLK)], zsem)

        def each(fn):
            for e in range(N_EXPERTS):
                begin = ends_ref[e - 1] if e else 0

                @pl.when(ends_ref[e] > begin)
                def _():
                    fn(fill(ends_ref[e] - MOE_BLK))

            def dead(i, c):
                fn(fill(i * MOE_BLK))
                return c

            lax.fori_loop(ends_ref[N_EXPERTS - 1] // MOE_BLK, n_blocks, dead, 0)

        each(lambda cp: cp.start())
        each(lambda cp: cp.wait())

    step = pl.program_id(0) * pl.num_programs(1) + pl.program_id(1)
    n_steps = pl.num_programs(0) * pl.num_programs(1)
    slot = lax.rem(step, 2)
    stage[slot] = h_ref[0]

    def body(t, c):
        for k in range(TOP_K):
            pltpu.make_async_copy(stage.at[slot, t], xg_ref.at[dest_ref[0, 0, k * tm + t]],
                                  sem.at[slot]).start(priority=k % 2)
        return c

    lax.fori_loop(0, tm, body, 0, unroll=8)

    def wait_tile(sl):
        pltpu.make_async_copy(xg_ref.at[pl.ds(0, TOP_K * tm)], xg_ref.at[pl.ds(0, TOP_K * tm)], sem.at[sl]).wait()

    @pl.when(step > 0)
    def _():
        wait_tile(1 - slot)

    @pl.when(step == n_steps - 1)
    def _():
        wait_tile(slot)


def _dispatch_call(ends, dest, h, n_blocks):
    bsz, s, _, dl = h.shape
    tm = MOE_TILE
    n_tiles = s // tm
    grid_spec = pltpu.PrefetchScalarGridSpec(
        num_scalar_prefetch=1,
        grid=(bsz, n_tiles),
        in_specs=[pl.BlockSpec((1, 1, TOP_K * tm), lambda b, j, en: (b * n_tiles + j, 0, 0), memory_space=pltpu.SMEM),
                  pl.BlockSpec((1, tm, SUBLANES, dl), lambda b, j, en: (b, j, 0, 0))],
        out_specs=pl.BlockSpec(memory_space=pl.ANY),
        scratch_shapes=[pltpu.VMEM((MOE_BLK, SUBLANES, dl), h.dtype), pltpu.VMEM((2, tm, SUBLANES, dl), h.dtype),
                        pltpu.SemaphoreType.DMA((2,)),
                        pltpu.SemaphoreType.DMA(())],
    )
    return pl.pallas_call(
        functools.partial(_dispatch_kernel, n_blocks=n_blocks),
        grid_spec=grid_spec,
        out_shape=jax.ShapeDtypeStruct((n_blocks * MOE_BLK, SUBLANES, dl), h.dtype),
        compiler_params=_params("arbitrary", "arbitrary"),
        name="moe_dispatch",
    )(ends, dest, h)


def _moe_kernel(blk_exp_ref, nact_ref, x_hbm, wgu_ref, bgu_ref, wd_ref, bd_ref, y_hbm, wgu_c, wd_c,
                xbuf, xsem, ybuf, ysem):
    i = pl.program_id(0)
    slot = lax.rem(i, 2)
    dl = x_hbm.shape[-1]

    def fetch(blk, sl_):
        for s in range(SUBLANES):
            pltpu.make_async_copy(x_hbm.at[pl.ds(blk * MOE_BLK, MOE_BLK), s, :],
                                  xbuf.at[sl_, :, pl.ds(dl * s, dl)], xsem.at[sl_]).start()

    @pl.when(i == 0)
    def _():
        fetch(0, 0)

    @pl.when(i + 1 < pl.num_programs(0))
    def _():
        fetch(i + 1, 1 - slot)

    def y_wait(sl_):
        pltpu.make_async_copy(ybuf.at[sl_], ybuf.at[sl_], ysem.at[sl_]).wait()

    @pl.when(i >= 2)
    def _():
        y_wait(slot)

    e = blk_exp_ref[i]
    prev = blk_exp_ref[jnp.maximum(i - 1, 0)]
    d, f2 = wgu_c.shape
    f = f2 // 2
    rows = 128

    @pl.when((i == 0) | (e != prev))
    def _():
        def cv(r, c):
            sl = pl.ds(pl.multiple_of(r * rows, rows), rows)
            wgu_c[sl, :] = wgu_ref[0, 0, sl, :].astype(wgu_c.dtype)
            return c
        lax.fori_loop(0, d // rows, cv, 0)

        def cv2(r, c):
            sl = pl.ds(pl.multiple_of(r * rows, rows), rows)
            wd_c[sl, :] = wd_ref[0, 0, sl, :].astype(wd_c.dtype)
            return c
        lax.fori_loop(0, f // rows, cv2, 0)

    pltpu.make_async_copy(xbuf.at[slot], xbuf.at[slot], xsem.at[slot]).wait()

    @pl.when(i < nact_ref[0])
    def _():
        gu = _dot(xbuf[slot].astype(MXU_DTYPE), wgu_c[...]) + bgu_ref[0, 0]
        gate = jnp.minimum(gu[:, :f], SWIGLU_LIMIT)
        up = jnp.clip(gu[:, f:], -SWIGLU_LIMIT, SWIGLU_LIMIT)
        act = gate * jax.nn.sigmoid(SWIGLU_ALPHA * gate) * (up + 1.0)
        ybuf[slot] = _dot(act.astype(MXU_DTYPE), wd_c[...]) + bd_ref[0, 0]

    @pl.when(i >= nact_ref[0])
    def _():
        ybuf[slot] = jnp.zeros(ybuf.shape[1:], ybuf.dtype)

    for s in range(SUBLANES):
        pltpu.make_async_copy(ybuf.at[slot, :, pl.ds(dl * s, dl)],
                              y_hbm.at[pl.ds(i * MOE_BLK, MOE_BLK), s, :], ysem.at[slot]).start()

    @pl.when(i == pl.num_programs(0) - 1)
    def _():
        y_wait(1 - slot)
        y_wait(slot)


def _moe_call(layer, blk_exp, n_active, xg, w_gate_up, b_gate_up, w_down, b_down):
    n_rows, _, dl = xg.shape
    depth, n_e, d, f2 = w_gate_up.shape
    f = f2 // 2
    n_blocks = n_rows // MOE_BLK
    assert n_blocks >= 2
    wsel = lambda i, be, na: (layer, be[i], 0, 0)
    grid_spec = pltpu.PrefetchScalarGridSpec(
        num_scalar_prefetch=2,
        grid=(n_blocks,),
        in_specs=[pl.BlockSpec(memory_space=pl.ANY),
                  pl.BlockSpec((1, 1, d, f2), wsel),
                  pl.BlockSpec((1, 1, 1, f2), wsel),
                  pl.BlockSpec((1, 1, f, d), wsel),
                  pl.BlockSpec((1, 1, 1, d), wsel)],
        out_specs=pl.BlockSpec(memory_space=pl.ANY),
        scratch_shapes=[pltpu.VMEM((d, f2), MXU_DTYPE), pltpu.VMEM((f, d), MXU_DTYPE),
                        pltpu.VMEM((2, MOE_BLK, d), F32), pltpu.SemaphoreType.DMA((2,)),
                        pltpu.VMEM((2, MOE_BLK, d), F32), pltpu.SemaphoreType.DMA((2,))],
    )
    return pl.pallas_call(
        _moe_kernel,
        grid_spec=grid_spec,
        out_shape=jax.ShapeDtypeStruct((n_rows, SUBLANES, dl), F32),
        compiler_params=_params("arbitrary"),
        name="moe_experts",
    )(blk_exp, n_active, xg, w_gate_up, b_gate_up.reshape(depth, n_e, 1, f2),
      w_down, b_down.reshape(depth, n_e, 1, d))


def _combine_kernel(dest_ref, dnext_ref, w_ref, gate_ref, x_hbm, yg_hbm, xo_hbm,
                    gbuf, xbuf, obuf, gsem, xsem, osem, *, n_tiles):
    tm = MOE_TILE
    dl = gbuf.shape[-1]
    step = pl.program_id(0) * n_tiles + pl.program_id(1)
    n_steps = pl.num_programs(0) * n_tiles
    slot = lax.rem(step, 2)

    def stream_copies(st, sl_, buf, hbm, sem, to_hbm):
        b = lax.div(st, n_tiles)
        r0 = lax.rem(st, n_tiles) * tm
        out = []
        for s in range(SUBLANES):
            rows = hbm.at[b, pl.ds(r0, tm), pl.ds(dl * s, dl)]
            tiles = buf.at[sl_, :, s, :]
            out.append(pltpu.make_async_copy(tiles, rows, sem.at[sl_]) if to_hbm
                       else pltpu.make_async_copy(rows, tiles, sem.at[sl_]))
        return out

    def fetch(dref, st, sl_):
        for cp in stream_copies(st, sl_, xbuf, x_hbm, xsem, False):
            cp.start()

        def body(t, c):
            for k in range(TOP_K):
                pltpu.make_async_copy(yg_hbm.at[dref[0, 0, k * tm + t]], gbuf.at[sl_, k, t],
                                      gsem.at[sl_]).start(priority=k % 2)
            return c

        lax.fori_loop(0, tm, body, 0, unroll=8)

    @pl.when(step == 0)
    def _():
        fetch(dest_ref, 0, 0)

    @pl.when(step + 1 < n_steps)
    def _():
        fetch(dnext_ref, step + 1, 1 - slot)

    def wait_all(buf, sem, sl_):
        pltpu.make_async_copy(buf.at[sl_], buf.at[sl_], sem.at[sl_]).wait()

    wait_all(gbuf, gsem, slot)
    wait_all(xbuf, xsem, slot)

    @pl.when(step >= 2)
    def _():
        wait_all(obuf, osem, slot)

    gate = gate_ref[0]

    def row(t, c):
        acc = gbuf[slot, 0, t] * w_ref[0, 0, t]
        for k in range(1, TOP_K):
            acc = acc + gbuf[slot, k, t] * w_ref[0, 0, k * tm + t]
        obuf[slot, t] = xbuf[slot, t] + gate * acc
        return c

    lax.fori_loop(0, tm, row, 0, unroll=8)
    for cp in stream_copies(step, slot, obuf, xo_hbm, osem, True):
        cp.start()

    @pl.when(step == n_steps - 1)
    def _():
        wait_all(obuf, osem, 1 - slot)
        wait_all(obuf, osem, slot)


def _combine_call(dest, topw, xs, mod_l, yg, ctx_len):
    bsz, s, d = xs.shape
    tm = MOE_TILE
    n_tiles = s // tm
    last = bsz * n_tiles - 1
    assert last >= 1
    dl = d // SUBLANES
    w_flat = topw.reshape(bsz, TOP_K, n_tiles, tm).transpose(0, 2, 1, 3).reshape(bsz * n_tiles, 1, TOP_K * tm)
    gate = mod_l[:, 5].reshape(MOD_ROWS, SUBLANES, dl)
    idx_spec = lambda ahead: pl.BlockSpec(
        (1, 1, TOP_K * tm), lambda b, j: (jnp.minimum(b * n_tiles + j + ahead, last), 0, 0), memory_space=pltpu.SMEM)
    tile_buf = lambda lead: pltpu.VMEM(lead + (tm, SUBLANES, dl), F32)
    return pl.pallas_call(
        functools.partial(_combine_kernel, n_tiles=n_tiles),
        grid=(bsz, n_tiles),
        in_specs=[idx_spec(0), idx_spec(1), idx_spec(0),
                  pl.BlockSpec((1, SUBLANES, dl), _mod_index(ctx_len // tm, bsz)),
                  pl.BlockSpec(memory_space=pl.ANY),
                  pl.BlockSpec(memory_space=pl.ANY)],
        out_specs=pl.BlockSpec(memory_space=pl.ANY),
        out_shape=jax.ShapeDtypeStruct((bsz, s, d), F32),
        scratch_shapes=[tile_buf((2, TOP_K)), tile_buf((2,)), tile_buf((2,)),
                        pltpu.SemaphoreType.DMA((2,)), pltpu.SemaphoreType.DMA((2,)), pltpu.SemaphoreType.DMA((2,))],
        compiler_params=_params("arbitrary", "arbitrary"),
        name="moe_combine",
    )(dest, dest, w_flat, gate, xs, yg)


def _final_kernel(x_ref, g_ref, o_ref):
    x = x_ref[0]
    o_ref[0] = x * lax.rsqrt(jnp.mean(x * x, axis=-1, keepdims=True) + RMS_EPS) * g_ref[...]


def _final_call(xs, g, ctx_len):
    bsz, s, d = xs.shape
    tm = MOE_TILE
    off = ctx_len // tm
    return pl.pallas_call(
        _final_kernel,
        grid=(bsz, (s - ctx_len) // tm),
        in_specs=[pl.BlockSpec((1, tm, d), lambda b, j: (b, j + off, 0)),
                  pl.BlockSpec((1, d), lambda b, j: (0, 0))],
        out_specs=pl.BlockSpec((1, tm, d), lambda b, j: (b, j, 0)),
        out_shape=jax.ShapeDtypeStruct((bsz, s - ctx_len, d), F32),
        compiler_params=_params("parallel", "parallel"),
        name="final_norm",
    )(xs, g)


def _rope_tables(seq, ctx_len):
    t = jnp.arange(seq)
    row = (t // GRID_W).astype(F32)
    col = (t % GRID_W).astype(F32)
    nf = HEAD_DIM // 4
    inv = ROPE_BASE ** (-jnp.arange(nf, dtype=F32) / nf)
    ar = row[:, None] * inv
    ac = col[:, None] * inv
    ang = jnp.concatenate([ar, ar, ac, ac], axis=-1)
    cos = jnp.concatenate([jnp.ones((ctx_len, HEAD_DIM), F32), jnp.cos(ang)], axis=0)
    sin = jnp.concatenate([jnp.zeros((ctx_len, HEAD_DIM), F32), jnp.sin(ang)], axis=0)
    reps = LANES // HEAD_DIM
    return jnp.tile(cos, (1, reps)), jnp.tile(sin, (1, reps))


def _rot_cols(w):
    q = HEAD_DIM // 4
    j = np.arange(HEAD_DIM)
    first = (j % (2 * q)) < q
    src = np.where(first, j + q, j - q)
    sign = np.where(first, -1.0, 1.0).astype(np.float32)
    n_heads = w.shape[1] // HEAD_DIM
    src_all = (np.arange(n_heads)[:, None] * HEAD_DIM + src[None, :]).reshape(-1)
    return w[:, src_all] * jnp.asarray(np.tile(sign, n_heads))


def _head_perm_cols(order):
    return (np.asarray(order)[:, None] * HEAD_DIM + np.arange(HEAD_DIM)[None, :]).reshape(-1)


def _inproj_weight(w_in_l):
    aq, ak, av, su, sq, sk, sv = jnp.split(
        w_in_l, np.cumsum([NA_W, NA_W, NA_W, S5_CH, SW_W, SW_KV_W])[:6].tolist(), axis=1)
    sq = sq[:, _head_perm_cols(SW_HEAD_ORDER)]
    return jnp.concatenate([aq, ak, av, sq, _rot_cols(sq), sk, _rot_cols(sk), sv, su], axis=1).astype(MXU_DTYPE)


def _chunk_major(su):
    bsz, s, _ = su.shape
    u = su.reshape(bsz, s // S5_CHUNK, S5_CHUNK, S5_GROUPS, S5_GROUP_CH).transpose(0, 1, 3, 2, 4)
    return u.reshape(bsz, s // S5_CHUNK, S5_GROUPS * S5_CHUNK * S5_GROUP_CH)


def _token_major(y_t):
    bsz, nc, _ = y_t.shape
    y = y_t.reshape(bsz, nc, S5_GROUPS, S5_CHUNK, S5_GROUP_CH).transpose(0, 1, 3, 2, 4)
    return y.reshape(bsz, nc * S5_CHUNK, S5_CH)


def kernel(x, c, ctx, c_ctx, w_mod, b_mod, g_mix, w_in, w_out, na_rpb, s5_a_re, s5_a_im, s5_log_step,
           s5_b_re, s5_b_im, s5_c_re, s5_c_im, s5_d, s5_w_glu, s5_b_glu, sw_sinks, g_ffn, w_router, b_router,
           w_gate_up, b_gate_up, w_down, b_down, g_final):
    bsz, seq, d = x.shape
    ctx_len = ctx.shape[1]
    depth = w_mod.shape[0]
    s = ctx_len + seq
    assert bsz + 1 <= MOD_ROWS and s % ROW_TILE == 0 and ctx_len % MOE_TILE == 0 and seq % MOE_TILE == 0
    assert seq % GRID_W == 0 and ctx_len % S5_CHUNK == 0

    xs = jnp.concatenate([ctx, x], axis=1)
    cond = jnp.zeros((MOD_ROWS, d), F32).at[:bsz].set(c).at[bsz].set(c_ctx)
    mod = _mod_call(cond, w_mod, b_mod).reshape(depth, MOD_ROWS, 6, d)
    cos2, sin2 = _rope_tables(seq, ctx_len)

    n_assign = bsz * s * TOP_K
    n_blocks = -(-(n_assign + N_EXPERTS * (MOE_BLK - 1)) // MOE_BLK)
    sw_rows = _head_perm_cols(SW_HEAD_ORDER)

    for l in range(depth):
        mod_l = mod[l]
        naq, nak, nav, swq, swk, swv, su = _inproj_call(
            xs, mod_l, g_mix[l].reshape(1, d), _inproj_weight(w_in[l]), cos2, sin2, ctx_len)
        ya = _na_call(naq, nak, nav, _na_bias_table(na_rpb[l]), ctx_len)
        yc = _sw_call(sw_sinks[l], swq, swk, swv, ctx_len)
        s5w = _s5_weights(s5_a_re[l], s5_a_im[l], s5_log_step[l], s5_b_re[l], s5_b_im[l], s5_c_re[l], s5_c_im[l])
        ys = _token_major(_s5_call(_chunk_major(su).astype(MXU_DTYPE), s5w, ctx_len // S5_CHUNK))

        wo = w_out[l]
        out_wts = (wo[:NA_W].astype(MXU_DTYPE),
                   wo[NA_W:NA_W + S5_CH].astype(MXU_DTYPE),
                   wo[NA_W + S5_CH:][sw_rows].astype(MXU_DTYPE),
                   s5_w_glu[l].astype(MXU_DTYPE), s5_b_glu[l].reshape(1, S5_CH).astype(F32),
                   s5_d[l].reshape(1, S5_CH).astype(F32), g_ffn[l].reshape(1, d),
                   w_router[l].T.astype(F32), b_router[l].reshape(N_EXPERTS, 1).astype(F32))
        xs, h, topi, topw = _outproj_call(xs, ya, ys, su, yc, mod_l, out_wts, ctx_len)

        dest, blk, ends = _route_call(topi, n_blocks)
        ends = ends[:, 0]
        xg = _dispatch_call(ends, dest, h, n_blocks)
        yg = _moe_call(l, blk[0, :n_blocks], ends[-1:] // MOE_BLK, xg, w_gate_up, b_gate_up, w_down, b_down)
        xs = _combine_call(dest, topw, xs, mod_l, yg, ctx_len)

    return _final_call(xs, g_final.reshape(1, d), ctx_len)
```

```python
import functools
import math

import numpy as np
import jax
import jax.numpy as jnp
from jax import lax
from jax.experimental import pallas as pl
from jax.experimental.pallas import tpu as pltpu

F32 = jnp.float32
MXU_DTYPE = jnp.bfloat16

GRID_W = 64
HEAD_DIM = 64
NA_HEADS = 6
NA_W = NA_HEADS * HEAD_DIM
NA_ROWS = 8
NA_COLS = 16
S5_GROUP_CH = 16
S5_CH = 256
S5_GROUPS = S5_CH // S5_GROUP_CH
S5_STATE = 64
S5_EIG_MAX = -1e-4
SW_HEADS = 6
SW_KV_HEADS = 2
SW_GRP = SW_HEADS // SW_KV_HEADS
SW_W = SW_HEADS * HEAD_DIM
SW_KV_W = SW_KV_HEADS * HEAD_DIM
SW_WINDOW = 128
SW_BLK = 128
ROPE_BASE = 10000.0
N_EXPERTS = 32
TOP_K = 4
MOE_BLK = 512
SWIGLU_LIMIT = 7.0
SWIGLU_ALPHA = 1.702
RMS_EPS = 1e-6
NEG_INF = -1e30

LANES = 128
ROW_TILE = 768
MOE_TILE = 768
FINAL_TILE = 256
S5_CHUNK = 16
NA_QROWS = 4
NA_KROWS = NA_QROWS + NA_ROWS
SW_QBLK = 2 * SW_BLK
SW_KBLK = SW_QBLK + 2 * SW_WINDOW
MOD_ROWS = 8
VMEM_LIMIT = 56 << 20
SUBLANES = 8

SW_HEAD_ORDER = tuple(g * SW_GRP + t for t in range(SW_GRP) for g in range(SW_KV_HEADS))


def _params(*sem):
    return pltpu.CompilerParams(dimension_semantics=sem, vmem_limit_bytes=VMEM_LIMIT)


def _dot(a, b):
    return jnp.dot(a, b, preferred_element_type=F32)


def _dot_nt(a, b):
    return lax.dot_general(a, b, (((1,), (1,)), ((), ())), preferred_element_type=F32)


def _split(a):
    hi = a.astype(MXU_DTYPE)
    lo = (a - hi.astype(F32)).astype(MXU_DTYPE)
    return hi, lo


def _dot3(a, b, nt=False):
    f = _dot_nt if nt else _dot
    ah, al = _split(a)
    bh, bl = _split(b)
    return f(ah, bh) + (f(ah, bl) + f(al, bh))


def _mod_kernel(cond_ref, w_ref, b_ref, o_ref):
    c = cond_ref[...]
    a = c * jax.nn.sigmoid(c)
    o_ref[0] = _dot3(a, w_ref[0]) + b_ref[0]


def _mod_call(cond, w_mod, b_mod):
    depth, d, n = w_mod.shape
    tn = n // 6
    return pl.pallas_call(
        _mod_kernel,
        grid=(depth, n // tn),
        in_specs=[pl.BlockSpec((MOD_ROWS, d), lambda l, j: (0, 0)),
                  pl.BlockSpec((1, d, tn), lambda l, j: (l, 0, j)),
                  pl.BlockSpec((1, 1, tn), lambda l, j: (l, 0, j))],
        out_specs=pl.BlockSpec((1, MOD_ROWS, tn), lambda l, j: (l, 0, j)),
        out_shape=jax.ShapeDtypeStruct((depth, MOD_ROWS, n), F32),
        compiler_params=_params("parallel", "parallel"),
        name="mod",
    )(cond, w_mod, b_mod.reshape(depth, 1, n))


C_AQ = 0
C_AK = C_AQ + NA_W
C_AV = C_AK + NA_W
C_SQ = C_AV + NA_W
C_SQR = C_SQ + SW_W
C_SK = C_SQR + SW_W
C_SKR = C_SK + SW_KV_W
C_SV = C_SKR + SW_KV_W
C_SU = C_SV + SW_KV_W
C_END = C_SU + S5_CH


def _rms_mod(x, g, shift, scale):
    y = x * lax.rsqrt(jnp.mean(x * x, axis=-1, keepdims=True) + RMS_EPS) * g
    return y * (1.0 + scale) + shift


def _mod_vectors(modb_ref, modc_ref, ctx_len, which):
    tm = ROW_TILE
    row = pl.program_id(1) * tm + lax.broadcasted_iota(jnp.int32, (tm, 1), 0)
    is_ctx = row < ctx_len
    mb = modb_ref[0]
    mc = modc_ref[0]
    return [jnp.where(is_ctx, mc[i:i + 1], mb[i:i + 1]) for i in which]


def _inproj_kernel(x_ref, modb_ref, modc_ref, g_ref, w_ref, cos_ref, sin_ref,
                   naq_ref, nak_ref, nav_ref, swq_ref, swk_ref, swv_ref, su_ref, *, ctx_len):
    shift, scale = _mod_vectors(modb_ref, modc_ref, ctx_len, (0, 1))
    h = _rms_mod(x_ref[0], g_ref[...], shift, scale).astype(MXU_DTYPE)
    p = _dot(h, w_ref[...])
    cos = cos_ref[...]
    sin = sin_ref[...]
    cos3 = jnp.concatenate([cos] * (SW_W // LANES), axis=1)
    sin3 = jnp.concatenate([sin] * (SW_W // LANES), axis=1)
    qk_scale = HEAD_DIM ** -0.5
    naq_ref[0] = (p[:, C_AQ:C_AK] * qk_scale).astype(naq_ref.dtype)
    nak_ref[0] = p[:, C_AK:C_AV].astype(nak_ref.dtype)
    nav_ref[0] = p[:, C_AV:C_SQ].astype(nav_ref.dtype)
    swq_ref[0] = ((p[:, C_SQ:C_SQR] * cos3 + p[:, C_SQR:C_SK] * sin3) * qk_scale).astype(swq_ref.dtype)
    swk_ref[0] = (p[:, C_SK:C_SKR] * cos + p[:, C_SKR:C_SV] * sin).astype(swk_ref.dtype)
    swv_ref[0] = p[:, C_SV:C_SU].astype(swv_ref.dtype)
    su_ref[0] = p[:, C_SU:C_END]


def _inproj_call(xs, mod_l, g, w_cat, cos2, sin2, ctx_len):
    bsz, s, d = xs.shape
    tm = ROW_TILE
    row = lambda b, j: (b, j, 0)
    const = lambda b, j: (0, 0)
    widths = (NA_W, NA_W, NA_W, SW_W, SW_KV_W, SW_KV_W, S5_CH)
    dtypes = (MXU_DTYPE,) * 6 + (F32,)
    return pl.pallas_call(
        functools.partial(_inproj_kernel, ctx_len=ctx_len),
        grid=(bsz, s // tm),
        in_specs=[pl.BlockSpec((1, tm, d), row),
                  pl.BlockSpec((1, 6, d), lambda b, j: (b, 0, 0)),
                  pl.BlockSpec((1, 6, d), lambda b, j: (bsz, 0, 0)),
                  pl.BlockSpec((1, d), const),
                  pl.BlockSpec((d, C_END), const),
                  pl.BlockSpec((tm, LANES), lambda b, j: (j, 0)),
                  pl.BlockSpec((tm, LANES), lambda b, j: (j, 0))],
        out_specs=[pl.BlockSpec((1, tm, w), row) for w in widths],
        out_shape=[jax.ShapeDtypeStruct((bsz, s, w), t) for w, t in zip(widths, dtypes)],
        compiler_params=_params("parallel", "parallel"),
        name="inproj",
    )(xs, mod_l, mod_l, g, w_cat, cos2, sin2)


def _half_masks():
    lane = lax.broadcasted_iota(jnp.int32, (1, LANES), 1)
    return lane < HEAD_DIM, lane >= HEAD_DIM


def _na_kernel(q_ref, k_ref, v_ref, bias_ref, o_ref, *, ctx_len, rows):
    i = pl.program_id(1)
    tq = NA_QROWS * GRID_W
    n_ctx_q = ctx_len // tq
    masks = _half_masks()
    nk = NA_KROWS * GRID_W

    def run(local):
        q = q_ref[0]
        if local:
            r0 = (i - n_ctx_q) * NA_QROWS
            start0 = jnp.clip(r0 - NA_ROWS // 2, 0, rows - NA_KROWS)
            start = pl.multiple_of(ctx_len + start0 * GRID_W, GRID_W)
            tab_idx, row_mask = {}, {}
            for a in range(NA_QROWS):
                r = r0 + a
                s_r = jnp.clip(r - NA_ROWS // 2, 0, rows - NA_ROWS)
                for p in range(NA_KROWS // 2):
                    kr = start0 + 2 * p
                    ok_lo = (kr >= s_r) & (kr < s_r + NA_ROWS)
                    ok_hi = (kr + 1 >= s_r) & (kr + 1 < s_r + NA_ROWS)
                    tab_idx[a, p] = jnp.clip(kr - r + NA_ROWS, 0, 2 * NA_ROWS - 1)
                    row_mask[a, p] = jnp.where(masks[0], jnp.where(ok_lo, 0.0, NEG_INF),
                                               jnp.where(ok_hi, 0.0, NEG_INF))
        outs = []
        for t in range(NA_W // LANES):
            sl = slice(LANES * t, LANES * (t + 1))
            qt = q[:, sl]
            zero = jnp.zeros_like(qt)
            qm = jnp.concatenate([jnp.where(masks[0], qt, zero), jnp.where(masks[1], qt, zero)], axis=0)
            kc = k_ref[0, 0:ctx_len, sl]
            vc = v_ref[0, 0:ctx_len, sl]
            s_cx = _dot_nt(qm, kc)
            m = jnp.max(s_cx, axis=-1, keepdims=True)
            if local:
                kw = k_ref[0, pl.ds(start, nk), sl]
                vw = v_ref[0, pl.ds(start, nk), sl]
                bias = jnp.concatenate(
                    [jnp.concatenate([bias_ref[2 * t + hh, pl.ds(tab_idx[a, p], 1)][0] + row_mask[a, p]
                                      for p in range(NA_KROWS // 2)], axis=-1)
                     for hh in range(2) for a in range(NA_QROWS)], axis=0)
                s_nb = _dot_nt(qm, kw) + bias
                m = jnp.maximum(m, jnp.max(s_nb, axis=-1, keepdims=True))
                p_nb = jnp.exp(s_nb - m)
            p_cx = jnp.exp(s_cx - m)
            den = jnp.sum(p_cx, axis=-1, keepdims=True)
            o = _dot(p_cx.astype(MXU_DTYPE), vc)
            if local:
                den = den + jnp.sum(p_nb, axis=-1, keepdims=True)
                o = o + _dot(p_nb.astype(MXU_DTYPE), vw)
            o = o / den
            outs.append(jnp.where(masks[0], o[:tq], o[tq:]))
        o_ref[0] = jnp.concatenate(outs, axis=-1).astype(o_ref.dtype)

    @pl.when(i < n_ctx_q)
    def _():
        run(False)

    @pl.when(i >= n_ctx_q)
    def _():
        run(True)


def _na_call(q, k, v, bias_tab, ctx_len):
    bsz, s, w = q.shape
    rows = (s - ctx_len) // GRID_W
    tq = NA_QROWS * GRID_W
    assert rows >= NA_KROWS and rows % NA_QROWS == 0 and ctx_len % tq == 0
    whole = lambda b, i: (b, 0, 0)
    return pl.pallas_call(
        functools.partial(_na_kernel, ctx_len=ctx_len, rows=rows),
        grid=(bsz, s // tq),
        in_specs=[pl.BlockSpec((1, tq, w), lambda b, i: (b, i, 0)),
                  pl.BlockSpec((1, s, w), whole),
                  pl.BlockSpec((1, s, w), whole),
                  pl.BlockSpec(bias_tab.shape, lambda b, i: (0, 0, 0, 0))],
        out_specs=pl.BlockSpec((1, tq, w), lambda b, i: (b, i, 0)),
        out_shape=jax.ShapeDtypeStruct((bsz, s, w), q.dtype),
        compiler_params=_params("parallel", "arbitrary"),
        name="na_attn",
    )(q, k, v, bias_tab)


def _na_bias_table(rpb):
    qcol = np.arange(GRID_W)[:, None]
    kcol = np.arange(GRID_W)[None, :]
    ws = np.clip(qcol - NA_COLS // 2, 0, GRID_W - NA_COLS)
    valid = (kcol >= ws) & (kcol < ws + NA_COLS)
    dc = np.clip(kcol - qcol + NA_COLS - 1, 0, 2 * NA_COLS - 2)
    full = jnp.where(valid[None, None], rpb[:, :, dc].astype(F32), NEG_INF)
    edge = jnp.full_like(full[:, :1], NEG_INF)
    full = jnp.concatenate([edge, full, edge], axis=1)
    return jnp.concatenate([full[:, :-1], full[:, 1:]], axis=-1)


def _sw_kernel(sink_ref, q_ref, k_ref, v_ref, o_ref, *, ctx_len, seq):
    i = pl.program_id(1)
    tq = SW_QBLK
    n_ctx_q = ctx_len // tq
    masks = _half_masks()
    nk = SW_KBLK
    first_head = lax.broadcasted_iota(jnp.int32, (2 * tq, 1), 0) < tq

    def run(local):
        q = q_ref[0]
        kc = k_ref[0, 0:ctx_len, :]
        vc = v_ref[0, 0:ctx_len, :]
        if local:
            n = i - n_ctx_q
            start_lat = jnp.clip(n * tq - SW_WINDOW, 0, seq - nk)
            start = pl.multiple_of(ctx_len + start_lat, SW_BLK)
            kw = k_ref[0, pl.ds(start, nk), :]
            vw = v_ref[0, pl.ds(start, nk), :]
            row = lax.broadcasted_iota(jnp.int32, (2 * tq, 1), 0)
            qpos = n * tq + jnp.where(first_head, row, row - tq)
            kpos = start_lat + lax.broadcasted_iota(jnp.int32, (1, nk), 1)
            valid = jnp.abs(qpos - kpos) <= SW_WINDOW
        outs = []
        for t in range(SW_W // LANES):
            qt = q[:, LANES * t:LANES * (t + 1)]
            zero = jnp.zeros_like(qt)
            qm = jnp.concatenate([jnp.where(masks[0], qt, zero), jnp.where(masks[1], qt, zero)], axis=0)
            sink = jnp.where(first_head, sink_ref[SW_HEAD_ORDER[2 * t]], sink_ref[SW_HEAD_ORDER[2 * t + 1]])
            s_cx = _dot_nt(qm, kc)
            m = jnp.maximum(jnp.max(s_cx, axis=-1, keepdims=True), sink)
            if local:
                s_loc = jnp.where(valid, _dot_nt(qm, kw), NEG_INF)
                m = jnp.maximum(m, jnp.max(s_loc, axis=-1, keepdims=True))
                p_loc = jnp.exp(s_loc - m)
            p_cx = jnp.exp(s_cx - m)
            den = jnp.sum(p_cx, axis=-1, keepdims=True) + jnp.exp(sink - m)
            o = _dot(p_cx.astype(MXU_DTYPE), vc)
            if local:
                den = den + jnp.sum(p_loc, axis=-1, keepdims=True)
                o = o + _dot(p_loc.astype(MXU_DTYPE), vw)
            o = o / den
            outs.append(jnp.where(masks[0], o[:tq], o[tq:]))
        o_ref[0] = jnp.concatenate(outs, axis=-1).astype(o_ref.dtype)

    @pl.when(i < n_ctx_q)
    def _():
        run(False)

    @pl.when(i >= n_ctx_q)
    def _():
        run(True)


def _sw_call(sinks, q, k, v, ctx_len):
    bsz, s, w = q.shape
    seq = s - ctx_len
    assert seq >= SW_KBLK and seq % SW_QBLK == 0 and ctx_len % SW_QBLK == 0
    whole = lambda b, i: (b, 0, 0)
    return pl.pallas_call(
        functools.partial(_sw_kernel, ctx_len=ctx_len, seq=seq),
        grid=(bsz, s // SW_QBLK),
        in_specs=[pl.BlockSpec(memory_space=pltpu.SMEM),
                  pl.BlockSpec((1, SW_QBLK, w), lambda b, i: (b, i, 0)),
                  pl.BlockSpec((1, s, SW_KV_W), whole),
                  pl.BlockSpec((1, s, SW_KV_W), whole)],
        out_specs=pl.BlockSpec((1, SW_QBLK, w), lambda b, i: (b, i, 0)),
        out_shape=jax.ShapeDtypeStruct((bsz, s, w), q.dtype),
        compiler_params=_params("parallel", "arbitrary"),
        name="sw_attn",
    )(sinks.astype(F32), q, k, v)


def _s5_kernel(u_ref, m_ref, wsr_ref, wsi_ref, wor_ref, woi_ref, lr_ref, li_ref, o_ref,
               sre, sim, xre, xim, acc, *, n_ctx_chunks):
    d = pl.program_id(1)
    bsz, nc, _ = u_ref.shape
    for b in range(bsz):
        ub = u_ref[b]
        sre[b] = _dot(ub, wsr_ref[0, 0])
        sim[b] = _dot(ub, wsi_ref[0, 0])
    lr = lr_ref[0, 0]
    li = li_ref[0, 0]

    def step(c, carry):
        new = []
        for b in range(bsz):
            xr, xi = carry[2 * b], carry[2 * b + 1]
            xre[b, pl.ds(c, 1), :] = xr
            xim[b, pl.ds(c, 1), :] = xi
            sr = sre[b, pl.ds(c, 1), :]
            si = sim[b, pl.ds(c, 1), :]
            new.append(lr * xr - li * xi + sr)
            new.append(lr * xi + li * xr + si)
        return tuple(new)

    zero = tuple(jnp.zeros((1, LANES), F32) for _ in range(2 * bsz))

    @pl.when(d == 0)
    def _():
        lax.fori_loop(0, nc, step, zero)

    @pl.when(d == 1)
    def _():
        carry = lax.fori_loop(0, n_ctx_chunks, lambda k, cr: step(n_ctx_chunks - 1 - k, cr), zero)
        lax.fori_loop(0, nc - n_ctx_chunks, lambda k, cr: step(nc - 1 - k, cr), carry)

    half = u_ref.shape[2] // 2
    for b in range(bsz):
        ub = u_ref[b]
        y_intra = jnp.concatenate([_dot(ub[:, :half], m_ref[0, 0]), _dot(ub[:, half:], m_ref[0, 1])], axis=-1)
        y = (y_intra + _dot(xre[b].astype(MXU_DTYPE), wor_ref[0, 0])
             + _dot(xim[b].astype(MXU_DTYPE), woi_ref[0, 0]))

        @pl.when(d == 0)
        def _():
            acc[b] = y

        @pl.when(d == 1)
        def _():
            o_ref[b] = (acc[b] + y).astype(o_ref.dtype)


def _s5_call(u_t, wts, n_ctx_chunks):
    m, wsr, wsi, wor, woi, lr, li = wts
    bsz, nc, width = u_t.shape
    pw = 2 * S5_CHUNK * S5_GROUP_CH
    n_pairs = width // pw
    blk = lambda shp: pl.BlockSpec((1, 1) + shp, lambda j, d: (d, j, 0, 0))
    return pl.pallas_call(
        functools.partial(_s5_kernel, n_ctx_chunks=n_ctx_chunks),
        grid=(n_pairs, 2),
        in_specs=[pl.BlockSpec((bsz, nc, pw), lambda j, d: (0, 0, j)),
                  pl.BlockSpec((1, 2, pw // 2, pw // 2), lambda j, d: (d, j, 0, 0)),
                  blk((pw, LANES)), blk((pw, LANES)), blk((LANES, pw)), blk((LANES, pw)),
                  blk((1, LANES)), blk((1, LANES))],
        out_specs=pl.BlockSpec((bsz, nc, pw), lambda j, d: (0, 0, j)),
        out_shape=jax.ShapeDtypeStruct((bsz, nc, width), MXU_DTYPE),
        scratch_shapes=[pltpu.VMEM((bsz, nc, LANES), F32) for _ in range(4)] + [pltpu.VMEM((bsz, nc, pw), F32)],
        compiler_params=_params("parallel", "arbitrary"),
        name="s5_scan",
    )(u_t, m, wsr, wsi, wor, woi, lr, li)


def _s5_weights(a_re, a_im, log_step, b_re, b_im, c_re, c_im):
    lc, g, p, h = S5_CHUNK, S5_GROUPS, S5_STATE, S5_GROUP_CH
    lam = lax.complex(jnp.minimum(a_re.astype(F32), S5_EIG_MAX), a_im.astype(F32))
    step = jnp.exp(log_step.astype(F32))[..., None]
    lam_bar = jnp.exp(lam * step)
    b_bar = ((lam_bar - 1.0) / lam)[..., None] * lax.complex(b_re.astype(F32), b_im.astype(F32))
    cc = lax.complex(c_re.astype(F32), c_im.astype(F32))
    dd = jnp.arange(lc + 1, dtype=F32)
    pw = jnp.exp((lam * step)[..., None] * dd)
    kern = jnp.real(jnp.einsum('zgop,zgpd,zgpi->zgdoi', cc, pw[..., :lc], b_bar))
    jj = np.arange(lc)[:, None]
    ii = np.arange(lc)[None, :]
    mats, wst, wout = [], [], []
    for z in range(2):
        lag = (ii - jj) if z == 0 else (jj - ii)
        ok = lag >= 0
        kz = kern[z][:, np.where(ok, lag, 0)]
        kz = jnp.where(ok[None, :, :, None, None], kz, 0.0)
        mats.append(kz.transpose(0, 1, 4, 2, 3).reshape(g, lc * h, lc * h))
        d_state = (lc - 1 - np.arange(lc)) if z == 0 else np.arange(lc)
        ws = pw[z][:, :, d_state][..., None] * b_bar[z][:, :, None, :]
        wst.append(ws.transpose(0, 2, 3, 1).reshape(g, lc * h, p))
        d_out = (np.arange(lc) + 1) if z == 0 else (lc - np.arange(lc))
        wo = cc[z][:, :, :, None] * pw[z][:, None, :, :][..., d_out]
        wout.append(wo.transpose(0, 2, 3, 1).reshape(g, p, lc * h))
    mats = jnp.stack(mats)
    wst = jnp.stack(wst)
    wout = jnp.stack(wout)

    def pair_rows(w):
        w = w.reshape(2, g // 2, 2, lc * h, p)
        z0 = jnp.zeros_like(w[:, :, 0])
        top = jnp.concatenate([w[:, :, 0], z0], axis=-1)
        bot = jnp.concatenate([z0, w[:, :, 1]], axis=-1)
        return jnp.concatenate([top, bot], axis=-2)

    def pair_cols(w):
        w = w.reshape(2, g // 2, 2, p, lc * h)
        z0 = jnp.zeros_like(w[:, :, 0])
        top = jnp.concatenate([w[:, :, 0], z0], axis=-1)
        bot = jnp.concatenate([z0, w[:, :, 1]], axis=-1)
        return jnp.concatenate([top, bot], axis=-2)

    lam_c = pw[..., lc].reshape(2, g // 2, 1, 2 * p)
    cast = lambda w: w.astype(MXU_DTYPE)
    return (cast(mats), cast(pair_rows(jnp.real(wst))), cast(pair_rows(jnp.imag(wst))),
            cast(pair_cols(jnp.real(wout))), cast(pair_cols(-jnp.imag(wout))),
            jnp.real(lam_c), jnp.imag(lam_c))


def _gelu_tanh(x):
    cdf = 0.5 * (1.0 + jnp.tanh(math.sqrt(2.0 / math.pi) * (x + 0.044715 * (x * x * x))))
    return x * cdf


def _outproj_kernel(x_ref, ya_ref, ys_ref, su_ref, yc_ref, modb_ref, modc_ref, woa_ref, wob_ref, woc_ref,
                    wglu_ref, bglu_ref, dsk_ref, g_ref, wr_ref, br_ref,
                    xo_ref, h_ref, topi_ref, topw_ref, rank_ref, cnt_ref, carry, *, ctx_len):
    gate, shift, scale = _mod_vectors(modb_ref, modc_ref, ctx_len, (2, 3, 4))
    y = dsk_ref[...] * su_ref[0] + ys_ref[0].astype(F32)
    gl = _gelu_tanh(y)
    yb = gl * jax.nn.sigmoid(_dot(gl.astype(MXU_DTYPE), wglu_ref[...]) + bglu_ref[...])
    mix = (_dot(ya_ref[0], woa_ref[...]) + _dot(yb.astype(MXU_DTYPE), wob_ref[...])
           + _dot(yc_ref[0], woc_ref[...]))
    x = x_ref[0] + gate * mix
    xo_ref[0] = x
    h = _rms_mod(x, g_ref[...], shift, scale)
    dl = h_ref.shape[-1]
    for sl in range(SUBLANES):
        h_ref[0, :, sl, :] = h[:, dl * sl:dl * (sl + 1)]
    logits = _dot3(wr_ref[...], h, nt=True) + br_ref[...]
    n_e, tm = logits.shape
    e_iota = lax.broadcasted_iota(jnp.int32, (n_e, tm), 0)
    vals, idxs = [], []
    for _ in range(TOP_K):
        mx = jnp.max(logits, axis=0, keepdims=True)
        ix = jnp.min(jnp.where(logits == mx, e_iota, n_e), axis=0, keepdims=True)
        vals.append(mx)
        idxs.append(ix)
        logits = jnp.where(e_iota == ix, -jnp.inf, logits)
    ex = [jnp.exp(v - vals[0]) for v in vals]
    den = ex[0] + ex[1] + ex[2] + ex[3]
    topi_ref[0] = jnp.concatenate(idxs, axis=0)
    topw_ref[0] = jnp.concatenate([e / den for e in ex], axis=0)

    @pl.when((pl.program_id(0) == 0) & (pl.program_id(1) == 0))
    def _():
        carry[...] = jnp.zeros_like(carry)

    sel = [ix == e_iota for ix in idxs]
    onehot = jnp.where(sel[0] | sel[1] | sel[2] | sel[3], 1.0, 0.0)
    before = (lax.broadcasted_iota(jnp.int32, (tm, tm), 0) < lax.broadcasted_iota(jnp.int32, (tm, tm), 1))
    pfx = _dot(onehot.astype(MXU_DTYPE), jnp.where(before, 1.0, 0.0).astype(MXU_DTYPE)) + carry[:, 0:1]
    rank_ref[0] = jnp.concatenate(
        [jnp.sum(jnp.where(sel[k], pfx, 0.0), axis=0, keepdims=True) for k in range(TOP_K)], axis=0)
    carry[...] = carry[...] + jnp.sum(onehot, axis=1, keepdims=True)
    cnt_ref[...] = carry[...]


def _outproj_call(xs, ya, ys, su, yc, mod_l, wts, ctx_len):
    bsz, s, d = xs.shape
    tm = ROW_TILE
    row = lambda b, j: (b, j, 0)
    const = lambda b, j: (0, 0)
    full = lambda a: pl.BlockSpec(a.shape, const)
    return pl.pallas_call(
        functools.partial(_outproj_kernel, ctx_len=ctx_len),
        grid=(bsz, s // tm),
        in_specs=[pl.BlockSpec((1, tm, d), row),
                  pl.BlockSpec((1, tm, NA_W), row),
                  pl.BlockSpec((1, tm, S5_CH), row),
                  pl.BlockSpec((1, tm, S5_CH), row),
                  pl.BlockSpec((1, tm, SW_W), row),
                  pl.BlockSpec((1, 6, d), lambda b, j: (b, 0, 0)),
                  pl.BlockSpec((1, 6, d), lambda b, j: (bsz, 0, 0))] + [full(a) for a in wts],
        out_specs=[pl.BlockSpec((1, tm, d), row),
                   pl.BlockSpec((1, tm, SUBLANES, d // SUBLANES), lambda b, j: (b, j, 0, 0)),
                   pl.BlockSpec((1, TOP_K, tm), lambda b, j: (b, 0, j)),
                   pl.BlockSpec((1, TOP_K, tm), lambda b, j: (b, 0, j)),
                   pl.BlockSpec((1, TOP_K, tm), lambda b, j: (b, 0, j)),
                   pl.BlockSpec((N_EXPERTS, LANES), lambda b, j: (0, 0))],
        out_shape=[jax.ShapeDtypeStruct((bsz, s, d), F32),
                   jax.ShapeDtypeStruct((bsz, s, SUBLANES, d // SUBLANES), F32),
                   jax.ShapeDtypeStruct((bsz, TOP_K, s), jnp.int32),
                   jax.ShapeDtypeStruct((bsz, TOP_K, s), F32),
                   jax.ShapeDtypeStruct((bsz, TOP_K, s), F32),
                   jax.ShapeDtypeStruct((N_EXPERTS, LANES), F32)],
        scratch_shapes=[pltpu.VMEM((N_EXPERTS, LANES), F32)],
        compiler_params=_params("arbitrary", "arbitrary"),
        name="outproj",
    )(xs, ya, ys, su, yc, mod_l, mod_l, *wts)


def _route_kernel(topi_ref, rank_ref, cnt_ref, dest_ref, blk_ref, ends_ref):
    idx = topi_ref[0]
    tm = idx.shape[1]
    e_iota = lax.broadcasted_iota(jnp.int32, (N_EXPERTS, tm), 0)
    counts = cnt_ref[:, 0:1]
    padded = jnp.ceil(counts * (1.0 / MOE_BLK)) * MOE_BLK
    r_i = lax.broadcasted_iota(jnp.int32, (N_EXPERTS, N_EXPERTS), 0)
    c_i = lax.broadcasted_iota(jnp.int32, (N_EXPERTS, N_EXPERTS), 1)
    padded_row = jnp.sum(jnp.where(r_i == c_i, padded, 0.0), axis=0, keepdims=True)
    pstart = jnp.sum(jnp.where(c_i < r_i, padded_row, 0.0), axis=1, keepdims=True)
    ends = pstart + padded
    rk = rank_ref[0]
    dest_ref[0] = jnp.concatenate(
        [jnp.sum(jnp.where(idx[k:k + 1] == e_iota, pstart, 0.0), axis=0, keepdims=True) + rk[k:k + 1]
         for k in range(TOP_K)], axis=1).astype(jnp.int32)
    nb = blk_ref.shape[1]
    blk_start = (lax.broadcasted_iota(jnp.int32, (N_EXPERTS, nb), 1) * MOE_BLK).astype(F32)
    owner = jnp.sum(jnp.where(ends <= blk_start, 1.0, 0.0), axis=0, keepdims=True)
    blk_ref[...] = jnp.minimum(owner, N_EXPERTS - 1.0).astype(jnp.int32)
    ends_ref[...] = jnp.broadcast_to(ends, ends_ref.shape).astype(jnp.int32)


def _route_call(topi, rank, cnt, n_blocks):
    bsz, _, s = topi.shape
    tm = MOE_TILE
    n_tiles = s // tm
    nb_pad = -(-n_blocks // LANES) * LANES
    tok = pl.BlockSpec((1, TOP_K, tm), lambda b, j: (b, 0, j))
    return pl.pallas_call(
        _route_kernel,
        grid=(bsz, n_tiles),
        in_specs=[tok, tok, pl.BlockSpec((N_EXPERTS, LANES), lambda b, j: (0, 0))],
        out_specs=[pl.BlockSpec((1, 1, TOP_K * tm), lambda b, j: (b * n_tiles + j, 0, 0)),
                   pl.BlockSpec((1, nb_pad), lambda b, j: (0, 0)),
                   pl.BlockSpec((N_EXPERTS, LANES), lambda b, j: (0, 0))],
        out_shape=[jax.ShapeDtypeStruct((bsz * n_tiles, 1, TOP_K * tm), jnp.int32),
                   jax.ShapeDtypeStruct((1, nb_pad), jnp.int32),
                   jax.ShapeDtypeStruct((N_EXPERTS, LANES), jnp.int32)],
        compiler_params=_params("arbitrary", "arbitrary"),
        name="route_dest",
    )(topi, rank, cnt)


def _dispatch_kernel(ends_ref, dest_ref, h_ref, xg_ref, zbuf, stage, sem, zsem, *, n_blocks):
    tm = h_ref.shape[1]

    @pl.when((pl.program_id(0) == 0) & (pl.program_id(1) == 0))
    def _():
        zbuf[...] = jnp.zeros_like(zbuf)

        def fill(row):
            return pltpu.make_async_copy(zbuf, xg_ref.at[pl.ds(pl.multiple_of(row, MOE_BLK), MOE_BLK)], zsem)

        def each(fn):
            for e in range(N_EXPERTS):
                begin = ends_ref[e - 1] if e else 0

                @pl.when(ends_ref[e] > begin)
                def _():
                    fn(fill(ends_ref[e] - MOE_BLK))

            def dead(i, c):
                fn(fill(i * MOE_BLK))
                return c

            lax.fori_loop(ends_ref[N_EXPERTS - 1] // MOE_BLK, n_blocks, dead, 0)

        each(lambda cp: cp.start())
        each(lambda cp: cp.wait())

    step = pl.program_id(0) * pl.num_programs(1) + pl.program_id(1)
    n_steps = pl.num_programs(0) * pl.num_programs(1)
    slot = lax.rem(step, 2)
    stage[slot] = h_ref[0]

    def body(t, c):
        for k in range(TOP_K):
            pltpu.make_async_copy(stage.at[slot, t], xg_ref.at[dest_ref[0, 0, k * tm + t]],
                                  sem.at[slot]).start(priority=k % 2)
        return c

    lax.fori_loop(0, tm, body, 0, unroll=8)

    def wait_tile(sl):
        pltpu.make_async_copy(xg_ref.at[pl.ds(0, TOP_K * tm)], xg_ref.at[pl.ds(0, TOP_K * tm)], sem.at[sl]).wait()

    @pl.when(step > 0)
    def _():
        wait_tile(1 - slot)

    @pl.when(step == n_steps - 1)
    def _():
        wait_tile(slot)


def _dispatch_call(ends, dest, h, n_blocks):
    bsz, s, _, dl = h.shape
    tm = MOE_TILE
    n_tiles = s // tm
    grid_spec = pltpu.PrefetchScalarGridSpec(
        num_scalar_prefetch=1,
        grid=(bsz, n_tiles),
        in_specs=[pl.BlockSpec((1, 1, TOP_K * tm), lambda b, j, en: (b * n_tiles + j, 0, 0), memory_space=pltpu.SMEM),
                  pl.BlockSpec((1, tm, SUBLANES, dl), lambda b, j, en: (b, j, 0, 0))],
        out_specs=pl.BlockSpec(memory_space=pl.ANY),
        scratch_shapes=[pltpu.VMEM((MOE_BLK, SUBLANES, dl), h.dtype), pltpu.VMEM((2, tm, SUBLANES, dl), h.dtype),
                        pltpu.SemaphoreType.DMA((2,)),
                        pltpu.SemaphoreType.DMA(())],
    )
    return pl.pallas_call(
        functools.partial(_dispatch_kernel, n_blocks=n_blocks),
        grid_spec=grid_spec,
        out_shape=jax.ShapeDtypeStruct((n_blocks * MOE_BLK, SUBLANES, dl), h.dtype),
        compiler_params=_params("arbitrary", "arbitrary"),
        name="moe_dispatch",
    )(ends, dest, h)


def _moe_kernel(blk_exp_ref, nact_ref, x_hbm, wgu_ref, bgu_ref, wd_ref, bd_ref, y_hbm, wgu_c, wd_c,
                xbuf, xsem, ybuf, ysem):
    i = pl.program_id(0)
    slot = lax.rem(i, 2)
    dl = x_hbm.shape[-1]

    def fetch(blk, sl_):
        for s in range(SUBLANES):
            pltpu.make_async_copy(x_hbm.at[pl.ds(blk * MOE_BLK, MOE_BLK), s, :],
                                  xbuf.at[sl_, :, pl.ds(dl * s, dl)], xsem.at[sl_]).start()

    @pl.when(i == 0)
    def _():
        fetch(0, 0)

    @pl.when(i + 1 < pl.num_programs(0))
    def _():
        fetch(i + 1, 1 - slot)

    def y_wait(sl_):
        pltpu.make_async_copy(ybuf.at[sl_], ybuf.at[sl_], ysem.at[sl_]).wait()

    @pl.when(i >= 2)
    def _():
        y_wait(slot)

    e = blk_exp_ref[i]
    prev = blk_exp_ref[jnp.maximum(i - 1, 0)]
    d, f2 = wgu_c.shape
    f = f2 // 2
    rows = 128

    @pl.when((i == 0) | (e != prev))
    def _():
        def cv(r, c):
            sl = pl.ds(pl.multiple_of(r * rows, rows), rows)
            wgu_c[sl, :] = wgu_ref[0, 0, sl, :].astype(wgu_c.dtype)
            return c
        lax.fori_loop(0, d // rows, cv, 0)

        def cv2(r, c):
            sl = pl.ds(pl.multiple_of(r * rows, rows), rows)
            wd_c[sl, :] = wd_ref[0, 0, sl, :].astype(wd_c.dtype)
            return c
        lax.fori_loop(0, f // rows, cv2, 0)

    pltpu.make_async_copy(xbuf.at[slot], xbuf.at[slot], xsem.at[slot]).wait()

    @pl.when(i < nact_ref[0])
    def _():
        gu = _dot(xbuf[slot].astype(MXU_DTYPE), wgu_c[...]) + bgu_ref[0, 0]
        gate = jnp.minimum(gu[:, :f], SWIGLU_LIMIT)
        up = jnp.clip(gu[:, f:], -SWIGLU_LIMIT, SWIGLU_LIMIT)
        act = gate * jax.nn.sigmoid(SWIGLU_ALPHA * gate) * (up + 1.0)
        ybuf[slot] = _dot(act.astype(MXU_DTYPE), wd_c[...]) + bd_ref[0, 0]

    @pl.when(i >= nact_ref[0])
    def _():
        ybuf[slot] = jnp.zeros(ybuf.shape[1:], ybuf.dtype)

    for s in range(SUBLANES):
        pltpu.make_async_copy(ybuf.at[slot, :, pl.ds(dl * s, dl)],
                              y_hbm.at[pl.ds(i * MOE_BLK, MOE_BLK), s, :], ysem.at[slot]).start()

    @pl.when(i == pl.num_programs(0) - 1)
    def _():
        y_wait(1 - slot)
        y_wait(slot)


def _moe_call(layer, blk_exp, n_active, xg, w_gate_up, b_gate_up, w_down, b_down):
    n_rows, _, dl = xg.shape
    depth, n_e, d, f2 = w_gate_up.shape
    f = f2 // 2
    n_blocks = n_rows // MOE_BLK
    assert n_blocks >= 2
    wsel = lambda i, be, na: (layer, be[i], 0, 0)
    grid_spec = pltpu.PrefetchScalarGridSpec(
        num_scalar_prefetch=2,
        grid=(n_blocks,),
        in_specs=[pl.BlockSpec(memory_space=pl.ANY),
                  pl.BlockSpec((1, 1, d, f2), wsel),
                  pl.BlockSpec((1, 1, 1, f2), wsel),
                  pl.BlockSpec((1, 1, f, d), wsel),
                  pl.BlockSpec((1, 1, 1, d), wsel)],
        out_specs=pl.BlockSpec(memory_space=pl.ANY),
        scratch_shapes=[pltpu.VMEM((d, f2), MXU_DTYPE), pltpu.VMEM((f, d), MXU_DTYPE),
                        pltpu.VMEM((2, MOE_BLK, d), F32), pltpu.SemaphoreType.DMA((2,)),
                        pltpu.VMEM((2, MOE_BLK, d), F32), pltpu.SemaphoreType.DMA((2,))],
    )
    return pl.pallas_call(
        _moe_kernel,
        grid_spec=grid_spec,
        out_shape=jax.ShapeDtypeStruct((n_rows, SUBLANES, dl), F32),
        compiler_params=_params("arbitrary"),
        name="moe_experts",
    )(blk_exp, n_active, xg, w_gate_up, b_gate_up.reshape(depth, n_e, 1, f2),
      w_down, b_down.reshape(depth, n_e, 1, d))


def _combine_kernel(dest_ref, dnext_ref, w_ref, gate_ref, x_hbm, yg_hbm, xo_hbm,
                    gbuf, xbuf, obuf, gsem, xsem, osem, *, n_tiles, ctx_len, ctx_row):
    tm = MOE_TILE
    dl = gbuf.shape[-1]
    step = pl.program_id(0) * n_tiles + pl.program_id(1)
    n_steps = pl.num_programs(0) * n_tiles
    slot = lax.rem(step, 2)

    def stream_copies(st, sl_, buf, hbm, sem, to_hbm):
        b = lax.div(st, n_tiles)
        r0 = lax.rem(st, n_tiles) * tm
        out = []
        for s in range(SUBLANES):
            rows = hbm.at[b, pl.ds(r0, tm), pl.ds(dl * s, dl)]
            tiles = buf.at[sl_, :, s, :]
            out.append(pltpu.make_async_copy(tiles, rows, sem.at[sl_]) if to_hbm
                       else pltpu.make_async_copy(rows, tiles, sem.at[sl_]))
        return out

    def fetch(dref, st, sl_):
        for cp in stream_copies(st, sl_, xbuf, x_hbm, xsem, False):
            cp.start()

        def body(t, c):
            for k in range(TOP_K):
                pltpu.make_async_copy(yg_hbm.at[dref[0, 0, k * tm + t]], gbuf.at[sl_, k, t],
                                      gsem.at[sl_]).start(priority=k % 2)
            return c

        lax.fori_loop(0, tm, body, 0, unroll=8)

    @pl.when(step == 0)
    def _():
        fetch(dest_ref, 0, 0)

    @pl.when(step + 1 < n_steps)
    def _():
        fetch(dnext_ref, step + 1, 1 - slot)

    def wait_all(buf, sem, sl_):
        pltpu.make_async_copy(buf.at[sl_], buf.at[sl_], sem.at[sl_]).wait()

    wait_all(gbuf, gsem, slot)
    wait_all(xbuf, xsem, slot)

    @pl.when(step >= 2)
    def _():
        wait_all(obuf, osem, slot)

    gate_lat = gate_ref[pl.program_id(0)]
    gate_ctx = gate_ref[ctx_row]
    r0 = pl.program_id(1) * tm

    def row(t, c):
        acc = gbuf[slot, 0, t] * w_ref[0, 0, t]
        for k in range(1, TOP_K):
            acc = acc + gbuf[slot, k, t] * w_ref[0, 0, k * tm + t]
        obuf[slot, t] = xbuf[slot, t] + jnp.where(r0 + t < ctx_len, gate_ctx, gate_lat) * acc
        return c

    lax.fori_loop(0, tm, row, 0, unroll=8)
    for cp in stream_copies(step, slot, obuf, xo_hbm, osem, True):
        cp.start()

    @pl.when(step == n_steps - 1)
    def _():
        wait_all(obuf, osem, 1 - slot)
        wait_all(obuf, osem, slot)


def _combine_call(dest, topw, xs, mod_l, yg, ctx_len):
    bsz, s, d = xs.shape
    tm = MOE_TILE
    n_tiles = s // tm
    last = bsz * n_tiles - 1
    assert last >= 1
    dl = d // SUBLANES
    w_flat = topw.reshape(bsz, TOP_K, n_tiles, tm).transpose(0, 2, 1, 3).reshape(bsz * n_tiles, 1, TOP_K * tm)
    gate = mod_l[:, 5].reshape(MOD_ROWS, SUBLANES, dl)
    idx_spec = lambda ahead: pl.BlockSpec(
        (1, 1, TOP_K * tm), lambda b, j: (jnp.minimum(b * n_tiles + j + ahead, last), 0, 0), memory_space=pltpu.SMEM)
    tile_buf = lambda lead: pltpu.VMEM(lead + (tm, SUBLANES, dl), F32)
    return pl.pallas_call(
        functools.partial(_combine_kernel, n_tiles=n_tiles, ctx_len=ctx_len, ctx_row=bsz),
        grid=(bsz, n_tiles),
        in_specs=[idx_spec(0), idx_spec(1), idx_spec(0),
                  pl.BlockSpec((MOD_ROWS, SUBLANES, dl), lambda b, j: (0, 0, 0)),
                  pl.BlockSpec(memory_space=pl.ANY),
                  pl.BlockSpec(memory_space=pl.ANY)],
        out_specs=pl.BlockSpec(memory_space=pl.ANY),
        out_shape=jax.ShapeDtypeStruct((bsz, s, d), F32),
        scratch_shapes=[tile_buf((2, TOP_K)), tile_buf((2,)), tile_buf((2,)),
                        pltpu.SemaphoreType.DMA((2,)), pltpu.SemaphoreType.DMA((2,)), pltpu.SemaphoreType.DMA((2,))],
        compiler_params=_params("arbitrary", "arbitrary"),
        name="moe_combine",
    )(dest, dest, w_flat, gate, xs, yg)


def _final_kernel(x_ref, g_ref, o_ref):
    x = x_ref[0]
    o_ref[0] = x * lax.rsqrt(jnp.mean(x * x, axis=-1, keepdims=True) + RMS_EPS) * g_ref[...]


def _final_call(xs, g, ctx_len):
    bsz, s, d = xs.shape
    tm = FINAL_TILE
    off = ctx_len // tm
    return pl.pallas_call(
        _final_kernel,
        grid=(bsz, (s - ctx_len) // tm),
        in_specs=[pl.BlockSpec((1, tm, d), lambda b, j: (b, j + off, 0)),
                  pl.BlockSpec((1, d), lambda b, j: (0, 0))],
        out_specs=pl.BlockSpec((1, tm, d), lambda b, j: (b, j, 0)),
        out_shape=jax.ShapeDtypeStruct((bsz, s - ctx_len, d), F32),
        compiler_params=_params("parallel", "parallel"),
        name="final_norm",
    )(xs, g)


def _rope_tables(seq, ctx_len):
    t = jnp.arange(seq)
    row = (t // GRID_W).astype(F32)
    col = (t % GRID_W).astype(F32)
    nf = HEAD_DIM // 4
    inv = ROPE_BASE ** (-jnp.arange(nf, dtype=F32) / nf)
    ar = row[:, None] * inv
    ac = col[:, None] * inv
    ang = jnp.concatenate([ar, ar, ac, ac], axis=-1)
    cos = jnp.concatenate([jnp.ones((ctx_len, HEAD_DIM), F32), jnp.cos(ang)], axis=0)
    sin = jnp.concatenate([jnp.zeros((ctx_len, HEAD_DIM), F32), jnp.sin(ang)], axis=0)
    reps = LANES // HEAD_DIM
    return jnp.tile(cos, (1, reps)), jnp.tile(sin, (1, reps))


def _rot_cols(w):
    q = HEAD_DIM // 4
    j = np.arange(HEAD_DIM)
    first = (j % (2 * q)) < q
    src = np.where(first, j + q, j - q)
    sign = np.where(first, -1.0, 1.0).astype(np.float32)
    n_heads = w.shape[1] // HEAD_DIM
    src_all = (np.arange(n_heads)[:, None] * HEAD_DIM + src[None, :]).reshape(-1)
    return w[:, src_all] * jnp.asarray(np.tile(sign, n_heads))


def _head_perm_cols(order):
    return (np.asarray(order)[:, None] * HEAD_DIM + np.arange(HEAD_DIM)[None, :]).reshape(-1)


def _inproj_weight(w_in_l):
    aq, ak, av, su, sq, sk, sv = jnp.split(
        w_in_l, np.cumsum([NA_W, NA_W, NA_W, S5_CH, SW_W, SW_KV_W])[:6].tolist(), axis=1)
    sq = sq[:, _head_perm_cols(SW_HEAD_ORDER)]
    return jnp.concatenate([aq, ak, av, sq, _rot_cols(sq), sk, _rot_cols(sk), sv, su], axis=1).astype(MXU_DTYPE)


def _chunk_major(su):
    bsz, s, _ = su.shape
    u = su.reshape(bsz, s // S5_CHUNK, S5_CHUNK, S5_GROUPS, S5_GROUP_CH).transpose(0, 1, 3, 2, 4)
    return u.reshape(bsz, s // S5_CHUNK, S5_GROUPS * S5_CHUNK * S5_GROUP_CH)


def _token_major(y_t):
    bsz, nc, _ = y_t.shape
    y = y_t.reshape(bsz, nc, S5_GROUPS, S5_CHUNK, S5_GROUP_CH).transpose(0, 1, 3, 2, 4)
    return y.reshape(bsz, nc * S5_CHUNK, S5_CH)


def kernel(x, c, ctx, c_ctx, w_mod, b_mod, g_mix, w_in, w_out, na_rpb, s5_a_re, s5_a_im, s5_log_step,
           s5_b_re, s5_b_im, s5_c_re, s5_c_im, s5_d, s5_w_glu, s5_b_glu, sw_sinks, g_ffn, w_router, b_router,
           w_gate_up, b_gate_up, w_down, b_down, g_final):
    bsz, seq, d = x.shape
    ctx_len = ctx.shape[1]
    depth = w_mod.shape[0]
    s = ctx_len + seq
    assert bsz + 1 <= MOD_ROWS and s % ROW_TILE == 0 and s % MOE_TILE == 0
    assert ctx_len % FINAL_TILE == 0 and seq % FINAL_TILE == 0
    assert seq % GRID_W == 0 and ctx_len % S5_CHUNK == 0

    xs = jnp.concatenate([ctx, x], axis=1)
    cond = jnp.zeros((MOD_ROWS, d), F32).at[:bsz].set(c).at[bsz].set(c_ctx)
    mod = _mod_call(cond, w_mod, b_mod).reshape(depth, MOD_ROWS, 6, d)
    cos2, sin2 = _rope_tables(seq, ctx_len)

    n_assign = bsz * s * TOP_K
    n_blocks = -(-(n_assign + N_EXPERTS * (MOE_BLK - 1)) // MOE_BLK)
    sw_rows = _head_perm_cols(SW_HEAD_ORDER)

    for l in range(depth):
        mod_l = mod[l]
        naq, nak, nav, swq, swk, swv, su = _inproj_call(
            xs, mod_l, g_mix[l].reshape(1, d), _inproj_weight(w_in[l]), cos2, sin2, ctx_len)
        ya = _na_call(naq, nak, nav, _na_bias_table(na_rpb[l]), ctx_len)
        yc = _sw_call(sw_sinks[l], swq, swk, swv, ctx_len)
        s5w = _s5_weights(s5_a_re[l], s5_a_im[l], s5_log_step[l], s5_b_re[l], s5_b_im[l], s5_c_re[l], s5_c_im[l])
        ys = _token_major(_s5_call(_chunk_major(su).astype(MXU_DTYPE), s5w, ctx_len // S5_CHUNK))

        wo = w_out[l]
        out_wts = (wo[:NA_W].astype(MXU_DTYPE),
                   wo[NA_W:NA_W + S5_CH].astype(MXU_DTYPE),
                   wo[NA_W + S5_CH:][sw_rows].astype(MXU_DTYPE),
                   s5_w_glu[l].astype(MXU_DTYPE), s5_b_glu[l].reshape(1, S5_CH).astype(F32),
                   s5_d[l].reshape(1, S5_CH).astype(F32), g_ffn[l].reshape(1, d),
                   w_router[l].T.astype(F32), b_router[l].reshape(N_EXPERTS, 1).astype(F32))
        xs, h, topi, topw, rank, cnt = _outproj_call(xs, ya, ys, su, yc, mod_l, out_wts, ctx_len)

        dest, blk, ends = _route_call(topi, rank, cnt, n_blocks)
        ends = ends[:, 0]
        xg = _dispatch_call(ends, dest, h, n_blocks)
        yg = _moe_call(l, blk[0, :n_blocks], ends[-1:] // MOE_BLK, xg, w_gate_up, b_gate_up, w_down, b_down)
        xs = _combine_call(dest, topw, xs, mod_l, yg, ctx_len)

    return _final_call(xs, g_final.reshape(1, d), ctx_len)
```

```python
import functools
import math

import numpy as np
import jax
import jax.numpy as jnp
from jax import lax
from jax.experimental import pallas as pl
from jax.experimental.pallas import tpu as pltpu

F32 = jnp.float32
MXU_DTYPE = jnp.bfloat16

GRID_W = 64
HEAD_DIM = 64
NA_HEADS = 6
NA_W = NA_HEADS * HEAD_DIM
NA_ROWS = 8
NA_COLS = 16
S5_GROUP_CH = 16
S5_CH = 256
S5_GROUPS = S5_CH // S5_GROUP_CH
S5_STATE = 64
S5_EIG_MAX = -1e-4
SW_HEADS = 6
SW_KV_HEADS = 2
SW_GRP = SW_HEADS // SW_KV_HEADS
SW_W = SW_HEADS * HEAD_DIM
SW_KV_W = SW_KV_HEADS * HEAD_DIM
SW_WINDOW = 128
SW_BLK = 128
ROPE_BASE = 10000.0
N_EXPERTS = 32
TOP_K = 4
MOE_BLK = 512
SWIGLU_LIMIT = 7.0
SWIGLU_ALPHA = 1.702
RMS_EPS = 1e-6
NEG_INF = -1e30

LANES = 128
ROW_TILE = 768
MOE_TILE = 256
S5_CHUNK = 16
NA_QROWS = 4
NA_KROWS = NA_QROWS + NA_ROWS
SW_QBLK = 2 * SW_BLK
SW_KBLK = SW_QBLK + 2 * SW_WINDOW
MOD_ROWS = 8
VMEM_LIMIT = 56 << 20
SUBLANES = 8

SW_HEAD_ORDER = tuple(g * SW_GRP + t for t in range(SW_GRP) for g in range(SW_KV_HEADS))


def _params(*sem):
    return pltpu.CompilerParams(dimension_semantics=sem, vmem_limit_bytes=VMEM_LIMIT)


def _dot(a, b):
    return jnp.dot(a, b, preferred_element_type=F32)


def _dot_nt(a, b):
    return lax.dot_general(a, b, (((1,), (1,)), ((), ())), preferred_element_type=F32)


def _split(a):
    hi = a.astype(MXU_DTYPE)
    lo = (a - hi.astype(F32)).astype(MXU_DTYPE)
    return hi, lo


def _dot3(a, b, nt=False):
    f = _dot_nt if nt else _dot
    ah, al = _split(a)
    bh, bl = _split(b)
    return f(ah, bh) + (f(ah, bl) + f(al, bh))


def _mod_kernel(cond_ref, w_ref, b_ref, o_ref):
    c = cond_ref[...]
    a = c * jax.nn.sigmoid(c)
    o_ref[0] = _dot3(a, w_ref[0]) + b_ref[0]


def _mod_call(cond, w_mod, b_mod):
    depth, d, n = w_mod.shape
    tn = n // 6
    return pl.pallas_call(
        _mod_kernel,
        grid=(depth, n // tn),
        in_specs=[pl.BlockSpec((MOD_ROWS, d), lambda l, j: (0, 0)),
                  pl.BlockSpec((1, d, tn), lambda l, j: (l, 0, j)),
                  pl.BlockSpec((1, 1, tn), lambda l, j: (l, 0, j))],
        out_specs=pl.BlockSpec((1, MOD_ROWS, tn), lambda l, j: (l, 0, j)),
        out_shape=jax.ShapeDtypeStruct((depth, MOD_ROWS, n), F32),
        compiler_params=_params("parallel", "parallel"),
        name="mod",
    )(cond, w_mod, b_mod.reshape(depth, 1, n))


C_AQ = 0
C_AK = C_AQ + NA_W
C_AV = C_AK + NA_W
C_SQ = C_AV + NA_W
C_SQR = C_SQ + SW_W
C_SK = C_SQR + SW_W
C_SKR = C_SK + SW_KV_W
C_SV = C_SKR + SW_KV_W
C_SU = C_SV + SW_KV_W
C_END = C_SU + S5_CH


def _rms_mod(x, g, shift, scale):
    y = x * lax.rsqrt(jnp.mean(x * x, axis=-1, keepdims=True) + RMS_EPS) * g
    return y * (1.0 + scale) + shift


def _mod_vectors(modb_ref, modc_ref, ctx_len, which):
    tm = ROW_TILE
    row = pl.program_id(1) * tm + lax.broadcasted_iota(jnp.int32, (tm, 1), 0)
    is_ctx = row < ctx_len
    mb = modb_ref[0]
    mc = modc_ref[0]
    return [jnp.where(is_ctx, mc[i:i + 1], mb[i:i + 1]) for i in which]


def _inproj_kernel(x_ref, modb_ref, modc_ref, g_ref, w_ref, cos_ref, sin_ref,
                   naq_ref, nak_ref, nav_ref, swq_ref, swk_ref, swv_ref, su_ref, *, ctx_len):
    shift, scale = _mod_vectors(modb_ref, modc_ref, ctx_len, (0, 1))
    h = _rms_mod(x_ref[0], g_ref[...], shift, scale).astype(MXU_DTYPE)
    p = _dot(h, w_ref[...])
    cos = cos_ref[...]
    sin = sin_ref[...]
    cos3 = jnp.concatenate([cos] * (SW_W // LANES), axis=1)
    sin3 = jnp.concatenate([sin] * (SW_W // LANES), axis=1)
    qk_scale = HEAD_DIM ** -0.5
    naq_ref[0] = (p[:, C_AQ:C_AK] * qk_scale).astype(naq_ref.dtype)
    nak_ref[0] = p[:, C_AK:C_AV].astype(nak_ref.dtype)
    nav_ref[0] = p[:, C_AV:C_SQ].astype(nav_ref.dtype)
    swq_ref[0] = ((p[:, C_SQ:C_SQR] * cos3 + p[:, C_SQR:C_SK] * sin3) * qk_scale).astype(swq_ref.dtype)
    swk_ref[0] = (p[:, C_SK:C_SKR] * cos + p[:, C_SKR:C_SV] * sin).astype(swk_ref.dtype)
    swv_ref[0] = p[:, C_SV:C_SU].astype(swv_ref.dtype)
    su_ref[0] = p[:, C_SU:C_END]


def _inproj_call(xs, mod_l, g, w_cat, cos2, sin2, ctx_len):
    bsz, s, d = xs.shape
    tm = ROW_TILE
    row = lambda b, j: (b, j, 0)
    const = lambda b, j: (0, 0)
    widths = (NA_W, NA_W, NA_W, SW_W, SW_KV_W, SW_KV_W, S5_CH)
    dtypes = (MXU_DTYPE,) * 6 + (F32,)
    return pl.pallas_call(
        functools.partial(_inproj_kernel, ctx_len=ctx_len),
        grid=(bsz, s // tm),
        in_specs=[pl.BlockSpec((1, tm, d), row),
                  pl.BlockSpec((1, 6, d), lambda b, j: (b, 0, 0)),
                  pl.BlockSpec((1, 6, d), lambda b, j: (bsz, 0, 0)),
                  pl.BlockSpec((1, d), const),
                  pl.BlockSpec((d, C_END), const),
                  pl.BlockSpec((tm, LANES), lambda b, j: (j, 0)),
                  pl.BlockSpec((tm, LANES), lambda b, j: (j, 0))],
        out_specs=[pl.BlockSpec((1, tm, w), row) for w in widths],
        out_shape=[jax.ShapeDtypeStruct((bsz, s, w), t) for w, t in zip(widths, dtypes)],
        compiler_params=_params("parallel", "parallel"),
        name="inproj",
    )(xs, mod_l, mod_l, g, w_cat, cos2, sin2)


def _half_masks():
    lane = lax.broadcasted_iota(jnp.int32, (1, LANES), 1)
    return lane < HEAD_DIM, lane >= HEAD_DIM


def _na_kernel(q_ref, k_ref, v_ref, bias_ref, o_ref, *, ctx_len, rows):
    i = pl.program_id(1)
    tq = NA_QROWS * GRID_W
    n_ctx_q = ctx_len // tq
    masks = _half_masks()
    nk = NA_KROWS * GRID_W

    def run(local):
        q = q_ref[0]
        if local:
            r0 = (i - n_ctx_q) * NA_QROWS
            start0 = jnp.clip(r0 - NA_ROWS // 2, 0, rows - NA_KROWS)
            start = pl.multiple_of(ctx_len + start0 * GRID_W, GRID_W)
            tab_idx, row_mask = {}, {}
            for a in range(NA_QROWS):
                r = r0 + a
                s_r = jnp.clip(r - NA_ROWS // 2, 0, rows - NA_ROWS)
                for p in range(NA_KROWS // 2):
                    kr = start0 + 2 * p
                    ok_lo = (kr >= s_r) & (kr < s_r + NA_ROWS)
                    ok_hi = (kr + 1 >= s_r) & (kr + 1 < s_r + NA_ROWS)
                    tab_idx[a, p] = jnp.clip(kr - r + NA_ROWS, 0, 2 * NA_ROWS - 1)
                    row_mask[a, p] = jnp.where(masks[0], jnp.where(ok_lo, 0.0, NEG_INF),
                                               jnp.where(ok_hi, 0.0, NEG_INF))
        outs = []
        for t in range(NA_W // LANES):
            sl = slice(LANES * t, LANES * (t + 1))
            qt = q[:, sl]
            zero = jnp.zeros_like(qt)
            qm = jnp.concatenate([jnp.where(masks[0], qt, zero), jnp.where(masks[1], qt, zero)], axis=0)
            kc = k_ref[0, 0:ctx_len, sl]
            vc = v_ref[0, 0:ctx_len, sl]
            s_cx = _dot_nt(qm, kc)
            m = jnp.max(s_cx, axis=-1, keepdims=True)
            if local:
                kw = k_ref[0, pl.ds(start, nk), sl]
                vw = v_ref[0, pl.ds(start, nk), sl]
                bias = jnp.concatenate(
                    [jnp.concatenate([bias_ref[2 * t + hh, pl.ds(tab_idx[a, p], 1)][0] + row_mask[a, p]
                                      for p in range(NA_KROWS // 2)], axis=-1)
                     for hh in range(2) for a in range(NA_QROWS)], axis=0)
                s_nb = _dot_nt(qm, kw) + bias
                m = jnp.maximum(m, jnp.max(s_nb, axis=-1, keepdims=True))
                p_nb = jnp.exp(s_nb - m)
            p_cx = jnp.exp(s_cx - m)
            den = jnp.sum(p_cx, axis=-1, keepdims=True)
            o = _dot(p_cx.astype(MXU_DTYPE), vc)
            if local:
                den = den + jnp.sum(p_nb, axis=-1, keepdims=True)
                o = o + _dot(p_nb.astype(MXU_DTYPE), vw)
            o = o / den
            outs.append(jnp.where(masks[0], o[:tq], o[tq:]))
        o_ref[0] = jnp.concatenate(outs, axis=-1).astype(o_ref.dtype)

    @pl.when(i < n_ctx_q)
    def _():
        run(False)

    @pl.when(i >= n_ctx_q)
    def _():
        run(True)


def _na_call(q, k, v, bias_tab, ctx_len):
    bsz, s, w = q.shape
    rows = (s - ctx_len) // GRID_W
    tq = NA_QROWS * GRID_W
    assert rows >= NA_KROWS and rows % NA_QROWS == 0 and ctx_len % tq == 0
    whole = lambda b, i: (b, 0, 0)
    return pl.pallas_call(
        functools.partial(_na_kernel, ctx_len=ctx_len, rows=rows),
        grid=(bsz, s // tq),
        in_specs=[pl.BlockSpec((1, tq, w), lambda b, i: (b, i, 0)),
                  pl.BlockSpec((1, s, w), whole),
                  pl.BlockSpec((1, s, w), whole),
                  pl.BlockSpec(bias_tab.shape, lambda b, i: (0, 0, 0, 0))],
        out_specs=pl.BlockSpec((1, tq, w), lambda b, i: (b, i, 0)),
        out_shape=jax.ShapeDtypeStruct((bsz, s, w), q.dtype),
        compiler_params=_params("parallel", "arbitrary"),
        name="na_attn",
    )(q, k, v, bias_tab)


def _na_bias_table(rpb):
    qcol = np.arange(GRID_W)[:, None]
    kcol = np.arange(GRID_W)[None, :]
    ws = np.clip(qcol - NA_COLS // 2, 0, GRID_W - NA_COLS)
    valid = (kcol >= ws) & (kcol < ws + NA_COLS)
    dc = np.clip(kcol - qcol + NA_COLS - 1, 0, 2 * NA_COLS - 2)
    full = jnp.where(valid[None, None], rpb[:, :, dc].astype(F32), NEG_INF)
    edge = jnp.full_like(full[:, :1], NEG_INF)
    full = jnp.concatenate([edge, full, edge], axis=1)
    return jnp.concatenate([full[:, :-1], full[:, 1:]], axis=-1)


def _sw_kernel(sink_ref, q_ref, k_ref, v_ref, o_ref, *, ctx_len, seq):
    i = pl.program_id(1)
    tq = SW_QBLK
    n_ctx_q = ctx_len // tq
    masks = _half_masks()
    nk = SW_KBLK
    first_head = lax.broadcasted_iota(jnp.int32, (2 * tq, 1), 0) < tq

    def run(local):
        q = q_ref[0]
        kc = k_ref[0, 0:ctx_len, :]
        vc = v_ref[0, 0:ctx_len, :]
        if local:
            n = i - n_ctx_q
            start_lat = jnp.clip(n * tq - SW_WINDOW, 0, seq - nk)
            start = pl.multiple_of(ctx_len + start_lat, SW_BLK)
            kw = k_ref[0, pl.ds(start, nk), :]
            vw = v_ref[0, pl.ds(start, nk), :]
            row = lax.broadcasted_iota(jnp.int32, (2 * tq, 1), 0)
            qpos = n * tq + jnp.where(first_head, row, row - tq)
            kpos = start_lat + lax.broadcasted_iota(jnp.int32, (1, nk), 1)
            valid = jnp.abs(qpos - kpos) <= SW_WINDOW
        outs = []
        for t in range(SW_W // LANES):
            qt = q[:, LANES * t:LANES * (t + 1)]
            zero = jnp.zeros_like(qt)
            qm = jnp.concatenate([jnp.where(masks[0], qt, zero), jnp.where(masks[1], qt, zero)], axis=0)
            sink = jnp.where(first_head, sink_ref[SW_HEAD_ORDER[2 * t]], sink_ref[SW_HEAD_ORDER[2 * t + 1]])
            s_cx = _dot_nt(qm, kc)
            m = jnp.maximum(jnp.max(s_cx, axis=-1, keepdims=True), sink)
            if local:
                s_loc = jnp.where(valid, _dot_nt(qm, kw), NEG_INF)
                m = jnp.maximum(m, jnp.max(s_loc, axis=-1, keepdims=True))
                p_loc = jnp.exp(s_loc - m)
            p_cx = jnp.exp(s_cx - m)
            den = jnp.sum(p_cx, axis=-1, keepdims=True) + jnp.exp(sink - m)
            o = _dot(p_cx.astype(MXU_DTYPE), vc)
            if local:
                den = den + jnp.sum(p_loc, axis=-1, keepdims=True)
                o = o + _dot(p_loc.astype(MXU_DTYPE), vw)
            o = o / den
            outs.append(jnp.where(masks[0], o[:tq], o[tq:]))
        o_ref[0] = jnp.concatenate(outs, axis=-1).astype(o_ref.dtype)

    @pl.when(i < n_ctx_q)
    def _():
        run(False)

    @pl.when(i >= n_ctx_q)
    def _():
        run(True)


def _sw_call(sinks, q, k, v, ctx_len):
    bsz, s, w = q.shape
    seq = s - ctx_len
    assert seq >= SW_KBLK and seq % SW_QBLK == 0 and ctx_len % SW_QBLK == 0
    whole = lambda b, i: (b, 0, 0)
    return pl.pallas_call(
        functools.partial(_sw_kernel, ctx_len=ctx_len, seq=seq),
        grid=(bsz, s // SW_QBLK),
        in_specs=[pl.BlockSpec(memory_space=pltpu.SMEM),
                  pl.BlockSpec((1, SW_QBLK, w), lambda b, i: (b, i, 0)),
                  pl.BlockSpec((1, s, SW_KV_W), whole),
                  pl.BlockSpec((1, s, SW_KV_W), whole)],
        out_specs=pl.BlockSpec((1, SW_QBLK, w), lambda b, i: (b, i, 0)),
        out_shape=jax.ShapeDtypeStruct((bsz, s, w), q.dtype),
        compiler_params=_params("parallel", "arbitrary"),
        name="sw_attn",
    )(sinks.astype(F32), q, k, v)


def _s5_kernel(u_ref, m_ref, wsr_ref, wsi_ref, wor_ref, woi_ref, lr_ref, li_ref, o_ref,
               sre, sim, xre, xim, acc, *, n_ctx_chunks):
    d = pl.program_id(1)
    bsz, nc, _ = u_ref.shape
    for b in range(bsz):
        ub = u_ref[b]
        sre[b] = _dot(ub, wsr_ref[0, 0])
        sim[b] = _dot(ub, wsi_ref[0, 0])
    lr = lr_ref[0, 0]
    li = li_ref[0, 0]

    def step(c, carry):
        new = []
        for b in range(bsz):
            xr, xi = carry[2 * b], carry[2 * b + 1]
            xre[b, pl.ds(c, 1), :] = xr
            xim[b, pl.ds(c, 1), :] = xi
            sr = sre[b, pl.ds(c, 1), :]
            si = sim[b, pl.ds(c, 1), :]
            new.append(lr * xr - li * xi + sr)
            new.append(lr * xi + li * xr + si)
        return tuple(new)

    zero = tuple(jnp.zeros((1, LANES), F32) for _ in range(2 * bsz))

    @pl.when(d == 0)
    def _():
        lax.fori_loop(0, nc, step, zero)

    @pl.when(d == 1)
    def _():
        carry = lax.fori_loop(0, n_ctx_chunks, lambda k, cr: step(n_ctx_chunks - 1 - k, cr), zero)
        lax.fori_loop(0, nc - n_ctx_chunks, lambda k, cr: step(nc - 1 - k, cr), carry)

    half = u_ref.shape[2] // 2
    for b in range(bsz):
        ub = u_ref[b]
        y_intra = jnp.concatenate([_dot(ub[:, :half], m_ref[0, 0]), _dot(ub[:, half:], m_ref[0, 1])], axis=-1)
        y = (y_intra + _dot(xre[b].astype(MXU_DTYPE), wor_ref[0, 0])
             + _dot(xim[b].astype(MXU_DTYPE), woi_ref[0, 0]))

        @pl.when(d == 0)
        def _():
            acc[b] = y

        @pl.when(d == 1)
        def _():
            o_ref[b] = (acc[b] + y).astype(o_ref.dtype)


def _s5_call(u_t, wts, n_ctx_chunks):
    m, wsr, wsi, wor, woi, lr, li = wts
    bsz, nc, width = u_t.shape
    pw = 2 * S5_CHUNK * S5_GROUP_CH
    n_pairs = width // pw
    blk = lambda shp: pl.BlockSpec((1, 1) + shp, lambda j, d: (d, j, 0, 0))
    return pl.pallas_call(
        functools.partial(_s5_kernel, n_ctx_chunks=n_ctx_chunks),
        grid=(n_pairs, 2),
        in_specs=[pl.BlockSpec((bsz, nc, pw), lambda j, d: (0, 0, j)),
                  pl.BlockSpec((1, 2, pw // 2, pw // 2), lambda j, d: (d, j, 0, 0)),
                  blk((pw, LANES)), blk((pw, LANES)), blk((LANES, pw)), blk((LANES, pw)),
                  blk((1, LANES)), blk((1, LANES))],
        out_specs=pl.BlockSpec((bsz, nc, pw), lambda j, d: (0, 0, j)),
        out_shape=jax.ShapeDtypeStruct((bsz, nc, width), MXU_DTYPE),
        scratch_shapes=[pltpu.VMEM((bsz, nc, LANES), F32) for _ in range(4)] + [pltpu.VMEM((bsz, nc, pw), F32)],
        compiler_params=_params("parallel", "arbitrary"),
        name="s5_scan",
    )(u_t, m, wsr, wsi, wor, woi, lr, li)


def _s5_weights(a_re, a_im, log_step, b_re, b_im, c_re, c_im):
    lc, g, p, h = S5_CHUNK, S5_GROUPS, S5_STATE, S5_GROUP_CH
    lam = lax.complex(jnp.minimum(a_re.astype(F32), S5_EIG_MAX), a_im.astype(F32))
    step = jnp.exp(log_step.astype(F32))[..., None]
    lam_bar = jnp.exp(lam * step)
    b_bar = ((lam_bar - 1.0) / lam)[..., None] * lax.complex(b_re.astype(F32), b_im.astype(F32))
    cc = lax.complex(c_re.astype(F32), c_im.astype(F32))
    dd = jnp.arange(lc + 1, dtype=F32)
    pw = jnp.exp((lam * step)[..., None] * dd)
    kern = jnp.real(jnp.einsum('zgop,zgpd,zgpi->zgdoi', cc, pw[..., :lc], b_bar))
    jj = np.arange(lc)[:, None]
    ii = np.arange(lc)[None, :]
    mats, wst, wout = [], [], []
    for z in range(2):
        lag = (ii - jj) if z == 0 else (jj - ii)
        ok = lag >= 0
        kz = kern[z][:, np.where(ok, lag, 0)]
        kz = jnp.where(ok[None, :, :, None, None], kz, 0.0)
        mats.append(kz.transpose(0, 1, 4, 2, 3).reshape(g, lc * h, lc * h))
        d_state = (lc - 1 - np.arange(lc)) if z == 0 else np.arange(lc)
        ws = pw[z][:, :, d_state][..., None] * b_bar[z][:, :, None, :]
        wst.append(ws.transpose(0, 2, 3, 1).reshape(g, lc * h, p))
        d_out = (np.arange(lc) + 1) if z == 0 else (lc - np.arange(lc))
        wo = cc[z][:, :, :, None] * pw[z][:, None, :, :][..., d_out]
        wout.append(wo.transpose(0, 2, 3, 1).reshape(g, p, lc * h))
    mats = jnp.stack(mats)
    wst = jnp.stack(wst)
    wout = jnp.stack(wout)

    def pair_rows(w):
        w = w.reshape(2, g // 2, 2, lc * h, p)
        z0 = jnp.zeros_like(w[:, :, 0])
        top = jnp.concatenate([w[:, :, 0], z0], axis=-1)
        bot = jnp.concatenate([z0, w[:, :, 1]], axis=-1)
        return jnp.concatenate([top, bot], axis=-2)

    def pair_cols(w):
        w = w.reshape(2, g // 2, 2, p, lc * h)
        z0 = jnp.zeros_like(w[:, :, 0])
        top = jnp.concatenate([w[:, :, 0], z0], axis=-1)
        bot = jnp.concatenate([z0, w[:, :, 1]], axis=-1)
        return jnp.concatenate([top, bot], axis=-2)

    lam_c = pw[..., lc].reshape(2, g // 2, 1, 2 * p)
    cast = lambda w: w.astype(MXU_DTYPE)
    return (cast(mats), cast(pair_rows(jnp.real(wst))), cast(pair_rows(jnp.imag(wst))),
            cast(pair_cols(jnp.real(wout))), cast(pair_cols(-jnp.imag(wout))),
            jnp.real(lam_c), jnp.imag(lam_c))


def _gelu_tanh(x):
    cdf = 0.5 * (1.0 + jnp.tanh(math.sqrt(2.0 / math.pi) * (x + 0.044715 * (x * x * x))))
    return x * cdf


def _outproj_kernel(x_ref, ya_ref, ys_ref, su_ref, yc_ref, modb_ref, modc_ref, woa_ref, wob_ref, woc_ref,
                    wglu_ref, bglu_ref, dsk_ref, g_ref, wr_ref, br_ref,
                    xo_ref, h_ref, topi_ref, topw_ref, rank_ref, cnt_ref, carry, *, ctx_len):
    gate, shift, scale = _mod_vectors(modb_ref, modc_ref, ctx_len, (2, 3, 4))
    y = dsk_ref[...] * su_ref[0] + ys_ref[0].astype(F32)
    gl = _gelu_tanh(y)
    yb = gl * jax.nn.sigmoid(_dot(gl.astype(MXU_DTYPE), wglu_ref[...]) + bglu_ref[...])
    mix = (_dot(ya_ref[0], woa_ref[...]) + _dot(yb.astype(MXU_DTYPE), wob_ref[...])
           + _dot(yc_ref[0], woc_ref[...]))
    x = x_ref[0] + gate * mix
    xo_ref[0] = x
    h = _rms_mod(x, g_ref[...], shift, scale)
    dl = h_ref.shape[-1]
    for sl in range(SUBLANES):
        h_ref[0, :, sl, :] = h[:, dl * sl:dl * (sl + 1)]
    logits = _dot3(wr_ref[...], h, nt=True) + br_ref[...]
    n_e, tm = logits.shape
    e_iota = lax.broadcasted_iota(jnp.int32, (n_e, tm), 0)
    vals, idxs = [], []
    for _ in range(TOP_K):
        mx = jnp.max(logits, axis=0, keepdims=True)
        ix = jnp.min(jnp.where(logits == mx, e_iota, n_e), axis=0, keepdims=True)
        vals.append(mx)
        idxs.append(ix)
        logits = jnp.where(e_iota == ix, -jnp.inf, logits)
    ex = [jnp.exp(v - vals[0]) for v in vals]
    den = ex[0] + ex[1] + ex[2] + ex[3]
    topi_ref[0] = jnp.concatenate(idxs, axis=0)
    topw_ref[0] = jnp.concatenate([e / den for e in ex], axis=0)

    @pl.when((pl.program_id(0) == 0) & (pl.program_id(1) == 0))
    def _():
        carry[...] = jnp.zeros_like(carry)

    sel = [ix == e_iota for ix in idxs]
    onehot = jnp.where(sel[0] | sel[1] | sel[2] | sel[3], 1.0, 0.0)
    before = (lax.broadcasted_iota(jnp.int32, (tm, tm), 0) < lax.broadcasted_iota(jnp.int32, (tm, tm), 1))
    pfx = _dot(onehot.astype(MXU_DTYPE), jnp.where(before, 1.0, 0.0).astype(MXU_DTYPE)) + carry[:, 0:1]
    rank_ref[0] = jnp.concatenate(
        [jnp.sum(jnp.where(sel[k], pfx, 0.0), axis=0, keepdims=True) for k in range(TOP_K)], axis=0)
    carry[...] = carry[...] + jnp.sum(onehot, axis=1, keepdims=True)
    cnt_ref[...] = carry[...]


def _outproj_call(xs, ya, ys, su, yc, mod_l, wts, ctx_len):
    bsz, s, d = xs.shape
    tm = ROW_TILE
    row = lambda b, j: (b, j, 0)
    const = lambda b, j: (0, 0)
    full = lambda a: pl.BlockSpec(a.shape, const)
    return pl.pallas_call(
        functools.partial(_outproj_kernel, ctx_len=ctx_len),
        grid=(bsz, s // tm),
        in_specs=[pl.BlockSpec((1, tm, d), row),
                  pl.BlockSpec((1, tm, NA_W), row),
                  pl.BlockSpec((1, tm, S5_CH), row),
                  pl.BlockSpec((1, tm, S5_CH), row),
                  pl.BlockSpec((1, tm, SW_W), row),
                  pl.BlockSpec((1, 6, d), lambda b, j: (b, 0, 0)),
                  pl.BlockSpec((1, 6, d), lambda b, j: (bsz, 0, 0))] + [full(a) for a in wts],
        out_specs=[pl.BlockSpec((1, tm, d), row),
                   pl.BlockSpec((1, tm, SUBLANES, d // SUBLANES), lambda b, j: (b, j, 0, 0)),
                   pl.BlockSpec((1, TOP_K, tm), lambda b, j: (b, 0, j)),
                   pl.BlockSpec((1, TOP_K, tm), lambda b, j: (b, 0, j)),
                   pl.BlockSpec((1, TOP_K, tm), lambda b, j: (b, 0, j)),
                   pl.BlockSpec((N_EXPERTS, LANES), lambda b, j: (0, 0))],
        out_shape=[jax.ShapeDtypeStruct((bsz, s, d), F32),
                   jax.ShapeDtypeStruct((bsz, s, SUBLANES, d // SUBLANES), F32),
                   jax.ShapeDtypeStruct((bsz, TOP_K, s), jnp.int32),
                   jax.ShapeDtypeStruct((bsz, TOP_K, s), F32),
                   jax.ShapeDtypeStruct((bsz, TOP_K, s), F32),
                   jax.ShapeDtypeStruct((N_EXPERTS, LANES), F32)],
        scratch_shapes=[pltpu.VMEM((N_EXPERTS, LANES), F32)],
        compiler_params=_params("arbitrary", "arbitrary"),
        name="outproj",
    )(xs, ya, ys, su, yc, mod_l, mod_l, *wts)


def _route_kernel(topi_ref, rank_ref, cnt_ref, dest_ref, blk_ref, ends_ref):
    idx = topi_ref[0]
    tm = idx.shape[1]
    e_iota = lax.broadcasted_iota(jnp.int32, (N_EXPERTS, tm), 0)
    counts = cnt_ref[:, 0:1]
    padded = jnp.ceil(counts * (1.0 / MOE_BLK)) * MOE_BLK
    r_i = lax.broadcasted_iota(jnp.int32, (N_EXPERTS, N_EXPERTS), 0)
    c_i = lax.broadcasted_iota(jnp.int32, (N_EXPERTS, N_EXPERTS), 1)
    padded_row = jnp.sum(jnp.where(r_i == c_i, padded, 0.0), axis=0, keepdims=True)
    pstart = jnp.sum(jnp.where(c_i < r_i, padded_row, 0.0), axis=1, keepdims=True)
    ends = pstart + padded
    rk = rank_ref[0]
    dest_ref[0] = jnp.concatenate(
        [jnp.sum(jnp.where(idx[k:k + 1] == e_iota, pstart, 0.0), axis=0, keepdims=True) + rk[k:k + 1]
         for k in range(TOP_K)], axis=1).astype(jnp.int32)
    nb = blk_ref.shape[1]
    blk_start = (lax.broadcasted_iota(jnp.int32, (N_EXPERTS, nb), 1) * MOE_BLK).astype(F32)
    owner = jnp.sum(jnp.where(ends <= blk_start, 1.0, 0.0), axis=0, keepdims=True)
    blk_ref[...] = jnp.minimum(owner, N_EXPERTS - 1.0).astype(jnp.int32)
    ends_ref[...] = jnp.broadcast_to(ends, ends_ref.shape).astype(jnp.int32)


def _route_call(topi, rank, cnt, n_blocks):
    bsz, _, s = topi.shape
    tm = MOE_TILE
    n_tiles = s // tm
    nb_pad = -(-n_blocks // LANES) * LANES
    tok = pl.BlockSpec((1, TOP_K, tm), lambda b, j: (b, 0, j))
    return pl.pallas_call(
        _route_kernel,
        grid=(bsz, n_tiles),
        in_specs=[tok, tok, pl.BlockSpec((N_EXPERTS, LANES), lambda b, j: (0, 0))],
        out_specs=[pl.BlockSpec((1, 1, TOP_K * tm), lambda b, j: (b * n_tiles + j, 0, 0)),
                   pl.BlockSpec((1, nb_pad), lambda b, j: (0, 0)),
                   pl.BlockSpec((N_EXPERTS, LANES), lambda b, j: (0, 0))],
        out_shape=[jax.ShapeDtypeStruct((bsz * n_tiles, 1, TOP_K * tm), jnp.int32),
                   jax.ShapeDtypeStruct((1, nb_pad), jnp.int32),
                   jax.ShapeDtypeStruct((N_EXPERTS, LANES), jnp.int32)],
        compiler_params=_params("arbitrary", "arbitrary"),
        name="route_dest",
    )(topi, rank, cnt)


def _dispatch_kernel(ends_ref, dest_ref, h_ref, xg_ref, zbuf, stage, sem, zsem, *, n_blocks):
    tm = h_ref.shape[1]

    @pl.when((pl.program_id(0) == 0) & (pl.program_id(1) == 0))
    def _():
        zbuf[...] = jnp.zeros_like(zbuf)

        def fill(row):
            return pltpu.make_async_copy(zbuf, xg_ref.at[pl.ds(pl.multiple_of(row, MOE_BLK), MOE_BLK)], zsem)

        def each(fn):
            for e in range(N_EXPERTS):
                begin = ends_ref[e - 1] if e else 0

                @pl.when(ends_ref[e] > begin)
                def _():
                    fn(fill(ends_ref[e] - MOE_BLK))

            def dead(i, c):
                fn(fill(i * MOE_BLK))
                return c

            lax.fori_loop(ends_ref[N_EXPERTS - 1] // MOE_BLK, n_blocks, dead, 0)

        each(lambda cp: cp.start())
        each(lambda cp: cp.wait())

    step = pl.program_id(0) * pl.num_programs(1) + pl.program_id(1)
    n_steps = pl.num_programs(0) * pl.num_programs(1)
    slot = lax.rem(step, 2)
    stage[slot] = h_ref[0]

    def body(t, c):
        for k in range(TOP_K):
            pltpu.make_async_copy(stage.at[slot, t], xg_ref.at[dest_ref[0, 0, k * tm + t]],
                                  sem.at[slot]).start(priority=k % 2)
        return c

    lax.fori_loop(0, tm, body, 0, unroll=8)

    def wait_tile(sl):
        pltpu.make_async_copy(xg_ref.at[pl.ds(0, TOP_K * tm)], xg_ref.at[pl.ds(0, TOP_K * tm)], sem.at[sl]).wait()

    @pl.when(step > 0)
    def _():
        wait_tile(1 - slot)

    @pl.when(step == n_steps - 1)
    def _():
        wait_tile(slot)


def _dispatch_call(ends, dest, h, n_blocks):
    bsz, s, _, dl = h.shape
    tm = MOE_TILE
    n_tiles = s // tm
    grid_spec = pltpu.PrefetchScalarGridSpec(
        num_scalar_prefetch=1,
        grid=(bsz, n_tiles),
        in_specs=[pl.BlockSpec((1, 1, TOP_K * tm), lambda b, j, en: (b * n_tiles + j, 0, 0), memory_space=pltpu.SMEM),
                  pl.BlockSpec((1, tm, SUBLANES, dl), lambda b, j, en: (b, j, 0, 0))],
        out_specs=pl.BlockSpec(memory_space=pl.ANY),
        scratch_shapes=[pltpu.VMEM((MOE_BLK, SUBLANES, dl), h.dtype), pltpu.VMEM((2, tm, SUBLANES, dl), h.dtype),
                        pltpu.SemaphoreType.DMA((2,)),
                        pltpu.SemaphoreType.DMA(())],
    )
    return pl.pallas_call(
        functools.partial(_dispatch_kernel, n_blocks=n_blocks),
        grid_spec=grid_spec,
        out_shape=jax.ShapeDtypeStruct((n_blocks * MOE_BLK, SUBLANES, dl), h.dtype),
        compiler_params=_params("arbitrary", "arbitrary"),
        name="moe_dispatch",
    )(ends, dest, h)


def _moe_kernel(blk_exp_ref, nact_ref, x_hbm, wgu_ref, bgu_ref, wd_ref, bd_ref, y_hbm, wgu_c, wd_c,
                xbuf, xsem, ybuf, ysem):
    i = pl.program_id(0)
    slot = lax.rem(i, 2)
    dl = x_hbm.shape[-1]

    def fetch(blk, sl_):
        for s in range(SUBLANES):
            pltpu.make_async_copy(x_hbm.at[pl.ds(blk * MOE_BLK, MOE_BLK), s, :],
                                  xbuf.at[sl_, :, pl.ds(dl * s, dl)], xsem.at[sl_]).start()

    @pl.when(i == 0)
    def _():
        fetch(0, 0)

    @pl.when(i + 1 < pl.num_programs(0))
    def _():
        fetch(i + 1, 1 - slot)

    def y_wait(sl_):
        pltpu.make_async_copy(ybuf.at[sl_], ybuf.at[sl_], ysem.at[sl_]).wait()

    @pl.when(i >= 2)
    def _():
        y_wait(slot)

    e = blk_exp_ref[i]
    prev = blk_exp_ref[jnp.maximum(i - 1, 0)]
    d, f2 = wgu_c.shape
    f = f2 // 2
    rows = 128

    @pl.when((i == 0) | (e != prev))
    def _():
        def cv(r, c):
            sl = pl.ds(pl.multiple_of(r * rows, rows), rows)
            wgu_c[sl, :] = wgu_ref[0, 0, sl, :].astype(wgu_c.dtype)
            return c
        lax.fori_loop(0, d // rows, cv, 0)

        def cv2(r, c):
            sl = pl.ds(pl.multiple_of(r * rows, rows), rows)
            wd_c[sl, :] = wd_ref[0, 0, sl, :].astype(wd_c.dtype)
            return c
        lax.fori_loop(0, f // rows, cv2, 0)

    pltpu.make_async_copy(xbuf.at[slot], xbuf.at[slot], xsem.at[slot]).wait()

    @pl.when(i < nact_ref[0])
    def _():
        gu = _dot(xbuf[slot].astype(MXU_DTYPE), wgu_c[...]) + bgu_ref[0, 0]
        gate = jnp.minimum(gu[:, :f], SWIGLU_LIMIT)
        up = jnp.clip(gu[:, f:], -SWIGLU_LIMIT, SWIGLU_LIMIT)
        act = gate * jax.nn.sigmoid(SWIGLU_ALPHA * gate) * (up + 1.0)
        ybuf[slot] = _dot(act.astype(MXU_DTYPE), wd_c[...]) + bd_ref[0, 0]

    @pl.when(i >= nact_ref[0])
    def _():
        ybuf[slot] = jnp.zeros(ybuf.shape[1:], ybuf.dtype)

    for s in range(SUBLANES):
        pltpu.make_async_copy(ybuf.at[slot, :, pl.ds(dl * s, dl)],
                              y_hbm.at[pl.ds(i * MOE_BLK, MOE_BLK), s, :], ysem.at[slot]).start()

    @pl.when(i == pl.num_programs(0) - 1)
    def _():
        y_wait(1 - slot)
        y_wait(slot)


def _moe_call(layer, blk_exp, n_active, xg, w_gate_up, b_gate_up, w_down, b_down):
    n_rows, _, dl = xg.shape
    depth, n_e, d, f2 = w_gate_up.shape
    f = f2 // 2
    n_blocks = n_rows // MOE_BLK
    assert n_blocks >= 2
    wsel = lambda i, be, na: (layer, be[i], 0, 0)
    grid_spec = pltpu.PrefetchScalarGridSpec(
        num_scalar_prefetch=2,
        grid=(n_blocks,),
        in_specs=[pl.BlockSpec(memory_space=pl.ANY),
                  pl.BlockSpec((1, 1, d, f2), wsel),
                  pl.BlockSpec((1, 1, 1, f2), wsel),
                  pl.BlockSpec((1, 1, f, d), wsel),
                  pl.BlockSpec((1, 1, 1, d), wsel)],
        out_specs=pl.BlockSpec(memory_space=pl.ANY),
        scratch_shapes=[pltpu.VMEM((d, f2), MXU_DTYPE), pltpu.VMEM((f, d), MXU_DTYPE),
                        pltpu.VMEM((2, MOE_BLK, d), F32), pltpu.SemaphoreType.DMA((2,)),
                        pltpu.VMEM((2, MOE_BLK, d), F32), pltpu.SemaphoreType.DMA((2,))],
    )
    return pl.pallas_call(
        _moe_kernel,
        grid_spec=grid_spec,
        out_shape=jax.ShapeDtypeStruct((n_rows, SUBLANES, dl), F32),
        compiler_params=_params("arbitrary"),
        name="moe_experts",
    )(blk_exp, n_active, xg, w_gate_up, b_gate_up.reshape(depth, n_e, 1, f2),
      w_down, b_down.reshape(depth, n_e, 1, d))


def _combine_kernel(dest_ref, dnext_ref, w_ref, gate_ref, x_hbm, yg_hbm, xo_hbm,
                    gbuf, xbuf, obuf, gsem, xsem, osem, *, n_tiles):
    tm = MOE_TILE
    dl = gbuf.shape[-1]
    step = pl.program_id(0) * n_tiles + pl.program_id(1)
    n_steps = pl.num_programs(0) * n_tiles
    slot = lax.rem(step, 2)

    def stream_copies(st, sl_, buf, hbm, sem, to_hbm):
        b = lax.div(st, n_tiles)
        r0 = lax.rem(st, n_tiles) * tm
        out = []
        for s in range(SUBLANES):
            rows = hbm.at[b, pl.ds(r0, tm), pl.ds(dl * s, dl)]
            tiles = buf.at[sl_, :, s, :]
            out.append(pltpu.make_async_copy(tiles, rows, sem.at[sl_]) if to_hbm
                       else pltpu.make_async_copy(rows, tiles, sem.at[sl_]))
        return out

    def fetch(dref, st, sl_):
        for cp in stream_copies(st, sl_, xbuf, x_hbm, xsem, False):
            cp.start()

        def body(t, c):
            for k in range(TOP_K):
                pltpu.make_async_copy(yg_hbm.at[dref[0, 0, k * tm + t]], gbuf.at[sl_, k, t],
                                      gsem.at[sl_]).start(priority=k % 2)
            return c

        lax.fori_loop(0, tm, body, 0, unroll=8)

    @pl.when(step == 0)
    def _():
        fetch(dest_ref, 0, 0)

    @pl.when(step + 1 < n_steps)
    def _():
        fetch(dnext_ref, step + 1, 1 - slot)

    def wait_all(buf, sem, sl_):
        pltpu.make_async_copy(buf.at[sl_], buf.at[sl_], sem.at[sl_]).wait()

    wait_all(gbuf, gsem, slot)
    wait_all(xbuf, xsem, slot)

    @pl.when(step >= 2)
    def _():
        wait_all(obuf, osem, slot)

    gate = gate_ref[0]

    def row(t, c):
        acc = gbuf[slot, 0, t] * w_ref[0, 0, t]
        for k in range(1, TOP_K):
            acc = acc + gbuf[slot, k, t] * w_ref[0, 0, k * tm + t]
        obuf[slot, t] = xbuf[slot, t] + gate * acc
        return c

    lax.fori_loop(0, tm, row, 0, unroll=8)
    for cp in stream_copies(step, slot, obuf, xo_hbm, osem, True):
        cp.start()

    @pl.when(step == n_steps - 1)
    def _():
        wait_all(obuf, osem, 1 - slot)
        wait_all(obuf, osem, slot)


def _combine_call(dest, topw, xs, mod_l, yg, ctx_len):
    bsz, s, d = xs.shape
    tm = MOE_TILE
    n_tiles = s // tm
    last = bsz * n_tiles - 1
    assert last >= 1
    dl = d // SUBLANES
    w_flat = topw.reshape(bsz, TOP_K, n_tiles, tm).transpose(0, 2, 1, 3).reshape(bsz * n_tiles, 1, TOP_K * tm)
    gate = mod_l[:, 5].reshape(MOD_ROWS, SUBLANES, dl)
    idx_spec = lambda ahead: pl.BlockSpec(
        (1, 1, TOP_K * tm), lambda b, j: (jnp.minimum(b * n_tiles + j + ahead, last), 0, 0), memory_space=pltpu.SMEM)
    tile_buf = lambda lead: pltpu.VMEM(lead + (tm, SUBLANES, dl), F32)
    return pl.pallas_call(
        functools.partial(_combine_kernel, n_tiles=n_tiles),
        grid=(bsz, n_tiles),
        in_specs=[idx_spec(0), idx_spec(1), idx_spec(0),
                  pl.BlockSpec((1, SUBLANES, dl), lambda b, j: (jnp.where(j < ctx_len // tm, bsz, b), 0, 0)),
                  pl.BlockSpec(memory_space=pl.ANY),
                  pl.BlockSpec(memory_space=pl.ANY)],
        out_specs=pl.BlockSpec(memory_space=pl.ANY),
        out_shape=jax.ShapeDtypeStruct((bsz, s, d), F32),
        scratch_shapes=[tile_buf((2, TOP_K)), tile_buf((2,)), tile_buf((2,)),
                        pltpu.SemaphoreType.DMA((2,)), pltpu.SemaphoreType.DMA((2,)), pltpu.SemaphoreType.DMA((2,))],
        compiler_params=_params("arbitrary", "arbitrary"),
        name="moe_combine",
    )(dest, dest, w_flat, gate, xs, yg)


def _final_kernel(x_ref, g_ref, o_ref):
    x = x_ref[0]
    o_ref[0] = x * lax.rsqrt(jnp.mean(x * x, axis=-1, keepdims=True) + RMS_EPS) * g_ref[...]


def _final_call(xs, g, ctx_len):
    bsz, s, d = xs.shape
    tm = MOE_TILE
    off = ctx_len // tm
    return pl.pallas_call(
        _final_kernel,
        grid=(bsz, (s - ctx_len) // tm),
        in_specs=[pl.BlockSpec((1, tm, d), lambda b, j: (b, j + off, 0)),
                  pl.BlockSpec((1, d), lambda b, j: (0, 0))],
        out_specs=pl.BlockSpec((1, tm, d), lambda b, j: (b, j, 0)),
        out_shape=jax.ShapeDtypeStruct((bsz, s - ctx_len, d), F32),
        compiler_params=_params("parallel", "parallel"),
        name="final_norm",
    )(xs, g)


def _rope_tables(seq, ctx_len):
    t = jnp.arange(seq)
    row = (t // GRID_W).astype(F32)
    col = (t % GRID_W).astype(F32)
    nf = HEAD_DIM // 4
    inv = ROPE_BASE ** (-jnp.arange(nf, dtype=F32) / nf)
    ar = row[:, None] * inv
    ac = col[:, None] * inv
    ang = jnp.concatenate([ar, ar, ac, ac], axis=-1)
    cos = jnp.concatenate([jnp.ones((ctx_len, HEAD_DIM), F32), jnp.cos(ang)], axis=0)
    sin = jnp.concatenate([jnp.zeros((ctx_len, HEAD_DIM), F32), jnp.sin(ang)], axis=0)
    reps = LANES // HEAD_DIM
    return jnp.tile(cos, (1, reps)), jnp.tile(sin, (1, reps))


def _rot_cols(w):
    q = HEAD_DIM // 4
    j = np.arange(HEAD_DIM)
    first = (j % (2 * q)) < q
    src = np.where(first, j + q, j - q)
    sign = np.where(first, -1.0, 1.0).astype(np.float32)
    n_heads = w.shape[1] // HEAD_DIM
    src_all = (np.arange(n_heads)[:, None] * HEAD_DIM + src[None, :]).reshape(-1)
    return w[:, src_all] * jnp.asarray(np.tile(sign, n_heads))


def _head_perm_cols(order):
    return (np.asarray(order)[:, None] * HEAD_DIM + np.arange(HEAD_DIM)[None, :]).reshape(-1)


def _inproj_weight(w_in_l):
    aq, ak, av, su, sq, sk, sv = jnp.split(
        w_in_l, np.cumsum([NA_W, NA_W, NA_W, S5_CH, SW_W, SW_KV_W])[:6].tolist(), axis=1)
    sq = sq[:, _head_perm_cols(SW_HEAD_ORDER)]
    return jnp.concatenate([aq, ak, av, sq, _rot_cols(sq), sk, _rot_cols(sk), sv, su], axis=1).astype(MXU_DTYPE)


def _chunk_major(su):
    bsz, s, _ = su.shape
    u = su.reshape(bsz, s // S5_CHUNK, S5_CHUNK, S5_GROUPS, S5_GROUP_CH).transpose(0, 1, 3, 2, 4)
    return u.reshape(bsz, s // S5_CHUNK, S5_GROUPS * S5_CHUNK * S5_GROUP_CH)


def _token_major(y_t):
    bsz, nc, _ = y_t.shape
    y = y_t.reshape(bsz, nc, S5_GROUPS, S5_CHUNK, S5_GROUP_CH).transpose(0, 1, 3, 2, 4)
    return y.reshape(bsz, nc * S5_CHUNK, S5_CH)


def kernel(x, c, ctx, c_ctx, w_mod, b_mod, g_mix, w_in, w_out, na_rpb, s5_a_re, s5_a_im, s5_log_step,
           s5_b_re, s5_b_im, s5_c_re, s5_c_im, s5_d, s5_w_glu, s5_b_glu, sw_sinks, g_ffn, w_router, b_router,
           w_gate_up, b_gate_up, w_down, b_down, g_final):
    bsz, seq, d = x.shape
    ctx_len = ctx.shape[1]
    depth = w_mod.shape[0]
    s = ctx_len + seq
    assert bsz + 1 <= MOD_ROWS and s % ROW_TILE == 0 and ctx_len % MOE_TILE == 0 and seq % MOE_TILE == 0
    assert seq % GRID_W == 0 and ctx_len % S5_CHUNK == 0

    xs = jnp.concatenate([ctx, x], axis=1)
    cond = jnp.zeros((MOD_ROWS, d), F32).at[:bsz].set(c).at[bsz].set(c_ctx)
    mod = _mod_call(cond, w_mod, b_mod).reshape(depth, MOD_ROWS, 6, d)
    cos2, sin2 = _rope_tables(seq, ctx_len)

    n_assign = bsz * s * TOP_K
    n_blocks = -(-(n_assign + N_EXPERTS * (MOE_BLK - 1)) // MOE_BLK)
    sw_rows = _head_perm_cols(SW_HEAD_ORDER)

    w_cat = jax.vmap(_inproj_weight)(w_in)
    bias_tab = jax.vmap(_na_bias_table)(na_rpb)
    s5w = jax.vmap(_s5_weights)(s5_a_re, s5_a_im, s5_log_step, s5_b_re, s5_b_im, s5_c_re, s5_c_im)
    wo_a = w_out[:, :NA_W].astype(MXU_DTYPE)
    wo_b = w_out[:, NA_W:NA_W + S5_CH].astype(MXU_DTYPE)
    wo_c = w_out[:, NA_W + S5_CH:][:, sw_rows].astype(MXU_DTYPE)
    w_glu = s5_w_glu.astype(MXU_DTYPE)
    w_rt = jnp.swapaxes(w_router, 1, 2).astype(F32)

    for l in range(depth):
        mod_l = mod[l]
        naq, nak, nav, swq, swk, swv, su = _inproj_call(
            xs, mod_l, g_mix[l].reshape(1, d), w_cat[l], cos2, sin2, ctx_len)
        ya = _na_call(naq, nak, nav, bias_tab[l], ctx_len)
        yc = _sw_call(sw_sinks[l], swq, swk, swv, ctx_len)
        ys = _token_major(_s5_call(_chunk_major(su).astype(MXU_DTYPE), tuple(w[l] for w in s5w),
                                   ctx_len // S5_CHUNK))

        out_wts = (wo_a[l], wo_b[l], wo_c[l], w_glu[l], s5_b_glu[l].reshape(1, S5_CH).astype(F32),
                   s5_d[l].reshape(1, S5_CH).astype(F32), g_ffn[l].reshape(1, d),
                   w_rt[l], b_router[l].reshape(N_EXPERTS, 1).astype(F32))
        xs, h, topi, topw, rank, cnt = _outproj_call(xs, ya, ys, su, yc, mod_l, out_wts, ctx_len)

        dest, blk, ends = _route_call(topi, rank, cnt, n_blocks)
        ends = ends[:, 0]
        xg = _dispatch_call(ends, dest, h, n_blocks)
        yg = _moe_call(l, blk[0, :n_blocks], ends[-1:] // MOE_BLK, xg, w_gate_up, b_gate_up, w_down, b_down)
        xs = _combine_call(dest, topw, xs, mod_l, yg, ctx_len)

    return _final_call(xs, g_final.reshape(1, d), ctx_len)
```

```python
import functools
import math

import numpy as np
import jax
import jax.numpy as jnp
from jax import lax
from jax.experimental import pallas as pl
from jax.experimental.pallas import tpu as pltpu

F32 = jnp.float32
MXU_DTYPE = jnp.bfloat16

GRID_W = 64
HEAD_DIM = 64
NA_HEADS = 6
NA_W = NA_HEADS * HEAD_DIM
NA_ROWS = 8
NA_COLS = 16
S5_GROUP_CH = 16
S5_CH = 256
S5_GROUPS = S5_CH // S5_GROUP_CH
S5_STATE = 64
S5_EIG_MAX = -1e-4
SW_HEADS = 6
SW_KV_HEADS = 2
SW_GRP = SW_HEADS // SW_KV_HEADS
SW_W = SW_HEADS * HEAD_DIM
SW_KV_W = SW_KV_HEADS * HEAD_DIM
SW_WINDOW = 128
SW_BLK = 128
ROPE_BASE = 10000.0
N_EXPERTS = 32
TOP_K = 4
MOE_BLK = 512
SWIGLU_LIMIT = 7.0
SWIGLU_ALPHA = 1.702
RMS_EPS = 1e-6
NEG_INF = -1e30

LANES = 128
ROW_TILE = 768
MOE_TILE = 256
S5_CHUNK = 16
NA_QROWS = 4
NA_KROWS = NA_QROWS + NA_ROWS
SW_QBLK = 2 * SW_BLK
SW_KBLK = SW_QBLK + 2 * SW_WINDOW
MOD_ROWS = 8
VMEM_LIMIT = 56 << 20
SUBLANES = 8

SW_HEAD_ORDER = tuple(g * SW_GRP + t for t in range(SW_GRP) for g in range(SW_KV_HEADS))


def _params(*sem):
    return pltpu.CompilerParams(dimension_semantics=sem, vmem_limit_bytes=VMEM_LIMIT)


def _dot(a, b):
    return jnp.dot(a, b, preferred_element_type=F32)


def _dot_nt(a, b):
    return lax.dot_general(a, b, (((1,), (1,)), ((), ())), preferred_element_type=F32)


def _split(a):
    hi = a.astype(MXU_DTYPE)
    lo = (a - hi.astype(F32)).astype(MXU_DTYPE)
    return hi, lo


def _dot3(a, b, nt=False):
    f = _dot_nt if nt else _dot
    ah, al = _split(a)
    bh, bl = _split(b)
    return f(ah, bh) + (f(ah, bl) + f(al, bh))


def _mod_kernel(cond_ref, w_ref, b_ref, o_ref):
    c = cond_ref[...]
    a = c * jax.nn.sigmoid(c)
    o_ref[0] = _dot3(a, w_ref[0]) + b_ref[0]


def _mod_call(cond, w_mod, b_mod):
    depth, d, n = w_mod.shape
    tn = n // 6
    return pl.pallas_call(
        _mod_kernel,
        grid=(depth, n // tn),
        in_specs=[pl.BlockSpec((MOD_ROWS, d), lambda l, j: (0, 0)),
                  pl.BlockSpec((1, d, tn), lambda l, j: (l, 0, j)),
                  pl.BlockSpec((1, 1, tn), lambda l, j: (l, 0, j))],
        out_specs=pl.BlockSpec((1, MOD_ROWS, tn), lambda l, j: (l, 0, j)),
        out_shape=jax.ShapeDtypeStruct((depth, MOD_ROWS, n), F32),
        compiler_params=_params("parallel", "parallel"),
        name="mod",
    )(cond, w_mod, b_mod.reshape(depth, 1, n))


C_AQ = 0
C_AK = C_AQ + NA_W
C_AV = C_AK + NA_W
C_SQ = C_AV + NA_W
C_SQR = C_SQ + SW_W
C_SK = C_SQR + SW_W
C_SKR = C_SK + SW_KV_W
C_SV = C_SKR + SW_KV_W
C_SU = C_SV + SW_KV_W
C_END = C_SU + S5_CH


def _rms(x, g):
    return x * lax.rsqrt(jnp.mean(x * x, axis=-1, keepdims=True) + RMS_EPS) * g


def _mod_segments(modb_ref, modc_ref, ctx_len):
    r = ctx_len % ROW_TILE
    out = []
    for a, b in ([(0, ROW_TILE)] if r == 0 else [(0, r), (r, ROW_TILE)]):
        is_ctx = pl.program_id(1) * ROW_TILE + a < ctx_len
        out.append((a, b, jnp.where(is_ctx, modc_ref[0], modb_ref[0])))
    return out


def _by_segment(segs, fn):
    return jnp.concatenate([fn(a, b, m) for a, b, m in segs], axis=0)


def _inproj_kernel(x_ref, modb_ref, modc_ref, g_ref, w_ref, cos_ref, sin_ref,
                   naq_ref, nak_ref, nav_ref, swq_ref, swk_ref, swv_ref, su_ref, *, ctx_len):
    segs = _mod_segments(modb_ref, modc_ref, ctx_len)
    y = _rms(x_ref[0], g_ref[...])
    h = _by_segment(segs, lambda a, b, m: y[a:b] * (1.0 + m[1:2]) + m[0:1]).astype(MXU_DTYPE)
    p = _dot(h, w_ref[...])
    cos = cos_ref[...]
    sin = sin_ref[...]
    cos3 = jnp.concatenate([cos] * (SW_W // LANES), axis=1)
    sin3 = jnp.concatenate([sin] * (SW_W // LANES), axis=1)
    qk_scale = HEAD_DIM ** -0.5
    naq_ref[0] = (p[:, C_AQ:C_AK] * qk_scale).astype(naq_ref.dtype)
    nak_ref[0] = p[:, C_AK:C_AV].astype(nak_ref.dtype)
    nav_ref[0] = p[:, C_AV:C_SQ].astype(nav_ref.dtype)
    swq_ref[0] = ((p[:, C_SQ:C_SQR] * cos3 + p[:, C_SQR:C_SK] * sin3) * qk_scale).astype(swq_ref.dtype)
    swk_ref[0] = (p[:, C_SK:C_SKR] * cos + p[:, C_SKR:C_SV] * sin).astype(swk_ref.dtype)
    swv_ref[0] = p[:, C_SV:C_SU].astype(swv_ref.dtype)
    su_ref[0] = p[:, C_SU:C_END]


def _inproj_call(xs, mod_l, g, w_cat, cos2, sin2, ctx_len):
    bsz, s, d = xs.shape
    tm = ROW_TILE
    row = lambda b, j: (b, j, 0)
    const = lambda b, j: (0, 0)
    widths = (NA_W, NA_W, NA_W, SW_W, SW_KV_W, SW_KV_W, S5_CH)
    dtypes = (MXU_DTYPE,) * 6 + (F32,)
    return pl.pallas_call(
        functools.partial(_inproj_kernel, ctx_len=ctx_len),
        grid=(bsz, s // tm),
        in_specs=[pl.BlockSpec((1, tm, d), row),
                  pl.BlockSpec((1, 6, d), lambda b, j: (b, 0, 0)),
                  pl.BlockSpec((1, 6, d), lambda b, j: (bsz, 0, 0)),
                  pl.BlockSpec((1, d), const),
                  pl.BlockSpec((d, C_END), const),
                  pl.BlockSpec((tm, LANES), lambda b, j: (j, 0)),
                  pl.BlockSpec((tm, LANES), lambda b, j: (j, 0))],
        out_specs=[pl.BlockSpec((1, tm, w), row) for w in widths],
        out_shape=[jax.ShapeDtypeStruct((bsz, s, w), t) for w, t in zip(widths, dtypes)],
        compiler_params=_params("parallel", "parallel"),
        name="inproj",
    )(xs, mod_l, mod_l, g, w_cat, cos2, sin2)


def _half_masks():
    lane = lax.broadcasted_iota(jnp.int32, (1, LANES), 1)
    return lane < HEAD_DIM, lane >= HEAD_DIM


def _na_kernel(q_ref, k_ref, v_ref, bias_ref, o_ref, *, ctx_len, rows):
    i = pl.program_id(1)
    tq = NA_QROWS * GRID_W
    n_ctx_q = ctx_len // tq
    masks = _half_masks()
    nk = NA_KROWS * GRID_W

    def run(local):
        q = q_ref[0]
        if local:
            r0 = (i - n_ctx_q) * NA_QROWS
            start0 = jnp.clip(r0 - NA_ROWS // 2, 0, rows - NA_KROWS)
            start = pl.multiple_of(ctx_len + start0 * GRID_W, GRID_W)
            tab_idx, row_mask = {}, {}
            for a in range(NA_QROWS):
                r = r0 + a
                s_r = jnp.clip(r - NA_ROWS // 2, 0, rows - NA_ROWS)
                for p in range(NA_KROWS // 2):
                    kr = start0 + 2 * p
                    ok_lo = (kr >= s_r) & (kr < s_r + NA_ROWS)
                    ok_hi = (kr + 1 >= s_r) & (kr + 1 < s_r + NA_ROWS)
                    tab_idx[a, p] = jnp.clip(kr - r + NA_ROWS, 0, 2 * NA_ROWS - 1)
                    row_mask[a, p] = jnp.where(masks[0], jnp.where(ok_lo, 0.0, NEG_INF),
                                               jnp.where(ok_hi, 0.0, NEG_INF))
        outs = []
        for t in range(NA_W // LANES):
            sl = slice(LANES * t, LANES * (t + 1))
            qt = q[:, sl]
            zero = jnp.zeros_like(qt)
            qm = jnp.concatenate([jnp.where(masks[0], qt, zero), jnp.where(masks[1], qt, zero)], axis=0)
            kc = k_ref[0, 0:ctx_len, sl]
            vc = v_ref[0, 0:ctx_len, sl]
            s_cx = _dot_nt(qm, kc)
            m = jnp.max(s_cx, axis=-1, keepdims=True)
            if local:
                kw = k_ref[0, pl.ds(start, nk), sl]
                vw = v_ref[0, pl.ds(start, nk), sl]
                bias = jnp.concatenate(
                    [jnp.concatenate([bias_ref[2 * t + hh, pl.ds(tab_idx[a, p], 1)][0] + row_mask[a, p]
                                      for p in range(NA_KROWS // 2)], axis=-1)
                     for hh in range(2) for a in range(NA_QROWS)], axis=0)
                s_nb = _dot_nt(qm, kw) + bias
                m = jnp.maximum(m, jnp.max(s_nb, axis=-1, keepdims=True))
                p_nb = jnp.exp(s_nb - m)
            p_cx = jnp.exp(s_cx - m)
            den = jnp.sum(p_cx, axis=-1, keepdims=True)
            o = _dot(p_cx.astype(MXU_DTYPE), vc)
            if local:
                den = den + jnp.sum(p_nb, axis=-1, keepdims=True)
                o = o + _dot(p_nb.astype(MXU_DTYPE), vw)
            o = o / den
            outs.append(jnp.where(masks[0], o[:tq], o[tq:]))
        o_ref[0] = jnp.concatenate(outs, axis=-1).astype(o_ref.dtype)

    @pl.when(i < n_ctx_q)
    def _():
        run(False)

    @pl.when(i >= n_ctx_q)
    def _():
        run(True)


def _na_call(q, k, v, bias_tab, ctx_len):
    bsz, s, w = q.shape
    rows = (s - ctx_len) // GRID_W
    tq = NA_QROWS * GRID_W
    assert rows >= NA_KROWS and rows % NA_QROWS == 0 and ctx_len % tq == 0
    whole = lambda b, i: (b, 0, 0)
    return pl.pallas_call(
        functools.partial(_na_kernel, ctx_len=ctx_len, rows=rows),
        grid=(bsz, s // tq),
        in_specs=[pl.BlockSpec((1, tq, w), lambda b, i: (b, i, 0)),
                  pl.BlockSpec((1, s, w), whole),
                  pl.BlockSpec((1, s, w), whole),
                  pl.BlockSpec(bias_tab.shape, lambda b, i: (0, 0, 0, 0))],
        out_specs=pl.BlockSpec((1, tq, w), lambda b, i: (b, i, 0)),
        out_shape=jax.ShapeDtypeStruct((bsz, s, w), q.dtype),
        compiler_params=_params("parallel", "arbitrary"),
        name="na_attn",
    )(q, k, v, bias_tab)


def _na_bias_table(rpb):
    qcol = np.arange(GRID_W)[:, None]
    kcol = np.arange(GRID_W)[None, :]
    ws = np.clip(qcol - NA_COLS // 2, 0, GRID_W - NA_COLS)
    valid = (kcol >= ws) & (kcol < ws + NA_COLS)
    dc = np.clip(kcol - qcol + NA_COLS - 1, 0, 2 * NA_COLS - 2)
    full = jnp.where(valid[None, None], rpb[:, :, dc].astype(F32), NEG_INF)
    edge = jnp.full_like(full[:, :1], NEG_INF)
    full = jnp.concatenate([edge, full, edge], axis=1)
    return jnp.concatenate([full[:, :-1], full[:, 1:]], axis=-1)


def _sw_kernel(sink_ref, q_ref, k_ref, v_ref, o_ref, *, ctx_len, seq):
    i = pl.program_id(1)
    tq = SW_QBLK
    n_ctx_q = ctx_len // tq
    masks = _half_masks()
    nk = SW_KBLK
    first_head = lax.broadcasted_iota(jnp.int32, (2 * tq, 1), 0) < tq

    def run(local):
        q = q_ref[0]
        kc = k_ref[0, 0:ctx_len, :]
        vc = v_ref[0, 0:ctx_len, :]
        if local:
            n = i - n_ctx_q
            start_lat = jnp.clip(n * tq - SW_WINDOW, 0, seq - nk)
            start = pl.multiple_of(ctx_len + start_lat, SW_BLK)
            kw = k_ref[0, pl.ds(start, nk), :]
            vw = v_ref[0, pl.ds(start, nk), :]
            row = lax.broadcasted_iota(jnp.int32, (2 * tq, 1), 0)
            qpos = n * tq + jnp.where(first_head, row, row - tq)
            kpos = start_lat + lax.broadcasted_iota(jnp.int32, (1, nk), 1)
            valid = jnp.abs(qpos - kpos) <= SW_WINDOW
        outs = []
        for t in range(SW_W // LANES):
            qt = q[:, LANES * t:LANES * (t + 1)]
            zero = jnp.zeros_like(qt)
            qm = jnp.concatenate([jnp.where(masks[0], qt, zero), jnp.where(masks[1], qt, zero)], axis=0)
            sink = jnp.where(first_head, sink_ref[SW_HEAD_ORDER[2 * t]], sink_ref[SW_HEAD_ORDER[2 * t + 1]])
            s_cx = _dot_nt(qm, kc)
            m = jnp.maximum(jnp.max(s_cx, axis=-1, keepdims=True), sink)
            if local:
                s_loc = jnp.where(valid, _dot_nt(qm, kw), NEG_INF)
                m = jnp.maximum(m, jnp.max(s_loc, axis=-1, keepdims=True))
                p_loc = jnp.exp(s_loc - m)
            p_cx = jnp.exp(s_cx - m)
            den = jnp.sum(p_cx, axis=-1, keepdims=True) + jnp.exp(sink - m)
            o = _dot(p_cx.astype(MXU_DTYPE), vc)
            if local:
                den = den + jnp.sum(p_loc, axis=-1, keepdims=True)
                o = o + _dot(p_loc.astype(MXU_DTYPE), vw)
            o = o / den
            outs.append(jnp.where(masks[0], o[:tq], o[tq:]))
        o_ref[0] = jnp.concatenate(outs, axis=-1).astype(o_ref.dtype)

    @pl.when(i < n_ctx_q)
    def _():
        run(False)

    @pl.when(i >= n_ctx_q)
    def _():
        run(True)


def _sw_call(sinks, q, k, v, ctx_len):
    bsz, s, w = q.shape
    seq = s - ctx_len
    assert seq >= SW_KBLK and seq % SW_QBLK == 0 and ctx_len % SW_QBLK == 0
    whole = lambda b, i: (b, 0, 0)
    return pl.pallas_call(
        functools.partial(_sw_kernel, ctx_len=ctx_len, seq=seq),
        grid=(bsz, s // SW_QBLK),
        in_specs=[pl.BlockSpec(memory_space=pltpu.SMEM),
                  pl.BlockSpec((1, SW_QBLK, w), lambda b, i: (b, i, 0)),
                  pl.BlockSpec((1, s, SW_KV_W), whole),
                  pl.BlockSpec((1, s, SW_KV_W), whole)],
        out_specs=pl.BlockSpec((1, SW_QBLK, w), lambda b, i: (b, i, 0)),
        out_shape=jax.ShapeDtypeStruct((bsz, s, w), q.dtype),
        compiler_params=_params("parallel", "arbitrary"),
        name="sw_attn",
    )(sinks.astype(F32), q, k, v)


def _s5_kernel(u_ref, m_ref, wsr_ref, wsi_ref, wor_ref, woi_ref, lr_ref, li_ref, o_ref,
               sre, sim, xre, xim, acc, *, n_ctx_chunks):
    d = pl.program_id(1)
    bsz, nc, _ = u_ref.shape
    for b in range(bsz):
        ub = u_ref[b]
        sre[b] = _dot(ub, wsr_ref[0, 0])
        sim[b] = _dot(ub, wsi_ref[0, 0])
    lr = lr_ref[0, 0]
    li = li_ref[0, 0]

    def step(c, carry):
        new = []
        for b in range(bsz):
            xr, xi = carry[2 * b], carry[2 * b + 1]
            xre[b, pl.ds(c, 1), :] = xr
            xim[b, pl.ds(c, 1), :] = xi
            sr = sre[b, pl.ds(c, 1), :]
            si = sim[b, pl.ds(c, 1), :]
            new.append(lr * xr - li * xi + sr)
            new.append(lr * xi + li * xr + si)
        return tuple(new)

    zero = tuple(jnp.zeros((1, LANES), F32) for _ in range(2 * bsz))

    @pl.when(d == 0)
    def _():
        lax.fori_loop(0, nc, step, zero)

    @pl.when(d == 1)
    def _():
        carry = lax.fori_loop(0, n_ctx_chunks, lambda k, cr: step(n_ctx_chunks - 1 - k, cr), zero)
        lax.fori_loop(0, nc - n_ctx_chunks, lambda k, cr: step(nc - 1 - k, cr), carry)

    half = u_ref.shape[2] // 2
    for b in range(bsz):
        ub = u_ref[b]
        y_intra = jnp.concatenate([_dot(ub[:, :half], m_ref[0, 0]), _dot(ub[:, half:], m_ref[0, 1])], axis=-1)
        y = (y_intra + _dot(xre[b].astype(MXU_DTYPE), wor_ref[0, 0])
             + _dot(xim[b].astype(MXU_DTYPE), woi_ref[0, 0]))

        @pl.when(d == 0)
        def _():
            acc[b] = y

        @pl.when(d == 1)
        def _():
            o_ref[b] = (acc[b] + y).astype(o_ref.dtype)


def _s5_call(u_t, wts, n_ctx_chunks):
    m, wsr, wsi, wor, woi, lr, li = wts
    bsz, nc, width = u_t.shape
    pw = 2 * S5_CHUNK * S5_GROUP_CH
    n_pairs = width // pw
    blk = lambda shp: pl.BlockSpec((1, 1) + shp, lambda j, d: (d, j, 0, 0))
    return pl.pallas_call(
        functools.partial(_s5_kernel, n_ctx_chunks=n_ctx_chunks),
        grid=(n_pairs, 2),
        in_specs=[pl.BlockSpec((bsz, nc, pw), lambda j, d: (0, 0, j)),
                  pl.BlockSpec((1, 2, pw // 2, pw // 2), lambda j, d: (d, j, 0, 0)),
                  blk((pw, LANES)), blk((pw, LANES)), blk((LANES, pw)), blk((LANES, pw)),
                  blk((1, LANES)), blk((1, LANES))],
        out_specs=pl.BlockSpec((bsz, nc, pw), lambda j, d: (0, 0, j)),
        out_shape=jax.ShapeDtypeStruct((bsz, nc, width), MXU_DTYPE),
        scratch_shapes=[pltpu.VMEM((bsz, nc, LANES), F32) for _ in range(4)] + [pltpu.VMEM((bsz, nc, pw), F32)],
        compiler_params=_params("parallel", "arbitrary"),
        name="s5_scan",
    )(u_t, m, wsr, wsi, wor, woi, lr, li)


def _s5_weights(a_re, a_im, log_step, b_re, b_im, c_re, c_im):
    lc, g, p, h = S5_CHUNK, S5_GROUPS, S5_STATE, S5_GROUP_CH
    lam = lax.complex(jnp.minimum(a_re.astype(F32), S5_EIG_MAX), a_im.astype(F32))
    step = jnp.exp(log_step.astype(F32))[..., None]
    lam_bar = jnp.exp(lam * step)
    b_bar = ((lam_bar - 1.0) / lam)[..., None] * lax.complex(b_re.astype(F32), b_im.astype(F32))
    cc = lax.complex(c_re.astype(F32), c_im.astype(F32))
    dd = jnp.arange(lc + 1, dtype=F32)
    pw = jnp.exp((lam * step)[..., None] * dd)
    kern = jnp.real(jnp.einsum('zgop,zgpd,zgpi->zgdoi', cc, pw[..., :lc], b_bar))
    jj = np.arange(lc)[:, None]
    ii = np.arange(lc)[None, :]
    mats, wst, wout = [], [], []
    for z in range(2):
        lag = (ii - jj) if z == 0 else (jj - ii)
        ok = lag >= 0
        kz = kern[z][:, np.where(ok, lag, 0)]
        kz = jnp.where(ok[None, :, :, None, None], kz, 0.0)
        mats.append(kz.transpose(0, 1, 4, 2, 3).reshape(g, lc * h, lc * h))
        d_state = (lc - 1 - np.arange(lc)) if z == 0 else np.arange(lc)
        ws = pw[z][:, :, d_state][..., None] * b_bar[z][:, :, None, :]
        wst.append(ws.transpose(0, 2, 3, 1).reshape(g, lc * h, p))
        d_out = (np.arange(lc) + 1) if z == 0 else (lc - np.arange(lc))
        wo = cc[z][:, :, :, None] * pw[z][:, None, :, :][..., d_out]
        wout.append(wo.transpose(0, 2, 3, 1).reshape(g, p, lc * h))
    mats = jnp.stack(mats)
    wst = jnp.stack(wst)
    wout = jnp.stack(wout)

    def pair_rows(w):
        w = w.reshape(2, g // 2, 2, lc * h, p)
        z0 = jnp.zeros_like(w[:, :, 0])
        top = jnp.concatenate([w[:, :, 0], z0], axis=-1)
        bot = jnp.concatenate([z0, w[:, :, 1]], axis=-1)
        return jnp.concatenate([top, bot], axis=-2)

    def pair_cols(w):
        w = w.reshape(2, g // 2, 2, p, lc * h)
        z0 = jnp.zeros_like(w[:, :, 0])
        top = jnp.concatenate([w[:, :, 0], z0], axis=-1)
        bot = jnp.concatenate([z0, w[:, :, 1]], axis=-1)
        return jnp.concatenate([top, bot], axis=-2)

    lam_c = pw[..., lc].reshape(2, g // 2, 1, 2 * p)
    cast = lambda w: w.astype(MXU_DTYPE)
    return (cast(mats), cast(pair_rows(jnp.real(wst))), cast(pair_rows(jnp.imag(wst))),
            cast(pair_cols(jnp.real(wout))), cast(pair_cols(-jnp.imag(wout))),
            jnp.real(lam_c), jnp.imag(lam_c))


def _gelu_tanh(x):
    cdf = 0.5 * (1.0 + jnp.tanh(math.sqrt(2.0 / math.pi) * (x + 0.044715 * (x * x * x))))
    return x * cdf


def _outproj_kernel(x_ref, ya_ref, ys_ref, su_ref, yc_ref, modb_ref, modc_ref, woa_ref, wob_ref, woc_ref,
                    wglu_ref, bglu_ref, dsk_ref, g_ref, wr_ref, br_ref,
                    xo_ref, h_hbm, topi_ref, topw_ref, rank_ref, cnt_ref, carry, hbuf, hsem, *, ctx_len):
    segs = _mod_segments(modb_ref, modc_ref, ctx_len)
    y = dsk_ref[...] * su_ref[0] + ys_ref[0].astype(F32)
    gl = _gelu_tanh(y)
    yb = gl * jax.nn.sigmoid(_dot(gl.astype(MXU_DTYPE), wglu_ref[...]) + bglu_ref[...])
    mix = (_dot(ya_ref[0], woa_ref[...]) + _dot(yb.astype(MXU_DTYPE), wob_ref[...])
           + _dot(yc_ref[0], woc_ref[...]))
    x_in = x_ref[0]
    x = _by_segment(segs, lambda a, b, m: x_in[a:b] + m[2:3] * mix[a:b])
    xo_ref[0] = x
    y = _rms(x, g_ref[...])
    h = _by_segment(segs, lambda a, b, m: y[a:b] * (1.0 + m[4:5]) + m[3:4])
    dl = h_hbm.shape[-1]
    tile = pl.program_id(1)
    step = pl.program_id(0) * pl.num_programs(1) + tile
    slot = lax.rem(step, 2)

    def h_wait(sl_):
        pltpu.make_async_copy(hbuf.at[sl_], hbuf.at[sl_], hsem.at[sl_]).wait()

    @pl.when(step >= 2)
    def _():
        h_wait(slot)

    hbuf[slot] = h
    for sl in range(SUBLANES):
        pltpu.make_async_copy(hbuf.at[slot, :, pl.ds(dl * sl, dl)],
                              h_hbm.at[pl.program_id(0), pl.ds(tile * ROW_TILE, ROW_TILE), sl, :],
                              hsem.at[slot]).start()

    @pl.when(step == pl.num_programs(0) * pl.num_programs(1) - 1)
    def _():
        h_wait(1 - slot)
        h_wait(slot)

    logits = _dot3(wr_ref[...], h, nt=True) + br_ref[...]
    n_e, tm = logits.shape
    e_iota = lax.broadcasted_iota(jnp.int32, (n_e, tm), 0)
    vals, idxs = [], []
    for _ in range(TOP_K):
        mx = jnp.max(logits, axis=0, keepdims=True)
        ix = jnp.min(jnp.where(logits == mx, e_iota, n_e), axis=0, keepdims=True)
        vals.append(mx)
        idxs.append(ix)
        logits = jnp.where(e_iota == ix, -jnp.inf, logits)
    ex = [jnp.exp(v - vals[0]) for v in vals]
    den = ex[0] + ex[1] + ex[2] + ex[3]
    topi_ref[0] = jnp.concatenate(idxs, axis=0)
    topw_ref[0] = jnp.concatenate([e / den for e in ex], axis=0)

    @pl.when((pl.program_id(0) == 0) & (pl.program_id(1) == 0))
    def _():
        carry[...] = jnp.zeros_like(carry)

    sel = [ix == e_iota for ix in idxs]
    onehot = jnp.where(sel[0] | sel[1] | sel[2] | sel[3], 1.0, 0.0)
    before = (lax.broadcasted_iota(jnp.int32, (tm, tm), 0) < lax.broadcasted_iota(jnp.int32, (tm, tm), 1))
    pfx = _dot(onehot.astype(MXU_DTYPE), jnp.where(before, 1.0, 0.0).astype(MXU_DTYPE)) + carry[:, 0:1]
    rank_ref[0] = jnp.concatenate(
        [jnp.sum(jnp.where(sel[k], pfx, 0.0), axis=0, keepdims=True) for k in range(TOP_K)], axis=0)
    carry[...] = carry[...] + jnp.sum(onehot, axis=1, keepdims=True)
    cnt_ref[...] = carry[...]


def _outproj_call(xs, ya, ys, su, yc, mod_l, wts, ctx_len):
    bsz, s, d = xs.shape
    tm = ROW_TILE
    row = lambda b, j: (b, j, 0)
    const = lambda b, j: (0, 0)
    full = lambda a: pl.BlockSpec(a.shape, const)
    return pl.pallas_call(
        functools.partial(_outproj_kernel, ctx_len=ctx_len),
        grid=(bsz, s // tm),
        in_specs=[pl.BlockSpec((1, tm, d), row),
                  pl.BlockSpec((1, tm, NA_W), row),
                  pl.BlockSpec((1, tm, S5_CH), row),
                  pl.BlockSpec((1, tm, S5_CH), row),
                  pl.BlockSpec((1, tm, SW_W), row),
                  pl.BlockSpec((1, 6, d), lambda b, j: (b, 0, 0)),
                  pl.BlockSpec((1, 6, d), lambda b, j: (bsz, 0, 0))] + [full(a) for a in wts],
        out_specs=[pl.BlockSpec((1, tm, d), row),
                   pl.BlockSpec(memory_space=pl.ANY),
                   pl.BlockSpec((1, TOP_K, tm), lambda b, j: (b, 0, j)),
                   pl.BlockSpec((1, TOP_K, tm), lambda b, j: (b, 0, j)),
                   pl.BlockSpec((1, TOP_K, tm), lambda b, j: (b, 0, j)),
                   pl.BlockSpec((N_EXPERTS, LANES), lambda b, j: (0, 0))],
        out_shape=[jax.ShapeDtypeStruct((bsz, s, d), F32),
                   jax.ShapeDtypeStruct((bsz, s, SUBLANES, d // SUBLANES), F32),
                   jax.ShapeDtypeStruct((bsz, TOP_K, s), jnp.int32),
                   jax.ShapeDtypeStruct((bsz, TOP_K, s), F32),
                   jax.ShapeDtypeStruct((bsz, TOP_K, s), F32),
                   jax.ShapeDtypeStruct((N_EXPERTS, LANES), F32)],
        scratch_shapes=[pltpu.VMEM((N_EXPERTS, LANES), F32), pltpu.VMEM((2, tm, d), F32),
                        pltpu.SemaphoreType.DMA((2,))],
        compiler_params=_params("arbitrary", "arbitrary"),
        name="outproj",
    )(xs, ya, ys, su, yc, mod_l, mod_l, *wts)


def _route_kernel(topi_ref, rank_ref, cnt_ref, dest_ref, blk_ref, ends_ref):
    idx = topi_ref[0]
    tm = idx.shape[1]
    e_iota = lax.broadcasted_iota(jnp.int32, (N_EXPERTS, tm), 0)
    counts = cnt_ref[:, 0:1]
    padded = jnp.ceil(counts * (1.0 / MOE_BLK)) * MOE_BLK
    r_i = lax.broadcasted_iota(jnp.int32, (N_EXPERTS, N_EXPERTS), 0)
    c_i = lax.broadcasted_iota(jnp.int32, (N_EXPERTS, N_EXPERTS), 1)
    padded_row = jnp.sum(jnp.where(r_i == c_i, padded, 0.0), axis=0, keepdims=True)
    pstart = jnp.sum(jnp.where(c_i < r_i, padded_row, 0.0), axis=1, keepdims=True)
    ends = pstart + padded
    rk = rank_ref[0]
    dest_ref[0] = jnp.concatenate(
        [jnp.sum(jnp.where(idx[k:k + 1] == e_iota, pstart, 0.0), axis=0, keepdims=True) + rk[k:k + 1]
         for k in range(TOP_K)], axis=1).astype(jnp.int32)
    nb = blk_ref.shape[1]
    blk_start = (lax.broadcasted_iota(jnp.int32, (N_EXPERTS, nb), 1) * MOE_BLK).astype(F32)
    owner = jnp.sum(jnp.where(ends <= blk_start, 1.0, 0.0), axis=0, keepdims=True)
    blk_ref[...] = jnp.minimum(owner, N_EXPERTS - 1.0).astype(jnp.int32)
    ends_ref[...] = jnp.broadcast_to(ends, ends_ref.shape).astype(jnp.int32)


def _route_call(topi, rank, cnt, n_blocks):
    bsz, _, s = topi.shape
    tm = MOE_TILE
    n_tiles = s // tm
    nb_pad = -(-n_blocks // LANES) * LANES
    tok = pl.BlockSpec((1, TOP_K, tm), lambda b, j: (b, 0, j))
    return pl.pallas_call(
        _route_kernel,
        grid=(bsz, n_tiles),
        in_specs=[tok, tok, pl.BlockSpec((N_EXPERTS, LANES), lambda b, j: (0, 0))],
        out_specs=[pl.BlockSpec((1, 1, TOP_K * tm), lambda b, j: (b * n_tiles + j, 0, 0)),
                   pl.BlockSpec((1, nb_pad), lambda b, j: (0, 0)),
                   pl.BlockSpec((N_EXPERTS, LANES), lambda b, j: (0, 0))],
        out_shape=[jax.ShapeDtypeStruct((bsz * n_tiles, 1, TOP_K * tm), jnp.int32),
                   jax.ShapeDtypeStruct((1, nb_pad), jnp.int32),
                   jax.ShapeDtypeStruct((N_EXPERTS, LANES), jnp.int32)],
        compiler_params=_params("arbitrary", "arbitrary"),
        name="route_dest",
    )(topi, rank, cnt)


def _dispatch_kernel(ends_ref, dest_ref, h_ref, xg_ref, zbuf, stage, sem, zsem, *, n_blocks):
    tm = h_ref.shape[1]

    @pl.when((pl.program_id(0) == 0) & (pl.program_id(1) == 0))
    def _():
        zbuf[...] = jnp.zeros_like(zbuf)

        def fill(row):
            return pltpu.make_async_copy(zbuf, xg_ref.at[pl.ds(pl.multiple_of(row, MOE_BLK), MOE_BLK)], zsem)

        def each(fn):
            for e in range(N_EXPERTS):
                begin = ends_ref[e - 1] if e else 0

                @pl.when(ends_ref[e] > begin)
                def _():
                    fn(fill(ends_ref[e] - MOE_BLK))

            def dead(i, c):
                fn(fill(i * MOE_BLK))
                return c

            lax.fori_loop(ends_ref[N_EXPERTS - 1] // MOE_BLK, n_blocks, dead, 0)

        each(lambda cp: cp.start())
        each(lambda cp: cp.wait())

    step = pl.program_id(0) * pl.num_programs(1) + pl.program_id(1)
    n_steps = pl.num_programs(0) * pl.num_programs(1)
    slot = lax.rem(step, 2)
    stage[slot] = h_ref[0]

    def body(t, c):
        for k in range(TOP_K):
            pltpu.make_async_copy(stage.at[slot, t], xg_ref.at[dest_ref[0, 0, k * tm + t]],
                                  sem.at[slot]).start(priority=k % 2)
        return c

    lax.fori_loop(0, tm, body, 0, unroll=8)

    def wait_tile(sl):
        pltpu.make_async_copy(xg_ref.at[pl.ds(0, TOP_K * tm)], xg_ref.at[pl.ds(0, TOP_K * tm)], sem.at[sl]).wait()

    @pl.when(step > 0)
    def _():
        wait_tile(1 - slot)

    @pl.when(step == n_steps - 1)
    def _():
        wait_tile(slot)


def _dispatch_call(ends, dest, h, n_blocks):
    bsz, s, _, dl = h.shape
    tm = MOE_TILE
    n_tiles = s // tm
    grid_spec = pltpu.PrefetchScalarGridSpec(
        num_scalar_prefetch=1,
        grid=(bsz, n_tiles),
        in_specs=[pl.BlockSpec((1, 1, TOP_K * tm), lambda b, j, en: (b * n_tiles + j, 0, 0), memory_space=pltpu.SMEM),
                  pl.BlockSpec((1, tm, SUBLANES, dl), lambda b, j, en: (b, j, 0, 0))],
        out_specs=pl.BlockSpec(memory_space=pl.ANY),
        scratch_shapes=[pltpu.VMEM((MOE_BLK, SUBLANES, dl), h.dtype), pltpu.VMEM((2, tm, SUBLANES, dl), h.dtype),
                        pltpu.SemaphoreType.DMA((2,)),
                        pltpu.SemaphoreType.DMA(())],
    )
    return pl.pallas_call(
        functools.partial(_dispatch_kernel, n_blocks=n_blocks),
        grid_spec=grid_spec,
        out_shape=jax.ShapeDtypeStruct((n_blocks * MOE_BLK, SUBLANES, dl), h.dtype),
        compiler_params=_params("arbitrary", "arbitrary"),
        name="moe_dispatch",
    )(ends, dest, h)


def _moe_kernel(blk_exp_ref, nact_ref, x_hbm, wgu_ref, bgu_ref, wd_ref, bd_ref, y_hbm, wgu_c, wd_c,
                xbuf, xsem, ybuf, ysem):
    i = pl.program_id(0)
    slot = lax.rem(i, 2)
    dl = x_hbm.shape[-1]

    def fetch(blk, sl_):
        for s in range(SUBLANES):
            pltpu.make_async_copy(x_hbm.at[pl.ds(blk * MOE_BLK, MOE_BLK), s, :],
                                  xbuf.at[sl_, :, pl.ds(dl * s, dl)], xsem.at[sl_]).start()

    @pl.when(i == 0)
    def _():
        fetch(0, 0)

    @pl.when(i + 1 < pl.num_programs(0))
    def _():
        fetch(i + 1, 1 - slot)

    def y_wait(sl_):
        pltpu.make_async_copy(ybuf.at[sl_], ybuf.at[sl_], ysem.at[sl_]).wait()

    @pl.when(i >= 2)
    def _():
        y_wait(slot)

    e = blk_exp_ref[i]
    prev = blk_exp_ref[jnp.maximum(i - 1, 0)]
    d, f2 = wgu_c.shape
    f = f2 // 2
    rows = 128

    @pl.when((i == 0) | (e != prev))
    def _():
        def cv(r, c):
            sl = pl.ds(pl.multiple_of(r * rows, rows), rows)
            wgu_c[sl, :] = wgu_ref[0, 0, sl, :].astype(wgu_c.dtype)
            return c
        lax.fori_loop(0, d // rows, cv, 0)

        def cv2(r, c):
            sl = pl.ds(pl.multiple_of(r * rows, rows), rows)
            wd_c[sl, :] = wd_ref[0, 0, sl, :].astype(wd_c.dtype)
            return c
        lax.fori_loop(0, f // rows, cv2, 0)

    pltpu.make_async_copy(xbuf.at[slot], xbuf.at[slot], xsem.at[slot]).wait()

    @pl.when(i < nact_ref[0])
    def _():
        gu = _dot(xbuf[slot].astype(MXU_DTYPE), wgu_c[...]) + bgu_ref[0, 0]
        gate = jnp.minimum(gu[:, :f], SWIGLU_LIMIT)
        up = jnp.clip(gu[:, f:], -SWIGLU_LIMIT, SWIGLU_LIMIT)
        act = gate * jax.nn.sigmoid(SWIGLU_ALPHA * gate) * (up + 1.0)
        ybuf[slot] = _dot(act.astype(MXU_DTYPE), wd_c[...]) + bd_ref[0, 0]

    @pl.when(i >= nact_ref[0])
    def _():
        ybuf[slot] = jnp.zeros(ybuf.shape[1:], ybuf.dtype)

    for s in range(SUBLANES):
        pltpu.make_async_copy(ybuf.at[slot, :, pl.ds(dl * s, dl)],
                              y_hbm.at[pl.ds(i * MOE_BLK, MOE_BLK), s, :], ysem.at[slot]).start()

    @pl.when(i == pl.num_programs(0) - 1)
    def _():
        y_wait(1 - slot)
        y_wait(slot)


def _moe_call(layer, blk_exp, n_active, xg, w_gate_up, b_gate_up, w_down, b_down):
    n_rows, _, dl = xg.shape
    depth, n_e, d, f2 = w_gate_up.shape
    f = f2 // 2
    n_blocks = n_rows // MOE_BLK
    assert n_blocks >= 2
    wsel = lambda i, be, na: (layer, be[i], 0, 0)
    grid_spec = pltpu.PrefetchScalarGridSpec(
        num_scalar_prefetch=2,
        grid=(n_blocks,),
        in_specs=[pl.BlockSpec(memory_space=pl.ANY),
                  pl.BlockSpec((1, 1, d, f2), wsel),
                  pl.BlockSpec((1, 1, 1, f2), wsel),
                  pl.BlockSpec((1, 1, f, d), wsel),
                  pl.BlockSpec((1, 1, 1, d), wsel)],
        out_specs=pl.BlockSpec(memory_space=pl.ANY),
        scratch_shapes=[pltpu.VMEM((d, f2), MXU_DTYPE), pltpu.VMEM((f, d), MXU_DTYPE),
                        pltpu.VMEM((2, MOE_BLK, d), F32), pltpu.SemaphoreType.DMA((2,)),
                        pltpu.VMEM((2, MOE_BLK, d), F32), pltpu.SemaphoreType.DMA((2,))],
    )
    return pl.pallas_call(
        _moe_kernel,
        grid_spec=grid_spec,
        out_shape=jax.ShapeDtypeStruct((n_rows, SUBLANES, dl), F32),
        compiler_params=_params("arbitrary"),
        name="moe_experts",
    )(blk_exp, n_active, xg, w_gate_up, b_gate_up.reshape(depth, n_e, 1, f2),
      w_down, b_down.reshape(depth, n_e, 1, d))


def _combine_kernel(dest_ref, dnext_ref, w_ref, gate_ref, x_hbm, yg_hbm, xo_hbm,
                    gbuf, xbuf, obuf, gsem, xsem, osem, *, n_tiles):
    tm = MOE_TILE
    dl = gbuf.shape[-1]
    step = pl.program_id(0) * n_tiles + pl.program_id(1)
    n_steps = pl.num_programs(0) * n_tiles
    slot = lax.rem(step, 2)

    def stream_copies(st, sl_, buf, hbm, sem, to_hbm):
        b = lax.div(st, n_tiles)
        r0 = lax.rem(st, n_tiles) * tm
        out = []
        for s in range(SUBLANES):
            rows = hbm.at[b, pl.ds(r0, tm), pl.ds(dl * s, dl)]
            tiles = buf.at[sl_, :, s, :]
            out.append(pltpu.make_async_copy(tiles, rows, sem.at[sl_]) if to_hbm
                       else pltpu.make_async_copy(rows, tiles, sem.at[sl_]))
        return out

    def fetch(dref, st, sl_):
        for cp in stream_copies(st, sl_, xbuf, x_hbm, xsem, False):
            cp.start()

        def body(t, c):
            for k in range(TOP_K):
                pltpu.make_async_copy(yg_hbm.at[dref[0, 0, k * tm + t]], gbuf.at[sl_, k, t],
                                      gsem.at[sl_]).start(priority=k % 2)
            return c

        lax.fori_loop(0, tm, body, 0, unroll=8)

    @pl.when(step == 0)
    def _():
        fetch(dest_ref, 0, 0)

    @pl.when(step + 1 < n_steps)
    def _():
        fetch(dnext_ref, step + 1, 1 - slot)

    def wait_all(buf, sem, sl_):
        pltpu.make_async_copy(buf.at[sl_], buf.at[sl_], sem.at[sl_]).wait()

    wait_all(gbuf, gsem, slot)
    wait_all(xbuf, xsem, slot)

    @pl.when(step >= 2)
    def _():
        wait_all(obuf, osem, slot)

    gate = gate_ref[0]

    def row(t, c):
        acc = gbuf[slot, 0, t] * w_ref[0, 0, t]
        for k in range(1, TOP_K):
            acc = acc + gbuf[slot, k, t] * w_ref[0, 0, k * tm + t]
        obuf[slot, t] = xbuf[slot, t] + gate * acc
        return c

    lax.fori_loop(0, tm, row, 0, unroll=8)
    for cp in stream_copies(step, slot, obuf, xo_hbm, osem, True):
        cp.start()

    @pl.when(step == n_steps - 1)
    def _():
        wait_all(obuf, osem, 1 - slot)
        wait_all(obuf, osem, slot)


def _combine_call(dest, topw, xs, mod_l, yg, ctx_len):
    bsz, s, d = xs.shape
    tm = MOE_TILE
    n_tiles = s // tm
    last = bsz * n_tiles - 1
    assert last >= 1
    dl = d // SUBLANES
    w_flat = topw.reshape(bsz, TOP_K, n_tiles, tm).transpose(0, 2, 1, 3).reshape(bsz * n_tiles, 1, TOP_K * tm)
    gate = mod_l[:, 5].reshape(MOD_ROWS, SUBLANES, dl)
    idx_spec = lambda ahead: pl.BlockSpec(
        (1, 1, TOP_K * tm), lambda b, j: (jnp.minimum(b * n_tiles + j + ahead, last), 0, 0), memory_space=pltpu.SMEM)
    tile_buf = lambda lead: pltpu.VMEM(lead + (tm, SUBLANES, dl), F32)
    return pl.pallas_call(
        functools.partial(_combine_kernel, n_tiles=n_tiles),
        grid=(bsz, n_tiles),
        in_specs=[idx_spec(0), idx_spec(1), idx_spec(0),
                  pl.BlockSpec((1, SUBLANES, dl), lambda b, j: (jnp.where(j < ctx_len // tm, bsz, b), 0, 0)),
                  pl.BlockSpec(memory_space=pl.ANY),
                  pl.BlockSpec(memory_space=pl.ANY)],
        out_specs=pl.BlockSpec(memory_space=pl.ANY),
        out_shape=jax.ShapeDtypeStruct((bsz, s, d), F32),
        scratch_shapes=[tile_buf((2, TOP_K)), tile_buf((2,)), tile_buf((2,)),
                        pltpu.SemaphoreType.DMA((2,)), pltpu.SemaphoreType.DMA((2,)), pltpu.SemaphoreType.DMA((2,))],
        compiler_params=_params("arbitrary", "arbitrary"),
        name="moe_combine",
    )(dest, dest, w_flat, gate, xs, yg)


def _final_kernel(x_ref, g_ref, o_ref):
    x = x_ref[0]
    o_ref[0] = x * lax.rsqrt(jnp.mean(x * x, axis=-1, keepdims=True) + RMS_EPS) * g_ref[...]


def _final_call(xs, g, ctx_len):
    bsz, s, d = xs.shape
    tm = MOE_TILE
    off = ctx_len // tm
    return pl.pallas_call(
        _final_kernel,
        grid=(bsz, (s - ctx_len) // tm),
        in_specs=[pl.BlockSpec((1, tm, d), lambda b, j: (b, j + off, 0)),
                  pl.BlockSpec((1, d), lambda b, j: (0, 0))],
        out_specs=pl.BlockSpec((1, tm, d), lambda b, j: (b, j, 0)),
        out_shape=jax.ShapeDtypeStruct((bsz, s - ctx_len, d), F32),
        compiler_params=_params("parallel", "parallel"),
        name="final_norm",
    )(xs, g)


def _rope_tables(seq, ctx_len):
    t = jnp.arange(seq)
    row = (t // GRID_W).astype(F32)
    col = (t % GRID_W).astype(F32)
    nf = HEAD_DIM // 4
    inv = ROPE_BASE ** (-jnp.arange(nf, dtype=F32) / nf)
    ar = row[:, None] * inv
    ac = col[:, None] * inv
    ang = jnp.concatenate([ar, ar, ac, ac], axis=-1)
    cos = jnp.concatenate([jnp.ones((ctx_len, HEAD_DIM), F32), jnp.cos(ang)], axis=0)
    sin = jnp.concatenate([jnp.zeros((ctx_len, HEAD_DIM), F32), jnp.sin(ang)], axis=0)
    reps = LANES // HEAD_DIM
    return jnp.tile(cos, (1, reps)), jnp.tile(sin, (1, reps))


def _rot_cols(w):
    q = HEAD_DIM // 4
    j = np.arange(HEAD_DIM)
    first = (j % (2 * q)) < q
    src = np.where(first, j + q, j - q)
    sign = np.where(first, -1.0, 1.0).astype(np.float32)
    n_heads = w.shape[1] // HEAD_DIM
    src_all = (np.arange(n_heads)[:, None] * HEAD_DIM + src[None, :]).reshape(-1)
    return w[:, src_all] * jnp.asarray(np.tile(sign, n_heads))


def _head_perm_cols(order):
    return (np.asarray(order)[:, None] * HEAD_DIM + np.arange(HEAD_DIM)[None, :]).reshape(-1)


def _inproj_weight(w_in_l):
    aq, ak, av, su, sq, sk, sv = jnp.split(
        w_in_l, np.cumsum([NA_W, NA_W, NA_W, S5_CH, SW_W, SW_KV_W])[:6].tolist(), axis=1)
    sq = sq[:, _head_perm_cols(SW_HEAD_ORDER)]
    return jnp.concatenate([aq, ak, av, sq, _rot_cols(sq), sk, _rot_cols(sk), sv, su], axis=1).astype(MXU_DTYPE)


def _chunk_major(su):
    bsz, s, _ = su.shape
    u = su.reshape(bsz, s // S5_CHUNK, S5_CHUNK, S5_GROUPS, S5_GROUP_CH).transpose(0, 1, 3, 2, 4)
    return u.reshape(bsz, s // S5_CHUNK, S5_GROUPS * S5_CHUNK * S5_GROUP_CH)


def _token_major(y_t):
    bsz, nc, _ = y_t.shape
    y = y_t.reshape(bsz, nc, S5_GROUPS, S5_CHUNK, S5_GROUP_CH).transpose(0, 1, 3, 2, 4)
    return y.reshape(bsz, nc * S5_CHUNK, S5_CH)


def kernel(x, c, ctx, c_ctx, w_mod, b_mod, g_mix, w_in, w_out, na_rpb, s5_a_re, s5_a_im, s5_log_step,
           s5_b_re, s5_b_im, s5_c_re, s5_c_im, s5_d, s5_w_glu, s5_b_glu, sw_sinks, g_ffn, w_router, b_router,
           w_gate_up, b_gate_up, w_down, b_down, g_final):
    bsz, seq, d = x.shape
    ctx_len = ctx.shape[1]
    depth = w_mod.shape[0]
    s = ctx_len + seq
    assert bsz + 1 <= MOD_ROWS and s % ROW_TILE == 0 and ctx_len % MOE_TILE == 0 and seq % MOE_TILE == 0
    assert seq % GRID_W == 0 and ctx_len % S5_CHUNK == 0

    xs = jnp.concatenate([ctx, x], axis=1)
    cond = jnp.zeros((MOD_ROWS, d), F32).at[:bsz].set(c).at[bsz].set(c_ctx)
    mod = _mod_call(cond, w_mod, b_mod).reshape(depth, MOD_ROWS, 6, d)
    cos2, sin2 = _rope_tables(seq, ctx_len)

    n_assign = bsz * s * TOP_K
    n_blocks = -(-(n_assign + N_EXPERTS * (MOE_BLK - 1)) // MOE_BLK)
    sw_rows = _head_perm_cols(SW_HEAD_ORDER)

    w_cat = jax.vmap(_inproj_weight)(w_in)
    bias_tab = jax.vmap(_na_bias_table)(na_rpb)
    s5w = jax.vmap(_s5_weights)(s5_a_re, s5_a_im, s5_log_step, s5_b_re, s5_b_im, s5_c_re, s5_c_im)
    wo_a = w_out[:, :NA_W].astype(MXU_DTYPE)
    wo_b = w_out[:, NA_W:NA_W + S5_CH].astype(MXU_DTYPE)
    wo_c = w_out[:, NA_W + S5_CH:][:, sw_rows].astype(MXU_DTYPE)
    w_glu = s5_w_glu.astype(MXU_DTYPE)
    w_rt = jnp.swapaxes(w_router, 1, 2).astype(F32)

    for l in range(depth):
        mod_l = mod[l]
        naq, nak, nav, swq, swk, swv, su = _inproj_call(
            xs, mod_l, g_mix[l].reshape(1, d), w_cat[l], cos2, sin2, ctx_len)
        ya = _na_call(naq, nak, nav, bias_tab[l], ctx_len)
        yc = _sw_call(sw_sinks[l], swq, swk, swv, ctx_len)
        ys = _token_major(_s5_call(_chunk_major(su).astype(MXU_DTYPE), tuple(w[l] for w in s5w),
                                   ctx_len // S5_CHUNK))

        out_wts = (wo_a[l], wo_b[l], wo_c[l], w_glu[l], s5_b_glu[l].reshape(1, S5_CH).astype(F32),
                   s5_d[l].reshape(1, S5_CH).astype(F32), g_ffn[l].reshape(1, d),
                   w_rt[l], b_router[l].reshape(N_EXPERTS, 1).astype(F32))
        xs, h, topi, topw, rank, cnt = _outproj_call(xs, ya, ys, su, yc, mod_l, out_wts, ctx_len)

        dest, blk, ends = _route_call(topi, rank, cnt, n_blocks)
        ends = ends[:, 0]
        xg = _dispatch_call(ends, dest, h, n_blocks)
        yg = _moe_call(l, blk[0, :n_blocks], ends[-1:] // MOE_BLK, xg, w_gate_up, b_gate_up, w_down, b_down)
        xs = _combine_call(dest, topw, xs, mod_l, yg, ctx_len)

    return _final_call(xs, g_final.reshape(1, d), ctx_len)
```

```python
import functools
import math

import numpy as np
import jax
import jax.numpy as jnp
from jax import lax
from jax.experimental import pallas as pl
from jax.experimental.pallas import tpu as pltpu

F32 = jnp.float32
MXU_DTYPE = jnp.bfloat16

GRID_W = 64
HEAD_DIM = 64
NA_HEADS = 6
NA_W = NA_HEADS * HEAD_DIM
NA_ROWS = 8
NA_COLS = 16
S5_GROUP_CH = 16
S5_CH = 256
S5_GROUPS = S5_CH // S5_GROUP_CH
S5_STATE = 64
S5_EIG_MAX = -1e-4
SW_HEADS = 6
SW_KV_HEADS = 2
SW_GRP = SW_HEADS // SW_KV_HEADS
SW_W = SW_HEADS * HEAD_DIM
SW_KV_W = SW_KV_HEADS * HEAD_DIM
SW_WINDOW = 128
SW_BLK = 128
ROPE_BASE = 10000.0
N_EXPERTS = 32
TOP_K = 4
MOE_BLK = 512
SWIGLU_LIMIT = 7.0
SWIGLU_ALPHA = 1.702
RMS_EPS = 1e-6
NEG_INF = -1e30

LANES = 128
ROW_TILE = 768
MOE_TILE = 256
S5_CHUNK = 16
NA_QROWS = 4
NA_KROWS = NA_QROWS + NA_ROWS
SW_QBLK = 2 * SW_BLK
SW_KBLK = SW_QBLK + 2 * SW_WINDOW
MOD_ROWS = 8
VMEM_LIMIT = 56 << 20
SUBLANES = 8

SW_HEAD_ORDER = tuple(g * SW_GRP + t for t in range(SW_GRP) for g in range(SW_KV_HEADS))


def _params(*sem):
    return pltpu.CompilerParams(dimension_semantics=sem, vmem_limit_bytes=VMEM_LIMIT)


def _dot(a, b):
    return jnp.dot(a, b, preferred_element_type=F32)


def _dot_nt(a, b):
    return lax.dot_general(a, b, (((1,), (1,)), ((), ())), preferred_element_type=F32)


def _split(a):
    hi = a.astype(MXU_DTYPE)
    lo = (a - hi.astype(F32)).astype(MXU_DTYPE)
    return hi, lo


def _dot3(a, b, nt=False):
    f = _dot_nt if nt else _dot
    ah, al = _split(a)
    bh, bl = _split(b)
    return f(ah, bh) + (f(ah, bl) + f(al, bh))


def _mod_kernel(cond_ref, w_ref, b_ref, o_ref):
    c = cond_ref[...]
    a = c * jax.nn.sigmoid(c)
    o_ref[0] = _dot3(a, w_ref[0]) + b_ref[0]


def _mod_call(cond, w_mod, b_mod):
    depth, d, n = w_mod.shape
    tn = n // 6
    return pl.pallas_call(
        _mod_kernel,
        grid=(depth, n // tn),
        in_specs=[pl.BlockSpec((MOD_ROWS, d), lambda l, j: (0, 0)),
                  pl.BlockSpec((1, d, tn), lambda l, j: (l, 0, j)),
                  pl.BlockSpec((1, 1, tn), lambda l, j: (l, 0, j))],
        out_specs=pl.BlockSpec((1, MOD_ROWS, tn), lambda l, j: (l, 0, j)),
        out_shape=jax.ShapeDtypeStruct((depth, MOD_ROWS, n), F32),
        compiler_params=_params("parallel", "parallel"),
        name="mod",
    )(cond, w_mod, b_mod.reshape(depth, 1, n))


C_AQ = 0
C_AK = C_AQ + NA_W
C_AV = C_AK + NA_W
C_SQ = C_AV + NA_W
C_SQR = C_SQ + SW_W
C_SK = C_SQR + SW_W
C_SKR = C_SK + SW_KV_W
C_SV = C_SKR + SW_KV_W
C_SU = C_SV + SW_KV_W
C_END = C_SU + S5_CH


def _rms(x, g):
    return x * lax.rsqrt(jnp.mean(x * x, axis=-1, keepdims=True) + RMS_EPS) * g


def _mod_segments(modb_ref, modc_ref, ctx_len):
    r = ctx_len % ROW_TILE
    out = []
    for a, b in ([(0, ROW_TILE)] if r == 0 else [(0, r), (r, ROW_TILE)]):
        is_ctx = pl.program_id(1) * ROW_TILE + a < ctx_len
        out.append((a, b, jnp.where(is_ctx, modc_ref[0], modb_ref[0])))
    return out


def _by_segment(segs, fn):
    return jnp.concatenate([fn(a, b, m) for a, b, m in segs], axis=0)


def _inproj_kernel(x_ref, modb_ref, modc_ref, g_ref, w_ref, cos_ref, sin_ref,
                   naq_ref, nak_ref, nav_ref, swq_ref, swk_ref, swv_ref, su_ref, *, ctx_len):
    segs = _mod_segments(modb_ref, modc_ref, ctx_len)
    y = _rms(x_ref[0], g_ref[...])
    h = _by_segment(segs, lambda a, b, m: y[a:b] * (1.0 + m[1:2]) + m[0:1]).astype(MXU_DTYPE)
    p = _dot(h, w_ref[...])
    cos = cos_ref[...]
    sin = sin_ref[...]
    cos3 = jnp.concatenate([cos] * (SW_W // LANES), axis=1)
    sin3 = jnp.concatenate([sin] * (SW_W // LANES), axis=1)
    qk_scale = HEAD_DIM ** -0.5
    naq_ref[0] = (p[:, C_AQ:C_AK] * qk_scale).astype(naq_ref.dtype)
    nak_ref[0] = p[:, C_AK:C_AV].astype(nak_ref.dtype)
    nav_ref[0] = p[:, C_AV:C_SQ].astype(nav_ref.dtype)
    swq_ref[0] = ((p[:, C_SQ:C_SQR] * cos3 + p[:, C_SQR:C_SK] * sin3) * qk_scale).astype(swq_ref.dtype)
    swk_ref[0] = (p[:, C_SK:C_SKR] * cos + p[:, C_SKR:C_SV] * sin).astype(swk_ref.dtype)
    swv_ref[0] = p[:, C_SV:C_SU].astype(swv_ref.dtype)
    su_ref[0] = p[:, C_SU:C_END]


def _inproj_call(xs, mod_l, g, w_cat, cos2, sin2, ctx_len):
    bsz, s, d = xs.shape
    tm = ROW_TILE
    row = lambda b, j: (b, j, 0)
    const = lambda b, j: (0, 0)
    widths = (NA_W, NA_W, NA_W, SW_W, SW_KV_W, SW_KV_W, S5_CH)
    dtypes = (MXU_DTYPE,) * 6 + (F32,)
    return pl.pallas_call(
        functools.partial(_inproj_kernel, ctx_len=ctx_len),
        grid=(bsz, s // tm),
        in_specs=[pl.BlockSpec((1, tm, d), row),
                  pl.BlockSpec((1, 6, d), lambda b, j: (b, 0, 0)),
                  pl.BlockSpec((1, 6, d), lambda b, j: (bsz, 0, 0)),
                  pl.BlockSpec((1, d), const),
                  pl.BlockSpec((d, C_END), const),
                  pl.BlockSpec((tm, LANES), lambda b, j: (j, 0)),
                  pl.BlockSpec((tm, LANES), lambda b, j: (j, 0))],
        out_specs=[pl.BlockSpec((1, tm, w), row) for w in widths],
        out_shape=[jax.ShapeDtypeStruct((bsz, s, w), t) for w, t in zip(widths, dtypes)],
        compiler_params=_params("parallel", "parallel"),
        name="inproj",
    )(xs, mod_l, mod_l, g, w_cat, cos2, sin2)


def _half_masks():
    lane = lax.broadcasted_iota(jnp.int32, (1, LANES), 1)
    return lane < HEAD_DIM, lane >= HEAD_DIM


def _na_kernel(q_ref, k_ref, v_ref, bias_ref, o_ref, *, ctx_len, rows):
    i = pl.program_id(1)
    tq = NA_QROWS * GRID_W
    n_ctx_q = ctx_len // tq
    masks = _half_masks()
    nk = NA_KROWS * GRID_W

    def run(local):
        q = q_ref[0]
        if local:
            r0 = (i - n_ctx_q) * NA_QROWS
            start0 = jnp.clip(r0 - NA_ROWS // 2, 0, rows - NA_KROWS)
            start = pl.multiple_of(ctx_len + start0 * GRID_W, GRID_W)
            tab_idx, row_mask = {}, {}
            for a in range(NA_QROWS):
                r = r0 + a
                s_r = jnp.clip(r - NA_ROWS // 2, 0, rows - NA_ROWS)
                for p in range(NA_KROWS // 2):
                    kr = start0 + 2 * p
                    ok_lo = (kr >= s_r) & (kr < s_r + NA_ROWS)
                    ok_hi = (kr + 1 >= s_r) & (kr + 1 < s_r + NA_ROWS)
                    tab_idx[a, p] = jnp.clip(kr - r + NA_ROWS, 0, 2 * NA_ROWS - 1)
                    row_mask[a, p] = jnp.where(masks[0], jnp.where(ok_lo, 0.0, NEG_INF),
                                               jnp.where(ok_hi, 0.0, NEG_INF))
        outs = []
        for t in range(NA_W // LANES):
            sl = slice(LANES * t, LANES * (t + 1))
            qt = q[:, sl]
            zero = jnp.zeros_like(qt)
            qm = jnp.concatenate([jnp.where(masks[0], qt, zero), jnp.where(masks[1], qt, zero)], axis=0)
            kc = k_ref[0, 0:ctx_len, sl]
            vc = v_ref[0, 0:ctx_len, sl]
            s_cx = _dot_nt(qm, kc)
            m = jnp.max(s_cx, axis=-1, keepdims=True)
            if local:
                kw = k_ref[0, pl.ds(start, nk), sl]
                vw = v_ref[0, pl.ds(start, nk), sl]
                bias = jnp.concatenate(
                    [jnp.concatenate([bias_ref[2 * t + hh, pl.ds(tab_idx[a, p], 1)][0] + row_mask[a, p]
                                      for p in range(NA_KROWS // 2)], axis=-1)
                     for hh in range(2) for a in range(NA_QROWS)], axis=0)
                s_nb = _dot_nt(qm, kw) + bias
                m = jnp.maximum(m, jnp.max(s_nb, axis=-1, keepdims=True))
                p_nb = jnp.exp(s_nb - m)
            p_cx = jnp.exp(s_cx - m)
            den = jnp.sum(p_cx, axis=-1, keepdims=True)
            o = _dot(p_cx.astype(MXU_DTYPE), vc)
            if local:
                den = den + jnp.sum(p_nb, axis=-1, keepdims=True)
                o = o + _dot(p_nb.astype(MXU_DTYPE), vw)
            o = o / den
            outs.append(jnp.where(masks[0], o[:tq], o[tq:]))
        o_ref[0] = jnp.concatenate(outs, axis=-1).astype(o_ref.dtype)

    @pl.when(i < n_ctx_q)
    def _():
        run(False)

    @pl.when(i >= n_ctx_q)
    def _():
        run(True)


def _na_call(q, k, v, bias_tab, ctx_len):
    bsz, s, w = q.shape
    rows = (s - ctx_len) // GRID_W
    tq = NA_QROWS * GRID_W
    assert rows >= NA_KROWS and rows % NA_QROWS == 0 and ctx_len % tq == 0
    whole = lambda b, i: (b, 0, 0)
    return pl.pallas_call(
        functools.partial(_na_kernel, ctx_len=ctx_len, rows=rows),
        grid=(bsz, s // tq),
        in_specs=[pl.BlockSpec((1, tq, w), lambda b, i: (b, i, 0)),
                  pl.BlockSpec((1, s, w), whole),
                  pl.BlockSpec((1, s, w), whole),
                  pl.BlockSpec(bias_tab.shape, lambda b, i: (0, 0, 0, 0))],
        out_specs=pl.BlockSpec((1, tq, w), lambda b, i: (b, i, 0)),
        out_shape=jax.ShapeDtypeStruct((bsz, s, w), q.dtype),
        compiler_params=_params("parallel", "arbitrary"),
        name="na_attn",
    )(q, k, v, bias_tab)


def _na_bias_table(rpb):
    qcol = np.arange(GRID_W)[:, None]
    kcol = np.arange(GRID_W)[None, :]
    ws = np.clip(qcol - NA_COLS // 2, 0, GRID_W - NA_COLS)
    valid = (kcol >= ws) & (kcol < ws + NA_COLS)
    dc = np.clip(kcol - qcol + NA_COLS - 1, 0, 2 * NA_COLS - 2)
    full = jnp.where(valid[None, None], rpb[:, :, dc].astype(F32), NEG_INF)
    edge = jnp.full_like(full[:, :1], NEG_INF)
    full = jnp.concatenate([edge, full, edge], axis=1)
    return jnp.concatenate([full[:, :-1], full[:, 1:]], axis=-1)


def _sw_kernel(sink_ref, q_ref, k_ref, v_ref, o_ref, *, ctx_len, seq):
    i = pl.program_id(1)
    tq = SW_QBLK
    n_ctx_q = ctx_len // tq
    masks = _half_masks()
    nk = SW_KBLK
    first_head = lax.broadcasted_iota(jnp.int32, (2 * tq, 1), 0) < tq

    def run(local):
        q = q_ref[0]
        kc = k_ref[0, 0:ctx_len, :]
        vc = v_ref[0, 0:ctx_len, :]
        if local:
            n = i - n_ctx_q
            start_lat = jnp.clip(n * tq - SW_WINDOW, 0, seq - nk)
            start = pl.multiple_of(ctx_len + start_lat, SW_BLK)
            kw = k_ref[0, pl.ds(start, nk), :]
            vw = v_ref[0, pl.ds(start, nk), :]
            row = lax.broadcasted_iota(jnp.int32, (2 * tq, 1), 0)
            qpos = n * tq + jnp.where(first_head, row, row - tq)
            kpos = start_lat + lax.broadcasted_iota(jnp.int32, (1, nk), 1)
            valid = jnp.abs(qpos - kpos) <= SW_WINDOW
        outs = []
        for t in range(SW_W // LANES):
            qt = q[:, LANES * t:LANES * (t + 1)]
            zero = jnp.zeros_like(qt)
            qm = jnp.concatenate([jnp.where(masks[0], qt, zero), jnp.where(masks[1], qt, zero)], axis=0)
            sink = jnp.where(first_head, sink_ref[SW_HEAD_ORDER[2 * t]], sink_ref[SW_HEAD_ORDER[2 * t + 1]])
            s_cx = _dot_nt(qm, kc)
            m = jnp.maximum(jnp.max(s_cx, axis=-1, keepdims=True), sink)
            if local:
                s_loc = jnp.where(valid, _dot_nt(qm, kw), NEG_INF)
                m = jnp.maximum(m, jnp.max(s_loc, axis=-1, keepdims=True))
                p_loc = jnp.exp(s_loc - m)
            p_cx = jnp.exp(s_cx - m)
            den = jnp.sum(p_cx, axis=-1, keepdims=True) + jnp.exp(sink - m)
            o = _dot(p_cx.astype(MXU_DTYPE), vc)
            if local:
                den = den + jnp.sum(p_loc, axis=-1, keepdims=True)
                o = o + _dot(p_loc.astype(MXU_DTYPE), vw)
            o = o / den
            outs.append(jnp.where(masks[0], o[:tq], o[tq:]))
        o_ref[0] = jnp.concatenate(outs, axis=-1).astype(o_ref.dtype)

    @pl.when(i < n_ctx_q)
    def _():
        run(False)

    @pl.when(i >= n_ctx_q)
    def _():
        run(True)


def _sw_call(sinks, q, k, v, ctx_len):
    bsz, s, w = q.shape
    seq = s - ctx_len
    assert seq >= SW_KBLK and seq % SW_QBLK == 0 and ctx_len % SW_QBLK == 0
    whole = lambda b, i: (b, 0, 0)
    return pl.pallas_call(
        functools.partial(_sw_kernel, ctx_len=ctx_len, seq=seq),
        grid=(bsz, s // SW_QBLK),
        in_specs=[pl.BlockSpec(memory_space=pltpu.SMEM),
                  pl.BlockSpec((1, SW_QBLK, w), lambda b, i: (b, i, 0)),
                  pl.BlockSpec((1, s, SW_KV_W), whole),
                  pl.BlockSpec((1, s, SW_KV_W), whole)],
        out_specs=pl.BlockSpec((1, SW_QBLK, w), lambda b, i: (b, i, 0)),
        out_shape=jax.ShapeDtypeStruct((bsz, s, w), q.dtype),
        compiler_params=_params("parallel", "arbitrary"),
        name="sw_attn",
    )(sinks.astype(F32), q, k, v)


def _s5_kernel(u_ref, m_ref, wsr_ref, wsi_ref, wor_ref, woi_ref, lr_ref, li_ref, o_ref,
               sre, sim, xre, xim, acc, *, n_ctx_chunks):
    d = pl.program_id(1)
    bsz, nc, _ = u_ref.shape
    for b in range(bsz):
        ub = u_ref[b]
        sre[b] = _dot(ub, wsr_ref[0, 0])
        sim[b] = _dot(ub, wsi_ref[0, 0])
    lr = lr_ref[0, 0]
    li = li_ref[0, 0]

    def step(c, carry):
        new = []
        for b in range(bsz):
            xr, xi = carry[2 * b], carry[2 * b + 1]
            xre[b, pl.ds(c, 1), :] = xr
            xim[b, pl.ds(c, 1), :] = xi
            sr = sre[b, pl.ds(c, 1), :]
            si = sim[b, pl.ds(c, 1), :]
            new.append(lr * xr - li * xi + sr)
            new.append(lr * xi + li * xr + si)
        return tuple(new)

    zero = tuple(jnp.zeros((1, LANES), F32) for _ in range(2 * bsz))

    @pl.when(d == 0)
    def _():
        lax.fori_loop(0, nc, step, zero, unroll=2)

    @pl.when(d == 1)
    def _():
        carry = lax.fori_loop(0, n_ctx_chunks, lambda k, cr: step(n_ctx_chunks - 1 - k, cr), zero, unroll=2)
        lax.fori_loop(0, nc - n_ctx_chunks, lambda k, cr: step(nc - 1 - k, cr), carry, unroll=2)

    half = u_ref.shape[2] // 2
    for b in range(bsz):
        ub = u_ref[b]
        y_intra = jnp.concatenate([_dot(ub[:, :half], m_ref[0, 0]), _dot(ub[:, half:], m_ref[0, 1])], axis=-1)
        y = (y_intra + _dot(xre[b].astype(MXU_DTYPE), wor_ref[0, 0])
             + _dot(xim[b].astype(MXU_DTYPE), woi_ref[0, 0]))

        @pl.when(d == 0)
        def _():
            acc[b] = y

        @pl.when(d == 1)
        def _():
            o_ref[b] = (acc[b] + y).astype(o_ref.dtype)


def _s5_call(u_t, wts, n_ctx_chunks):
    m, wsr, wsi, wor, woi, lr, li = wts
    bsz, nc, width = u_t.shape
    pw = 2 * S5_CHUNK * S5_GROUP_CH
    n_pairs = width // pw
    blk = lambda shp: pl.BlockSpec((1, 1) + shp, lambda j, d: (d, j, 0, 0))
    return pl.pallas_call(
        functools.partial(_s5_kernel, n_ctx_chunks=n_ctx_chunks),
        grid=(n_pairs, 2),
        in_specs=[pl.BlockSpec((bsz, nc, pw), lambda j, d: (0, 0, j)),
                  pl.BlockSpec((1, 2, pw // 2, pw // 2), lambda j, d: (d, j, 0, 0)),
                  blk((pw, LANES)), blk((pw, LANES)), blk((LANES, pw)), blk((LANES, pw)),
                  blk((1, LANES)), blk((1, LANES))],
        out_specs=pl.BlockSpec((bsz, nc, pw), lambda j, d: (0, 0, j)),
        out_shape=jax.ShapeDtypeStruct((bsz, nc, width), MXU_DTYPE),
        scratch_shapes=[pltpu.VMEM((bsz, nc, LANES), F32) for _ in range(4)] + [pltpu.VMEM((bsz, nc, pw), F32)],
        compiler_params=_params("parallel", "arbitrary"),
        name="s5_scan",
    )(u_t, m, wsr, wsi, wor, woi, lr, li)


def _s5_weights(a_re, a_im, log_step, b_re, b_im, c_re, c_im):
    lc, g, p, h = S5_CHUNK, S5_GROUPS, S5_STATE, S5_GROUP_CH
    lam = lax.complex(jnp.minimum(a_re.astype(F32), S5_EIG_MAX), a_im.astype(F32))
    step = jnp.exp(log_step.astype(F32))[..., None]
    lam_bar = jnp.exp(lam * step)
    b_bar = ((lam_bar - 1.0) / lam)[..., None] * lax.complex(b_re.astype(F32), b_im.astype(F32))
    cc = lax.complex(c_re.astype(F32), c_im.astype(F32))
    dd = jnp.arange(lc + 1, dtype=F32)
    pw = jnp.exp((lam * step)[..., None] * dd)
    kern = jnp.real(jnp.einsum('zgop,zgpd,zgpi->zgdoi', cc, pw[..., :lc], b_bar))
    jj = np.arange(lc)[:, None]
    ii = np.arange(lc)[None, :]
    mats, wst, wout = [], [], []
    for z in range(2):
        lag = (ii - jj) if z == 0 else (jj - ii)
        ok = lag >= 0
        kz = kern[z][:, np.where(ok, lag, 0)]
        kz = jnp.where(ok[None, :, :, None, None], kz, 0.0)
        mats.append(kz.transpose(0, 1, 4, 2, 3).reshape(g, lc * h, lc * h))
        d_state = (lc - 1 - np.arange(lc)) if z == 0 else np.arange(lc)
        ws = pw[z][:, :, d_state][..., None] * b_bar[z][:, :, None, :]
        wst.append(ws.transpose(0, 2, 3, 1).reshape(g, lc * h, p))
        d_out = (np.arange(lc) + 1) if z == 0 else (lc - np.arange(lc))
        wo = cc[z][:, :, :, None] * pw[z][:, None, :, :][..., d_out]
        wout.append(wo.transpose(0, 2, 3, 1).reshape(g, p, lc * h))
    mats = jnp.stack(mats)
    wst = jnp.stack(wst)
    wout = jnp.stack(wout)

    def pair_rows(w):
        w = w.reshape(2, g // 2, 2, lc * h, p)
        z0 = jnp.zeros_like(w[:, :, 0])
        top = jnp.concatenate([w[:, :, 0], z0], axis=-1)
        bot = jnp.concatenate([z0, w[:, :, 1]], axis=-1)
        return jnp.concatenate([top, bot], axis=-2)

    def pair_cols(w):
        w = w.reshape(2, g // 2, 2, p, lc * h)
        z0 = jnp.zeros_like(w[:, :, 0])
        top = jnp.concatenate([w[:, :, 0], z0], axis=-1)
        bot = jnp.concatenate([z0, w[:, :, 1]], axis=-1)
        return jnp.concatenate([top, bot], axis=-2)

    lam_c = pw[..., lc].reshape(2, g // 2, 1, 2 * p)
    cast = lambda w: w.astype(MXU_DTYPE)
    return (cast(mats), cast(pair_rows(jnp.real(wst))), cast(pair_rows(jnp.imag(wst))),
            cast(pair_cols(jnp.real(wout))), cast(pair_cols(-jnp.imag(wout))),
            jnp.real(lam_c), jnp.imag(lam_c))


def _gelu_tanh(x):
    cdf = 0.5 * (1.0 + jnp.tanh(math.sqrt(2.0 / math.pi) * (x + 0.044715 * (x * x * x))))
    return x * cdf


def _outproj_kernel(x_ref, ya_ref, ys_ref, su_ref, yc_ref, modb_ref, modc_ref, woa_ref, wob_ref, woc_ref,
                    wglu_ref, bglu_ref, dsk_ref, g_ref, wr_ref, br_ref,
                    xo_ref, h_hbm, topi_ref, topw_ref, rank_ref, cnt_ref, carry, hbuf, hsem, *, ctx_len):
    segs = _mod_segments(modb_ref, modc_ref, ctx_len)
    y = dsk_ref[...] * su_ref[0] + ys_ref[0].astype(F32)
    gl = _gelu_tanh(y)
    yb = gl * jax.nn.sigmoid(_dot(gl.astype(MXU_DTYPE), wglu_ref[...]) + bglu_ref[...])
    mix = (_dot(ya_ref[0], woa_ref[...]) + _dot(yb.astype(MXU_DTYPE), wob_ref[...])
           + _dot(yc_ref[0], woc_ref[...]))
    x_in = x_ref[0]
    x = _by_segment(segs, lambda a, b, m: x_in[a:b] + m[2:3] * mix[a:b])
    xo_ref[0] = x
    y = _rms(x, g_ref[...])
    h = _by_segment(segs, lambda a, b, m: y[a:b] * (1.0 + m[4:5]) + m[3:4])
    dl = h_hbm.shape[-1]
    tile = pl.program_id(1)
    step = pl.program_id(0) * pl.num_programs(1) + tile
    slot = lax.rem(step, 2)

    def h_wait(sl_):
        pltpu.make_async_copy(hbuf.at[sl_], hbuf.at[sl_], hsem.at[sl_]).wait()

    @pl.when(step >= 2)
    def _():
        h_wait(slot)

    hbuf[slot] = h
    for sl in range(SUBLANES):
        pltpu.make_async_copy(hbuf.at[slot, :, pl.ds(dl * sl, dl)],
                              h_hbm.at[pl.program_id(0), pl.ds(tile * ROW_TILE, ROW_TILE), sl, :],
                              hsem.at[slot]).start()

    @pl.when(step == pl.num_programs(0) * pl.num_programs(1) - 1)
    def _():
        h_wait(1 - slot)
        h_wait(slot)

    logits = _dot3(wr_ref[...], h, nt=True) + br_ref[...]
    n_e, tm = logits.shape
    e_iota = lax.broadcasted_iota(jnp.int32, (n_e, tm), 0)
    vals, idxs = [], []
    for _ in range(TOP_K):
        mx = jnp.max(logits, axis=0, keepdims=True)
        ix = jnp.min(jnp.where(logits == mx, e_iota, n_e), axis=0, keepdims=True)
        vals.append(mx)
        idxs.append(ix)
        logits = jnp.where(e_iota == ix, -jnp.inf, logits)
    ex = [jnp.exp(v - vals[0]) for v in vals]
    den = ex[0] + ex[1] + ex[2] + ex[3]
    topi_ref[0] = jnp.concatenate(idxs, axis=0)
    topw_ref[0] = jnp.concatenate([e / den for e in ex], axis=0)

    @pl.when((pl.program_id(0) == 0) & (pl.program_id(1) == 0))
    def _():
        carry[...] = jnp.zeros_like(carry)

    sel = [ix == e_iota for ix in idxs]
    onehot = jnp.where(sel[0] | sel[1] | sel[2] | sel[3], 1.0, 0.0)
    before = (lax.broadcasted_iota(jnp.int32, (tm, tm), 0) < lax.broadcasted_iota(jnp.int32, (tm, tm), 1))
    pfx = _dot(onehot.astype(MXU_DTYPE), jnp.where(before, 1.0, 0.0).astype(MXU_DTYPE)) + carry[:, 0:1]
    rank_ref[0] = jnp.concatenate(
        [jnp.sum(jnp.where(sel[k], pfx, 0.0), axis=0, keepdims=True) for k in range(TOP_K)], axis=0)
    carry[...] = carry[...] + jnp.sum(onehot, axis=1, keepdims=True)
    cnt_ref[...] = carry[...]


def _outproj_call(xs, ya, ys, su, yc, mod_l, wts, ctx_len):
    bsz, s, d = xs.shape
    tm = ROW_TILE
    row = lambda b, j: (b, j, 0)
    const = lambda b, j: (0, 0)
    full = lambda a: pl.BlockSpec(a.shape, const)
    return pl.pallas_call(
        functools.partial(_outproj_kernel, ctx_len=ctx_len),
        grid=(bsz, s // tm),
        in_specs=[pl.BlockSpec((1, tm, d), row),
                  pl.BlockSpec((1, tm, NA_W), row),
                  pl.BlockSpec((1, tm, S5_CH), row),
                  pl.BlockSpec((1, tm, S5_CH), row),
                  pl.BlockSpec((1, tm, SW_W), row),
                  pl.BlockSpec((1, 6, d), lambda b, j: (b, 0, 0)),
                  pl.BlockSpec((1, 6, d), lambda b, j: (bsz, 0, 0))] + [full(a) for a in wts],
        out_specs=[pl.BlockSpec((1, tm, d), row),
                   pl.BlockSpec(memory_space=pl.ANY),
                   pl.BlockSpec((1, TOP_K, tm), lambda b, j: (b, 0, j)),
                   pl.BlockSpec((1, TOP_K, tm), lambda b, j: (b, 0, j)),
                   pl.BlockSpec((1, TOP_K, tm), lambda b, j: (b, 0, j)),
                   pl.BlockSpec((N_EXPERTS, LANES), lambda b, j: (0, 0))],
        out_shape=[jax.ShapeDtypeStruct((bsz, s, d), F32),
                   jax.ShapeDtypeStruct((bsz, s, SUBLANES, d // SUBLANES), F32),
                   jax.ShapeDtypeStruct((bsz, TOP_K, s), jnp.int32),
                   jax.ShapeDtypeStruct((bsz, TOP_K, s), F32),
                   jax.ShapeDtypeStruct((bsz, TOP_K, s), F32),
                   jax.ShapeDtypeStruct((N_EXPERTS, LANES), F32)],
        scratch_shapes=[pltpu.VMEM((N_EXPERTS, LANES), F32), pltpu.VMEM((2, tm, d), F32),
                        pltpu.SemaphoreType.DMA((2,))],
        compiler_params=_params("arbitrary", "arbitrary"),
        name="outproj",
    )(xs, ya, ys, su, yc, mod_l, mod_l, *wts)


def _route_kernel(topi_ref, rank_ref, cnt_ref, dest_ref, blk_ref, ends_ref):
    idx = topi_ref[0]
    tm = idx.shape[1]
    e_iota = lax.broadcasted_iota(jnp.int32, (N_EXPERTS, tm), 0)
    counts = cnt_ref[:, 0:1]
    padded = jnp.ceil(counts * (1.0 / MOE_BLK)) * MOE_BLK
    r_i = lax.broadcasted_iota(jnp.int32, (N_EXPERTS, N_EXPERTS), 0)
    c_i = lax.broadcasted_iota(jnp.int32, (N_EXPERTS, N_EXPERTS), 1)
    padded_row = jnp.sum(jnp.where(r_i == c_i, padded, 0.0), axis=0, keepdims=True)
    pstart = jnp.sum(jnp.where(c_i < r_i, padded_row, 0.0), axis=1, keepdims=True)
    ends = pstart + padded
    rk = rank_ref[0]
    dest = [(jnp.sum(jnp.where(idx[k:k + 1] == e_iota, pstart, 0.0), axis=0, keepdims=True) + rk[k:k + 1]
             ).astype(jnp.int32) for k in range(TOP_K)]
    for u in range(tm // MOE_TILE):
        dest_ref[u] = jnp.concatenate([d[:, u * MOE_TILE:(u + 1) * MOE_TILE] for d in dest], axis=1)
    nb = blk_ref.shape[1]
    blk_start = (lax.broadcasted_iota(jnp.int32, (N_EXPERTS, nb), 1) * MOE_BLK).astype(F32)
    owner = jnp.sum(jnp.where(ends <= blk_start, 1.0, 0.0), axis=0, keepdims=True)
    blk_ref[...] = jnp.minimum(owner, N_EXPERTS - 1.0).astype(jnp.int32)
    ends_ref[...] = jnp.broadcast_to(ends, ends_ref.shape).astype(jnp.int32)


def _route_call(topi, rank, cnt, n_blocks):
    bsz, _, s = topi.shape
    tm = ROW_TILE
    sub = tm // MOE_TILE
    n_tiles = s // tm
    nb_pad = -(-n_blocks // LANES) * LANES
    tok = pl.BlockSpec((1, TOP_K, tm), lambda b, j: (b, 0, j))
    return pl.pallas_call(
        _route_kernel,
        grid=(bsz, n_tiles),
        in_specs=[tok, tok, pl.BlockSpec((N_EXPERTS, LANES), lambda b, j: (0, 0))],
        out_specs=[pl.BlockSpec((sub, 1, TOP_K * MOE_TILE), lambda b, j: (b * n_tiles + j, 0, 0)),
                   pl.BlockSpec((1, nb_pad), lambda b, j: (0, 0)),
                   pl.BlockSpec((N_EXPERTS, LANES), lambda b, j: (0, 0))],
        out_shape=[jax.ShapeDtypeStruct((bsz * n_tiles * sub, 1, TOP_K * MOE_TILE), jnp.int32),
                   jax.ShapeDtypeStruct((1, nb_pad), jnp.int32),
                   jax.ShapeDtypeStruct((N_EXPERTS, LANES), jnp.int32)],
        compiler_params=_params("arbitrary", "arbitrary"),
        name="route_dest",
    )(topi, rank, cnt)


def _dispatch_kernel(ends_ref, dest_ref, h_ref, xg_ref, zbuf, stage, sem, zsem, *, n_blocks):
    tm = h_ref.shape[1]

    @pl.when((pl.program_id(0) == 0) & (pl.program_id(1) == 0))
    def _():
        zbuf[...] = jnp.zeros_like(zbuf)

        def fill(row):
            return pltpu.make_async_copy(zbuf, xg_ref.at[pl.ds(pl.multiple_of(row, MOE_BLK), MOE_BLK)], zsem)

        def each(fn):
            for e in range(N_EXPERTS):
                begin = ends_ref[e - 1] if e else 0

                @pl.when(ends_ref[e] > begin)
                def _():
                    fn(fill(ends_ref[e] - MOE_BLK))

            def dead(i, c):
                fn(fill(i * MOE_BLK))
                return c

            lax.fori_loop(ends_ref[N_EXPERTS - 1] // MOE_BLK, n_blocks, dead, 0)

        each(lambda cp: cp.start())
        each(lambda cp: cp.wait())

    step = pl.program_id(0) * pl.num_programs(1) + pl.program_id(1)
    n_steps = pl.num_programs(0) * pl.num_programs(1)
    slot = lax.rem(step, 2)
    stage[slot] = h_ref[0]

    def body(t, c):
        for k in range(TOP_K):
            pltpu.make_async_copy(stage.at[slot, t], xg_ref.at[dest_ref[0, 0, k * tm + t]],
                                  sem.at[slot]).start(priority=k % 2)
        return c

    lax.fori_loop(0, tm, body, 0, unroll=8)

    def wait_tile(sl):
        pltpu.make_async_copy(xg_ref.at[pl.ds(0, TOP_K * tm)], xg_ref.at[pl.ds(0, TOP_K * tm)], sem.at[sl]).wait()

    @pl.when(step > 0)
    def _():
        wait_tile(1 - slot)

    @pl.when(step == n_steps - 1)
    def _():
        wait_tile(slot)


def _dispatch_call(ends, dest, h, n_blocks):
    bsz, s, _, dl = h.shape
    tm = MOE_TILE
    n_tiles = s // tm
    grid_spec = pltpu.PrefetchScalarGridSpec(
        num_scalar_prefetch=1,
        grid=(bsz, n_tiles),
        in_specs=[pl.BlockSpec((1, 1, TOP_K * tm), lambda b, j, en: (b * n_tiles + j, 0, 0), memory_space=pltpu.SMEM),
                  pl.BlockSpec((1, tm, SUBLANES, dl), lambda b, j, en: (b, j, 0, 0))],
        out_specs=pl.BlockSpec(memory_space=pl.ANY),
        scratch_shapes=[pltpu.VMEM((MOE_BLK, SUBLANES, dl), h.dtype), pltpu.VMEM((2, tm, SUBLANES, dl), h.dtype),
                        pltpu.SemaphoreType.DMA((2,)),
                        pltpu.SemaphoreType.DMA(())],
    )
    return pl.pallas_call(
        functools.partial(_dispatch_kernel, n_blocks=n_blocks),
        grid_spec=grid_spec,
        out_shape=jax.ShapeDtypeStruct((n_blocks * MOE_BLK, SUBLANES, dl), h.dtype),
        compiler_params=_params("arbitrary", "arbitrary"),
        name="moe_dispatch",
    )(ends, dest, h)


def _moe_kernel(blk_exp_ref, nact_ref, x_hbm, wgu_ref, bgu_ref, wd_ref, bd_ref, y_hbm, wgu_c, wd_c,
                xbuf, xsem, ybuf, ysem):
    i = pl.program_id(0)
    slot = lax.rem(i, 2)
    dl = x_hbm.shape[-1]

    def fetch(blk, sl_):
        for s in range(SUBLANES):
            pltpu.make_async_copy(x_hbm.at[pl.ds(blk * MOE_BLK, MOE_BLK), s, :],
                                  xbuf.at[sl_, :, pl.ds(dl * s, dl)], xsem.at[sl_]).start()

    @pl.when(i == 0)
    def _():
        fetch(0, 0)

    @pl.when(i + 1 < pl.num_programs(0))
    def _():
        fetch(i + 1, 1 - slot)

    def y_wait(sl_):
        pltpu.make_async_copy(ybuf.at[sl_], ybuf.at[sl_], ysem.at[sl_]).wait()

    @pl.when(i >= 2)
    def _():
        y_wait(slot)

    e = blk_exp_ref[i]
    prev = blk_exp_ref[jnp.maximum(i - 1, 0)]
    d, f2 = wgu_c.shape
    f = f2 // 2
    rows = 128

    @pl.when((i == 0) | (e != prev))
    def _():
        def cv(r, c):
            sl = pl.ds(pl.multiple_of(r * rows, rows), rows)
            wgu_c[sl, :] = wgu_ref[0, 0, sl, :].astype(wgu_c.dtype)
            return c
        lax.fori_loop(0, d // rows, cv, 0)

        def cv2(r, c):
            sl = pl.ds(pl.multiple_of(r * rows, rows), rows)
            wd_c[sl, :] = wd_ref[0, 0, sl, :].astype(wd_c.dtype)
            return c
        lax.fori_loop(0, f // rows, cv2, 0)

    pltpu.make_async_copy(xbuf.at[slot], xbuf.at[slot], xsem.at[slot]).wait()

    @pl.when(i < nact_ref[0])
    def _():
        gu = _dot(xbuf[slot].astype(MXU_DTYPE), wgu_c[...]) + bgu_ref[0, 0]
        gate = jnp.minimum(gu[:, :f], SWIGLU_LIMIT)
        up = jnp.clip(gu[:, f:], -SWIGLU_LIMIT, SWIGLU_LIMIT)
        act = gate * jax.nn.sigmoid(SWIGLU_ALPHA * gate) * (up + 1.0)
        ybuf[slot] = _dot(act.astype(MXU_DTYPE), wd_c[...]) + bd_ref[0, 0]

    @pl.when(i >= nact_ref[0])
    def _():
        ybuf[slot] = jnp.zeros(ybuf.shape[1:], ybuf.dtype)

    for s in range(SUBLANES):
        pltpu.make_async_copy(ybuf.at[slot, :, pl.ds(dl * s, dl)],
                              y_hbm.at[pl.ds(i * MOE_BLK, MOE_BLK), s, :], ysem.at[slot]).start()

    @pl.when(i == pl.num_programs(0) - 1)
    def _():
        y_wait(1 - slot)
        y_wait(slot)


def _moe_call(layer, blk_exp, n_active, xg, w_gate_up, b_gate_up, w_down, b_down):
    n_rows, _, dl = xg.shape
    depth, n_e, d, f2 = w_gate_up.shape
    f = f2 // 2
    n_blocks = n_rows // MOE_BLK
    assert n_blocks >= 2
    wsel = lambda i, be, na: (layer, be[i], 0, 0)
    grid_spec = pltpu.PrefetchScalarGridSpec(
        num_scalar_prefetch=2,
        grid=(n_blocks,),
        in_specs=[pl.BlockSpec(memory_space=pl.ANY),
                  pl.BlockSpec((1, 1, d, f2), wsel),
                  pl.BlockSpec((1, 1, 1, f2), wsel),
                  pl.BlockSpec((1, 1, f, d), wsel),
                  pl.BlockSpec((1, 1, 1, d), wsel)],
        out_specs=pl.BlockSpec(memory_space=pl.ANY),
        scratch_shapes=[pltpu.VMEM((d, f2), MXU_DTYPE), pltpu.VMEM((f, d), MXU_DTYPE),
                        pltpu.VMEM((2, MOE_BLK, d), F32), pltpu.SemaphoreType.DMA((2,)),
                        pltpu.VMEM((2, MOE_BLK, d), F32), pltpu.SemaphoreType.DMA((2,))],
    )
    return pl.pallas_call(
        _moe_kernel,
        grid_spec=grid_spec,
        out_shape=jax.ShapeDtypeStruct((n_rows, SUBLANES, dl), F32),
        compiler_params=_params("arbitrary"),
        name="moe_experts",
    )(blk_exp, n_active, xg, w_gate_up, b_gate_up.reshape(depth, n_e, 1, f2),
      w_down, b_down.reshape(depth, n_e, 1, d))


def _combine_kernel(dest_ref, dnext_ref, w_ref, gate_ref, x_hbm, yg_hbm, xo_hbm,
                    gbuf, xbuf, obuf, gsem, xsem, osem, *, n_tiles):
    tm = MOE_TILE
    dl = gbuf.shape[-1]
    step = pl.program_id(0) * n_tiles + pl.program_id(1)
    n_steps = pl.num_programs(0) * n_tiles
    slot = lax.rem(step, 2)

    def stream_copies(st, sl_, buf, hbm, sem, to_hbm):
        b = lax.div(st, n_tiles)
        r0 = lax.rem(st, n_tiles) * tm
        out = []
        for s in range(SUBLANES):
            rows = hbm.at[b, pl.ds(r0, tm), pl.ds(dl * s, dl)]
            tiles = buf.at[sl_, :, s, :]
            out.append(pltpu.make_async_copy(tiles, rows, sem.at[sl_]) if to_hbm
                       else pltpu.make_async_copy(rows, tiles, sem.at[sl_]))
        return out

    def fetch(dref, st, sl_):
        for cp in stream_copies(st, sl_, xbuf, x_hbm, xsem, False):
            cp.start()

        def body(t, c):
            for k in range(TOP_K):
                pltpu.make_async_copy(yg_hbm.at[dref[0, 0, k * tm + t]], gbuf.at[sl_, k, t],
                                      gsem.at[sl_]).start(priority=k % 2)
            return c

        lax.fori_loop(0, tm, body, 0, unroll=8)

    @pl.when(step == 0)
    def _():
        fetch(dest_ref, 0, 0)

    @pl.when(step + 1 < n_steps)
    def _():
        fetch(dnext_ref, step + 1, 1 - slot)

    def wait_all(buf, sem, sl_):
        pltpu.make_async_copy(buf.at[sl_], buf.at[sl_], sem.at[sl_]).wait()

    wait_all(gbuf, gsem, slot)
    wait_all(xbuf, xsem, slot)

    @pl.when(step >= 2)
    def _():
        wait_all(obuf, osem, slot)

    gate = gate_ref[0]

    def row(t, c):
        acc = gbuf[slot, 0, t] * w_ref[0, 0, t]
        for k in range(1, TOP_K):
            acc = acc + gbuf[slot, k, t] * w_ref[0, 0, k * tm + t]
        obuf[slot, t] = xbuf[slot, t] + gate * acc
        return c

    lax.fori_loop(0, tm, row, 0, unroll=8)
    for cp in stream_copies(step, slot, obuf, xo_hbm, osem, True):
        cp.start()

    @pl.when(step == n_steps - 1)
    def _():
        wait_all(obuf, osem, 1 - slot)
        wait_all(obuf, osem, slot)


def _combine_call(dest, topw, xs, mod_l, yg, ctx_len):
    bsz, s, d = xs.shape
    tm = MOE_TILE
    n_tiles = s // tm
    last = bsz * n_tiles - 1
    assert last >= 1
    dl = d // SUBLANES
    w_flat = topw.reshape(bsz, TOP_K, n_tiles, tm).transpose(0, 2, 1, 3).reshape(bsz * n_tiles, 1, TOP_K * tm)
    gate = mod_l[:, 5].reshape(MOD_ROWS, SUBLANES, dl)
    idx_spec = lambda ahead: pl.BlockSpec(
        (1, 1, TOP_K * tm), lambda b, j: (jnp.minimum(b * n_tiles + j + ahead, last), 0, 0), memory_space=pltpu.SMEM)
    tile_buf = lambda lead: pltpu.VMEM(lead + (tm, SUBLANES, dl), F32)
    return pl.pallas_call(
        functools.partial(_combine_kernel, n_tiles=n_tiles),
        grid=(bsz, n_tiles),
        in_specs=[idx_spec(0), idx_spec(1), idx_spec(0),
                  pl.BlockSpec((1, SUBLANES, dl), lambda b, j: (jnp.where(j < ctx_len // tm, bsz, b), 0, 0)),
                  pl.BlockSpec(memory_space=pl.ANY),
                  pl.BlockSpec(memory_space=pl.ANY)],
        out_specs=pl.BlockSpec(memory_space=pl.ANY),
        out_shape=jax.ShapeDtypeStruct((bsz, s, d), F32),
        scratch_shapes=[tile_buf((2, TOP_K)), tile_buf((2,)), tile_buf((2,)),
                        pltpu.SemaphoreType.DMA((2,)), pltpu.SemaphoreType.DMA((2,)), pltpu.SemaphoreType.DMA((2,))],
        compiler_params=_params("arbitrary", "arbitrary"),
        name="moe_combine",
    )(dest, dest, w_flat, gate, xs, yg)


def _final_kernel(x_ref, g_ref, o_ref):
    x = x_ref[0]
    o_ref[0] = x * lax.rsqrt(jnp.mean(x * x, axis=-1, keepdims=True) + RMS_EPS) * g_ref[...]


def _final_call(xs, g, ctx_len):
    bsz, s, d = xs.shape
    tm = MOE_TILE
    off = ctx_len // tm
    return pl.pallas_call(
        _final_kernel,
        grid=(bsz, (s - ctx_len) // tm),
        in_specs=[pl.BlockSpec((1, tm, d), lambda b, j: (b, j + off, 0)),
                  pl.BlockSpec((1, d), lambda b, j: (0, 0))],
        out_specs=pl.BlockSpec((1, tm, d), lambda b, j: (b, j, 0)),
        out_shape=jax.ShapeDtypeStruct((bsz, s - ctx_len, d), F32),
        compiler_params=_params("parallel", "parallel"),
        name="final_norm",
    )(xs, g)


def _rope_tables(seq, ctx_len):
    t = jnp.arange(seq)
    row = (t // GRID_W).astype(F32)
    col = (t % GRID_W).astype(F32)
    nf = HEAD_DIM // 4
    inv = ROPE_BASE ** (-jnp.arange(nf, dtype=F32) / nf)
    ar = row[:, None] * inv
    ac = col[:, None] * inv
    ang = jnp.concatenate([ar, ar, ac, ac], axis=-1)
    cos = jnp.concatenate([jnp.ones((ctx_len, HEAD_DIM), F32), jnp.cos(ang)], axis=0)
    sin = jnp.concatenate([jnp.zeros((ctx_len, HEAD_DIM), F32), jnp.sin(ang)], axis=0)
    reps = LANES // HEAD_DIM
    return jnp.tile(cos, (1, reps)), jnp.tile(sin, (1, reps))


def _rot_cols(w):
    q = HEAD_DIM // 4
    j = np.arange(HEAD_DIM)
    first = (j % (2 * q)) < q
    src = np.where(first, j + q, j - q)
    sign = np.where(first, -1.0, 1.0).astype(np.float32)
    n_heads = w.shape[1] // HEAD_DIM
    src_all = (np.arange(n_heads)[:, None] * HEAD_DIM + src[None, :]).reshape(-1)
    return w[:, src_all] * jnp.asarray(np.tile(sign, n_heads))


def _head_perm_cols(order):
    return (np.asarray(order)[:, None] * HEAD_DIM + np.arange(HEAD_DIM)[None, :]).reshape(-1)


def _inproj_weight(w_in_l):
    aq, ak, av, su, sq, sk, sv = jnp.split(
        w_in_l, np.cumsum([NA_W, NA_W, NA_W, S5_CH, SW_W, SW_KV_W])[:6].tolist(), axis=1)
    sq = sq[:, _head_perm_cols(SW_HEAD_ORDER)]
    return jnp.concatenate([aq, ak, av, sq, _rot_cols(sq), sk, _rot_cols(sk), sv, su], axis=1).astype(MXU_DTYPE)


def _chunk_major(su):
    bsz, s, _ = su.shape
    u = su.reshape(bsz, s // S5_CHUNK, S5_CHUNK, S5_GROUPS, S5_GROUP_CH).transpose(0, 1, 3, 2, 4)
    return u.reshape(bsz, s // S5_CHUNK, S5_GROUPS * S5_CHUNK * S5_GROUP_CH)


def _token_major(y_t):
    bsz, nc, _ = y_t.shape
    y = y_t.reshape(bsz, nc, S5_GROUPS, S5_CHUNK, S5_GROUP_CH).transpose(0, 1, 3, 2, 4)
    return y.reshape(bsz, nc * S5_CHUNK, S5_CH)


def kernel(x, c, ctx, c_ctx, w_mod, b_mod, g_mix, w_in, w_out, na_rpb, s5_a_re, s5_a_im, s5_log_step,
           s5_b_re, s5_b_im, s5_c_re, s5_c_im, s5_d, s5_w_glu, s5_b_glu, sw_sinks, g_ffn, w_router, b_router,
           w_gate_up, b_gate_up, w_down, b_down, g_final):
    bsz, seq, d = x.shape
    ctx_len = ctx.shape[1]
    depth = w_mod.shape[0]
    s = ctx_len + seq
    assert bsz + 1 <= MOD_ROWS and s % ROW_TILE == 0 and ctx_len % MOE_TILE == 0 and seq % MOE_TILE == 0
    assert ROW_TILE % MOE_TILE == 0
    assert seq % GRID_W == 0 and ctx_len % S5_CHUNK == 0

    xs = jnp.concatenate([ctx, x], axis=1)
    cond = jnp.zeros((MOD_ROWS, d), F32).at[:bsz].set(c).at[bsz].set(c_ctx)
    mod = _mod_call(cond, w_mod, b_mod).reshape(depth, MOD_ROWS, 6, d)
    cos2, sin2 = _rope_tables(seq, ctx_len)

    n_assign = bsz * s * TOP_K
    n_blocks = -(-(n_assign + N_EXPERTS * (MOE_BLK - 1)) // MOE_BLK)
    sw_rows = _head_perm_cols(SW_HEAD_ORDER)

    w_cat = jax.vmap(_inproj_weight)(w_in)
    bias_tab = jax.vmap(_na_bias_table)(na_rpb)
    s5w = jax.vmap(_s5_weights)(s5_a_re, s5_a_im, s5_log_step, s5_b_re, s5_b_im, s5_c_re, s5_c_im)
    wo_a = w_out[:, :NA_W].astype(MXU_DTYPE)
    wo_b = w_out[:, NA_W:NA_W + S5_CH].astype(MXU_DTYPE)
    wo_c = w_out[:, NA_W + S5_CH:][:, sw_rows].astype(MXU_DTYPE)
    w_glu = s5_w_glu.astype(MXU_DTYPE)
    w_rt = jnp.swapaxes(w_router, 1, 2).astype(F32)

    for l in range(depth):
        mod_l = mod[l]
        naq, nak, nav, swq, swk, swv, su = _inproj_call(
            xs, mod_l, g_mix[l].reshape(1, d), w_cat[l], cos2, sin2, ctx_len)
        ya = _na_call(naq, nak, nav, bias_tab[l], ctx_len)
        yc = _sw_call(sw_sinks[l], swq, swk, swv, ctx_len)
        ys = _token_major(_s5_call(_chunk_major(su).astype(MXU_DTYPE), tuple(w[l] for w in s5w),
                                   ctx_len // S5_CHUNK))

        out_wts = (wo_a[l], wo_b[l], wo_c[l], w_glu[l], s5_b_glu[l].reshape(1, S5_CH).astype(F32),
                   s5_d[l].reshape(1, S5_CH).astype(F32), g_ffn[l].reshape(1, d),
                   w_rt[l], b_router[l].reshape(N_EXPERTS, 1).astype(F32))
        xs, h, topi, topw, rank, cnt = _outproj_call(xs, ya, ys, su, yc, mod_l, out_wts, ctx_len)

        dest, blk, ends = _route_call(topi, rank, cnt, n_blocks)
        ends = ends[:, 0]
        xg = _dispatch_call(ends, dest, h, n_blocks)
        yg = _moe_call(l, blk[0, :n_blocks], ends[-1:] // MOE_BLK, xg, w_gate_up, b_gate_up, w_down, b_down)
        xs = _combine_call(dest, topw, xs, mod_l, yg, ctx_len)

    return _final_call(xs, g_final.reshape(1, d), ctx_len)
```

```python
import functools
import math

import numpy as np
import jax
import jax.numpy as jnp
from jax import lax
from jax.experimental import pallas as pl
from jax.experimental.pallas import tpu as pltpu

F32 = jnp.float32
MXU_DTYPE = jnp.bfloat16

GRID_W = 64
HEAD_DIM = 64
NA_HEADS = 6
NA_W = NA_HEADS * HEAD_DIM
NA_ROWS = 8
NA_COLS = 16
S5_GROUP_CH = 16
S5_CH = 256
S5_GROUPS = S5_CH // S5_GROUP_CH
S5_STATE = 64
S5_EIG_MAX = -1e-4
SW_HEADS = 6
SW_KV_HEADS = 2
SW_GRP = SW_HEADS // SW_KV_HEADS
SW_W = SW_HEADS * HEAD_DIM
SW_KV_W = SW_KV_HEADS * HEAD_DIM
SW_WINDOW = 128
SW_BLK = 128
ROPE_BASE = 10000.0
N_EXPERTS = 32
TOP_K = 4
MOE_BLK = 512
SWIGLU_LIMIT = 7.0
SWIGLU_ALPHA = 1.702
RMS_EPS = 1e-6
NEG_INF = -1e30

LANES = 128
ROW_TILE = 768
MOE_TILE = 256
S5_CHUNK = 16
NA_QROWS = 4
NA_KROWS = NA_QROWS + NA_ROWS
SW_QBLK = 2 * SW_BLK
SW_KBLK = SW_QBLK + 2 * SW_WINDOW
MOD_ROWS = 8
VMEM_LIMIT = 56 << 20
SUBLANES = 8

SW_HEAD_ORDER = tuple(g * SW_GRP + t for t in range(SW_GRP) for g in range(SW_KV_HEADS))


def _params(*sem):
    return pltpu.CompilerParams(dimension_semantics=sem, vmem_limit_bytes=VMEM_LIMIT)


def _dot(a, b):
    return jnp.dot(a, b, preferred_element_type=F32)


def _dot_nt(a, b):
    return lax.dot_general(a, b, (((1,), (1,)), ((), ())), preferred_element_type=F32)


def _split(a):
    hi = a.astype(MXU_DTYPE)
    lo = (a - hi.astype(F32)).astype(MXU_DTYPE)
    return hi, lo


def _dot3(a, b, nt=False):
    f = _dot_nt if nt else _dot
    ah, al = _split(a)
    bh, bl = _split(b)
    return f(ah, bh) + (f(ah, bl) + f(al, bh))


def _mod_kernel(cond_ref, w_ref, b_ref, o_ref):
    c = cond_ref[...]
    a = c * jax.nn.sigmoid(c)
    o_ref[0] = _dot3(a, w_ref[0]) + b_ref[0]


def _mod_call(cond, w_mod, b_mod):
    depth, d, n = w_mod.shape
    tn = n // 6
    return pl.pallas_call(
        _mod_kernel,
        grid=(depth, n // tn),
        in_specs=[pl.BlockSpec((MOD_ROWS, d), lambda l, j: (0, 0)),
                  pl.BlockSpec((1, d, tn), lambda l, j: (l, 0, j)),
                  pl.BlockSpec((1, 1, tn), lambda l, j: (l, 0, j))],
        out_specs=pl.BlockSpec((1, MOD_ROWS, tn), lambda l, j: (l, 0, j)),
        out_shape=jax.ShapeDtypeStruct((depth, MOD_ROWS, n), F32),
        compiler_params=_params("parallel", "parallel"),
        name="mod",
    )(cond, w_mod, b_mod.reshape(depth, 1, n))


C_AQ = 0
C_AK = C_AQ + NA_W
C_AV = C_AK + NA_W
C_SQ = C_AV + NA_W
C_SQR = C_SQ + SW_W
C_SK = C_SQR + SW_W
C_SKR = C_SK + SW_KV_W
C_SV = C_SKR + SW_KV_W
C_SU = C_SV + SW_KV_W
C_END = C_SU + S5_CH


def _rms(x, g):
    return x * lax.rsqrt(jnp.mean(x * x, axis=-1, keepdims=True) + RMS_EPS) * g


def _mod_segments(modb_ref, modc_ref, ctx_len):
    r = ctx_len % ROW_TILE
    out = []
    for a, b in ([(0, ROW_TILE)] if r == 0 else [(0, r), (r, ROW_TILE)]):
        is_ctx = pl.program_id(1) * ROW_TILE + a < ctx_len
        out.append((a, b, jnp.where(is_ctx, modc_ref[0], modb_ref[0])))
    return out


def _by_segment(segs, fn):
    return jnp.concatenate([fn(a, b, m) for a, b, m in segs], axis=0)


def _inproj_kernel(x_ref, modb_ref, modc_ref, g_ref, w_ref, cos_ref, sin_ref,
                   naq_ref, nak_ref, nav_ref, swq_ref, swk_ref, swv_ref, su_ref, *, ctx_len):
    segs = _mod_segments(modb_ref, modc_ref, ctx_len)
    y = _rms(x_ref[0], g_ref[...])
    h = _by_segment(segs, lambda a, b, m: y[a:b] * (1.0 + m[1:2]) + m[0:1]).astype(MXU_DTYPE)
    p = _dot(h, w_ref[...])
    cos = cos_ref[...]
    sin = sin_ref[...]
    cos3 = jnp.concatenate([cos] * (SW_W // LANES), axis=1)
    sin3 = jnp.concatenate([sin] * (SW_W // LANES), axis=1)
    qk_scale = HEAD_DIM ** -0.5
    naq_ref[0] = (p[:, C_AQ:C_AK] * qk_scale).astype(naq_ref.dtype)
    nak_ref[0] = p[:, C_AK:C_AV].astype(nak_ref.dtype)
    nav_ref[0] = p[:, C_AV:C_SQ].astype(nav_ref.dtype)
    swq_ref[0] = ((p[:, C_SQ:C_SQR] * cos3 + p[:, C_SQR:C_SK] * sin3) * qk_scale).astype(swq_ref.dtype)
    swk_ref[0] = (p[:, C_SK:C_SKR] * cos + p[:, C_SKR:C_SV] * sin).astype(swk_ref.dtype)
    swv_ref[0] = p[:, C_SV:C_SU].astype(swv_ref.dtype)
    su_ref[0] = p[:, C_SU:C_END]


def _inproj_call(xs, mod_l, g, w_cat, cos2, sin2, ctx_len):
    bsz, s, d = xs.shape
    tm = ROW_TILE
    row = lambda b, j: (b, j, 0)
    const = lambda b, j: (0, 0)
    widths = (NA_W, NA_W, NA_W, SW_W, SW_KV_W, SW_KV_W, S5_CH)
    dtypes = (MXU_DTYPE,) * 6 + (F32,)
    return pl.pallas_call(
        functools.partial(_inproj_kernel, ctx_len=ctx_len),
        grid=(bsz, s // tm),
        in_specs=[pl.BlockSpec((1, tm, d), row),
                  pl.BlockSpec((1, 6, d), lambda b, j: (b, 0, 0)),
                  pl.BlockSpec((1, 6, d), lambda b, j: (bsz, 0, 0)),
                  pl.BlockSpec((1, d), const),
                  pl.BlockSpec((d, C_END), const),
                  pl.BlockSpec((tm, LANES), lambda b, j: (j, 0)),
                  pl.BlockSpec((tm, LANES), lambda b, j: (j, 0))],
        out_specs=[pl.BlockSpec((1, tm, w), row) for w in widths],
        out_shape=[jax.ShapeDtypeStruct((bsz, s, w), t) for w, t in zip(widths, dtypes)],
        compiler_params=_params("parallel", "parallel"),
        name="inproj",
    )(xs, mod_l, mod_l, g, w_cat, cos2, sin2)


def _half_masks():
    lane = lax.broadcasted_iota(jnp.int32, (1, LANES), 1)
    return lane < HEAD_DIM, lane >= HEAD_DIM


def _na_kernel(q_ref, k_ref, v_ref, bias_ref, o_ref, *, ctx_len, rows):
    i = pl.program_id(1)
    tq = NA_QROWS * GRID_W
    n_ctx_q = ctx_len // tq
    masks = _half_masks()
    nk = NA_KROWS * GRID_W

    def run(local):
        q = q_ref[0]
        if local:
            r0 = (i - n_ctx_q) * NA_QROWS
            start0 = jnp.clip(r0 - NA_ROWS // 2, 0, rows - NA_KROWS)
            start = pl.multiple_of(ctx_len + start0 * GRID_W, GRID_W)
            tab_idx, row_mask = {}, {}
            for a in range(NA_QROWS):
                r = r0 + a
                s_r = jnp.clip(r - NA_ROWS // 2, 0, rows - NA_ROWS)
                for p in range(NA_KROWS // 2):
                    kr = start0 + 2 * p
                    ok_lo = (kr >= s_r) & (kr < s_r + NA_ROWS)
                    ok_hi = (kr + 1 >= s_r) & (kr + 1 < s_r + NA_ROWS)
                    tab_idx[a, p] = jnp.clip(kr - r + NA_ROWS, 0, 2 * NA_ROWS - 1)
                    row_mask[a, p] = jnp.where(masks[0], jnp.where(ok_lo, 0.0, NEG_INF),
                                               jnp.where(ok_hi, 0.0, NEG_INF))
        outs = []
        for t in range(NA_W // LANES):
            sl = slice(LANES * t, LANES * (t + 1))
            qt = q[:, sl]
            zero = jnp.zeros_like(qt)
            qm = jnp.concatenate([jnp.where(masks[0], qt, zero), jnp.where(masks[1], qt, zero)], axis=0)
            kc = k_ref[0, 0:ctx_len, sl]
            vc = v_ref[0, 0:ctx_len, sl]
            s_cx = _dot_nt(qm, kc)
            m = jnp.max(s_cx, axis=-1, keepdims=True)
            if local:
                kw = k_ref[0, pl.ds(start, nk), sl]
                vw = v_ref[0, pl.ds(start, nk), sl]
                bias = jnp.concatenate(
                    [jnp.concatenate([bias_ref[2 * t + hh, pl.ds(tab_idx[a, p], 1)][0] + row_mask[a, p]
                                      for p in range(NA_KROWS // 2)], axis=-1)
                     for hh in range(2) for a in range(NA_QROWS)], axis=0)
                s_nb = _dot_nt(qm, kw) + bias
                m = jnp.maximum(m, jnp.max(s_nb, axis=-1, keepdims=True))
                p_nb = jnp.exp(s_nb - m)
            p_cx = jnp.exp(s_cx - m)
            den = jnp.sum(p_cx, axis=-1, keepdims=True)
            o = _dot(p_cx.astype(MXU_DTYPE), vc)
            if local:
                den = den + jnp.sum(p_nb, axis=-1, keepdims=True)
                o = o + _dot(p_nb.astype(MXU_DTYPE), vw)
            o = o / den
            outs.append(jnp.where(masks[0], o[:tq], o[tq:]))
        o_ref[0] = jnp.concatenate(outs, axis=-1).astype(o_ref.dtype)

    @pl.when(i < n_ctx_q)
    def _():
        run(False)

    @pl.when(i >= n_ctx_q)
    def _():
        run(True)


def _na_call(q, k, v, bias_tab, ctx_len):
    bsz, s, w = q.shape
    rows = (s - ctx_len) // GRID_W
    tq = NA_QROWS * GRID_W
    assert rows >= NA_KROWS and rows % NA_QROWS == 0 and ctx_len % tq == 0
    whole = lambda b, i: (b, 0, 0)
    return pl.pallas_call(
        functools.partial(_na_kernel, ctx_len=ctx_len, rows=rows),
        grid=(bsz, s // tq),
        in_specs=[pl.BlockSpec((1, tq, w), lambda b, i: (b, i, 0)),
                  pl.BlockSpec((1, s, w), whole),
                  pl.BlockSpec((1, s, w), whole),
                  pl.BlockSpec(bias_tab.shape, lambda b, i: (0, 0, 0, 0))],
        out_specs=pl.BlockSpec((1, tq, w), lambda b, i: (b, i, 0)),
        out_shape=jax.ShapeDtypeStruct((bsz, s, w), q.dtype),
        compiler_params=_params("parallel", "arbitrary"),
        name="na_attn",
    )(q, k, v, bias_tab)


def _na_bias_table(rpb):
    qcol = np.arange(GRID_W)[:, None]
    kcol = np.arange(GRID_W)[None, :]
    ws = np.clip(qcol - NA_COLS // 2, 0, GRID_W - NA_COLS)
    valid = (kcol >= ws) & (kcol < ws + NA_COLS)
    dc = np.clip(kcol - qcol + NA_COLS - 1, 0, 2 * NA_COLS - 2)
    full = jnp.where(valid[None, None], rpb[:, :, dc].astype(F32), NEG_INF)
    edge = jnp.full_like(full[:, :1], NEG_INF)
    full = jnp.concatenate([edge, full, edge], axis=1)
    return jnp.concatenate([full[:, :-1], full[:, 1:]], axis=-1)


def _sw_kernel(sink_ref, q_ref, k_ref, v_ref, o_ref, *, ctx_len, seq):
    i = pl.program_id(1)
    tq = SW_QBLK
    n_ctx_q = ctx_len // tq
    masks = _half_masks()
    nk = SW_KBLK
    first_head = lax.broadcasted_iota(jnp.int32, (2 * tq, 1), 0) < tq

    def run(local):
        q = q_ref[0]
        kc = k_ref[0, 0:ctx_len, :]
        vc = v_ref[0, 0:ctx_len, :]
        if local:
            n = i - n_ctx_q
            start_lat = jnp.clip(n * tq - SW_WINDOW, 0, seq - nk)
            start = pl.multiple_of(ctx_len + start_lat, SW_BLK)
            kw = k_ref[0, pl.ds(start, nk), :]
            vw = v_ref[0, pl.ds(start, nk), :]
            row = lax.broadcasted_iota(jnp.int32, (2 * tq, 1), 0)
            qpos = n * tq + jnp.where(first_head, row, row - tq)
            kpos = start_lat + lax.broadcasted_iota(jnp.int32, (1, nk), 1)
            valid = jnp.abs(qpos - kpos) <= SW_WINDOW
        outs = []
        for t in range(SW_W // LANES):
            qt = q[:, LANES * t:LANES * (t + 1)]
            zero = jnp.zeros_like(qt)
            qm = jnp.concatenate([jnp.where(masks[0], qt, zero), jnp.where(masks[1], qt, zero)], axis=0)
            sink = jnp.where(first_head, sink_ref[SW_HEAD_ORDER[2 * t]], sink_ref[SW_HEAD_ORDER[2 * t + 1]])
            s_cx = _dot_nt(qm, kc)
            m = jnp.maximum(jnp.max(s_cx, axis=-1, keepdims=True), sink)
            if local:
                s_loc = jnp.where(valid, _dot_nt(qm, kw), NEG_INF)
                m = jnp.maximum(m, jnp.max(s_loc, axis=-1, keepdims=True))
                p_loc = jnp.exp(s_loc - m)
            p_cx = jnp.exp(s_cx - m)
            den = jnp.sum(p_cx, axis=-1, keepdims=True) + jnp.exp(sink - m)
            o = _dot(p_cx.astype(MXU_DTYPE), vc)
            if local:
                den = den + jnp.sum(p_loc, axis=-1, keepdims=True)
                o = o + _dot(p_loc.astype(MXU_DTYPE), vw)
            o = o / den
            outs.append(jnp.where(masks[0], o[:tq], o[tq:]))
        o_ref[0] = jnp.concatenate(outs, axis=-1).astype(o_ref.dtype)

    @pl.when(i < n_ctx_q)
    def _():
        run(False)

    @pl.when(i >= n_ctx_q)
    def _():
        run(True)


def _sw_call(sinks, q, k, v, ctx_len):
    bsz, s, w = q.shape
    seq = s - ctx_len
    assert seq >= SW_KBLK and seq % SW_QBLK == 0 and ctx_len % SW_QBLK == 0
    whole = lambda b, i: (b, 0, 0)
    return pl.pallas_call(
        functools.partial(_sw_kernel, ctx_len=ctx_len, seq=seq),
        grid=(bsz, s // SW_QBLK),
        in_specs=[pl.BlockSpec(memory_space=pltpu.SMEM),
                  pl.BlockSpec((1, SW_QBLK, w), lambda b, i: (b, i, 0)),
                  pl.BlockSpec((1, s, SW_KV_W), whole),
                  pl.BlockSpec((1, s, SW_KV_W), whole)],
        out_specs=pl.BlockSpec((1, SW_QBLK, w), lambda b, i: (b, i, 0)),
        out_shape=jax.ShapeDtypeStruct((bsz, s, w), q.dtype),
        compiler_params=_params("parallel", "arbitrary"),
        name="sw_attn",
    )(sinks.astype(F32), q, k, v)


def _s5_kernel(u_ref, m_ref, wsr_ref, wsi_ref, wor_ref, woi_ref, lr_ref, li_ref, o_ref,
               sre, sim, xre, xim, acc, *, n_ctx_chunks):
    d = pl.program_id(1)
    bsz, nc, _ = u_ref.shape
    for b in range(bsz):
        ub = u_ref[b]
        sre[b] = _dot(ub, wsr_ref[0, 0])
        sim[b] = _dot(ub, wsi_ref[0, 0])
    lr = lr_ref[0, 0]
    li = li_ref[0, 0]

    def step(c, carry):
        new = []
        for b in range(bsz):
            xr, xi = carry[2 * b], carry[2 * b + 1]
            xre[b, pl.ds(c, 1), :] = xr
            xim[b, pl.ds(c, 1), :] = xi
            sr = sre[b, pl.ds(c, 1), :]
            si = sim[b, pl.ds(c, 1), :]
            new.append(lr * xr - li * xi + sr)
            new.append(lr * xi + li * xr + si)
        return tuple(new)

    zero = tuple(jnp.zeros((1, LANES), F32) for _ in range(2 * bsz))

    @pl.when(d == 0)
    def _():
        lax.fori_loop(0, nc, step, zero, unroll=2)

    @pl.when(d == 1)
    def _():
        carry = lax.fori_loop(0, n_ctx_chunks, lambda k, cr: step(n_ctx_chunks - 1 - k, cr), zero, unroll=2)
        lax.fori_loop(0, nc - n_ctx_chunks, lambda k, cr: step(nc - 1 - k, cr), carry, unroll=2)

    half = u_ref.shape[2] // 2
    for b in range(bsz):
        ub = u_ref[b]
        y_intra = jnp.concatenate([_dot(ub[:, :half], m_ref[0, 0]), _dot(ub[:, half:], m_ref[0, 1])], axis=-1)
        y = (y_intra + _dot(xre[b].astype(MXU_DTYPE), wor_ref[0, 0])
             + _dot(xim[b].astype(MXU_DTYPE), woi_ref[0, 0]))

        @pl.when(d == 0)
        def _():
            acc[b] = y

        @pl.when(d == 1)
        def _():
            o_ref[b] = (acc[b] + y).astype(o_ref.dtype)


def _s5_call(u_t, wts, n_ctx_chunks):
    m, wsr, wsi, wor, woi, lr, li = wts
    bsz, nc, width = u_t.shape
    pw = 2 * S5_CHUNK * S5_GROUP_CH
    n_pairs = width // pw
    blk = lambda shp: pl.BlockSpec((1, 1) + shp, lambda j, d: (d, j, 0, 0))
    return pl.pallas_call(
        functools.partial(_s5_kernel, n_ctx_chunks=n_ctx_chunks),
        grid=(n_pairs, 2),
        in_specs=[pl.BlockSpec((bsz, nc, pw), lambda j, d: (0, 0, j)),
                  pl.BlockSpec((1, 2, pw // 2, pw // 2), lambda j, d: (d, j, 0, 0)),
                  blk((pw, LANES)), blk((pw, LANES)), blk((LANES, pw)), blk((LANES, pw)),
                  blk((1, LANES)), blk((1, LANES))],
        out_specs=pl.BlockSpec((bsz, nc, pw), lambda j, d: (0, 0, j)),
        out_shape=jax.ShapeDtypeStruct((bsz, nc, width), MXU_DTYPE),
        scratch_shapes=[pltpu.VMEM((bsz, nc, LANES), F32) for _ in range(4)] + [pltpu.VMEM((bsz, nc, pw), F32)],
        compiler_params=_params("parallel", "arbitrary"),
        name="s5_scan",
    )(u_t, m, wsr, wsi, wor, woi, lr, li)


def _s5_weights(a_re, a_im, log_step, b_re, b_im, c_re, c_im):
    lc, g, p, h = S5_CHUNK, S5_GROUPS, S5_STATE, S5_GROUP_CH
    lam = lax.complex(jnp.minimum(a_re.astype(F32), S5_EIG_MAX), a_im.astype(F32))
    step = jnp.exp(log_step.astype(F32))[..., None]
    lam_bar = jnp.exp(lam * step)
    b_bar = ((lam_bar - 1.0) / lam)[..., None] * lax.complex(b_re.astype(F32), b_im.astype(F32))
    cc = lax.complex(c_re.astype(F32), c_im.astype(F32))
    dd = jnp.arange(lc + 1, dtype=F32)
    pw = jnp.exp((lam * step)[..., None] * dd)
    kern = jnp.real(jnp.einsum('zgop,zgpd,zgpi->zgdoi', cc, pw[..., :lc], b_bar))
    jj = np.arange(lc)[:, None]
    ii = np.arange(lc)[None, :]
    mats, wst, wout = [], [], []
    for z in range(2):
        lag = (ii - jj) if z == 0 else (jj - ii)
        ok = lag >= 0
        kz = kern[z][:, np.where(ok, lag, 0)]
        kz = jnp.where(ok[None, :, :, None, None], kz, 0.0)
        mats.append(kz.transpose(0, 1, 4, 2, 3).reshape(g, lc * h, lc * h))
        d_state = (lc - 1 - np.arange(lc)) if z == 0 else np.arange(lc)
        ws = pw[z][:, :, d_state][..., None] * b_bar[z][:, :, None, :]
        wst.append(ws.transpose(0, 2, 3, 1).reshape(g, lc * h, p))
        d_out = (np.arange(lc) + 1) if z == 0 else (lc - np.arange(lc))
        wo = cc[z][:, :, :, None] * pw[z][:, None, :, :][..., d_out]
        wout.append(wo.transpose(0, 2, 3, 1).reshape(g, p, lc * h))
    mats = jnp.stack(mats)
    wst = jnp.stack(wst)
    wout = jnp.stack(wout)

    def pair_rows(w):
        w = w.reshape(2, g // 2, 2, lc * h, p)
        z0 = jnp.zeros_like(w[:, :, 0])
        top = jnp.concatenate([w[:, :, 0], z0], axis=-1)
        bot = jnp.concatenate([z0, w[:, :, 1]], axis=-1)
        return jnp.concatenate([top, bot], axis=-2)

    def pair_cols(w):
        w = w.reshape(2, g // 2, 2, p, lc * h)
        z0 = jnp.zeros_like(w[:, :, 0])
        top = jnp.concatenate([w[:, :, 0], z0], axis=-1)
        bot = jnp.concatenate([z0, w[:, :, 1]], axis=-1)
        return jnp.concatenate([top, bot], axis=-2)

    lam_c = pw[..., lc].reshape(2, g // 2, 1, 2 * p)
    cast = lambda w: w.astype(MXU_DTYPE)
    return (cast(mats), cast(pair_rows(jnp.real(wst))), cast(pair_rows(jnp.imag(wst))),
            cast(pair_cols(jnp.real(wout))), cast(pair_cols(-jnp.imag(wout))),
            jnp.real(lam_c), jnp.imag(lam_c))


def _gelu_tanh(x):
    cdf = 0.5 * (1.0 + jnp.tanh(math.sqrt(2.0 / math.pi) * (x + 0.044715 * (x * x * x))))
    return x * cdf


def _outproj_kernel(x_ref, ya_ref, ys_ref, su_ref, yc_ref, modb_ref, modc_ref, woa_ref, wob_ref, woc_ref,
                    wglu_ref, bglu_ref, dsk_ref, g_ref, wr_ref, br_ref,
                    xo_ref, h_hbm, topi_ref, topw_ref, rank_ref, cnt_ref, carry, hbuf, hsem, *, ctx_len):
    segs = _mod_segments(modb_ref, modc_ref, ctx_len)
    y = dsk_ref[...] * su_ref[0] + ys_ref[0].astype(F32)
    gl = _gelu_tanh(y)
    yb = gl * jax.nn.sigmoid(_dot(gl.astype(MXU_DTYPE), wglu_ref[...]) + bglu_ref[...])
    mix = (_dot(ya_ref[0], woa_ref[...]) + _dot(yb.astype(MXU_DTYPE), wob_ref[...])
           + _dot(yc_ref[0], woc_ref[...]))
    x_in = x_ref[0]
    x = _by_segment(segs, lambda a, b, m: x_in[a:b] + m[2:3] * mix[a:b])
    xo_ref[0] = x
    y = _rms(x, g_ref[...])
    h = _by_segment(segs, lambda a, b, m: y[a:b] * (1.0 + m[4:5]) + m[3:4])
    dl = h_hbm.shape[-1]
    tile = pl.program_id(1)
    step = pl.program_id(0) * pl.num_programs(1) + tile
    slot = lax.rem(step, 2)

    def h_wait(sl_):
        pltpu.make_async_copy(hbuf.at[sl_], hbuf.at[sl_], hsem.at[sl_]).wait()

    @pl.when(step >= 2)
    def _():
        h_wait(slot)

    hbuf[slot] = h
    for sl in range(SUBLANES):
        pltpu.make_async_copy(hbuf.at[slot, :, pl.ds(dl * sl, dl)],
                              h_hbm.at[pl.program_id(0), pl.ds(tile * ROW_TILE, ROW_TILE), sl, :],
                              hsem.at[slot]).start()

    @pl.when(step == pl.num_programs(0) * pl.num_programs(1) - 1)
    def _():
        h_wait(1 - slot)
        h_wait(slot)

    logits = _dot3(wr_ref[...], h, nt=True) + br_ref[...]
    n_e, tm = logits.shape
    e_iota = lax.broadcasted_iota(jnp.int32, (n_e, tm), 0)
    vals, idxs = [], []
    for _ in range(TOP_K):
        mx = jnp.max(logits, axis=0, keepdims=True)
        ix = jnp.min(jnp.where(logits == mx, e_iota, n_e), axis=0, keepdims=True)
        vals.append(mx)
        idxs.append(ix)
        logits = jnp.where(e_iota == ix, -jnp.inf, logits)
    ex = [jnp.exp(v - vals[0]) for v in vals]
    den = ex[0] + ex[1] + ex[2] + ex[3]
    topi_ref[0] = jnp.concatenate(idxs, axis=0)
    topw_ref[0] = jnp.concatenate([e / den for e in ex], axis=0)

    @pl.when((pl.program_id(0) == 0) & (pl.program_id(1) == 0))
    def _():
        carry[...] = jnp.zeros_like(carry)

    sel = [ix == e_iota for ix in idxs]
    onehot = jnp.where(sel[0] | sel[1] | sel[2] | sel[3], 1.0, 0.0)
    before = (lax.broadcasted_iota(jnp.int32, (tm, tm), 0) < lax.broadcasted_iota(jnp.int32, (tm, tm), 1))
    pfx = _dot(onehot.astype(MXU_DTYPE), jnp.where(before, 1.0, 0.0).astype(MXU_DTYPE)) + carry[:, 0:1]
    rank_ref[0] = jnp.concatenate(
        [jnp.sum(jnp.where(sel[k], pfx, 0.0), axis=0, keepdims=True) for k in range(TOP_K)], axis=0)
    carry[...] = carry[...] + jnp.sum(onehot, axis=1, keepdims=True)
    cnt_ref[...] = carry[...]


def _outproj_call(xs, ya, ys, su, yc, mod_l, wts, ctx_len):
    bsz, s, d = xs.shape
    tm = ROW_TILE
    row = lambda b, j: (b, j, 0)
    const = lambda b, j: (0, 0)
    full = lambda a: pl.BlockSpec(a.shape, const)
    return pl.pallas_call(
        functools.partial(_outproj_kernel, ctx_len=ctx_len),
        grid=(bsz, s // tm),
        in_specs=[pl.BlockSpec((1, tm, d), row),
                  pl.BlockSpec((1, tm, NA_W), row),
                  pl.BlockSpec((1, tm, S5_CH), row),
                  pl.BlockSpec((1, tm, S5_CH), row),
                  pl.BlockSpec((1, tm, SW_W), row),
                  pl.BlockSpec((1, 6, d), lambda b, j: (b, 0, 0)),
                  pl.BlockSpec((1, 6, d), lambda b, j: (bsz, 0, 0))] + [full(a) for a in wts],
        out_specs=[pl.BlockSpec((1, tm, d), row),
                   pl.BlockSpec(memory_space=pl.ANY),
                   pl.BlockSpec((1, TOP_K, tm), lambda b, j: (b, 0, j)),
                   pl.BlockSpec((1, TOP_K, tm), lambda b, j: (b, 0, j)),
                   pl.BlockSpec((1, TOP_K, tm), lambda b, j: (b, 0, j)),
                   pl.BlockSpec((N_EXPERTS, LANES), lambda b, j: (0, 0))],
        out_shape=[jax.ShapeDtypeStruct((bsz, s, d), F32),
                   jax.ShapeDtypeStruct((bsz, s, SUBLANES, d // SUBLANES), F32),
                   jax.ShapeDtypeStruct((bsz, TOP_K, s), jnp.int32),
                   jax.ShapeDtypeStruct((bsz, TOP_K, s), F32),
                   jax.ShapeDtypeStruct((bsz, TOP_K, s), F32),
                   jax.ShapeDtypeStruct((N_EXPERTS, LANES), F32)],
        scratch_shapes=[pltpu.VMEM((N_EXPERTS, LANES), F32), pltpu.VMEM((2, tm, d), F32),
                        pltpu.SemaphoreType.DMA((2,))],
        compiler_params=_params("arbitrary", "arbitrary"),
        name="outproj",
    )(xs, ya, ys, su, yc, mod_l, mod_l, *wts)


def _route_kernel(topi_ref, rank_ref, cnt_ref, dest_ref, blk_ref, ends_ref):
    idx = topi_ref[0]
    tm = idx.shape[1]
    e_iota = lax.broadcasted_iota(jnp.int32, (N_EXPERTS, tm), 0)
    counts = cnt_ref[:, 0:1]
    padded = jnp.ceil(counts * (1.0 / MOE_BLK)) * MOE_BLK
    r_i = lax.broadcasted_iota(jnp.int32, (N_EXPERTS, N_EXPERTS), 0)
    c_i = lax.broadcasted_iota(jnp.int32, (N_EXPERTS, N_EXPERTS), 1)
    padded_row = jnp.sum(jnp.where(r_i == c_i, padded, 0.0), axis=0, keepdims=True)
    pstart = jnp.sum(jnp.where(c_i < r_i, padded_row, 0.0), axis=1, keepdims=True)
    ends = pstart + padded
    rk = rank_ref[0]
    dest = [(jnp.sum(jnp.where(idx[k:k + 1] == e_iota, pstart, 0.0), axis=0, keepdims=True) + rk[k:k + 1]
             ).astype(jnp.int32) for k in range(TOP_K)]
    for u in range(tm // MOE_TILE):
        dest_ref[u] = jnp.concatenate([d[:, u * MOE_TILE:(u + 1) * MOE_TILE] for d in dest], axis=1)
    nb = blk_ref.shape[1]
    blk_start = (lax.broadcasted_iota(jnp.int32, (N_EXPERTS, nb), 1) * MOE_BLK).astype(F32)
    owner = jnp.sum(jnp.where(ends <= blk_start, 1.0, 0.0), axis=0, keepdims=True)
    blk_ref[...] = jnp.minimum(owner, N_EXPERTS - 1.0).astype(jnp.int32)
    ends_ref[...] = jnp.broadcast_to(ends, ends_ref.shape).astype(jnp.int32)


def _route_call(topi, rank, cnt, n_blocks):
    bsz, _, s = topi.shape
    tm = ROW_TILE
    sub = tm // MOE_TILE
    n_tiles = s // tm
    nb_pad = -(-n_blocks // LANES) * LANES
    tok = pl.BlockSpec((1, TOP_K, tm), lambda b, j: (b, 0, j))
    return pl.pallas_call(
        _route_kernel,
        grid=(bsz, n_tiles),
        in_specs=[tok, tok, pl.BlockSpec((N_EXPERTS, LANES), lambda b, j: (0, 0))],
        out_specs=[pl.BlockSpec((sub, 1, TOP_K * MOE_TILE), lambda b, j: (b * n_tiles + j, 0, 0)),
                   pl.BlockSpec((1, nb_pad), lambda b, j: (0, 0)),
                   pl.BlockSpec((N_EXPERTS, LANES), lambda b, j: (0, 0))],
        out_shape=[jax.ShapeDtypeStruct((bsz * n_tiles * sub, 1, TOP_K * MOE_TILE), jnp.int32),
                   jax.ShapeDtypeStruct((1, nb_pad), jnp.int32),
                   jax.ShapeDtypeStruct((N_EXPERTS, LANES), jnp.int32)],
        compiler_params=_params("arbitrary", "arbitrary"),
        name="route_dest",
    )(topi, rank, cnt)


def _dispatch_kernel(ends_ref, dest_ref, h_ref, xg_ref, zbuf, stage, sem, zsem, *, n_blocks):
    tm = h_ref.shape[1]

    @pl.when((pl.program_id(0) == 0) & (pl.program_id(1) == 0))
    def _():
        zbuf[...] = jnp.zeros_like(zbuf)

        def fill(row):
            return pltpu.make_async_copy(zbuf, xg_ref.at[pl.ds(pl.multiple_of(row, MOE_BLK), MOE_BLK)], zsem)

        def each(fn):
            for e in range(N_EXPERTS):
                begin = ends_ref[e - 1] if e else 0

                @pl.when(ends_ref[e] > begin)
                def _():
                    fn(fill(ends_ref[e] - MOE_BLK))

            def dead(i, c):
                fn(fill(i * MOE_BLK))
                return c

            lax.fori_loop(ends_ref[N_EXPERTS - 1] // MOE_BLK, n_blocks, dead, 0)

        each(lambda cp: cp.start())
        each(lambda cp: cp.wait())

    step = pl.program_id(0) * pl.num_programs(1) + pl.program_id(1)
    n_steps = pl.num_programs(0) * pl.num_programs(1)
    slot = lax.rem(step, 2)
    stage[slot] = h_ref[0]

    def body(t, c):
        for k in range(TOP_K):
            pltpu.make_async_copy(stage.at[slot, t], xg_ref.at[dest_ref[0, 0, k * tm + t]],
                                  sem.at[slot]).start(priority=k % 2)
        return c

    lax.fori_loop(0, tm, body, 0, unroll=8)

    def wait_tile(sl):
        pltpu.make_async_copy(xg_ref.at[pl.ds(0, TOP_K * tm)], xg_ref.at[pl.ds(0, TOP_K * tm)], sem.at[sl]).wait()

    @pl.when(step > 0)
    def _():
        wait_tile(1 - slot)

    @pl.when(step == n_steps - 1)
    def _():
        wait_tile(slot)


def _dispatch_call(ends, dest, h, n_blocks):
    bsz, s, _, dl = h.shape
    tm = MOE_TILE
    n_tiles = s // tm
    grid_spec = pltpu.PrefetchScalarGridSpec(
        num_scalar_prefetch=1,
        grid=(bsz, n_tiles),
        in_specs=[pl.BlockSpec((1, 1, TOP_K * tm), lambda b, j, en: (b * n_tiles + j, 0, 0), memory_space=pltpu.SMEM),
                  pl.BlockSpec((1, tm, SUBLANES, dl), lambda b, j, en: (b, j, 0, 0))],
        out_specs=pl.BlockSpec(memory_space=pl.ANY),
        scratch_shapes=[pltpu.VMEM((MOE_BLK, SUBLANES, dl), h.dtype), pltpu.VMEM((2, tm, SUBLANES, dl), h.dtype),
                        pltpu.SemaphoreType.DMA((2,)),
                        pltpu.SemaphoreType.DMA(())],
    )
    return pl.pallas_call(
        functools.partial(_dispatch_kernel, n_blocks=n_blocks),
        grid_spec=grid_spec,
        out_shape=jax.ShapeDtypeStruct((n_blocks * MOE_BLK, SUBLANES, dl), h.dtype),
        compiler_params=_params("arbitrary", "arbitrary"),
        name="moe_dispatch",
    )(ends, dest, h)


def _moe_kernel(blk_exp_ref, nact_ref, x_hbm, wgu_ref, bgu_ref, wd_ref, bd_ref, y_hbm, wgu_c, wd_c,
                xbuf, xsem, ybuf, ysem):
    i = pl.program_id(0)
    slot = lax.rem(i, 2)
    dl = x_hbm.shape[-1]

    def fetch(blk, sl_):
        for s in range(SUBLANES):
            pltpu.make_async_copy(x_hbm.at[pl.ds(blk * MOE_BLK, MOE_BLK), s, :],
                                  xbuf.at[sl_, :, pl.ds(dl * s, dl)], xsem.at[sl_]).start()

    @pl.when(i == 0)
    def _():
        fetch(0, 0)

    @pl.when(i + 1 < pl.num_programs(0))
    def _():
        fetch(i + 1, 1 - slot)

    def y_wait(sl_):
        pltpu.make_async_copy(ybuf.at[sl_], ybuf.at[sl_], ysem.at[sl_]).wait()

    @pl.when(i >= 2)
    def _():
        y_wait(slot)

    e = blk_exp_ref[i]
    prev = blk_exp_ref[jnp.maximum(i - 1, 0)]
    d, f2 = wgu_c.shape
    f = f2 // 2
    rows = 128

    @pl.when((i == 0) | (e != prev))
    def _():
        def cv(r, c):
            sl = pl.ds(pl.multiple_of(r * rows, rows), rows)
            wgu_c[sl, :] = wgu_ref[0, 0, sl, :].astype(wgu_c.dtype)
            return c
        lax.fori_loop(0, d // rows, cv, 0)

        def cv2(r, c):
            sl = pl.ds(pl.multiple_of(r * rows, rows), rows)
            wd_c[sl, :] = wd_ref[0, 0, sl, :].astype(wd_c.dtype)
            return c
        lax.fori_loop(0, f // rows, cv2, 0)

    pltpu.make_async_copy(xbuf.at[slot], xbuf.at[slot], xsem.at[slot]).wait()

    @pl.when(i < nact_ref[0])
    def _():
        gu = _dot(xbuf[slot].astype(MXU_DTYPE), wgu_c[...]) + bgu_ref[0, 0]
        gate = jnp.minimum(gu[:, :f], SWIGLU_LIMIT)
        up = jnp.clip(gu[:, f:], -SWIGLU_LIMIT, SWIGLU_LIMIT)
        act = gate * jax.nn.sigmoid(SWIGLU_ALPHA * gate) * (up + 1.0)
        ybuf[slot] = _dot(act.astype(MXU_DTYPE), wd_c[...]) + bd_ref[0, 0]

    @pl.when(i >= nact_ref[0])
    def _():
        ybuf[slot] = jnp.zeros(ybuf.shape[1:], ybuf.dtype)

    for s in range(SUBLANES):
        pltpu.make_async_copy(ybuf.at[slot, :, pl.ds(dl * s, dl)],
                              y_hbm.at[pl.ds(i * MOE_BLK, MOE_BLK), s, :], ysem.at[slot]).start()

    @pl.when(i == pl.num_programs(0) - 1)
    def _():
        y_wait(1 - slot)
        y_wait(slot)


def _moe_call(layer, blk_exp, n_active, xg, w_gate_up, b_gate_up, w_down, b_down):
    n_rows, _, dl = xg.shape
    depth, n_e, d, f2 = w_gate_up.shape
    f = f2 // 2
    n_blocks = n_rows // MOE_BLK
    assert n_blocks >= 2
    wsel = lambda i, be, na: (layer, be[i], 0, 0)
    grid_spec = pltpu.PrefetchScalarGridSpec(
        num_scalar_prefetch=2,
        grid=(n_blocks,),
        in_specs=[pl.BlockSpec(memory_space=pl.ANY),
                  pl.BlockSpec((1, 1, d, f2), wsel),
                  pl.BlockSpec((1, 1, 1, f2), wsel),
                  pl.BlockSpec((1, 1, f, d), wsel),
                  pl.BlockSpec((1, 1, 1, d), wsel)],
        out_specs=pl.BlockSpec(memory_space=pl.ANY),
        scratch_shapes=[pltpu.VMEM((d, f2), MXU_DTYPE), pltpu.VMEM((f, d), MXU_DTYPE),
                        pltpu.VMEM((2, MOE_BLK, d), F32), pltpu.SemaphoreType.DMA((2,)),
                        pltpu.VMEM((2, MOE_BLK, d), F32), pltpu.SemaphoreType.DMA((2,))],
    )
    return pl.pallas_call(
        _moe_kernel,
        grid_spec=grid_spec,
        out_shape=jax.ShapeDtypeStruct((n_rows, SUBLANES, dl), F32),
        compiler_params=_params("arbitrary"),
        name="moe_experts",
    )(blk_exp, n_active, xg, w_gate_up, b_gate_up.reshape(depth, n_e, 1, f2),
      w_down, b_down.reshape(depth, n_e, 1, d))


def _combine_kernel(dest_ref, dnext_ref, w_ref, gate_ref, x_hbm, yg_hbm, xo_hbm,
                    gbuf, xbuf, obuf, gsem, xsem, osem, *, n_tiles):
    tm = MOE_TILE
    dl = gbuf.shape[-1]
    step = pl.program_id(0) * n_tiles + pl.program_id(1)
    n_steps = pl.num_programs(0) * n_tiles
    slot = lax.rem(step, 2)

    def stream_copies(st, sl_, buf, hbm, sem, to_hbm):
        b = lax.div(st, n_tiles)
        r0 = lax.rem(st, n_tiles) * tm
        out = []
        for s in range(SUBLANES):
            rows = hbm.at[b, pl.ds(r0, tm), pl.ds(dl * s, dl)]
            tiles = buf.at[sl_, :, s, :]
            out.append(pltpu.make_async_copy(tiles, rows, sem.at[sl_]) if to_hbm
                       else pltpu.make_async_copy(rows, tiles, sem.at[sl_]))
        return out

    def fetch(dref, st, sl_):
        for cp in stream_copies(st, sl_, xbuf, x_hbm, xsem, False):
            cp.start()

        def body(t, c):
            for k in range(TOP_K):
                pltpu.make_async_copy(yg_hbm.at[dref[0, 0, k * tm + t]], gbuf.at[sl_, k, t],
                                      gsem.at[sl_]).start(priority=k % 2)
            return c

        lax.fori_loop(0, tm, body, 0, unroll=8)

    @pl.when(step == 0)
    def _():
        fetch(dest_ref, 0, 0)

    @pl.when(step + 1 < n_steps)
    def _():
        fetch(dnext_ref, step + 1, 1 - slot)

    def wait_all(buf, sem, sl_):
        pltpu.make_async_copy(buf.at[sl_], buf.at[sl_], sem.at[sl_]).wait()

    wait_all(gbuf, gsem, slot)
    wait_all(xbuf, xsem, slot)

    @pl.when(step >= 2)
    def _():
        wait_all(obuf, osem, slot)

    gate = gate_ref[0]

    def row(t, c):
        acc = gbuf[slot, 0, t] * w_ref[0, 0, t]
        for k in range(1, TOP_K):
            acc = acc + gbuf[slot, k, t] * w_ref[0, 0, k * tm + t]
        obuf[slot, t] = xbuf[slot, t] + gate * acc
        return c

    lax.fori_loop(0, tm, row, 0, unroll=8)
    for cp in stream_copies(step, slot, obuf, xo_hbm, osem, True):
        cp.start()

    @pl.when(step == n_steps - 1)
    def _():
        wait_all(obuf, osem, 1 - slot)
        wait_all(obuf, osem, slot)


def _combine_call(dest, topw, xs, mod_l, yg, ctx_len):
    bsz, s, d = xs.shape
    tm = MOE_TILE
    n_tiles = s // tm
    last = bsz * n_tiles - 1
    assert last >= 1
    dl = d // SUBLANES
    w_flat = topw.reshape(bsz, TOP_K, n_tiles, tm).transpose(0, 2, 1, 3).reshape(bsz * n_tiles, 1, TOP_K * tm)
    gate = mod_l[:, 5].reshape(MOD_ROWS, SUBLANES, dl)
    idx_spec = lambda ahead: pl.BlockSpec(
        (1, 1, TOP_K * tm), lambda b, j: (jnp.minimum(b * n_tiles + j + ahead, last), 0, 0), memory_space=pltpu.SMEM)
    tile_buf = lambda lead: pltpu.VMEM(lead + (tm, SUBLANES, dl), F32)
    return pl.pallas_call(
        functools.partial(_combine_kernel, n_tiles=n_tiles),
        grid=(bsz, n_tiles),
        in_specs=[idx_spec(0), idx_spec(1), idx_spec(0),
                  pl.BlockSpec((1, SUBLANES, dl), lambda b, j: (jnp.where(j < ctx_len // tm, bsz, b), 0, 0)),
                  pl.BlockSpec(memory_space=pl.ANY),
                  pl.BlockSpec(memory_space=pl.ANY)],
        out_specs=pl.BlockSpec(memory_space=pl.ANY),
        out_shape=jax.ShapeDtypeStruct((bsz, s, d), F32),
        scratch_shapes=[tile_buf((2, TOP_K)), tile_buf((2,)), tile_buf((2,)),
                        pltpu.SemaphoreType.DMA((2,)), pltpu.SemaphoreType.DMA((2,)), pltpu.SemaphoreType.DMA((2,))],
        compiler_params=_params("arbitrary", "arbitrary"),
        name="moe_combine",
    )(dest, dest, w_flat, gate, xs, yg)


def _final_kernel(x_ref, g_ref, o_ref):
    x = x_ref[0]
    o_ref[0] = x * lax.rsqrt(jnp.mean(x * x, axis=-1, keepdims=True) + RMS_EPS) * g_ref[...]


def _final_call(xs, g, ctx_len):
    bsz, s, d = xs.shape
    tm = MOE_TILE
    off = ctx_len // tm
    return pl.pallas_call(
        _final_kernel,
        grid=(bsz, (s - ctx_len) // tm),
        in_specs=[pl.BlockSpec((1, tm, d), lambda b, j: (b, j + off, 0)),
                  pl.BlockSpec((1, d), lambda b, j: (0, 0))],
        out_specs=pl.BlockSpec((1, tm, d), lambda b, j: (b, j, 0)),
        out_shape=jax.ShapeDtypeStruct((bsz, s - ctx_len, d), F32),
        compiler_params=_params("parallel", "parallel"),
        name="final_norm",
    )(xs, g)


def _rope_tables(seq, ctx_len):
    t = jnp.arange(seq)
    row = (t // GRID_W).astype(F32)
    col = (t % GRID_W).astype(F32)
    nf = HEAD_DIM // 4
    inv = ROPE_BASE ** (-jnp.arange(nf, dtype=F32) / nf)
    ar = row[:, None] * inv
    ac = col[:, None] * inv
    ang = jnp.concatenate([ar, ar, ac, ac], axis=-1)
    cos = jnp.concatenate([jnp.ones((ctx_len, HEAD_DIM), F32), jnp.cos(ang)], axis=0)
    sin = jnp.concatenate([jnp.zeros((ctx_len, HEAD_DIM), F32), jnp.sin(ang)], axis=0)
    reps = LANES // HEAD_DIM
    return jnp.tile(cos, (1, reps)), jnp.tile(sin, (1, reps))


def _rot_cols(w):
    q = HEAD_DIM // 4
    j = np.arange(HEAD_DIM)
    first = (j % (2 * q)) < q
    src = np.where(first, j + q, j - q)
    sign = np.where(first, -1.0, 1.0).astype(np.float32)
    n_heads = w.shape[1] // HEAD_DIM
    src_all = (np.arange(n_heads)[:, None] * HEAD_DIM + src[None, :]).reshape(-1)
    return w[:, src_all] * jnp.asarray(np.tile(sign, n_heads))


def _head_perm_cols(order):
    return (np.asarray(order)[:, None] * HEAD_DIM + np.arange(HEAD_DIM)[None, :]).reshape(-1)


def _inproj_weight(w_in_l):
    aq, ak, av, su, sq, sk, sv = jnp.split(
        w_in_l, np.cumsum([NA_W, NA_W, NA_W, S5_CH, SW_W, SW_KV_W])[:6].tolist(), axis=1)
    sq = sq[:, _head_perm_cols(SW_HEAD_ORDER)]
    return jnp.concatenate([aq, ak, av, sq, _rot_cols(sq), sk, _rot_cols(sk), sv, su], axis=1).astype(MXU_DTYPE)


def _swap_kernel(v_ref, o_ref, *, inner, h):
    b = pl.program_id(1)
    kdim, ndim = v_ref.shape[1], o_ref.shape[1]
    n = lax.broadcasted_iota(jnp.int32, (1, ndim), 1)
    shift = h.bit_length() - 1
    src = lax.shift_right_logical(n, shift) * inner + b * h + (n & (h - 1))
    k = lax.broadcasted_iota(jnp.int32, (kdim, ndim), 0)
    perm = jnp.where(k == src, 1.0, 0.0).astype(MXU_DTYPE)
    o_ref[...] = _dot(v_ref[...].astype(MXU_DTYPE), perm).astype(o_ref.dtype)


def _swap_call(v, rows, n_a, n_b, h):
    r, width = v.shape
    assert width == n_a * n_b * h and h & (h - 1) == 0 and r % rows == 0
    return pl.pallas_call(
        functools.partial(_swap_kernel, inner=n_b * h, h=h),
        grid=(r // rows, n_b),
        in_specs=[pl.BlockSpec((rows, width), lambda i, b: (i, 0))],
        out_specs=pl.BlockSpec((rows, n_a * h), lambda i, b: (i, b)),
        out_shape=jax.ShapeDtypeStruct((r, width), MXU_DTYPE),
        compiler_params=_params("parallel", "arbitrary"),
        name="s5_swap",
    )(v)


def _chunk_major(su):
    bsz, s, _ = su.shape
    nc = s // S5_CHUNK
    u = _swap_call(su.reshape(bsz * nc, S5_CHUNK * S5_CH), nc, S5_CHUNK, S5_GROUPS, S5_GROUP_CH)
    return u.reshape(bsz, nc, S5_CHUNK * S5_CH)


def _token_major(y_t):
    bsz, nc, width = y_t.shape
    y = _swap_call(y_t.reshape(bsz * nc, width), nc, S5_GROUPS, S5_CHUNK, S5_GROUP_CH)
    return y.reshape(bsz, nc * S5_CHUNK, S5_CH)


def kernel(x, c, ctx, c_ctx, w_mod, b_mod, g_mix, w_in, w_out, na_rpb, s5_a_re, s5_a_im, s5_log_step,
           s5_b_re, s5_b_im, s5_c_re, s5_c_im, s5_d, s5_w_glu, s5_b_glu, sw_sinks, g_ffn, w_router, b_router,
           w_gate_up, b_gate_up, w_down, b_down, g_final):
    bsz, seq, d = x.shape
    ctx_len = ctx.shape[1]
    depth = w_mod.shape[0]
    s = ctx_len + seq
    assert bsz + 1 <= MOD_ROWS and s % ROW_TILE == 0 and ctx_len % MOE_TILE == 0 and seq % MOE_TILE == 0
    assert ROW_TILE % MOE_TILE == 0
    assert seq % GRID_W == 0 and ctx_len % S5_CHUNK == 0

    xs = jnp.concatenate([ctx, x], axis=1)
    cond = jnp.zeros((MOD_ROWS, d), F32).at[:bsz].set(c).at[bsz].set(c_ctx)
    mod = _mod_call(cond, w_mod, b_mod).reshape(depth, MOD_ROWS, 6, d)
    cos2, sin2 = _rope_tables(seq, ctx_len)

    n_assign = bsz * s * TOP_K
    n_blocks = -(-(n_assign + N_EXPERTS * (MOE_BLK - 1)) // MOE_BLK)
    sw_rows = _head_perm_cols(SW_HEAD_ORDER)

    w_cat = jax.vmap(_inproj_weight)(w_in)
    bias_tab = jax.vmap(_na_bias_table)(na_rpb)
    s5w = jax.vmap(_s5_weights)(s5_a_re, s5_a_im, s5_log_step, s5_b_re, s5_b_im, s5_c_re, s5_c_im)
    wo_a = w_out[:, :NA_W].astype(MXU_DTYPE)
    wo_b = w_out[:, NA_W:NA_W + S5_CH].astype(MXU_DTYPE)
    wo_c = w_out[:, NA_W + S5_CH:][:, sw_rows].astype(MXU_DTYPE)
    w_glu = s5_w_glu.astype(MXU_DTYPE)
    w_rt = jnp.swapaxes(w_router, 1, 2).astype(F32)

    for l in range(depth):
        mod_l = mod[l]
        naq, nak, nav, swq, swk, swv, su = _inproj_call(
            xs, mod_l, g_mix[l].reshape(1, d), w_cat[l], cos2, sin2, ctx_len)
        ya = _na_call(naq, nak, nav, bias_tab[l], ctx_len)
        yc = _sw_call(sw_sinks[l], swq, swk, swv, ctx_len)
        ys = _token_major(_s5_call(_chunk_major(su), tuple(w[l] for w in s5w), ctx_len // S5_CHUNK))

        out_wts = (wo_a[l], wo_b[l], wo_c[l], w_glu[l], s5_b_glu[l].reshape(1, S5_CH).astype(F32),
                   s5_d[l].reshape(1, S5_CH).astype(F32), g_ffn[l].reshape(1, d),
                   w_rt[l], b_router[l].reshape(N_EXPERTS, 1).astype(F32))
        xs, h, topi, topw, rank, cnt = _outproj_call(xs, ya, ys, su, yc, mod_l, out_wts, ctx_len)

        dest, blk, ends = _route_call(topi, rank, cnt, n_blocks)
        ends = ends[:, 0]
        xg = _dispatch_call(ends, dest, h, n_blocks)
        yg = _moe_call(l, blk[0, :n_blocks], ends[-1:] // MOE_BLK, xg, w_gate_up, b_gate_up, w_down, b_down)
        xs = _combine_call(dest, topw, xs, mod_l, yg, ctx_len)

    return _final_call(xs, g_final.reshape(1, d), ctx_len)
```

```python
import functools
import math

import numpy as np
import jax
import jax.numpy as jnp
from jax import lax
from jax.experimental import pallas as pl
from jax.experimental.pallas import tpu as pltpu

F32 = jnp.float32
MXU_DTYPE = jnp.bfloat16

GRID_W = 64
HEAD_DIM = 64
NA_HEADS = 6
NA_W = NA_HEADS * HEAD_DIM
NA_ROWS = 8
NA_COLS = 16
S5_GROUP_CH = 16
S5_CH = 256
S5_GROUPS = S5_CH // S5_GROUP_CH
S5_STATE = 64
S5_EIG_MAX = -1e-4
SW_HEADS = 6
SW_KV_HEADS = 2
SW_GRP = SW_HEADS // SW_KV_HEADS
SW_W = SW_HEADS * HEAD_DIM
SW_KV_W = SW_KV_HEADS * HEAD_DIM
SW_WINDOW = 128
SW_BLK = 128
ROPE_BASE = 10000.0
N_EXPERTS = 32
TOP_K = 4
MOE_BLK = 512
SWIGLU_LIMIT = 7.0
SWIGLU_ALPHA = 1.702
RMS_EPS = 1e-6
NEG_INF = -1e30

LANES = 128
ROW_TILE = 768
MOE_TILE = 256
S5_CHUNK = 16
NA_QROWS = 4
NA_KROWS = NA_QROWS + NA_ROWS
SW_QBLK = 2 * SW_BLK
SW_KBLK = SW_QBLK + 2 * SW_WINDOW
MOD_ROWS = 8
VMEM_LIMIT = 56 << 20
SUBLANES = 8

SW_HEAD_ORDER = tuple(g * SW_GRP + t for t in range(SW_GRP) for g in range(SW_KV_HEADS))


def _params(*sem):
    return pltpu.CompilerParams(dimension_semantics=sem, vmem_limit_bytes=VMEM_LIMIT)


def _dot(a, b):
    return jnp.dot(a, b, preferred_element_type=F32)


def _dot_nt(a, b):
    return lax.dot_general(a, b, (((1,), (1,)), ((), ())), preferred_element_type=F32)


def _split(a):
    hi = a.astype(MXU_DTYPE)
    lo = (a - hi.astype(F32)).astype(MXU_DTYPE)
    return hi, lo


def _dot3(a, b, nt=False):
    f = _dot_nt if nt else _dot
    ah, al = _split(a)
    bh, bl = _split(b)
    return f(ah, bh) + (f(ah, bl) + f(al, bh))


def _mod_kernel(cond_ref, w_ref, b_ref, o_ref):
    c = cond_ref[...]
    a = c * jax.nn.sigmoid(c)
    o_ref[0] = _dot3(a, w_ref[0]) + b_ref[0]


def _mod_call(cond, w_mod, b_mod):
    depth, d, n = w_mod.shape
    tn = n // 6
    return pl.pallas_call(
        _mod_kernel,
        grid=(depth, n // tn),
        in_specs=[pl.BlockSpec((MOD_ROWS, d), lambda l, j: (0, 0)),
                  pl.BlockSpec((1, d, tn), lambda l, j: (l, 0, j)),
                  pl.BlockSpec((1, 1, tn), lambda l, j: (l, 0, j))],
        out_specs=pl.BlockSpec((1, MOD_ROWS, tn), lambda l, j: (l, 0, j)),
        out_shape=jax.ShapeDtypeStruct((depth, MOD_ROWS, n), F32),
        compiler_params=_params("parallel", "parallel"),
        name="mod",
    )(cond, w_mod, b_mod.reshape(depth, 1, n))


C_AQ = 0
C_AK = C_AQ + NA_W
C_AV = C_AK + NA_W
C_SQ = C_AV + NA_W
C_SQR = C_SQ + SW_W
C_SK = C_SQR + SW_W
C_SKR = C_SK + SW_KV_W
C_SV = C_SKR + SW_KV_W
C_SU = C_SV + SW_KV_W
C_END = C_SU + S5_CH


def _rms(x, g):
    return x * lax.rsqrt(jnp.mean(x * x, axis=-1, keepdims=True) + RMS_EPS) * g


def _mod_segments(modb_ref, modc_ref, ctx_len):
    r = ctx_len % ROW_TILE
    out = []
    for a, b in ([(0, ROW_TILE)] if r == 0 else [(0, r), (r, ROW_TILE)]):
        is_ctx = pl.program_id(1) * ROW_TILE + a < ctx_len
        out.append((a, b, jnp.where(is_ctx, modc_ref[0], modb_ref[0])))
    return out


def _by_segment(segs, fn):
    return jnp.concatenate([fn(a, b, m) for a, b, m in segs], axis=0)


def _chunk_row_copies(buf, slot, hbm, sem, to_hbm):
    chunks, _, width = buf.shape[1:]
    c0 = (pl.program_id(0) * pl.num_programs(1) + pl.program_id(1)) * chunks
    out = []
    for i in range(S5_CHUNK):
        tile = buf.at[slot, :, i, :]
        rows = hbm.at[pl.ds(c0, chunks), pl.ds(width * i, width)]
        out.append(pltpu.make_async_copy(tile, rows, sem.at[slot]) if to_hbm
                   else pltpu.make_async_copy(rows, tile, sem.at[slot]))
    return out


def _inproj_kernel(x_ref, modb_ref, modc_ref, g_ref, w_ref, cos_ref, sin_ref,
                   naq_ref, nak_ref, nav_ref, swq_ref, swk_ref, swv_ref, su_ref, sur_hbm, sbuf, ssem, *, ctx_len):
    segs = _mod_segments(modb_ref, modc_ref, ctx_len)
    y = _rms(x_ref[0], g_ref[...])
    h = _by_segment(segs, lambda a, b, m: y[a:b] * (1.0 + m[1:2]) + m[0:1]).astype(MXU_DTYPE)
    p = _dot(h, w_ref[...])
    cos = cos_ref[...]
    sin = sin_ref[...]
    cos3 = jnp.concatenate([cos] * (SW_W // LANES), axis=1)
    sin3 = jnp.concatenate([sin] * (SW_W // LANES), axis=1)
    qk_scale = HEAD_DIM ** -0.5
    naq_ref[0] = (p[:, C_AQ:C_AK] * qk_scale).astype(naq_ref.dtype)
    nak_ref[0] = p[:, C_AK:C_AV].astype(nak_ref.dtype)
    nav_ref[0] = p[:, C_AV:C_SQ].astype(nav_ref.dtype)
    swq_ref[0] = ((p[:, C_SQ:C_SQR] * cos3 + p[:, C_SQR:C_SK] * sin3) * qk_scale).astype(swq_ref.dtype)
    swk_ref[0] = (p[:, C_SK:C_SKR] * cos + p[:, C_SKR:C_SV] * sin).astype(swk_ref.dtype)
    swv_ref[0] = p[:, C_SV:C_SU].astype(swv_ref.dtype)
    su = p[:, C_SU:C_END]
    su_ref[0] = su
    step = pl.program_id(0) * pl.num_programs(1) + pl.program_id(1)
    slot = lax.rem(step, 2)

    def s_wait(sl_):
        pltpu.make_async_copy(sbuf.at[sl_], sbuf.at[sl_], ssem.at[sl_]).wait()

    @pl.when(step >= 2)
    def _():
        s_wait(slot)

    sbuf[slot] = su.reshape(sbuf.shape[1:])
    for cp in _chunk_row_copies(sbuf, slot, sur_hbm, ssem, True):
        cp.start()

    @pl.when(step == pl.num_programs(0) * pl.num_programs(1) - 1)
    def _():
        s_wait(1 - slot)
        s_wait(slot)


def _inproj_call(xs, mod_l, g, w_cat, cos2, sin2, ctx_len):
    bsz, s, d = xs.shape
    tm = ROW_TILE
    row = lambda b, j: (b, j, 0)
    const = lambda b, j: (0, 0)
    widths = (NA_W, NA_W, NA_W, SW_W, SW_KV_W, SW_KV_W, S5_CH)
    dtypes = (MXU_DTYPE,) * 6 + (F32,)
    return pl.pallas_call(
        functools.partial(_inproj_kernel, ctx_len=ctx_len),
        grid=(bsz, s // tm),
        in_specs=[pl.BlockSpec((1, tm, d), row),
                  pl.BlockSpec((1, 6, d), lambda b, j: (b, 0, 0)),
                  pl.BlockSpec((1, 6, d), lambda b, j: (bsz, 0, 0)),
                  pl.BlockSpec((1, d), const),
                  pl.BlockSpec((d, C_END), const),
                  pl.BlockSpec((tm, LANES), lambda b, j: (j, 0)),
                  pl.BlockSpec((tm, LANES), lambda b, j: (j, 0))],
        out_specs=[pl.BlockSpec((1, tm, w), row) for w in widths] + [pl.BlockSpec(memory_space=pl.ANY)],
        out_shape=[jax.ShapeDtypeStruct((bsz, s, w), t) for w, t in zip(widths, dtypes)]
        + [jax.ShapeDtypeStruct((bsz * s // S5_CHUNK, S5_CHUNK * S5_CH), F32)],
        scratch_shapes=[pltpu.VMEM((2, tm // S5_CHUNK, S5_CHUNK, S5_CH), F32), pltpu.SemaphoreType.DMA((2,))],
        compiler_params=_params("arbitrary", "arbitrary"),
        name="inproj",
    )(xs, mod_l, mod_l, g, w_cat, cos2, sin2)


def _half_masks():
    lane = lax.broadcasted_iota(jnp.int32, (1, LANES), 1)
    return lane < HEAD_DIM, lane >= HEAD_DIM


def _na_kernel(q_ref, k_ref, v_ref, bias_ref, o_ref, *, ctx_len, rows):
    i = pl.program_id(1)
    tq = NA_QROWS * GRID_W
    n_ctx_q = ctx_len // tq
    masks = _half_masks()
    nk = NA_KROWS * GRID_W

    def run(local):
        q = q_ref[0]
        if local:
            r0 = (i - n_ctx_q) * NA_QROWS
            start0 = jnp.clip(r0 - NA_ROWS // 2, 0, rows - NA_KROWS)
            start = pl.multiple_of(ctx_len + start0 * GRID_W, GRID_W)
            tab_idx, row_mask = {}, {}
            for a in range(NA_QROWS):
                r = r0 + a
                s_r = jnp.clip(r - NA_ROWS // 2, 0, rows - NA_ROWS)
                for p in range(NA_KROWS // 2):
                    kr = start0 + 2 * p
                    ok_lo = (kr >= s_r) & (kr < s_r + NA_ROWS)
                    ok_hi = (kr + 1 >= s_r) & (kr + 1 < s_r + NA_ROWS)
                    tab_idx[a, p] = jnp.clip(kr - r + NA_ROWS, 0, 2 * NA_ROWS - 1)
                    row_mask[a, p] = jnp.where(masks[0], jnp.where(ok_lo, 0.0, NEG_INF),
                                               jnp.where(ok_hi, 0.0, NEG_INF))
        outs = []
        for t in range(NA_W // LANES):
            sl = slice(LANES * t, LANES * (t + 1))
            qt = q[:, sl]
            zero = jnp.zeros_like(qt)
            qm = jnp.concatenate([jnp.where(masks[0], qt, zero), jnp.where(masks[1], qt, zero)], axis=0)
            kc = k_ref[0, 0:ctx_len, sl]
            vc = v_ref[0, 0:ctx_len, sl]
            s_cx = _dot_nt(qm, kc)
            m = jnp.max(s_cx, axis=-1, keepdims=True)
            if local:
                kw = k_ref[0, pl.ds(start, nk), sl]
                vw = v_ref[0, pl.ds(start, nk), sl]
                bias = jnp.concatenate(
                    [jnp.concatenate([bias_ref[2 * t + hh, pl.ds(tab_idx[a, p], 1)][0] + row_mask[a, p]
                                      for p in range(NA_KROWS // 2)], axis=-1)
                     for hh in range(2) for a in range(NA_QROWS)], axis=0)
                s_nb = _dot_nt(qm, kw) + bias
                m = jnp.maximum(m, jnp.max(s_nb, axis=-1, keepdims=True))
                p_nb = jnp.exp(s_nb - m)
            p_cx = jnp.exp(s_cx - m)
            den = jnp.sum(p_cx, axis=-1, keepdims=True)
            o = _dot(p_cx.astype(MXU_DTYPE), vc)
            if local:
                den = den + jnp.sum(p_nb, axis=-1, keepdims=True)
                o = o + _dot(p_nb.astype(MXU_DTYPE), vw)
            o = o / den
            outs.append(jnp.where(masks[0], o[:tq], o[tq:]))
        o_ref[0] = jnp.concatenate(outs, axis=-1).astype(o_ref.dtype)

    @pl.when(i < n_ctx_q)
    def _():
        run(False)

    @pl.when(i >= n_ctx_q)
    def _():
        run(True)


def _na_call(q, k, v, bias_tab, ctx_len):
    bsz, s, w = q.shape
    rows = (s - ctx_len) // GRID_W
    tq = NA_QROWS * GRID_W
    assert rows >= NA_KROWS and rows % NA_QROWS == 0 and ctx_len % tq == 0
    whole = lambda b, i: (b, 0, 0)
    return pl.pallas_call(
        functools.partial(_na_kernel, ctx_len=ctx_len, rows=rows),
        grid=(bsz, s // tq),
        in_specs=[pl.BlockSpec((1, tq, w), lambda b, i: (b, i, 0)),
                  pl.BlockSpec((1, s, w), whole),
                  pl.BlockSpec((1, s, w), whole),
                  pl.BlockSpec(bias_tab.shape, lambda b, i: (0, 0, 0, 0))],
        out_specs=pl.BlockSpec((1, tq, w), lambda b, i: (b, i, 0)),
        out_shape=jax.ShapeDtypeStruct((bsz, s, w), q.dtype),
        compiler_params=_params("parallel", "arbitrary"),
        name="na_attn",
    )(q, k, v, bias_tab)


def _na_bias_table(rpb):
    qcol = np.arange(GRID_W)[:, None]
    kcol = np.arange(GRID_W)[None, :]
    ws = np.clip(qcol - NA_COLS // 2, 0, GRID_W - NA_COLS)
    valid = (kcol >= ws) & (kcol < ws + NA_COLS)
    dc = np.clip(kcol - qcol + NA_COLS - 1, 0, 2 * NA_COLS - 2)
    full = jnp.where(valid[None, None], rpb[:, :, dc].astype(F32), NEG_INF)
    edge = jnp.full_like(full[:, :1], NEG_INF)
    full = jnp.concatenate([edge, full, edge], axis=1)
    return jnp.concatenate([full[:, :-1], full[:, 1:]], axis=-1)


def _sw_kernel(sink_ref, q_ref, k_ref, v_ref, o_ref, *, ctx_len, seq):
    i = pl.program_id(1)
    tq = SW_QBLK
    n_ctx_q = ctx_len // tq
    masks = _half_masks()
    nk = SW_KBLK
    first_head = lax.broadcasted_iota(jnp.int32, (2 * tq, 1), 0) < tq

    def run(local):
        q = q_ref[0]
        kc = k_ref[0, 0:ctx_len, :]
        vc = v_ref[0, 0:ctx_len, :]
        if local:
            n = i - n_ctx_q
            start_lat = jnp.clip(n * tq - SW_WINDOW, 0, seq - nk)
            start = pl.multiple_of(ctx_len + start_lat, SW_BLK)
            kw = k_ref[0, pl.ds(start, nk), :]
            vw = v_ref[0, pl.ds(start, nk), :]
            row = lax.broadcasted_iota(jnp.int32, (2 * tq, 1), 0)
            qpos = n * tq + jnp.where(first_head, row, row - tq)
            kpos = start_lat + lax.broadcasted_iota(jnp.int32, (1, nk), 1)
            valid = jnp.abs(qpos - kpos) <= SW_WINDOW
        outs = []
        for t in range(SW_W // LANES):
            qt = q[:, LANES * t:LANES * (t + 1)]
            zero = jnp.zeros_like(qt)
            qm = jnp.concatenate([jnp.where(masks[0], qt, zero), jnp.where(masks[1], qt, zero)], axis=0)
            sink = jnp.where(first_head, sink_ref[SW_HEAD_ORDER[2 * t]], sink_ref[SW_HEAD_ORDER[2 * t + 1]])
            s_cx = _dot_nt(qm, kc)
            m = jnp.maximum(jnp.max(s_cx, axis=-1, keepdims=True), sink)
            if local:
                s_loc = jnp.where(valid, _dot_nt(qm, kw), NEG_INF)
                m = jnp.maximum(m, jnp.max(s_loc, axis=-1, keepdims=True))
                p_loc = jnp.exp(s_loc - m)
            p_cx = jnp.exp(s_cx - m)
            den = jnp.sum(p_cx, axis=-1, keepdims=True) + jnp.exp(sink - m)
            o = _dot(p_cx.astype(MXU_DTYPE), vc)
            if local:
                den = den + jnp.sum(p_loc, axis=-1, keepdims=True)
                o = o + _dot(p_loc.astype(MXU_DTYPE), vw)
            o = o / den
            outs.append(jnp.where(masks[0], o[:tq], o[tq:]))
        o_ref[0] = jnp.concatenate(outs, axis=-1).astype(o_ref.dtype)

    @pl.when(i < n_ctx_q)
    def _():
        run(False)

    @pl.when(i >= n_ctx_q)
    def _():
        run(True)


def _sw_call(sinks, q, k, v, ctx_len):
    bsz, s, w = q.shape
    seq = s - ctx_len
    assert seq >= SW_KBLK and seq % SW_QBLK == 0 and ctx_len % SW_QBLK == 0
    whole = lambda b, i: (b, 0, 0)
    return pl.pallas_call(
        functools.partial(_sw_kernel, ctx_len=ctx_len, seq=seq),
        grid=(bsz, s // SW_QBLK),
        in_specs=[pl.BlockSpec(memory_space=pltpu.SMEM),
                  pl.BlockSpec((1, SW_QBLK, w), lambda b, i: (b, i, 0)),
                  pl.BlockSpec((1, s, SW_KV_W), whole),
                  pl.BlockSpec((1, s, SW_KV_W), whole)],
        out_specs=pl.BlockSpec((1, SW_QBLK, w), lambda b, i: (b, i, 0)),
        out_shape=jax.ShapeDtypeStruct((bsz, s, w), q.dtype),
        compiler_params=_params("parallel", "arbitrary"),
        name="sw_attn",
    )(sinks.astype(F32), q, k, v)


def _s5_kernel(u_ref, m_ref, wsr_ref, wsi_ref, wor_ref, woi_ref, lr_ref, li_ref, o_ref,
               sre, sim, xre, xim, acc, *, n_ctx_chunks):
    d = pl.program_id(1)
    bsz, nc, _ = u_ref.shape
    for b in range(bsz):
        ub = u_ref[b]
        sre[b] = _dot(ub, wsr_ref[0, 0])
        sim[b] = _dot(ub, wsi_ref[0, 0])
    lr = lr_ref[0, 0]
    li = li_ref[0, 0]

    def step(c, carry):
        new = []
        for b in range(bsz):
            xr, xi = carry[2 * b], carry[2 * b + 1]
            xre[b, pl.ds(c, 1), :] = xr
            xim[b, pl.ds(c, 1), :] = xi
            sr = sre[b, pl.ds(c, 1), :]
            si = sim[b, pl.ds(c, 1), :]
            new.append(lr * xr - li * xi + sr)
            new.append(lr * xi + li * xr + si)
        return tuple(new)

    zero = tuple(jnp.zeros((1, LANES), F32) for _ in range(2 * bsz))

    @pl.when(d == 0)
    def _():
        lax.fori_loop(0, nc, step, zero, unroll=2)

    @pl.when(d == 1)
    def _():
        carry = lax.fori_loop(0, n_ctx_chunks, lambda k, cr: step(n_ctx_chunks - 1 - k, cr), zero, unroll=2)
        lax.fori_loop(0, nc - n_ctx_chunks, lambda k, cr: step(nc - 1 - k, cr), carry, unroll=2)

    half = u_ref.shape[2] // 2
    for b in range(bsz):
        ub = u_ref[b]
        y_intra = jnp.concatenate([_dot(ub[:, :half], m_ref[0, 0]), _dot(ub[:, half:], m_ref[0, 1])], axis=-1)
        y = (y_intra + _dot(xre[b].astype(MXU_DTYPE), wor_ref[0, 0])
             + _dot(xim[b].astype(MXU_DTYPE), woi_ref[0, 0]))

        @pl.when(d == 0)
        def _():
            acc[b] = y

        @pl.when(d == 1)
        def _():
            o_ref[b] = (acc[b] + y).astype(o_ref.dtype)


def _s5_call(u_t, wts, n_ctx_chunks):
    m, wsr, wsi, wor, woi, lr, li = wts
    bsz, nc, width = u_t.shape
    pw = 2 * S5_CHUNK * S5_GROUP_CH
    n_pairs = width // pw
    blk = lambda shp: pl.BlockSpec((1, 1) + shp, lambda j, d: (d, j, 0, 0))
    return pl.pallas_call(
        functools.partial(_s5_kernel, n_ctx_chunks=n_ctx_chunks),
        grid=(n_pairs, 2),
        in_specs=[pl.BlockSpec((bsz, nc, pw), lambda j, d: (0, 0, j)),
                  pl.BlockSpec((1, 2, pw // 2, pw // 2), lambda j, d: (d, j, 0, 0)),
                  blk((pw, LANES)), blk((pw, LANES)), blk((LANES, pw)), blk((LANES, pw)),
                  blk((1, LANES)), blk((1, LANES))],
        out_specs=pl.BlockSpec((bsz, nc, pw), lambda j, d: (0, 0, j)),
        out_shape=jax.ShapeDtypeStruct((bsz, nc, width), MXU_DTYPE),
        scratch_shapes=[pltpu.VMEM((bsz, nc, LANES), F32) for _ in range(4)] + [pltpu.VMEM((bsz, nc, pw), F32)],
        compiler_params=_params("parallel", "arbitrary"),
        name="s5_scan",
    )(u_t, m, wsr, wsi, wor, woi, lr, li)


def _s5_weights(a_re, a_im, log_step, b_re, b_im, c_re, c_im):
    lc, g, p, h = S5_CHUNK, S5_GROUPS, S5_STATE, S5_GROUP_CH
    lam = lax.complex(jnp.minimum(a_re.astype(F32), S5_EIG_MAX), a_im.astype(F32))
    step = jnp.exp(log_step.astype(F32))[..., None]
    lam_bar = jnp.exp(lam * step)
    b_bar = ((lam_bar - 1.0) / lam)[..., None] * lax.complex(b_re.astype(F32), b_im.astype(F32))
    cc = lax.complex(c_re.astype(F32), c_im.astype(F32))
    dd = jnp.arange(lc + 1, dtype=F32)
    pw = jnp.exp((lam * step)[..., None] * dd)
    kern = jnp.real(jnp.einsum('zgop,zgpd,zgpi->zgdoi', cc, pw[..., :lc], b_bar))
    jj = np.arange(lc)[:, None]
    ii = np.arange(lc)[None, :]
    mats, wst, wout = [], [], []
    for z in range(2):
        lag = (ii - jj) if z == 0 else (jj - ii)
        ok = lag >= 0
        kz = kern[z][:, np.where(ok, lag, 0)]
        kz = jnp.where(ok[None, :, :, None, None], kz, 0.0)
        mats.append(kz.transpose(0, 1, 4, 2, 3).reshape(g, lc * h, lc * h))
        d_state = (lc - 1 - np.arange(lc)) if z == 0 else np.arange(lc)
        ws = pw[z][:, :, d_state][..., None] * b_bar[z][:, :, None, :]
        wst.append(ws.transpose(0, 2, 3, 1).reshape(g, lc * h, p))
        d_out = (np.arange(lc) + 1) if z == 0 else (lc - np.arange(lc))
        wo = cc[z][:, :, :, None] * pw[z][:, None, :, :][..., d_out]
        wout.append(wo.transpose(0, 2, 3, 1).reshape(g, p, lc * h))
    mats = jnp.stack(mats)
    wst = jnp.stack(wst)
    wout = jnp.stack(wout)

    def pair_rows(w):
        w = w.reshape(2, g // 2, 2, lc * h, p)
        z0 = jnp.zeros_like(w[:, :, 0])
        top = jnp.concatenate([w[:, :, 0], z0], axis=-1)
        bot = jnp.concatenate([z0, w[:, :, 1]], axis=-1)
        return jnp.concatenate([top, bot], axis=-2)

    def pair_cols(w):
        w = w.reshape(2, g // 2, 2, p, lc * h)
        z0 = jnp.zeros_like(w[:, :, 0])
        top = jnp.concatenate([w[:, :, 0], z0], axis=-1)
        bot = jnp.concatenate([z0, w[:, :, 1]], axis=-1)
        return jnp.concatenate([top, bot], axis=-2)

    lam_c = pw[..., lc].reshape(2, g // 2, 1, 2 * p)
    cast = lambda w: w.astype(MXU_DTYPE)
    return (cast(mats), cast(pair_rows(jnp.real(wst))), cast(pair_rows(jnp.imag(wst))),
            cast(pair_cols(jnp.real(wout))), cast(pair_cols(-jnp.imag(wout))),
            jnp.real(lam_c), jnp.imag(lam_c))


def _gelu_tanh(x):
    cdf = 0.5 * (1.0 + jnp.tanh(math.sqrt(2.0 / math.pi) * (x + 0.044715 * (x * x * x))))
    return x * cdf


def _outproj_kernel(x_ref, ya_ref, ysr_hbm, su_ref, yc_ref, modb_ref, modc_ref, woa_ref, wob_ref, woc_ref,
                    wglu_ref, bglu_ref, dsk_ref, g_ref, wr_ref, br_ref,
                    xo_ref, h_hbm, topi_ref, topw_ref, rank_ref, cnt_ref, carry, hbuf, hsem, ybuf, ysem,
                    *, ctx_len):
    segs = _mod_segments(modb_ref, modc_ref, ctx_len)
    n_steps = pl.num_programs(0) * pl.num_programs(1)
    step0 = pl.program_id(0) * pl.num_programs(1) + pl.program_id(1)
    slot0 = lax.rem(step0, 2)
    chunks = ybuf.shape[1]

    def y_fetch(st, sl_):
        for i in range(S5_CHUNK):
            pltpu.make_async_copy(ysr_hbm.at[pl.ds(st * chunks, chunks), pl.ds(S5_CH * i, S5_CH)],
                                  ybuf.at[sl_, :, i, :], ysem.at[sl_]).start()

    @pl.when(step0 == 0)
    def _():
        y_fetch(0, 0)

    @pl.when(step0 + 1 < n_steps)
    def _():
        y_fetch(step0 + 1, 1 - slot0)

    pltpu.make_async_copy(ybuf.at[slot0], ybuf.at[slot0], ysem.at[slot0]).wait()
    ys = ybuf[slot0].reshape(chunks * S5_CHUNK, S5_CH)
    y = dsk_ref[...] * su_ref[0] + ys
    gl = _gelu_tanh(y)
    yb = gl * jax.nn.sigmoid(_dot(gl.astype(MXU_DTYPE), wglu_ref[...]) + bglu_ref[...])
    mix = (_dot(ya_ref[0], woa_ref[...]) + _dot(yb.astype(MXU_DTYPE), wob_ref[...])
           + _dot(yc_ref[0], woc_ref[...]))
    x_in = x_ref[0]
    x = _by_segment(segs, lambda a, b, m: x_in[a:b] + m[2:3] * mix[a:b])
    xo_ref[0] = x
    y = _rms(x, g_ref[...])
    h = _by_segment(segs, lambda a, b, m: y[a:b] * (1.0 + m[4:5]) + m[3:4])
    dl = h_hbm.shape[-1]
    tile = pl.program_id(1)
    step = pl.program_id(0) * pl.num_programs(1) + tile
    slot = lax.rem(step, 2)

    def h_wait(sl_):
        pltpu.make_async_copy(hbuf.at[sl_], hbuf.at[sl_], hsem.at[sl_]).wait()

    @pl.when(step >= 2)
    def _():
        h_wait(slot)

    hbuf[slot] = h
    for sl in range(SUBLANES):
        pltpu.make_async_copy(hbuf.at[slot, :, pl.ds(dl * sl, dl)],
                              h_hbm.at[pl.program_id(0), pl.ds(tile * ROW_TILE, ROW_TILE), sl, :],
                              hsem.at[slot]).start()

    @pl.when(step == pl.num_programs(0) * pl.num_programs(1) - 1)
    def _():
        h_wait(1 - slot)
        h_wait(slot)

    logits = _dot3(wr_ref[...], h, nt=True) + br_ref[...]
    n_e, tm = logits.shape
    e_iota = lax.broadcasted_iota(jnp.int32, (n_e, tm), 0)
    vals, idxs = [], []
    for _ in range(TOP_K):
        mx = jnp.max(logits, axis=0, keepdims=True)
        ix = jnp.min(jnp.where(logits == mx, e_iota, n_e), axis=0, keepdims=True)
        vals.append(mx)
        idxs.append(ix)
        logits = jnp.where(e_iota == ix, -jnp.inf, logits)
    ex = [jnp.exp(v - vals[0]) for v in vals]
    den = ex[0] + ex[1] + ex[2] + ex[3]
    topi_ref[0] = jnp.concatenate(idxs, axis=0)
    topw_ref[0] = jnp.concatenate([e / den for e in ex], axis=0)

    @pl.when((pl.program_id(0) == 0) & (pl.program_id(1) == 0))
    def _():
        carry[...] = jnp.zeros_like(carry)

    sel = [ix == e_iota for ix in idxs]
    onehot = jnp.where(sel[0] | sel[1] | sel[2] | sel[3], 1.0, 0.0)
    before = (lax.broadcasted_iota(jnp.int32, (tm, tm), 0) < lax.broadcasted_iota(jnp.int32, (tm, tm), 1))
    pfx = _dot(onehot.astype(MXU_DTYPE), jnp.where(before, 1.0, 0.0).astype(MXU_DTYPE)) + carry[:, 0:1]
    rank_ref[0] = jnp.concatenate(
        [jnp.sum(jnp.where(sel[k], pfx, 0.0), axis=0, keepdims=True) for k in range(TOP_K)], axis=0)
    carry[...] = carry[...] + jnp.sum(onehot, axis=1, keepdims=True)
    cnt_ref[...] = carry[...]


def _outproj_call(xs, ya, ys, su, yc, mod_l, wts, ctx_len):
    bsz, s, d = xs.shape
    tm = ROW_TILE
    row = lambda b, j: (b, j, 0)
    const = lambda b, j: (0, 0)
    full = lambda a: pl.BlockSpec(a.shape, const)
    return pl.pallas_call(
        functools.partial(_outproj_kernel, ctx_len=ctx_len),
        grid=(bsz, s // tm),
        in_specs=[pl.BlockSpec((1, tm, d), row),
                  pl.BlockSpec((1, tm, NA_W), row),
                  pl.BlockSpec(memory_space=pl.ANY),
                  pl.BlockSpec((1, tm, S5_CH), row),
                  pl.BlockSpec((1, tm, SW_W), row),
                  pl.BlockSpec((1, 6, d), lambda b, j: (b, 0, 0)),
                  pl.BlockSpec((1, 6, d), lambda b, j: (bsz, 0, 0))] + [full(a) for a in wts],
        out_specs=[pl.BlockSpec((1, tm, d), row),
                   pl.BlockSpec(memory_space=pl.ANY),
                   pl.BlockSpec((1, TOP_K, tm), lambda b, j: (b, 0, j)),
                   pl.BlockSpec((1, TOP_K, tm), lambda b, j: (b, 0, j)),
                   pl.BlockSpec((1, TOP_K, tm), lambda b, j: (b, 0, j)),
                   pl.BlockSpec((N_EXPERTS, LANES), lambda b, j: (0, 0))],
        out_shape=[jax.ShapeDtypeStruct((bsz, s, d), F32),
                   jax.ShapeDtypeStruct((bsz, s, SUBLANES, d // SUBLANES), F32),
                   jax.ShapeDtypeStruct((bsz, TOP_K, s), jnp.int32),
                   jax.ShapeDtypeStruct((bsz, TOP_K, s), F32),
                   jax.ShapeDtypeStruct((bsz, TOP_K, s), F32),
                   jax.ShapeDtypeStruct((N_EXPERTS, LANES), F32)],
        scratch_shapes=[pltpu.VMEM((N_EXPERTS, LANES), F32), pltpu.VMEM((2, tm, d), F32),
                        pltpu.SemaphoreType.DMA((2,)),
                        pltpu.VMEM((2, tm // S5_CHUNK, S5_CHUNK, S5_CH), F32), pltpu.SemaphoreType.DMA((2,))],
        compiler_params=_params("arbitrary", "arbitrary"),
        name="outproj",
    )(xs, ya, ys, su, yc, mod_l, mod_l, *wts)


def _route_kernel(topi_ref, rank_ref, cnt_ref, dest_ref, blk_ref, ends_ref):
    idx = topi_ref[0]
    tm = idx.shape[1]
    e_iota = lax.broadcasted_iota(jnp.int32, (N_EXPERTS, tm), 0)
    counts = cnt_ref[:, 0:1]
    padded = jnp.ceil(counts * (1.0 / MOE_BLK)) * MOE_BLK
    r_i = lax.broadcasted_iota(jnp.int32, (N_EXPERTS, N_EXPERTS), 0)
    c_i = lax.broadcasted_iota(jnp.int32, (N_EXPERTS, N_EXPERTS), 1)
    padded_row = jnp.sum(jnp.where(r_i == c_i, padded, 0.0), axis=0, keepdims=True)
    pstart = jnp.sum(jnp.where(c_i < r_i, padded_row, 0.0), axis=1, keepdims=True)
    ends = pstart + padded
    rk = rank_ref[0]
    dest = [(jnp.sum(jnp.where(idx[k:k + 1] == e_iota, pstart, 0.0), axis=0, keepdims=True) + rk[k:k + 1]
             ).astype(jnp.int32) for k in range(TOP_K)]
    for u in range(tm // MOE_TILE):
        dest_ref[u] = jnp.concatenate([d[:, u * MOE_TILE:(u + 1) * MOE_TILE] for d in dest], axis=1)
    nb = blk_ref.shape[1]
    blk_start = (lax.broadcasted_iota(jnp.int32, (N_EXPERTS, nb), 1) * MOE_BLK).astype(F32)
    owner = jnp.sum(jnp.where(ends <= blk_start, 1.0, 0.0), axis=0, keepdims=True)
    blk_ref[...] = jnp.minimum(owner, N_EXPERTS - 1.0).astype(jnp.int32)
    ends_ref[...] = jnp.broadcast_to(ends, ends_ref.shape).astype(jnp.int32)


def _route_call(topi, rank, cnt, n_blocks):
    bsz, _, s = topi.shape
    tm = ROW_TILE
    sub = tm // MOE_TILE
    n_tiles = s // tm
    nb_pad = -(-n_blocks // LANES) * LANES
    tok = pl.BlockSpec((1, TOP_K, tm), lambda b, j: (b, 0, j))
    return pl.pallas_call(
        _route_kernel,
        grid=(bsz, n_tiles),
        in_specs=[tok, tok, pl.BlockSpec((N_EXPERTS, LANES), lambda b, j: (0, 0))],
        out_specs=[pl.BlockSpec((sub, 1, TOP_K * MOE_TILE), lambda b, j: (b * n_tiles + j, 0, 0)),
                   pl.BlockSpec((1, nb_pad), lambda b, j: (0, 0)),
                   pl.BlockSpec((N_EXPERTS, LANES), lambda b, j: (0, 0))],
        out_shape=[jax.ShapeDtypeStruct((bsz * n_tiles * sub, 1, TOP_K * MOE_TILE), jnp.int32),
                   jax.ShapeDtypeStruct((1, nb_pad), jnp.int32),
                   jax.ShapeDtypeStruct((N_EXPERTS, LANES), jnp.int32)],
        compiler_params=_params("arbitrary", "arbitrary"),
        name="route_dest",
    )(topi, rank, cnt)


def _dispatch_kernel(ends_ref, dest_ref, h_ref, xg_ref, zbuf, stage, sem, zsem, *, n_blocks):
    tm = h_ref.shape[1]

    @pl.when((pl.program_id(0) == 0) & (pl.program_id(1) == 0))
    def _():
        zbuf[...] = jnp.zeros_like(zbuf)

        def fill(row):
            return pltpu.make_async_copy(zbuf, xg_ref.at[pl.ds(pl.multiple_of(row, MOE_BLK), MOE_BLK)], zsem)

        def each(fn):
            for e in range(N_EXPERTS):
                begin = ends_ref[e - 1] if e else 0

                @pl.when(ends_ref[e] > begin)
                def _():
                    fn(fill(ends_ref[e] - MOE_BLK))

            def dead(i, c):
                fn(fill(i * MOE_BLK))
                return c

            lax.fori_loop(ends_ref[N_EXPERTS - 1] // MOE_BLK, n_blocks, dead, 0)

        each(lambda cp: cp.start())
        each(lambda cp: cp.wait())

    step = pl.program_id(0) * pl.num_programs(1) + pl.program_id(1)
    n_steps = pl.num_programs(0) * pl.num_programs(1)
    slot = lax.rem(step, 2)
    stage[slot] = h_ref[0]

    def body(t, c):
        for k in range(TOP_K):
            pltpu.make_async_copy(stage.at[slot, t], xg_ref.at[dest_ref[0, 0, k * tm + t]],
                                  sem.at[slot]).start(priority=k % 2)
        return c

    lax.fori_loop(0, tm, body, 0, unroll=8)

    def wait_tile(sl):
        pltpu.make_async_copy(xg_ref.at[pl.ds(0, TOP_K * tm)], xg_ref.at[pl.ds(0, TOP_K * tm)], sem.at[sl]).wait()

    @pl.when(step > 0)
    def _():
        wait_tile(1 - slot)

    @pl.when(step == n_steps - 1)
    def _():
        wait_tile(slot)


def _dispatch_call(ends, dest, h, n_blocks):
    bsz, s, _, dl = h.shape
    tm = MOE_TILE
    n_tiles = s // tm
    grid_spec = pltpu.PrefetchScalarGridSpec(
        num_scalar_prefetch=1,
        grid=(bsz, n_tiles),
        in_specs=[pl.BlockSpec((1, 1, TOP_K * tm), lambda b, j, en: (b * n_tiles + j, 0, 0), memory_space=pltpu.SMEM),
                  pl.BlockSpec((1, tm, SUBLANES, dl), lambda b, j, en: (b, j, 0, 0))],
        out_specs=pl.BlockSpec(memory_space=pl.ANY),
        scratch_shapes=[pltpu.VMEM((MOE_BLK, SUBLANES, dl), h.dtype), pltpu.VMEM((2, tm, SUBLANES, dl), h.dtype),
                        pltpu.SemaphoreType.DMA((2,)),
                        pltpu.SemaphoreType.DMA(())],
    )
    return pl.pallas_call(
        functools.partial(_dispatch_kernel, n_blocks=n_blocks),
        grid_spec=grid_spec,
        out_shape=jax.ShapeDtypeStruct((n_blocks * MOE_BLK, SUBLANES, dl), h.dtype),
        compiler_params=_params("arbitrary", "arbitrary"),
        name="moe_dispatch",
    )(ends, dest, h)


def _moe_kernel(blk_exp_ref, nact_ref, x_hbm, wgu_ref, bgu_ref, wd_ref, bd_ref, y_hbm, wgu_c, wd_c,
                xbuf, xsem, ybuf, ysem):
    i = pl.program_id(0)
    slot = lax.rem(i, 2)
    dl = x_hbm.shape[-1]

    def fetch(blk, sl_):
        for s in range(SUBLANES):
            pltpu.make_async_copy(x_hbm.at[pl.ds(blk * MOE_BLK, MOE_BLK), s, :],
                                  xbuf.at[sl_, :, pl.ds(dl * s, dl)], xsem.at[sl_]).start()

    @pl.when(i == 0)
    def _():
        fetch(0, 0)

    @pl.when(i + 1 < pl.num_programs(0))
    def _():
        fetch(i + 1, 1 - slot)

    def y_wait(sl_):
        pltpu.make_async_copy(ybuf.at[sl_], ybuf.at[sl_], ysem.at[sl_]).wait()

    @pl.when(i >= 2)
    def _():
        y_wait(slot)

    e = blk_exp_ref[i]
    prev = blk_exp_ref[jnp.maximum(i - 1, 0)]
    d, f2 = wgu_c.shape
    f = f2 // 2
    rows = 128

    @pl.when((i == 0) | (e != prev))
    def _():
        def cv(r, c):
            sl = pl.ds(pl.multiple_of(r * rows, rows), rows)
            wgu_c[sl, :] = wgu_ref[0, 0, sl, :].astype(wgu_c.dtype)
            return c
        lax.fori_loop(0, d // rows, cv, 0)

        def cv2(r, c):
            sl = pl.ds(pl.multiple_of(r * rows, rows), rows)
            wd_c[sl, :] = wd_ref[0, 0, sl, :].astype(wd_c.dtype)
            return c
        lax.fori_loop(0, f // rows, cv2, 0)

    pltpu.make_async_copy(xbuf.at[slot], xbuf.at[slot], xsem.at[slot]).wait()

    @pl.when(i < nact_ref[0])
    def _():
        gu = _dot(xbuf[slot].astype(MXU_DTYPE), wgu_c[...]) + bgu_ref[0, 0]
        gate = jnp.minimum(gu[:, :f], SWIGLU_LIMIT)
        up = jnp.clip(gu[:, f:], -SWIGLU_LIMIT, SWIGLU_LIMIT)
        act = gate * jax.nn.sigmoid(SWIGLU_ALPHA * gate) * (up + 1.0)
        ybuf[slot] = _dot(act.astype(MXU_DTYPE), wd_c[...]) + bd_ref[0, 0]

    @pl.when(i >= nact_ref[0])
    def _():
        ybuf[slot] = jnp.zeros(ybuf.shape[1:], ybuf.dtype)

    for s in range(SUBLANES):
        pltpu.make_async_copy(ybuf.at[slot, :, pl.ds(dl * s, dl)],
                              y_hbm.at[pl.ds(i * MOE_BLK, MOE_BLK), s, :], ysem.at[slot]).start()

    @pl.when(i == pl.num_programs(0) - 1)
    def _():
        y_wait(1 - slot)
        y_wait(slot)


def _moe_call(layer, blk_exp, n_active, xg, w_gate_up, b_gate_up, w_down, b_down):
    n_rows, _, dl = xg.shape
    depth, n_e, d, f2 = w_gate_up.shape
    f = f2 // 2
    n_blocks = n_rows // MOE_BLK
    assert n_blocks >= 2
    wsel = lambda i, be, na: (layer, be[i], 0, 0)
    grid_spec = pltpu.PrefetchScalarGridSpec(
        num_scalar_prefetch=2,
        grid=(n_blocks,),
        in_specs=[pl.BlockSpec(memory_space=pl.ANY),
                  pl.BlockSpec((1, 1, d, f2), wsel),
                  pl.BlockSpec((1, 1, 1, f2), wsel),
                  pl.BlockSpec((1, 1, f, d), wsel),
                  pl.BlockSpec((1, 1, 1, d), wsel)],
        out_specs=pl.BlockSpec(memory_space=pl.ANY),
        scratch_shapes=[pltpu.VMEM((d, f2), MXU_DTYPE), pltpu.VMEM((f, d), MXU_DTYPE),
                        pltpu.VMEM((2, MOE_BLK, d), F32), pltpu.SemaphoreType.DMA((2,)),
                        pltpu.VMEM((2, MOE_BLK, d), F32), pltpu.SemaphoreType.DMA((2,))],
    )
    return pl.pallas_call(
        _moe_kernel,
        grid_spec=grid_spec,
        out_shape=jax.ShapeDtypeStruct((n_rows, SUBLANES, dl), F32),
        compiler_params=_params("arbitrary"),
        name="moe_experts",
    )(blk_exp, n_active, xg, w_gate_up, b_gate_up.reshape(depth, n_e, 1, f2),
      w_down, b_down.reshape(depth, n_e, 1, d))


def _combine_kernel(dest_ref, dnext_ref, w_ref, gate_ref, x_hbm, yg_hbm, xo_hbm,
                    gbuf, xbuf, obuf, gsem, xsem, osem, *, n_tiles):
    tm = MOE_TILE
    dl = gbuf.shape[-1]
    step = pl.program_id(0) * n_tiles + pl.program_id(1)
    n_steps = pl.num_programs(0) * n_tiles
    slot = lax.rem(step, 2)

    def stream_copies(st, sl_, buf, hbm, sem, to_hbm):
        b = lax.div(st, n_tiles)
        r0 = lax.rem(st, n_tiles) * tm
        out = []
        for s in range(SUBLANES):
            rows = hbm.at[b, pl.ds(r0, tm), pl.ds(dl * s, dl)]
            tiles = buf.at[sl_, :, s, :]
            out.append(pltpu.make_async_copy(tiles, rows, sem.at[sl_]) if to_hbm
                       else pltpu.make_async_copy(rows, tiles, sem.at[sl_]))
        return out

    def fetch(dref, st, sl_):
        for cp in stream_copies(st, sl_, xbuf, x_hbm, xsem, False):
            cp.start()

        def body(t, c):
            for k in range(TOP_K):
                pltpu.make_async_copy(yg_hbm.at[dref[0, 0, k * tm + t]], gbuf.at[sl_, k, t],
                                      gsem.at[sl_]).start(priority=k % 2)
            return c

        lax.fori_loop(0, tm, body, 0, unroll=8)

    @pl.when(step == 0)
    def _():
        fetch(dest_ref, 0, 0)

    @pl.when(step + 1 < n_steps)
    def _():
        fetch(dnext_ref, step + 1, 1 - slot)

    def wait_all(buf, sem, sl_):
        pltpu.make_async_copy(buf.at[sl_], buf.at[sl_], sem.at[sl_]).wait()

    wait_all(gbuf, gsem, slot)
    wait_all(xbuf, xsem, slot)

    @pl.when(step >= 2)
    def _():
        wait_all(obuf, osem, slot)

    gate = gate_ref[0]

    def row(t, c):
        acc = gbuf[slot, 0, t] * w_ref[0, 0, t]
        for k in range(1, TOP_K):
            acc = acc + gbuf[slot, k, t] * w_ref[0, 0, k * tm + t]
        obuf[slot, t] = xbuf[slot, t] + gate * acc
        return c

    lax.fori_loop(0, tm, row, 0, unroll=8)
    for cp in stream_copies(step, slot, obuf, xo_hbm, osem, True):
        cp.start()

    @pl.when(step == n_steps - 1)
    def _():
        wait_all(obuf, osem, 1 - slot)
        wait_all(obuf, osem, slot)


def _combine_call(dest, topw, xs, mod_l, yg, ctx_len):
    bsz, s, d = xs.shape
    tm = MOE_TILE
    n_tiles = s // tm
    last = bsz * n_tiles - 1
    assert last >= 1
    dl = d // SUBLANES
    w_flat = topw.reshape(bsz, TOP_K, n_tiles, tm).transpose(0, 2, 1, 3).reshape(bsz * n_tiles, 1, TOP_K * tm)
    gate = mod_l[:, 5].reshape(MOD_ROWS, SUBLANES, dl)
    idx_spec = lambda ahead: pl.BlockSpec(
        (1, 1, TOP_K * tm), lambda b, j: (jnp.minimum(b * n_tiles + j + ahead, last), 0, 0), memory_space=pltpu.SMEM)
    tile_buf = lambda lead: pltpu.VMEM(lead + (tm, SUBLANES, dl), F32)
    return pl.pallas_call(
        functools.partial(_combine_kernel, n_tiles=n_tiles),
        grid=(bsz, n_tiles),
        in_specs=[idx_spec(0), idx_spec(1), idx_spec(0),
                  pl.BlockSpec((1, SUBLANES, dl), lambda b, j: (jnp.where(j < ctx_len // tm, bsz, b), 0, 0)),
                  pl.BlockSpec(memory_space=pl.ANY),
                  pl.BlockSpec(memory_space=pl.ANY)],
        out_specs=pl.BlockSpec(memory_space=pl.ANY),
        out_shape=jax.ShapeDtypeStruct((bsz, s, d), F32),
        scratch_shapes=[tile_buf((2, TOP_K)), tile_buf((2,)), tile_buf((2,)),
                        pltpu.SemaphoreType.DMA((2,)), pltpu.SemaphoreType.DMA((2,)), pltpu.SemaphoreType.DMA((2,))],
        compiler_params=_params("arbitrary", "arbitrary"),
        name="moe_combine",
    )(dest, dest, w_flat, gate, xs, yg)


def _final_kernel(x_ref, g_ref, o_ref):
    x = x_ref[0]
    o_ref[0] = x * lax.rsqrt(jnp.mean(x * x, axis=-1, keepdims=True) + RMS_EPS) * g_ref[...]


def _final_call(xs, g, ctx_len):
    bsz, s, d = xs.shape
    tm = MOE_TILE
    off = ctx_len // tm
    return pl.pallas_call(
        _final_kernel,
        grid=(bsz, (s - ctx_len) // tm),
        in_specs=[pl.BlockSpec((1, tm, d), lambda b, j: (b, j + off, 0)),
                  pl.BlockSpec((1, d), lambda b, j: (0, 0))],
        out_specs=pl.BlockSpec((1, tm, d), lambda b, j: (b, j, 0)),
        out_shape=jax.ShapeDtypeStruct((bsz, s - ctx_len, d), F32),
        compiler_params=_params("parallel", "parallel"),
        name="final_norm",
    )(xs, g)


def _rope_tables(seq, ctx_len):
    t = jnp.arange(seq)
    row = (t // GRID_W).astype(F32)
    col = (t % GRID_W).astype(F32)
    nf = HEAD_DIM // 4
    inv = ROPE_BASE ** (-jnp.arange(nf, dtype=F32) / nf)
    ar = row[:, None] * inv
    ac = col[:, None] * inv
    ang = jnp.concatenate([ar, ar, ac, ac], axis=-1)
    cos = jnp.concatenate([jnp.ones((ctx_len, HEAD_DIM), F32), jnp.cos(ang)], axis=0)
    sin = jnp.concatenate([jnp.zeros((ctx_len, HEAD_DIM), F32), jnp.sin(ang)], axis=0)
    reps = LANES // HEAD_DIM
    return jnp.tile(cos, (1, reps)), jnp.tile(sin, (1, reps))


def _rot_cols(w):
    q = HEAD_DIM // 4
    j = np.arange(HEAD_DIM)
    first = (j % (2 * q)) < q
    src = np.where(first, j + q, j - q)
    sign = np.where(first, -1.0, 1.0).astype(np.float32)
    n_heads = w.shape[1] // HEAD_DIM
    src_all = (np.arange(n_heads)[:, None] * HEAD_DIM + src[None, :]).reshape(-1)
    return w[:, src_all] * jnp.asarray(np.tile(sign, n_heads))


def _head_perm_cols(order):
    return (np.asarray(order)[:, None] * HEAD_DIM + np.arange(HEAD_DIM)[None, :]).reshape(-1)


def _inproj_weight(w_in_l):
    aq, ak, av, su, sq, sk, sv = jnp.split(
        w_in_l, np.cumsum([NA_W, NA_W, NA_W, S5_CH, SW_W, SW_KV_W])[:6].tolist(), axis=1)
    sq = sq[:, _head_perm_cols(SW_HEAD_ORDER)]
    return jnp.concatenate([aq, ak, av, sq, _rot_cols(sq), sk, _rot_cols(sk), sv, su], axis=1).astype(MXU_DTYPE)


def _swap_kernel(v_ref, o_ref, *, inner, h):
    b = pl.program_id(1)
    kdim, ndim = v_ref.shape[1], o_ref.shape[1]
    n = lax.broadcasted_iota(jnp.int32, (1, ndim), 1)
    shift = h.bit_length() - 1
    src = lax.shift_right_logical(n, shift) * inner + b * h + (n & (h - 1))
    k = lax.broadcasted_iota(jnp.int32, (kdim, ndim), 0)
    perm = jnp.where(k == src, 1.0, 0.0).astype(MXU_DTYPE)
    o_ref[...] = _dot(v_ref[...].astype(MXU_DTYPE), perm).astype(o_ref.dtype)


def _swap_call(v, rows, n_a, n_b, h, out_dtype):
    r, width = v.shape
    assert width == n_a * n_b * h and h & (h - 1) == 0 and r % rows == 0
    return pl.pallas_call(
        functools.partial(_swap_kernel, inner=n_b * h, h=h),
        grid=(r // rows, n_b),
        in_specs=[pl.BlockSpec((rows, width), lambda i, b: (i, 0))],
        out_specs=pl.BlockSpec((rows, n_a * h), lambda i, b: (i, b)),
        out_shape=jax.ShapeDtypeStruct((r, width), out_dtype),
        compiler_params=_params("parallel", "arbitrary"),
        name="s5_swap",
    )(v)


def _chunk_major(su_rows, bsz):
    nc = su_rows.shape[0] // bsz
    u = _swap_call(su_rows, nc, S5_CHUNK, S5_GROUPS, S5_GROUP_CH, MXU_DTYPE)
    return u.reshape(bsz, nc, S5_CHUNK * S5_CH)


def _token_rows(y_t):
    bsz, nc, width = y_t.shape
    return _swap_call(y_t.reshape(bsz * nc, width), nc, S5_GROUPS, S5_CHUNK, S5_GROUP_CH, F32)


def kernel(x, c, ctx, c_ctx, w_mod, b_mod, g_mix, w_in, w_out, na_rpb, s5_a_re, s5_a_im, s5_log_step,
           s5_b_re, s5_b_im, s5_c_re, s5_c_im, s5_d, s5_w_glu, s5_b_glu, sw_sinks, g_ffn, w_router, b_router,
           w_gate_up, b_gate_up, w_down, b_down, g_final):
    bsz, seq, d = x.shape
    ctx_len = ctx.shape[1]
    depth = w_mod.shape[0]
    s = ctx_len + seq
    assert bsz + 1 <= MOD_ROWS and s % ROW_TILE == 0 and ctx_len % MOE_TILE == 0 and seq % MOE_TILE == 0
    assert ROW_TILE % MOE_TILE == 0 and ROW_TILE % S5_CHUNK == 0 and bsz * (s // ROW_TILE) >= 2
    assert seq % GRID_W == 0 and ctx_len % S5_CHUNK == 0

    xs = jnp.concatenate([ctx, x], axis=1)
    cond = jnp.zeros((MOD_ROWS, d), F32).at[:bsz].set(c).at[bsz].set(c_ctx)
    mod = _mod_call(cond, w_mod, b_mod).reshape(depth, MOD_ROWS, 6, d)
    cos2, sin2 = _rope_tables(seq, ctx_len)

    n_assign = bsz * s * TOP_K
    n_blocks = -(-(n_assign + N_EXPERTS * (MOE_BLK - 1)) // MOE_BLK)
    sw_rows = _head_perm_cols(SW_HEAD_ORDER)

    w_cat = jax.vmap(_inproj_weight)(w_in)
    bias_tab = jax.vmap(_na_bias_table)(na_rpb)
    s5w = jax.vmap(_s5_weights)(s5_a_re, s5_a_im, s5_log_step, s5_b_re, s5_b_im, s5_c_re, s5_c_im)
    wo_a = w_out[:, :NA_W].astype(MXU_DTYPE)
    wo_b = w_out[:, NA_W:NA_W + S5_CH].astype(MXU_DTYPE)
    wo_c = w_out[:, NA_W + S5_CH:][:, sw_rows].astype(MXU_DTYPE)
    w_glu = s5_w_glu.astype(MXU_DTYPE)
    w_rt = jnp.swapaxes(w_router, 1, 2).astype(F32)

    for l in range(depth):
        mod_l = mod[l]
        naq, nak, nav, swq, swk, swv, su, su_rows = _inproj_call(
            xs, mod_l, g_mix[l].reshape(1, d), w_cat[l], cos2, sin2, ctx_len)
        ya = _na_call(naq, nak, nav, bias_tab[l], ctx_len)
        yc = _sw_call(sw_sinks[l], swq, swk, swv, ctx_len)
        ys = _token_rows(_s5_call(_chunk_major(su_rows, bsz), tuple(w[l] for w in s5w), ctx_len // S5_CHUNK))

        out_wts = (wo_a[l], wo_b[l], wo_c[l], w_glu[l], s5_b_glu[l].reshape(1, S5_CH).astype(F32),
                   s5_d[l].reshape(1, S5_CH).astype(F32), g_ffn[l].reshape(1, d),
                   w_rt[l], b_router[l].reshape(N_EXPERTS, 1).astype(F32))
        xs, h, topi, topw, rank, cnt = _outproj_call(xs, ya, ys, su, yc, mod_l, out_wts, ctx_len)

        dest, blk, ends = _route_call(topi, rank, cnt, n_blocks)
        ends = ends[:, 0]
        xg = _dispatch_call(ends, dest, h, n_blocks)
        yg = _moe_call(l, blk[0, :n_blocks], ends[-1:] // MOE_BLK, xg, w_gate_up, b_gate_up, w_down, b_down)
        xs = _combine_call(dest, topw, xs, mod_l, yg, ctx_len)

    return _final_call(xs, g_final.reshape(1, d), ctx_len)
```

```python
import functools
import math

import numpy as np
import jax
import jax.numpy as jnp
from jax import lax
from jax.experimental import pallas as pl
from jax.experimental.pallas import tpu as pltpu

F32 = jnp.float32
MXU_DTYPE = jnp.bfloat16

GRID_W = 64
HEAD_DIM = 64
NA_HEADS = 6
NA_W = NA_HEADS * HEAD_DIM
NA_ROWS = 8
NA_COLS = 16
S5_GROUP_CH = 16
S5_CH = 256
S5_GROUPS = S5_CH // S5_GROUP_CH
S5_STATE = 64
S5_EIG_MAX = -1e-4
SW_HEADS = 6
SW_KV_HEADS = 2
SW_GRP = SW_HEADS // SW_KV_HEADS
SW_W = SW_HEADS * HEAD_DIM
SW_KV_W = SW_KV_HEADS * HEAD_DIM
SW_WINDOW = 128
SW_BLK = 128
ROPE_BASE = 10000.0
N_EXPERTS = 32
TOP_K = 4
MOE_BLK = 512
SWIGLU_LIMIT = 7.0
SWIGLU_ALPHA = 1.702
RMS_EPS = 1e-6
NEG_INF = -1e30

LANES = 128
ROW_TILE = 768
MOE_TILE = 256
S5_CHUNK = 16
NA_QROWS = 4
NA_KROWS = NA_QROWS + NA_ROWS
SW_QBLK = 2 * SW_BLK
SW_KBLK = SW_QBLK + 2 * SW_WINDOW
MOD_ROWS = 8
VMEM_LIMIT = 56 << 20
SUBLANES = 8

SW_HEAD_ORDER = tuple(g * SW_GRP + t for t in range(SW_GRP) for g in range(SW_KV_HEADS))


def _params(*sem):
    return pltpu.CompilerParams(dimension_semantics=sem, vmem_limit_bytes=VMEM_LIMIT)


def _dot(a, b):
    return jnp.dot(a, b, preferred_element_type=F32)


def _dot_nt(a, b):
    return lax.dot_general(a, b, (((1,), (1,)), ((), ())), preferred_element_type=F32)


def _split(a):
    hi = a.astype(MXU_DTYPE)
    lo = (a - hi.astype(F32)).astype(MXU_DTYPE)
    return hi, lo


def _dot3(a, b, nt=False):
    f = _dot_nt if nt else _dot
    ah, al = _split(a)
    bh, bl = _split(b)
    return f(ah, bh) + (f(ah, bl) + f(al, bh))


def _mod_kernel(cond_ref, w_ref, b_ref, o_ref):
    c = cond_ref[...]
    a = c * jax.nn.sigmoid(c)
    o_ref[0] = _dot3(a, w_ref[0]) + b_ref[0]


def _mod_call(cond, w_mod, b_mod):
    depth, d, n = w_mod.shape
    tn = n // 6
    return pl.pallas_call(
        _mod_kernel,
        grid=(depth, n // tn),
        in_specs=[pl.BlockSpec((MOD_ROWS, d), lambda l, j: (0, 0)),
                  pl.BlockSpec((1, d, tn), lambda l, j: (l, 0, j)),
                  pl.BlockSpec((1, 1, tn), lambda l, j: (l, 0, j))],
        out_specs=pl.BlockSpec((1, MOD_ROWS, tn), lambda l, j: (l, 0, j)),
        out_shape=jax.ShapeDtypeStruct((depth, MOD_ROWS, n), F32),
        compiler_params=_params("parallel", "parallel"),
        name="mod",
    )(cond, w_mod, b_mod.reshape(depth, 1, n))


C_AQ = 0
C_AK = C_AQ + NA_W
C_AV = C_AK + NA_W
C_SQ = C_AV + NA_W
C_SQR = C_SQ + SW_W
C_SK = C_SQR + SW_W
C_SKR = C_SK + SW_KV_W
C_SV = C_SKR + SW_KV_W
C_SU = C_SV + SW_KV_W
C_END = C_SU + S5_CH


def _rms(x, g):
    return x * lax.rsqrt(jnp.mean(x * x, axis=-1, keepdims=True) + RMS_EPS) * g


def _mod_segments(modb_ref, modc_ref, ctx_len):
    r = ctx_len % ROW_TILE
    out = []
    for a, b in ([(0, ROW_TILE)] if r == 0 else [(0, r), (r, ROW_TILE)]):
        is_ctx = pl.program_id(1) * ROW_TILE + a < ctx_len
        out.append((a, b, jnp.where(is_ctx, modc_ref[0], modb_ref[0])))
    return out


def _by_segment(segs, fn):
    return jnp.concatenate([fn(a, b, m) for a, b, m in segs], axis=0)


def _chunk_row_copies(buf, slot, hbm, sem, to_hbm):
    chunks, _, width = buf.shape[1:]
    c0 = (pl.program_id(0) * pl.num_programs(1) + pl.program_id(1)) * chunks
    out = []
    for i in range(S5_CHUNK):
        tile = buf.at[slot, :, i, :]
        rows = hbm.at[pl.ds(c0, chunks), pl.ds(width * i, width)]
        out.append(pltpu.make_async_copy(tile, rows, sem.at[slot]) if to_hbm
                   else pltpu.make_async_copy(rows, tile, sem.at[slot]))
    return out


def _inproj_kernel(x_ref, modb_ref, modc_ref, g_ref, w_ref, cos_ref, sin_ref,
                   naq_ref, nak_ref, nav_ref, swq_ref, swk_ref, swv_ref, su_ref, sur_hbm, sbuf, ssem, *, ctx_len):
    segs = _mod_segments(modb_ref, modc_ref, ctx_len)
    y = _rms(x_ref[0], g_ref[...])
    h = _by_segment(segs, lambda a, b, m: y[a:b] * (1.0 + m[1:2]) + m[0:1]).astype(MXU_DTYPE)
    p = _dot(h, w_ref[...])
    cos = cos_ref[...]
    sin = sin_ref[...]
    cos3 = jnp.concatenate([cos] * (SW_W // LANES), axis=1)
    sin3 = jnp.concatenate([sin] * (SW_W // LANES), axis=1)
    qk_scale = HEAD_DIM ** -0.5
    naq_ref[0] = (p[:, C_AQ:C_AK] * qk_scale).astype(naq_ref.dtype)
    nak_ref[0] = p[:, C_AK:C_AV].astype(nak_ref.dtype)
    nav_ref[0] = p[:, C_AV:C_SQ].astype(nav_ref.dtype)
    swq_ref[0] = ((p[:, C_SQ:C_SQR] * cos3 + p[:, C_SQR:C_SK] * sin3) * qk_scale).astype(swq_ref.dtype)
    swk_ref[0] = (p[:, C_SK:C_SKR] * cos + p[:, C_SKR:C_SV] * sin).astype(swk_ref.dtype)
    swv_ref[0] = p[:, C_SV:C_SU].astype(swv_ref.dtype)
    su = p[:, C_SU:C_END]
    su_ref[0] = su
    step = pl.program_id(0) * pl.num_programs(1) + pl.program_id(1)
    slot = lax.rem(step, 2)

    def s_wait(sl_):
        pltpu.make_async_copy(sbuf.at[sl_], sbuf.at[sl_], ssem.at[sl_]).wait()

    @pl.when(step >= 2)
    def _():
        s_wait(slot)

    sbuf[slot] = su.reshape(sbuf.shape[1:])
    for cp in _chunk_row_copies(sbuf, slot, sur_hbm, ssem, True):
        cp.start()

    @pl.when(step == pl.num_programs(0) * pl.num_programs(1) - 1)
    def _():
        s_wait(1 - slot)
        s_wait(slot)


def _inproj_call(xs, mod_l, g, w_cat, cos2, sin2, ctx_len):
    bsz, s, d = xs.shape
    tm = ROW_TILE
    row = lambda b, j: (b, j, 0)
    const = lambda b, j: (0, 0)
    widths = (NA_W, NA_W, NA_W, SW_W, SW_KV_W, SW_KV_W, S5_CH)
    dtypes = (MXU_DTYPE,) * 6 + (F32,)
    return pl.pallas_call(
        functools.partial(_inproj_kernel, ctx_len=ctx_len),
        grid=(bsz, s // tm),
        in_specs=[pl.BlockSpec((1, tm, d), row),
                  pl.BlockSpec((1, 6, d), lambda b, j: (b, 0, 0)),
                  pl.BlockSpec((1, 6, d), lambda b, j: (bsz, 0, 0)),
                  pl.BlockSpec((1, d), const),
                  pl.BlockSpec((d, C_END), const),
                  pl.BlockSpec((tm, LANES), lambda b, j: (j, 0)),
                  pl.BlockSpec((tm, LANES), lambda b, j: (j, 0))],
        out_specs=[pl.BlockSpec((1, tm, w), row) for w in widths] + [pl.BlockSpec(memory_space=pl.ANY)],
        out_shape=[jax.ShapeDtypeStruct((bsz, s, w), t) for w, t in zip(widths, dtypes)]
        + [jax.ShapeDtypeStruct((bsz * s // S5_CHUNK, S5_CHUNK * S5_CH), F32)],
        scratch_shapes=[pltpu.VMEM((2, tm // S5_CHUNK, S5_CHUNK, S5_CH), F32), pltpu.SemaphoreType.DMA((2,))],
        compiler_params=_params("arbitrary", "arbitrary"),
        name="inproj",
    )(xs, mod_l, mod_l, g, w_cat, cos2, sin2)


def _half_masks():
    lane = lax.broadcasted_iota(jnp.int32, (1, LANES), 1)
    return lane < HEAD_DIM, lane >= HEAD_DIM


def _na_kernel(q_ref, k_ref, v_ref, bias_ref, o_ref, *, ctx_len, rows):
    i = pl.program_id(1)
    tq = NA_QROWS * GRID_W
    n_ctx_q = ctx_len // tq
    masks = _half_masks()
    nk = NA_KROWS * GRID_W

    def run(local):
        q = q_ref[0]
        if local:
            r0 = (i - n_ctx_q) * NA_QROWS
            start0 = jnp.clip(r0 - NA_ROWS // 2, 0, rows - NA_KROWS)
            start = pl.multiple_of(ctx_len + start0 * GRID_W, GRID_W)
            tab_idx, row_mask = {}, {}
            for a in range(NA_QROWS):
                r = r0 + a
                s_r = jnp.clip(r - NA_ROWS // 2, 0, rows - NA_ROWS)
                for p in range(NA_KROWS // 2):
                    kr = start0 + 2 * p
                    ok_lo = (kr >= s_r) & (kr < s_r + NA_ROWS)
                    ok_hi = (kr + 1 >= s_r) & (kr + 1 < s_r + NA_ROWS)
                    tab_idx[a, p] = jnp.clip(kr - r + NA_ROWS, 0, 2 * NA_ROWS - 1)
                    row_mask[a, p] = jnp.where(masks[0], jnp.where(ok_lo, 0.0, NEG_INF),
                                               jnp.where(ok_hi, 0.0, NEG_INF))
        outs = []
        for t in range(NA_W // LANES):
            sl = slice(LANES * t, LANES * (t + 1))
            qt = q[:, sl]
            zero = jnp.zeros_like(qt)
            qm = jnp.concatenate([jnp.where(masks[0], qt, zero), jnp.where(masks[1], qt, zero)], axis=0)
            kc = k_ref[0, 0:ctx_len, sl]
            vc = v_ref[0, 0:ctx_len, sl]
            s_cx = _dot_nt(qm, kc)
            m = jnp.max(s_cx, axis=-1, keepdims=True)
            if local:
                kw = k_ref[0, pl.ds(start, nk), sl]
                vw = v_ref[0, pl.ds(start, nk), sl]
                bias = jnp.concatenate(
                    [jnp.concatenate([bias_ref[2 * t + hh, pl.ds(tab_idx[a, p], 1)][0] + row_mask[a, p]
                                      for p in range(NA_KROWS // 2)], axis=-1)
                     for hh in range(2) for a in range(NA_QROWS)], axis=0)
                s_nb = _dot_nt(qm, kw) + bias
                m = jnp.maximum(m, jnp.max(s_nb, axis=-1, keepdims=True))
                p_nb = jnp.exp(s_nb - m)
            p_cx = jnp.exp(s_cx - m)
            den = jnp.sum(p_cx, axis=-1, keepdims=True)
            o = _dot(p_cx.astype(MXU_DTYPE), vc)
            if local:
                den = den + jnp.sum(p_nb, axis=-1, keepdims=True)
                o = o + _dot(p_nb.astype(MXU_DTYPE), vw)
            o = o / den
            outs.append(jnp.where(masks[0], o[:tq], o[tq:]))
        o_ref[0] = jnp.concatenate(outs, axis=-1).astype(o_ref.dtype)

    @pl.when(i < n_ctx_q)
    def _():
        run(False)

    @pl.when(i >= n_ctx_q)
    def _():
        run(True)


def _na_call(q, k, v, bias_tab, ctx_len):
    bsz, s, w = q.shape
    rows = (s - ctx_len) // GRID_W
    tq = NA_QROWS * GRID_W
    assert rows >= NA_KROWS and rows % NA_QROWS == 0 and ctx_len % tq == 0
    whole = lambda b, i: (b, 0, 0)
    return pl.pallas_call(
        functools.partial(_na_kernel, ctx_len=ctx_len, rows=rows),
        grid=(bsz, s // tq),
        in_specs=[pl.BlockSpec((1, tq, w), lambda b, i: (b, i, 0)),
                  pl.BlockSpec((1, s, w), whole),
                  pl.BlockSpec((1, s, w), whole),
                  pl.BlockSpec(bias_tab.shape, lambda b, i: (0, 0, 0, 0))],
        out_specs=pl.BlockSpec((1, tq, w), lambda b, i: (b, i, 0)),
        out_shape=jax.ShapeDtypeStruct((bsz, s, w), q.dtype),
        compiler_params=_params("parallel", "arbitrary"),
        name="na_attn",
    )(q, k, v, bias_tab)


def _na_bias_table(rpb):
    qcol = np.arange(GRID_W)[:, None]
    kcol = np.arange(GRID_W)[None, :]
    ws = np.clip(qcol - NA_COLS // 2, 0, GRID_W - NA_COLS)
    valid = (kcol >= ws) & (kcol < ws + NA_COLS)
    dc = np.clip(kcol - qcol + NA_COLS - 1, 0, 2 * NA_COLS - 2)
    full = jnp.where(valid[None, None], rpb[:, :, dc].astype(F32), NEG_INF)
    edge = jnp.full_like(full[:, :1], NEG_INF)
    full = jnp.concatenate([edge, full, edge], axis=1)
    return jnp.concatenate([full[:, :-1], full[:, 1:]], axis=-1)


def _sw_kernel(sink_ref, q_ref, k_ref, v_ref, o_ref, *, ctx_len, seq):
    i = pl.program_id(1)
    tq = SW_QBLK
    n_ctx_q = ctx_len // tq
    masks = _half_masks()
    nk = SW_KBLK
    first_head = lax.broadcasted_iota(jnp.int32, (2 * tq, 1), 0) < tq

    def run(local):
        q = q_ref[0]
        kc = k_ref[0, 0:ctx_len, :]
        vc = v_ref[0, 0:ctx_len, :]
        if local:
            n = i - n_ctx_q
            start_lat = jnp.clip(n * tq - SW_WINDOW, 0, seq - nk)
            start = pl.multiple_of(ctx_len + start_lat, SW_BLK)
            kw = k_ref[0, pl.ds(start, nk), :]
            vw = v_ref[0, pl.ds(start, nk), :]
            row = lax.broadcasted_iota(jnp.int32, (2 * tq, 1), 0)
            qpos = n * tq + jnp.where(first_head, row, row - tq)
            kpos = start_lat + lax.broadcasted_iota(jnp.int32, (1, nk), 1)
            valid = jnp.abs(qpos - kpos) <= SW_WINDOW
        outs = []
        for t in range(SW_W // LANES):
            qt = q[:, LANES * t:LANES * (t + 1)]
            zero = jnp.zeros_like(qt)
            qm = jnp.concatenate([jnp.where(masks[0], qt, zero), jnp.where(masks[1], qt, zero)], axis=0)
            sink = jnp.where(first_head, sink_ref[SW_HEAD_ORDER[2 * t]], sink_ref[SW_HEAD_ORDER[2 * t + 1]])
            s_cx = _dot_nt(qm, kc)
            m = jnp.maximum(jnp.max(s_cx, axis=-1, keepdims=True), sink)
            if local:
                s_loc = jnp.where(valid, _dot_nt(qm, kw), NEG_INF)
                m = jnp.maximum(m, jnp.max(s_loc, axis=-1, keepdims=True))
                p_loc = jnp.exp(s_loc - m)
            p_cx = jnp.exp(s_cx - m)
            den = jnp.sum(p_cx, axis=-1, keepdims=True) + jnp.exp(sink - m)
            o = _dot(p_cx.astype(MXU_DTYPE), vc)
            if local:
                den = den + jnp.sum(p_loc, axis=-1, keepdims=True)
                o = o + _dot(p_loc.astype(MXU_DTYPE), vw)
            o = o / den
            outs.append(jnp.where(masks[0], o[:tq], o[tq:]))
        o_ref[0] = jnp.concatenate(outs, axis=-1).astype(o_ref.dtype)

    @pl.when(i < n_ctx_q)
    def _():
        run(False)

    @pl.when(i >= n_ctx_q)
    def _():
        run(True)


def _sw_call(sinks, q, k, v, ctx_len):
    bsz, s, w = q.shape
    seq = s - ctx_len
    assert seq >= SW_KBLK and seq % SW_QBLK == 0 and ctx_len % SW_QBLK == 0
    whole = lambda b, i: (b, 0, 0)
    return pl.pallas_call(
        functools.partial(_sw_kernel, ctx_len=ctx_len, seq=seq),
        grid=(bsz, s // SW_QBLK),
        in_specs=[pl.BlockSpec(memory_space=pltpu.SMEM),
                  pl.BlockSpec((1, SW_QBLK, w), lambda b, i: (b, i, 0)),
                  pl.BlockSpec((1, s, SW_KV_W), whole),
                  pl.BlockSpec((1, s, SW_KV_W), whole)],
        out_specs=pl.BlockSpec((1, SW_QBLK, w), lambda b, i: (b, i, 0)),
        out_shape=jax.ShapeDtypeStruct((bsz, s, w), q.dtype),
        compiler_params=_params("parallel", "arbitrary"),
        name="sw_attn",
    )(sinks.astype(F32), q, k, v)


def _s5_kernel(u_ref, m_ref, wsr_ref, wsi_ref, wor_ref, woi_ref, lr_ref, li_ref, o_ref,
               sre, sim, xre, xim, acc, *, n_ctx_chunks):
    d = pl.program_id(1)
    bsz, nc, _ = u_ref.shape
    for b in range(bsz):
        ub = u_ref[b]
        sre[b] = _dot(ub, wsr_ref[0, 0])
        sim[b] = _dot(ub, wsi_ref[0, 0])
    lr = lr_ref[0, 0]
    li = li_ref[0, 0]

    def step(c, carry):
        new = []
        for b in range(bsz):
            xr, xi = carry[2 * b], carry[2 * b + 1]
            xre[b, pl.ds(c, 1), :] = xr
            xim[b, pl.ds(c, 1), :] = xi
            sr = sre[b, pl.ds(c, 1), :]
            si = sim[b, pl.ds(c, 1), :]
            new.append(lr * xr - li * xi + sr)
            new.append(lr * xi + li * xr + si)
        return tuple(new)

    zero = tuple(jnp.zeros((1, LANES), F32) for _ in range(2 * bsz))

    @pl.when(d == 0)
    def _():
        lax.fori_loop(0, nc, step, zero, unroll=2)

    @pl.when(d == 1)
    def _():
        carry = lax.fori_loop(0, n_ctx_chunks, lambda k, cr: step(n_ctx_chunks - 1 - k, cr), zero, unroll=2)
        lax.fori_loop(0, nc - n_ctx_chunks, lambda k, cr: step(nc - 1 - k, cr), carry, unroll=2)

    half = u_ref.shape[2] // 2
    for b in range(bsz):
        ub = u_ref[b]
        y_intra = jnp.concatenate([_dot(ub[:, :half], m_ref[0, 0]), _dot(ub[:, half:], m_ref[0, 1])], axis=-1)
        y = (y_intra + _dot(xre[b].astype(MXU_DTYPE), wor_ref[0, 0])
             + _dot(xim[b].astype(MXU_DTYPE), woi_ref[0, 0]))

        @pl.when(d == 0)
        def _():
            acc[b] = y

        @pl.when(d == 1)
        def _():
            o_ref[b] = (acc[b] + y).astype(o_ref.dtype)


def _s5_call(u_t, wts, n_ctx_chunks):
    m, wsr, wsi, wor, woi, lr, li = wts
    bsz, nc, width = u_t.shape
    pw = 2 * S5_CHUNK * S5_GROUP_CH
    n_pairs = width // pw
    blk = lambda shp: pl.BlockSpec((1, 1) + shp, lambda j, d: (d, j, 0, 0))
    return pl.pallas_call(
        functools.partial(_s5_kernel, n_ctx_chunks=n_ctx_chunks),
        grid=(n_pairs, 2),
        in_specs=[pl.BlockSpec((bsz, nc, pw), lambda j, d: (0, 0, j)),
                  pl.BlockSpec((1, 2, pw // 2, pw // 2), lambda j, d: (d, j, 0, 0)),
                  blk((pw, LANES)), blk((pw, LANES)), blk((LANES, pw)), blk((LANES, pw)),
                  blk((1, LANES)), blk((1, LANES))],
        out_specs=pl.BlockSpec((bsz, nc, pw), lambda j, d: (0, 0, j)),
        out_shape=jax.ShapeDtypeStruct((bsz, nc, width), MXU_DTYPE),
        scratch_shapes=[pltpu.VMEM((bsz, nc, LANES), F32) for _ in range(4)] + [pltpu.VMEM((bsz, nc, pw), F32)],
        compiler_params=_params("parallel", "arbitrary"),
        name="s5_scan",
    )(u_t, m, wsr, wsi, wor, woi, lr, li)


def _s5_weights(a_re, a_im, log_step, b_re, b_im, c_re, c_im):
    lc, g, p, h = S5_CHUNK, S5_GROUPS, S5_STATE, S5_GROUP_CH
    lam = lax.complex(jnp.minimum(a_re.astype(F32), S5_EIG_MAX), a_im.astype(F32))
    step = jnp.exp(log_step.astype(F32))[..., None]
    lam_bar = jnp.exp(lam * step)
    b_bar = ((lam_bar - 1.0) / lam)[..., None] * lax.complex(b_re.astype(F32), b_im.astype(F32))
    cc = lax.complex(c_re.astype(F32), c_im.astype(F32))
    dd = jnp.arange(lc + 1, dtype=F32)
    pw = jnp.exp((lam * step)[..., None] * dd)
    kern = jnp.real(jnp.einsum('zgop,zgpd,zgpi->zgdoi', cc, pw[..., :lc], b_bar))
    jj = np.arange(lc)[:, None]
    ii = np.arange(lc)[None, :]
    mats, wst, wout = [], [], []
    for z in range(2):
        lag = (ii - jj) if z == 0 else (jj - ii)
        ok = lag >= 0
        kz = kern[z][:, np.where(ok, lag, 0)]
        kz = jnp.where(ok[None, :, :, None, None], kz, 0.0)
        mats.append(kz.transpose(0, 1, 4, 2, 3).reshape(g, lc * h, lc * h))
        d_state = (lc - 1 - np.arange(lc)) if z == 0 else np.arange(lc)
        ws = pw[z][:, :, d_state][..., None] * b_bar[z][:, :, None, :]
        wst.append(ws.transpose(0, 2, 3, 1).reshape(g, lc * h, p))
        d_out = (np.arange(lc) + 1) if z == 0 else (lc - np.arange(lc))
        wo = cc[z][:, :, :, None] * pw[z][:, None, :, :][..., d_out]
        wout.append(wo.transpose(0, 2, 3, 1).reshape(g, p, lc * h))
    mats = jnp.stack(mats)
    wst = jnp.stack(wst)
    wout = jnp.stack(wout)

    def pair_rows(w):
        w = w.reshape(2, g // 2, 2, lc * h, p)
        z0 = jnp.zeros_like(w[:, :, 0])
        top = jnp.concatenate([w[:, :, 0], z0], axis=-1)
        bot = jnp.concatenate([z0, w[:, :, 1]], axis=-1)
        return jnp.concatenate([top, bot], axis=-2)

    def pair_cols(w):
        w = w.reshape(2, g // 2, 2, p, lc * h)
        z0 = jnp.zeros_like(w[:, :, 0])
        top = jnp.concatenate([w[:, :, 0], z0], axis=-1)
        bot = jnp.concatenate([z0, w[:, :, 1]], axis=-1)
        return jnp.concatenate([top, bot], axis=-2)

    lam_c = pw[..., lc].reshape(2, g // 2, 1, 2 * p)
    cast = lambda w: w.astype(MXU_DTYPE)
    return (cast(mats), cast(pair_rows(jnp.real(wst))), cast(pair_rows(jnp.imag(wst))),
            cast(pair_cols(jnp.real(wout))), cast(pair_cols(-jnp.imag(wout))),
            jnp.real(lam_c), jnp.imag(lam_c))


def _gelu_tanh(x):
    cdf = 0.5 * (1.0 + jnp.tanh(math.sqrt(2.0 / math.pi) * (x + 0.044715 * (x * x * x))))
    return x * cdf


def _outproj_kernel(x_ref, ya_ref, ysr_hbm, su_ref, yc_ref, modb_ref, modc_ref, woa_ref, wob_ref, woc_ref,
                    wglu_ref, bglu_ref, dsk_ref, g_ref, wr_ref, br_ref,
                    xo_ref, h_hbm, topi_ref, topw_ref, rank_ref, cnt_ref, carry, hbuf, hsem, ybuf, ysem,
                    *, ctx_len):
    segs = _mod_segments(modb_ref, modc_ref, ctx_len)
    n_steps = pl.num_programs(0) * pl.num_programs(1)
    step0 = pl.program_id(0) * pl.num_programs(1) + pl.program_id(1)
    slot0 = lax.rem(step0, 2)
    chunks = ybuf.shape[1]

    def y_fetch(st, sl_):
        for i in range(S5_CHUNK):
            pltpu.make_async_copy(ysr_hbm.at[pl.ds(st * chunks, chunks), pl.ds(S5_CH * i, S5_CH)],
                                  ybuf.at[sl_, :, i, :], ysem.at[sl_]).start()

    @pl.when(step0 == 0)
    def _():
        y_fetch(0, 0)

    @pl.when(step0 + 1 < n_steps)
    def _():
        y_fetch(step0 + 1, 1 - slot0)

    pltpu.make_async_copy(ybuf.at[slot0], ybuf.at[slot0], ysem.at[slot0]).wait()
    ys = ybuf[slot0].reshape(chunks * S5_CHUNK, S5_CH)
    y = dsk_ref[...] * su_ref[0] + ys
    gl = _gelu_tanh(y)
    yb = gl * jax.nn.sigmoid(_dot(gl.astype(MXU_DTYPE), wglu_ref[...]) + bglu_ref[...])
    mix = (_dot(ya_ref[0], woa_ref[...]) + _dot(yb.astype(MXU_DTYPE), wob_ref[...])
           + _dot(yc_ref[0], woc_ref[...]))
    x_in = x_ref[0]
    x = _by_segment(segs, lambda a, b, m: x_in[a:b] + m[2:3] * mix[a:b])
    xo_ref[0] = x
    y = _rms(x, g_ref[...])
    h = _by_segment(segs, lambda a, b, m: y[a:b] * (1.0 + m[4:5]) + m[3:4])
    dl = h_hbm.shape[-1]
    tile = pl.program_id(1)
    step = pl.program_id(0) * pl.num_programs(1) + tile
    slot = lax.rem(step, 2)

    def h_wait(sl_):
        pltpu.make_async_copy(hbuf.at[sl_], hbuf.at[sl_], hsem.at[sl_]).wait()

    @pl.when(step >= 2)
    def _():
        h_wait(slot)

    hbuf[slot] = h
    for sl in range(SUBLANES):
        pltpu.make_async_copy(hbuf.at[slot, :, pl.ds(dl * sl, dl)],
                              h_hbm.at[pl.program_id(0), pl.ds(tile * ROW_TILE, ROW_TILE), sl, :],
                              hsem.at[slot]).start()

    @pl.when(step == pl.num_programs(0) * pl.num_programs(1) - 1)
    def _():
        h_wait(1 - slot)
        h_wait(slot)

    logits = _dot3(wr_ref[...], h, nt=True) + br_ref[...]
    n_e, tm = logits.shape
    e_iota = lax.broadcasted_iota(jnp.int32, (n_e, tm), 0)
    vals, idxs = [], []
    for _ in range(TOP_K):
        mx = jnp.max(logits, axis=0, keepdims=True)
        ix = jnp.min(jnp.where(logits == mx, e_iota, n_e), axis=0, keepdims=True)
        vals.append(mx)
        idxs.append(ix)
        logits = jnp.where(e_iota == ix, -jnp.inf, logits)
    ex = [jnp.exp(v - vals[0]) for v in vals]
    den = ex[0] + ex[1] + ex[2] + ex[3]
    topi_ref[0] = jnp.concatenate(idxs, axis=0)
    topw_ref[0] = jnp.concatenate([e / den for e in ex], axis=0)

    @pl.when((pl.program_id(0) == 0) & (pl.program_id(1) == 0))
    def _():
        carry[...] = jnp.zeros_like(carry)

    sel = [ix == e_iota for ix in idxs]
    onehot = jnp.where(sel[0] | sel[1] | sel[2] | sel[3], 1.0, 0.0)
    before = (lax.broadcasted_iota(jnp.int32, (tm, tm), 0) < lax.broadcasted_iota(jnp.int32, (tm, tm), 1))
    pfx = _dot(onehot.astype(MXU_DTYPE), jnp.where(before, 1.0, 0.0).astype(MXU_DTYPE)) + carry[:, 0:1]
    rank_ref[0] = jnp.concatenate(
        [jnp.sum(jnp.where(sel[k], pfx, 0.0), axis=0, keepdims=True) for k in range(TOP_K)], axis=0)
    carry[...] = carry[...] + jnp.sum(onehot, axis=1, keepdims=True)
    cnt_ref[...] = carry[...]


def _outproj_call(xs, ya, ys, su, yc, mod_l, wts, ctx_len):
    bsz, s, d = xs.shape
    tm = ROW_TILE
    row = lambda b, j: (b, j, 0)
    const = lambda b, j: (0, 0)
    full = lambda a: pl.BlockSpec(a.shape, const)
    return pl.pallas_call(
        functools.partial(_outproj_kernel, ctx_len=ctx_len),
        grid=(bsz, s // tm),
        in_specs=[pl.BlockSpec((1, tm, d), row),
                  pl.BlockSpec((1, tm, NA_W), row),
                  pl.BlockSpec(memory_space=pl.ANY),
                  pl.BlockSpec((1, tm, S5_CH), row),
                  pl.BlockSpec((1, tm, SW_W), row),
                  pl.BlockSpec((1, 6, d), lambda b, j: (b, 0, 0)),
                  pl.BlockSpec((1, 6, d), lambda b, j: (bsz, 0, 0))] + [full(a) for a in wts],
        out_specs=[pl.BlockSpec((1, tm, d), row),
                   pl.BlockSpec(memory_space=pl.ANY),
                   pl.BlockSpec((1, TOP_K, tm), lambda b, j: (b, 0, j)),
                   pl.BlockSpec((1, TOP_K, tm), lambda b, j: (b, 0, j)),
                   pl.BlockSpec((1, TOP_K, tm), lambda b, j: (b, 0, j)),
                   pl.BlockSpec((N_EXPERTS, LANES), lambda b, j: (0, 0))],
        out_shape=[jax.ShapeDtypeStruct((bsz, s, d), F32),
                   jax.ShapeDtypeStruct((bsz, s, SUBLANES, d // SUBLANES), F32),
                   jax.ShapeDtypeStruct((bsz, TOP_K, s), jnp.int32),
                   jax.ShapeDtypeStruct((bsz, TOP_K, s), F32),
                   jax.ShapeDtypeStruct((bsz, TOP_K, s), F32),
                   jax.ShapeDtypeStruct((N_EXPERTS, LANES), F32)],
        scratch_shapes=[pltpu.VMEM((N_EXPERTS, LANES), F32), pltpu.VMEM((2, tm, d), F32),
                        pltpu.SemaphoreType.DMA((2,)),
                        pltpu.VMEM((2, tm // S5_CHUNK, S5_CHUNK, S5_CH), F32), pltpu.SemaphoreType.DMA((2,))],
        compiler_params=_params("arbitrary", "arbitrary"),
        name="outproj",
    )(xs, ya, ys, su, yc, mod_l, mod_l, *wts)


def _route_kernel(topi_ref, rank_ref, cnt_ref, dest_ref, blk_ref, ends_ref):
    idx = topi_ref[0]
    tm = idx.shape[1]
    e_iota = lax.broadcasted_iota(jnp.int32, (N_EXPERTS, tm), 0)
    counts = cnt_ref[:, 0:1]
    padded = jnp.ceil(counts * (1.0 / MOE_BLK)) * MOE_BLK
    r_i = lax.broadcasted_iota(jnp.int32, (N_EXPERTS, N_EXPERTS), 0)
    c_i = lax.broadcasted_iota(jnp.int32, (N_EXPERTS, N_EXPERTS), 1)
    padded_row = jnp.sum(jnp.where(r_i == c_i, padded, 0.0), axis=0, keepdims=True)
    pstart = jnp.sum(jnp.where(c_i < r_i, padded_row, 0.0), axis=1, keepdims=True)
    ends = pstart + padded
    rk = rank_ref[0]
    dest = [(jnp.sum(jnp.where(idx[k:k + 1] == e_iota, pstart, 0.0), axis=0, keepdims=True) + rk[k:k + 1]
             ).astype(jnp.int32) for k in range(TOP_K)]
    for u in range(tm // MOE_TILE):
        dest_ref[u] = jnp.concatenate([d[:, u * MOE_TILE:(u + 1) * MOE_TILE] for d in dest], axis=1)
    nb = blk_ref.shape[1]
    blk_start = (lax.broadcasted_iota(jnp.int32, (N_EXPERTS, nb), 1) * MOE_BLK).astype(F32)
    owner = jnp.sum(jnp.where(ends <= blk_start, 1.0, 0.0), axis=0, keepdims=True)
    blk_ref[...] = jnp.minimum(owner, N_EXPERTS - 1.0).astype(jnp.int32)
    ends_ref[...] = jnp.broadcast_to(ends, ends_ref.shape).astype(jnp.int32)


def _route_call(topi, rank, cnt, n_blocks):
    bsz, _, s = topi.shape
    tm = ROW_TILE
    sub = tm // MOE_TILE
    n_tiles = s // tm
    nb_pad = -(-n_blocks // LANES) * LANES
    tok = pl.BlockSpec((1, TOP_K, tm), lambda b, j: (b, 0, j))
    return pl.pallas_call(
        _route_kernel,
        grid=(bsz, n_tiles),
        in_specs=[tok, tok, pl.BlockSpec((N_EXPERTS, LANES), lambda b, j: (0, 0))],
        out_specs=[pl.BlockSpec((sub, 1, TOP_K * MOE_TILE), lambda b, j: (b * n_tiles + j, 0, 0)),
                   pl.BlockSpec((1, nb_pad), lambda b, j: (0, 0)),
                   pl.BlockSpec((N_EXPERTS, LANES), lambda b, j: (0, 0))],
        out_shape=[jax.ShapeDtypeStruct((bsz * n_tiles * sub, 1, TOP_K * MOE_TILE), jnp.int32),
                   jax.ShapeDtypeStruct((1, nb_pad), jnp.int32),
                   jax.ShapeDtypeStruct((N_EXPERTS, LANES), jnp.int32)],
        compiler_params=_params("arbitrary", "arbitrary"),
        name="route_dest",
    )(topi, rank, cnt)


def _dispatch_kernel(ends_ref, dest_ref, h_ref, xg_ref, zbuf, stage, sem, zsem, *, n_blocks):
    tm = h_ref.shape[1]

    @pl.when((pl.program_id(0) == 0) & (pl.program_id(1) == 0))
    def _():
        zbuf[...] = jnp.zeros_like(zbuf)

        def fill(row):
            return pltpu.make_async_copy(zbuf, xg_ref.at[pl.ds(pl.multiple_of(row, MOE_BLK), MOE_BLK)], zsem)

        def each(fn):
            for e in range(N_EXPERTS):
                begin = ends_ref[e - 1] if e else 0

                @pl.when(ends_ref[e] > begin)
                def _():
                    fn(fill(ends_ref[e] - MOE_BLK))

            def dead(i, c):
                fn(fill(i * MOE_BLK))
                return c

            lax.fori_loop(ends_ref[N_EXPERTS - 1] // MOE_BLK, n_blocks, dead, 0)

        each(lambda cp: cp.start())
        each(lambda cp: cp.wait())

    step = pl.program_id(0) * pl.num_programs(1) + pl.program_id(1)
    n_steps = pl.num_programs(0) * pl.num_programs(1)
    slot = lax.rem(step, 2)
    stage[slot] = h_ref[0]

    def body(t, c):
        for k in range(TOP_K):
            pltpu.make_async_copy(stage.at[slot, t], xg_ref.at[dest_ref[0, 0, k * tm + t]],
                                  sem.at[slot]).start(priority=k % 2)
        return c

    lax.fori_loop(0, tm, body, 0, unroll=8)

    def wait_tile(sl):
        pltpu.make_async_copy(xg_ref.at[pl.ds(0, TOP_K * tm)], xg_ref.at[pl.ds(0, TOP_K * tm)], sem.at[sl]).wait()

    @pl.when(step > 0)
    def _():
        wait_tile(1 - slot)

    @pl.when(step == n_steps - 1)
    def _():
        wait_tile(slot)


def _dispatch_call(ends, dest, h, n_blocks):
    bsz, s, _, dl = h.shape
    tm = MOE_TILE
    n_tiles = s // tm
    grid_spec = pltpu.PrefetchScalarGridSpec(
        num_scalar_prefetch=1,
        grid=(bsz, n_tiles),
        in_specs=[pl.BlockSpec((1, 1, TOP_K * tm), lambda b, j, en: (b * n_tiles + j, 0, 0), memory_space=pltpu.SMEM),
                  pl.BlockSpec((1, tm, SUBLANES, dl), lambda b, j, en: (b, j, 0, 0))],
        out_specs=pl.BlockSpec(memory_space=pl.ANY),
        scratch_shapes=[pltpu.VMEM((MOE_BLK, SUBLANES, dl), h.dtype), pltpu.VMEM((2, tm, SUBLANES, dl), h.dtype),
                        pltpu.SemaphoreType.DMA((2,)),
                        pltpu.SemaphoreType.DMA(())],
    )
    return pl.pallas_call(
        functools.partial(_dispatch_kernel, n_blocks=n_blocks),
        grid_spec=grid_spec,
        out_shape=jax.ShapeDtypeStruct((n_blocks * MOE_BLK, SUBLANES, dl), h.dtype),
        compiler_params=_params("arbitrary", "arbitrary"),
        name="moe_dispatch",
    )(ends, dest, h)


def _moe_kernel(blk_exp_ref, nact_ref, x_hbm, wgu_ref, bgu_ref, wd_ref, bd_ref, y_hbm, wgu_c, wd_c,
                xbuf, xsem, ybuf, ysem):
    i = pl.program_id(0)
    slot = lax.rem(i, 2)
    dl = x_hbm.shape[-1]

    def fetch(blk, sl_):
        for s in range(SUBLANES):
            pltpu.make_async_copy(x_hbm.at[pl.ds(blk * MOE_BLK, MOE_BLK), s, :],
                                  xbuf.at[sl_, :, pl.ds(dl * s, dl)], xsem.at[sl_]).start()

    @pl.when(i == 0)
    def _():
        fetch(0, 0)

    @pl.when(i + 1 < pl.num_programs(0))
    def _():
        fetch(i + 1, 1 - slot)

    def y_wait(sl_):
        pltpu.make_async_copy(ybuf.at[sl_], ybuf.at[sl_], ysem.at[sl_]).wait()

    @pl.when(i >= 2)
    def _():
        y_wait(slot)

    e = blk_exp_ref[i]
    prev = blk_exp_ref[jnp.maximum(i - 1, 0)]
    d, f2 = wgu_c.shape
    f = f2 // 2
    rows = 128

    @pl.when((i == 0) | (e != prev))
    def _():
        def cv(r, c):
            sl = pl.ds(pl.multiple_of(r * rows, rows), rows)
            wgu_c[sl, :] = wgu_ref[0, 0, sl, :].astype(wgu_c.dtype)
            return c
        lax.fori_loop(0, d // rows, cv, 0)

        def cv2(r, c):
            sl = pl.ds(pl.multiple_of(r * rows, rows), rows)
            wd_c[sl, :] = wd_ref[0, 0, sl, :].astype(wd_c.dtype)
            return c
        lax.fori_loop(0, f // rows, cv2, 0)

    pltpu.make_async_copy(xbuf.at[slot], xbuf.at[slot], xsem.at[slot]).wait()

    @pl.when(i < nact_ref[0])
    def _():
        gu = _dot(xbuf[slot].astype(MXU_DTYPE), wgu_c[...]) + bgu_ref[0, 0]
        gate = jnp.minimum(gu[:, :f], SWIGLU_LIMIT)
        up = jnp.clip(gu[:, f:], -SWIGLU_LIMIT, SWIGLU_LIMIT)
        act = gate * jax.nn.sigmoid(SWIGLU_ALPHA * gate) * (up + 1.0)
        ybuf[slot] = _dot(act.astype(MXU_DTYPE), wd_c[...]) + bd_ref[0, 0]

    @pl.when(i >= nact_ref[0])
    def _():
        ybuf[slot] = jnp.zeros(ybuf.shape[1:], ybuf.dtype)

    for s in range(SUBLANES):
        pltpu.make_async_copy(ybuf.at[slot, :, pl.ds(dl * s, dl)],
                              y_hbm.at[pl.ds(i * MOE_BLK, MOE_BLK), s, :], ysem.at[slot]).start()

    @pl.when(i == pl.num_programs(0) - 1)
    def _():
        y_wait(1 - slot)
        y_wait(slot)


def _moe_call(layer, blk_exp, n_active, xg, w_gate_up, b_gate_up, w_down, b_down):
    n_rows, _, dl = xg.shape
    depth, n_e, d, f2 = w_gate_up.shape
    f = f2 // 2
    n_blocks = n_rows // MOE_BLK
    assert n_blocks >= 2
    wsel = lambda i, be, na: (layer, be[i], 0, 0)
    grid_spec = pltpu.PrefetchScalarGridSpec(
        num_scalar_prefetch=2,
        grid=(n_blocks,),
        in_specs=[pl.BlockSpec(memory_space=pl.ANY),
                  pl.BlockSpec((1, 1, d, f2), wsel),
                  pl.BlockSpec((1, 1, 1, f2), wsel),
                  pl.BlockSpec((1, 1, f, d), wsel),
                  pl.BlockSpec((1, 1, 1, d), wsel)],
        out_specs=pl.BlockSpec(memory_space=pl.ANY),
        scratch_shapes=[pltpu.VMEM((d, f2), MXU_DTYPE), pltpu.VMEM((f, d), MXU_DTYPE),
                        pltpu.VMEM((2, MOE_BLK, d), F32), pltpu.SemaphoreType.DMA((2,)),
                        pltpu.VMEM((2, MOE_BLK, d), F32), pltpu.SemaphoreType.DMA((2,))],
    )
    return pl.pallas_call(
        _moe_kernel,
        grid_spec=grid_spec,
        out_shape=jax.ShapeDtypeStruct((n_rows, SUBLANES, dl), F32),
        compiler_params=_params("arbitrary"),
        name="moe_experts",
    )(blk_exp, n_active, xg, w_gate_up, b_gate_up.reshape(depth, n_e, 1, f2),
      w_down, b_down.reshape(depth, n_e, 1, d))


def _combine_kernel(dest_ref, dnext_ref, w_ref, gate_ref, x_hbm, yg_hbm, xo_hbm,
                    gbuf, xbuf, obuf, gsem, xsem, osem, *, n_tiles):
    tm = MOE_TILE
    dl = gbuf.shape[-1]
    step = pl.program_id(0) * n_tiles + pl.program_id(1)
    n_steps = pl.num_programs(0) * n_tiles
    slot = lax.rem(step, 2)

    def stream_copies(st, sl_, buf, hbm, sem, to_hbm):
        b = lax.div(st, n_tiles)
        r0 = lax.rem(st, n_tiles) * tm
        out = []
        for s in range(SUBLANES):
            rows = hbm.at[b, pl.ds(r0, tm), pl.ds(dl * s, dl)]
            tiles = buf.at[sl_, :, s, :]
            out.append(pltpu.make_async_copy(tiles, rows, sem.at[sl_]) if to_hbm
                       else pltpu.make_async_copy(rows, tiles, sem.at[sl_]))
        return out

    def fetch(dref, st, sl_):
        for cp in stream_copies(st, sl_, xbuf, x_hbm, xsem, False):
            cp.start()

        def body(t, c):
            for k in range(TOP_K):
                pltpu.make_async_copy(yg_hbm.at[dref[0, 0, k * tm + t]], gbuf.at[sl_, k, t],
                                      gsem.at[sl_]).start(priority=k % 2)
            return c

        lax.fori_loop(0, tm, body, 0, unroll=8)

    @pl.when(step == 0)
    def _():
        fetch(dest_ref, 0, 0)

    @pl.when(step + 1 < n_steps)
    def _():
        fetch(dnext_ref, step + 1, 1 - slot)

    def wait_all(buf, sem, sl_):
        pltpu.make_async_copy(buf.at[sl_], buf.at[sl_], sem.at[sl_]).wait()

    wait_all(gbuf, gsem, slot)
    wait_all(xbuf, xsem, slot)

    @pl.when(step >= 2)
    def _():
        wait_all(obuf, osem, slot)

    gate = gate_ref[0]

    def row(t, c):
        acc = gbuf[slot, 0, t] * w_ref[0, 0, t]
        for k in range(1, TOP_K):
            acc = acc + gbuf[slot, k, t] * w_ref[0, 0, k * tm + t]
        obuf[slot, t] = xbuf[slot, t] + gate * acc
        return c

    lax.fori_loop(0, tm, row, 0, unroll=8)
    for cp in stream_copies(step, slot, obuf, xo_hbm, osem, True):
        cp.start()

    @pl.when(step == n_steps - 1)
    def _():
        wait_all(obuf, osem, 1 - slot)
        wait_all(obuf, osem, slot)


def _combine_call(dest, topw, xs, mod_l, yg, ctx_len):
    bsz, s, d = xs.shape
    tm = MOE_TILE
    n_tiles = s // tm
    last = bsz * n_tiles - 1
    assert last >= 1
    dl = d // SUBLANES
    w_flat = topw.reshape(bsz, TOP_K, n_tiles, tm).transpose(0, 2, 1, 3).reshape(bsz * n_tiles, 1, TOP_K * tm)
    gate = mod_l[:, 5].reshape(MOD_ROWS, SUBLANES, dl)
    idx_spec = lambda ahead: pl.BlockSpec(
        (1, 1, TOP_K * tm), lambda b, j: (jnp.minimum(b * n_tiles + j + ahead, last), 0, 0), memory_space=pltpu.SMEM)
    tile_buf = lambda lead: pltpu.VMEM(lead + (tm, SUBLANES, dl), F32)
    return pl.pallas_call(
        functools.partial(_combine_kernel, n_tiles=n_tiles),
        grid=(bsz, n_tiles),
        in_specs=[idx_spec(0), idx_spec(1), idx_spec(0),
                  pl.BlockSpec((1, SUBLANES, dl), lambda b, j: (jnp.where(j < ctx_len // tm, bsz, b), 0, 0)),
                  pl.BlockSpec(memory_space=pl.ANY),
                  pl.BlockSpec(memory_space=pl.ANY)],
        out_specs=pl.BlockSpec(memory_space=pl.ANY),
        out_shape=jax.ShapeDtypeStruct((bsz, s, d), F32),
        scratch_shapes=[tile_buf((2, TOP_K)), tile_buf((2,)), tile_buf((2,)),
                        pltpu.SemaphoreType.DMA((2,)), pltpu.SemaphoreType.DMA((2,)), pltpu.SemaphoreType.DMA((2,))],
        compiler_params=_params("arbitrary", "arbitrary"),
        name="moe_combine",
    )(dest, dest, w_flat, gate, xs, yg)


def _final_kernel(x_ref, g_ref, o_ref):
    x = x_ref[0]
    o_ref[0] = x * lax.rsqrt(jnp.mean(x * x, axis=-1, keepdims=True) + RMS_EPS) * g_ref[...]


def _final_call(xs, g, ctx_len):
    bsz, s, d = xs.shape
    tm = MOE_TILE
    off = ctx_len // tm
    return pl.pallas_call(
        _final_kernel,
        grid=(bsz, (s - ctx_len) // tm),
        in_specs=[pl.BlockSpec((1, tm, d), lambda b, j: (b, j + off, 0)),
                  pl.BlockSpec((1, d), lambda b, j: (0, 0))],
        out_specs=pl.BlockSpec((1, tm, d), lambda b, j: (b, j, 0)),
        out_shape=jax.ShapeDtypeStruct((bsz, s - ctx_len, d), F32),
        compiler_params=_params("parallel", "parallel"),
        name="final_norm",
    )(xs, g)


def _rope_tables(seq, ctx_len):
    t = jnp.arange(seq)
    row = (t // GRID_W).astype(F32)
    col = (t % GRID_W).astype(F32)
    nf = HEAD_DIM // 4
    inv = ROPE_BASE ** (-jnp.arange(nf, dtype=F32) / nf)
    ar = row[:, None] * inv
    ac = col[:, None] * inv
    ang = jnp.concatenate([ar, ar, ac, ac], axis=-1)
    cos = jnp.concatenate([jnp.ones((ctx_len, HEAD_DIM), F32), jnp.cos(ang)], axis=0)
    sin = jnp.concatenate([jnp.zeros((ctx_len, HEAD_DIM), F32), jnp.sin(ang)], axis=0)
    reps = LANES // HEAD_DIM
    return jnp.tile(cos, (1, reps)), jnp.tile(sin, (1, reps))


def _rot_cols(w):
    q = HEAD_DIM // 4
    j = np.arange(HEAD_DIM)
    first = (j % (2 * q)) < q
    src = np.where(first, j + q, j - q)
    sign = np.where(first, -1.0, 1.0).astype(np.float32)
    n_heads = w.shape[1] // HEAD_DIM
    src_all = (np.arange(n_heads)[:, None] * HEAD_DIM + src[None, :]).reshape(-1)
    return w[:, src_all] * jnp.asarray(np.tile(sign, n_heads))


def _head_perm_cols(order):
    return (np.asarray(order)[:, None] * HEAD_DIM + np.arange(HEAD_DIM)[None, :]).reshape(-1)


def _inproj_weight(w_in_l):
    aq, ak, av, su, sq, sk, sv = jnp.split(
        w_in_l, np.cumsum([NA_W, NA_W, NA_W, S5_CH, SW_W, SW_KV_W])[:6].tolist(), axis=1)
    sq = sq[:, _head_perm_cols(SW_HEAD_ORDER)]
    return jnp.concatenate([aq, ak, av, sq, _rot_cols(sq), sk, _rot_cols(sk), sv, su], axis=1).astype(MXU_DTYPE)


def _swap_kernel(v_ref, o_ref, *, n_a, n_b, h):
    per = LANES // h
    lane_blk = lax.shift_right_logical(lax.broadcasted_iota(jnp.int32, (1, LANES), 1), h.bit_length() - 1)
    for b in range(n_b):
        for dv in range(n_a * h // LANES):
            acc = None
            for a in range(dv * per, (dv + 1) * per):
                col = a * n_b * h + (b // per) * LANES
                src = v_ref[:, col:col + LANES].astype(F32)
                moved = pltpu.roll(src, ((a - b) % per) * h, axis=1)
                acc = moved if acc is None else jnp.where(lane_blk == a % per, moved, acc)
            col = b * n_a * h + dv * LANES
            o_ref[:, col:col + LANES] = acc.astype(o_ref.dtype)


def _swap_call(v, rows, n_a, n_b, h, out_dtype):
    r, width = v.shape
    assert width == n_a * n_b * h and h & (h - 1) == 0 and LANES % h == 0 and r % rows == 0
    assert (n_a * h) % LANES == 0 and (n_b * h) % LANES == 0
    return pl.pallas_call(
        functools.partial(_swap_kernel, n_a=n_a, n_b=n_b, h=h),
        grid=(r // rows,),
        in_specs=[pl.BlockSpec((rows, width), lambda i: (i, 0))],
        out_specs=pl.BlockSpec((rows, width), lambda i: (i, 0)),
        out_shape=jax.ShapeDtypeStruct((r, width), out_dtype),
        compiler_params=_params("parallel"),
        name="s5_swap",
    )(v)


def _chunk_major(su_rows, bsz):
    nc = su_rows.shape[0] // bsz
    u = _swap_call(su_rows, nc, S5_CHUNK, S5_GROUPS, S5_GROUP_CH, MXU_DTYPE)
    return u.reshape(bsz, nc, S5_CHUNK * S5_CH)


def _token_rows(y_t):
    bsz, nc, width = y_t.shape
    return _swap_call(y_t.reshape(bsz * nc, width), nc, S5_GROUPS, S5_CHUNK, S5_GROUP_CH, F32)


def kernel(x, c, ctx, c_ctx, w_mod, b_mod, g_mix, w_in, w_out, na_rpb, s5_a_re, s5_a_im, s5_log_step,
           s5_b_re, s5_b_im, s5_c_re, s5_c_im, s5_d, s5_w_glu, s5_b_glu, sw_sinks, g_ffn, w_router, b_router,
           w_gate_up, b_gate_up, w_down, b_down, g_final):
    bsz, seq, d = x.shape
    ctx_len = ctx.shape[1]
    depth = w_mod.shape[0]
    s = ctx_len + seq
    assert bsz + 1 <= MOD_ROWS and s % ROW_TILE == 0 and ctx_len % MOE_TILE == 0 and seq % MOE_TILE == 0
    assert ROW_TILE % MOE_TILE == 0 and ROW_TILE % S5_CHUNK == 0 and bsz * (s // ROW_TILE) >= 2
    assert seq % GRID_W == 0 and ctx_len % S5_CHUNK == 0

    xs = jnp.concatenate([ctx, x], axis=1)
    cond = jnp.zeros((MOD_ROWS, d), F32).at[:bsz].set(c).at[bsz].set(c_ctx)
    mod = _mod_call(cond, w_mod, b_mod).reshape(depth, MOD_ROWS, 6, d)
    cos2, sin2 = _rope_tables(seq, ctx_len)

    n_assign = bsz * s * TOP_K
    n_blocks = -(-(n_assign + N_EXPERTS * (MOE_BLK - 1)) // MOE_BLK)
    sw_rows = _head_perm_cols(SW_HEAD_ORDER)

    w_cat = jax.vmap(_inproj_weight)(w_in)
    bias_tab = jax.vmap(_na_bias_table)(na_rpb)
    s5w = jax.vmap(_s5_weights)(s5_a_re, s5_a_im, s5_log_step, s5_b_re, s5_b_im, s5_c_re, s5_c_im)
    wo_a = w_out[:, :NA_W].astype(MXU_DTYPE)
    wo_b = w_out[:, NA_W:NA_W + S5_CH].astype(MXU_DTYPE)
    wo_c = w_out[:, NA_W + S5_CH:][:, sw_rows].astype(MXU_DTYPE)
    w_glu = s5_w_glu.astype(MXU_DTYPE)
    w_rt = jnp.swapaxes(w_router, 1, 2).astype(F32)

    for l in range(depth):
        mod_l = mod[l]
        naq, nak, nav, swq, swk, swv, su, su_rows = _inproj_call(
            xs, mod_l, g_mix[l].reshape(1, d), w_cat[l], cos2, sin2, ctx_len)
        ya = _na_call(naq, nak, nav, bias_tab[l], ctx_len)
        yc = _sw_call(sw_sinks[l], swq, swk, swv, ctx_len)
        ys = _token_rows(_s5_call(_chunk_major(su_rows, bsz), tuple(w[l] for w in s5w), ctx_len // S5_CHUNK))

        out_wts = (wo_a[l], wo_b[l], wo_c[l], w_glu[l], s5_b_glu[l].reshape(1, S5_CH).astype(F32),
                   s5_d[l].reshape(1, S5_CH).astype(F32), g_ffn[l].reshape(1, d),
                   w_rt[l], b_router[l].reshape(N_EXPERTS, 1).astype(F32))
        xs, h, topi, topw, rank, cnt = _outproj_call(xs, ya, ys, su, yc, mod_l, out_wts, ctx_len)

        dest, blk, ends = _route_call(topi, rank, cnt, n_blocks)
        ends = ends[:, 0]
        xg = _dispatch_call(ends, dest, h, n_blocks)
        yg = _moe_call(l, blk[0, :n_blocks], ends[-1:] // MOE_BLK, xg, w_gate_up, b_gate_up, w_down, b_down)
        xs = _combine_call(dest, topw, xs, mod_l, yg, ctx_len)

    return _final_call(xs, g_final.reshape(1, d), ctx_len)
```

```python
import functools
import math

import numpy as np
import jax
import jax.numpy as jnp
from jax import lax
from jax.experimental import pallas as pl
from jax.experimental.pallas import tpu as pltpu

F32 = jnp.float32
MXU_DTYPE = jnp.bfloat16

GRID_W = 64
HEAD_DIM = 64
NA_HEADS = 6
NA_W = NA_HEADS * HEAD_DIM
NA_ROWS = 8
NA_COLS = 16
S5_GROUP_CH = 16
S5_CH = 256
S5_GROUPS = S5_CH // S5_GROUP_CH
S5_STATE = 64
S5_EIG_MAX = -1e-4
SW_HEADS = 6
SW_KV_HEADS = 2
SW_GRP = SW_HEADS // SW_KV_HEADS
SW_W = SW_HEADS * HEAD_DIM
SW_KV_W = SW_KV_HEADS * HEAD_DIM
SW_WINDOW = 128
SW_BLK = 128
ROPE_BASE = 10000.0
N_EXPERTS = 32
TOP_K = 4
MOE_BLK = 512
SWIGLU_LIMIT = 7.0
SWIGLU_ALPHA = 1.702
RMS_EPS = 1e-6
NEG_INF = -1e30

LANES = 128
ROW_TILE = 768
MOE_TILE = 256
S5_CHUNK = 16
NA_QROWS = 4
NA_KROWS = NA_QROWS + NA_ROWS
SW_QBLK = 2 * SW_BLK
SW_KBLK = SW_QBLK + 2 * SW_WINDOW
MOD_ROWS = 8
VMEM_LIMIT = 56 << 20
SUBLANES = 8

SW_HEAD_ORDER = tuple(g * SW_GRP + t for t in range(SW_GRP) for g in range(SW_KV_HEADS))


def _params(*sem):
    return pltpu.CompilerParams(dimension_semantics=sem, vmem_limit_bytes=VMEM_LIMIT)


def _dot(a, b):
    return jnp.dot(a, b, preferred_element_type=F32)


def _dot_nt(a, b):
    return lax.dot_general(a, b, (((1,), (1,)), ((), ())), preferred_element_type=F32)


def _split(a):
    hi = a.astype(MXU_DTYPE)
    lo = (a - hi.astype(F32)).astype(MXU_DTYPE)
    return hi, lo


def _dot3(a, b, nt=False):
    f = _dot_nt if nt else _dot
    ah, al = _split(a)
    bh, bl = _split(b)
    return f(ah, bh) + (f(ah, bl) + f(al, bh))


def _mod_kernel(cond_ref, w_ref, b_ref, o_ref):
    c = cond_ref[...]
    a = c * jax.nn.sigmoid(c)
    o_ref[0] = _dot3(a, w_ref[0]) + b_ref[0]


def _mod_call(cond, w_mod, b_mod):
    depth, d, n = w_mod.shape
    tn = n // 6
    return pl.pallas_call(
        _mod_kernel,
        grid=(depth, n // tn),
        in_specs=[pl.BlockSpec((MOD_ROWS, d), lambda l, j: (0, 0)),
                  pl.BlockSpec((1, d, tn), lambda l, j: (l, 0, j)),
                  pl.BlockSpec((1, 1, tn), lambda l, j: (l, 0, j))],
        out_specs=pl.BlockSpec((1, MOD_ROWS, tn), lambda l, j: (l, 0, j)),
        out_shape=jax.ShapeDtypeStruct((depth, MOD_ROWS, n), F32),
        compiler_params=_params("parallel", "parallel"),
        name="mod",
    )(cond, w_mod, b_mod.reshape(depth, 1, n))


C_AQ = 0
C_AK = C_AQ + NA_W
C_AV = C_AK + NA_W
C_SQ = C_AV + NA_W
C_SQR = C_SQ + SW_W
C_SK = C_SQR + SW_W
C_SKR = C_SK + SW_KV_W
C_SV = C_SKR + SW_KV_W
C_SU = C_SV + SW_KV_W
C_END = C_SU + S5_CH


def _rms(x, g):
    return x * lax.rsqrt(jnp.mean(x * x, axis=-1, keepdims=True) + RMS_EPS) * g


def _mod_segments(modb_ref, modc_ref, ctx_len):
    r = ctx_len % ROW_TILE
    out = []
    for a, b in ([(0, ROW_TILE)] if r == 0 else [(0, r), (r, ROW_TILE)]):
        is_ctx = pl.program_id(1) * ROW_TILE + a < ctx_len
        out.append((a, b, jnp.where(is_ctx, modc_ref[0], modb_ref[0])))
    return out


def _by_segment(segs, fn):
    return jnp.concatenate([fn(a, b, m) for a, b, m in segs], axis=0)


def _chunk_row_copies(buf, slot, hbm, sem, to_hbm):
    chunks, _, width = buf.shape[1:]
    c0 = (pl.program_id(0) * pl.num_programs(1) + pl.program_id(1)) * chunks
    out = []
    for i in range(S5_CHUNK):
        tile = buf.at[slot, :, i, :]
        rows = hbm.at[pl.ds(c0, chunks), pl.ds(width * i, width)]
        out.append(pltpu.make_async_copy(tile, rows, sem.at[slot]) if to_hbm
                   else pltpu.make_async_copy(rows, tile, sem.at[slot]))
    return out


def _inproj_kernel(x_ref, modb_ref, modc_ref, g_ref, w_ref, cos_ref, sin_ref,
                   naq_ref, nak_ref, nav_ref, swq_ref, swk_ref, swv_ref, su_ref, sur_hbm, sbuf, ssem, *, ctx_len):
    segs = _mod_segments(modb_ref, modc_ref, ctx_len)
    y = _rms(x_ref[0], g_ref[...])
    h = _by_segment(segs, lambda a, b, m: y[a:b] * (1.0 + m[1:2]) + m[0:1]).astype(MXU_DTYPE)
    p = _dot(h, w_ref[...])
    cos = cos_ref[...]
    sin = sin_ref[...]
    cos3 = jnp.concatenate([cos] * (SW_W // LANES), axis=1)
    sin3 = jnp.concatenate([sin] * (SW_W // LANES), axis=1)
    qk_scale = HEAD_DIM ** -0.5
    naq_ref[0] = (p[:, C_AQ:C_AK] * qk_scale).astype(naq_ref.dtype)
    nak_ref[0] = p[:, C_AK:C_AV].astype(nak_ref.dtype)
    nav_ref[0] = p[:, C_AV:C_SQ].astype(nav_ref.dtype)
    swq_ref[0] = ((p[:, C_SQ:C_SQR] * cos3 + p[:, C_SQR:C_SK] * sin3) * qk_scale).astype(swq_ref.dtype)
    swk_ref[0] = (p[:, C_SK:C_SKR] * cos + p[:, C_SKR:C_SV] * sin).astype(swk_ref.dtype)
    swv_ref[0] = p[:, C_SV:C_SU].astype(swv_ref.dtype)
    su = p[:, C_SU:C_END]
    su_ref[0] = su
    step = pl.program_id(0) * pl.num_programs(1) + pl.program_id(1)
    slot = lax.rem(step, 2)

    def s_wait(sl_):
        pltpu.make_async_copy(sbuf.at[sl_], sbuf.at[sl_], ssem.at[sl_]).wait()

    @pl.when(step >= 2)
    def _():
        s_wait(slot)

    sbuf[slot] = su.reshape(sbuf.shape[1:])
    for cp in _chunk_row_copies(sbuf, slot, sur_hbm, ssem, True):
        cp.start()

    @pl.when(step == pl.num_programs(0) * pl.num_programs(1) - 1)
    def _():
        s_wait(1 - slot)
        s_wait(slot)


def _inproj_call(xs, mod_l, g, w_cat, cos2, sin2, ctx_len):
    bsz, s, d = xs.shape
    tm = ROW_TILE
    row = lambda b, j: (b, j, 0)
    const = lambda b, j: (0, 0)
    widths = (NA_W, NA_W, NA_W, SW_W, SW_KV_W, SW_KV_W, S5_CH)
    dtypes = (MXU_DTYPE,) * 6 + (F32,)
    return pl.pallas_call(
        functools.partial(_inproj_kernel, ctx_len=ctx_len),
        grid=(bsz, s // tm),
        in_specs=[pl.BlockSpec((1, tm, d), row),
                  pl.BlockSpec((1, 6, d), lambda b, j: (b, 0, 0)),
                  pl.BlockSpec((1, 6, d), lambda b, j: (bsz, 0, 0)),
                  pl.BlockSpec((1, d), const),
                  pl.BlockSpec((d, C_END), const),
                  pl.BlockSpec((tm, LANES), lambda b, j: (j, 0)),
                  pl.BlockSpec((tm, LANES), lambda b, j: (j, 0))],
        out_specs=[pl.BlockSpec((1, tm, w), row) for w in widths] + [pl.BlockSpec(memory_space=pl.ANY)],
        out_shape=[jax.ShapeDtypeStruct((bsz, s, w), t) for w, t in zip(widths, dtypes)]
        + [jax.ShapeDtypeStruct((bsz * s // S5_CHUNK, S5_CHUNK * S5_CH), F32)],
        scratch_shapes=[pltpu.VMEM((2, tm // S5_CHUNK, S5_CHUNK, S5_CH), F32), pltpu.SemaphoreType.DMA((2,))],
        compiler_params=_params("arbitrary", "arbitrary"),
        name="inproj",
    )(xs, mod_l, mod_l, g, w_cat, cos2, sin2)


def _half_masks():
    lane = lax.broadcasted_iota(jnp.int32, (1, LANES), 1)
    return lane < HEAD_DIM, lane >= HEAD_DIM


def _na_kernel(q_ref, k_ref, v_ref, bias_ref, o_ref, *, ctx_len, rows):
    i = pl.program_id(1)
    tq = NA_QROWS * GRID_W
    n_ctx_q = ctx_len // tq
    masks = _half_masks()
    nk = NA_KROWS * GRID_W

    def run(local):
        q = q_ref[0]
        if local:
            r0 = (i - n_ctx_q) * NA_QROWS
            start0 = jnp.clip(r0 - NA_ROWS // 2, 0, rows - NA_KROWS)
            start = pl.multiple_of(ctx_len + start0 * GRID_W, GRID_W)
            tab_idx, row_mask = {}, {}
            for a in range(NA_QROWS):
                r = r0 + a
                s_r = jnp.clip(r - NA_ROWS // 2, 0, rows - NA_ROWS)
                for p in range(NA_KROWS // 2):
                    kr = start0 + 2 * p
                    ok_lo = (kr >= s_r) & (kr < s_r + NA_ROWS)
                    ok_hi = (kr + 1 >= s_r) & (kr + 1 < s_r + NA_ROWS)
                    tab_idx[a, p] = jnp.clip(kr - r + NA_ROWS, 0, 2 * NA_ROWS - 1)
                    row_mask[a, p] = jnp.where(masks[0], jnp.where(ok_lo, 0.0, NEG_INF),
                                               jnp.where(ok_hi, 0.0, NEG_INF))
        outs = []
        for t in range(NA_W // LANES):
            sl = slice(LANES * t, LANES * (t + 1))
            qt = q[:, sl]
            zero = jnp.zeros_like(qt)
            qm = jnp.concatenate([jnp.where(masks[0], qt, zero), jnp.where(masks[1], qt, zero)], axis=0)
            kc = k_ref[0, 0:ctx_len, sl]
            vc = v_ref[0, 0:ctx_len, sl]
            s_cx = _dot_nt(qm, kc)
            m = jnp.max(s_cx, axis=-1, keepdims=True)
            if local:
                kw = k_ref[0, pl.ds(start, nk), sl]
                vw = v_ref[0, pl.ds(start, nk), sl]
                bias = jnp.concatenate(
                    [jnp.concatenate([bias_ref[2 * t + hh, pl.ds(tab_idx[a, p], 1)][0] + row_mask[a, p]
                                      for p in range(NA_KROWS // 2)], axis=-1)
                     for hh in range(2) for a in range(NA_QROWS)], axis=0)
                s_nb = _dot_nt(qm, kw) + bias
                m = jnp.maximum(m, jnp.max(s_nb, axis=-1, keepdims=True))
                p_nb = jnp.exp(s_nb - m)
            p_cx = jnp.exp(s_cx - m)
            den = jnp.sum(p_cx, axis=-1, keepdims=True)
            o = _dot(p_cx.astype(MXU_DTYPE), vc)
            if local:
                den = den + jnp.sum(p_nb, axis=-1, keepdims=True)
                o = o + _dot(p_nb.astype(MXU_DTYPE), vw)
            o = o / den
            outs.append(jnp.where(masks[0], o[:tq], o[tq:]))
        o_ref[0] = jnp.concatenate(outs, axis=-1).astype(o_ref.dtype)

    @pl.when(i < n_ctx_q)
    def _():
        run(False)

    @pl.when(i >= n_ctx_q)
    def _():
        run(True)


def _na_call(q, k, v, bias_tab, ctx_len):
    bsz, s, w = q.shape
    rows = (s - ctx_len) // GRID_W
    tq = NA_QROWS * GRID_W
    assert rows >= NA_KROWS and rows % NA_QROWS == 0 and ctx_len % tq == 0
    whole = lambda b, i: (b, 0, 0)
    return pl.pallas_call(
        functools.partial(_na_kernel, ctx_len=ctx_len, rows=rows),
        grid=(bsz, s // tq),
        in_specs=[pl.BlockSpec((1, tq, w), lambda b, i: (b, i, 0)),
                  pl.BlockSpec((1, s, w), whole),
                  pl.BlockSpec((1, s, w), whole),
                  pl.BlockSpec(bias_tab.shape, lambda b, i: (0, 0, 0, 0))],
        out_specs=pl.BlockSpec((1, tq, w), lambda b, i: (b, i, 0)),
        out_shape=jax.ShapeDtypeStruct((bsz, s, w), q.dtype),
        compiler_params=_params("parallel", "arbitrary"),
        name="na_attn",
    )(q, k, v, bias_tab)


def _na_bias_table(rpb):
    qcol = np.arange(GRID_W)[:, None]
    kcol = np.arange(GRID_W)[None, :]
    ws = np.clip(qcol - NA_COLS // 2, 0, GRID_W - NA_COLS)
    valid = (kcol >= ws) & (kcol < ws + NA_COLS)
    dc = np.clip(kcol - qcol + NA_COLS - 1, 0, 2 * NA_COLS - 2)
    full = jnp.where(valid[None, None], rpb[:, :, dc].astype(F32), NEG_INF)
    edge = jnp.full_like(full[:, :1], NEG_INF)
    full = jnp.concatenate([edge, full, edge], axis=1)
    return jnp.concatenate([full[:, :-1], full[:, 1:]], axis=-1)


def _sw_kernel(sink_ref, q_ref, k_ref, v_ref, o_ref, *, ctx_len, seq):
    i = pl.program_id(1)
    tq = SW_QBLK
    n_ctx_q = ctx_len // tq
    masks = _half_masks()
    nk = SW_KBLK
    first_head = lax.broadcasted_iota(jnp.int32, (2 * tq, 1), 0) < tq

    def run(local):
        q = q_ref[0]
        kc = k_ref[0, 0:ctx_len, :]
        vc = v_ref[0, 0:ctx_len, :]
        if local:
            n = i - n_ctx_q
            start_lat = jnp.clip(n * tq - SW_WINDOW, 0, seq - nk)
            start = pl.multiple_of(ctx_len + start_lat, SW_BLK)
            kw = k_ref[0, pl.ds(start, nk), :]
            vw = v_ref[0, pl.ds(start, nk), :]
            row = lax.broadcasted_iota(jnp.int32, (2 * tq, 1), 0)
            qpos = n * tq + jnp.where(first_head, row, row - tq)
            kpos = start_lat + lax.broadcasted_iota(jnp.int32, (1, nk), 1)
            valid = jnp.abs(qpos - kpos) <= SW_WINDOW
        outs = []
        for t in range(SW_W // LANES):
            qt = q[:, LANES * t:LANES * (t + 1)]
            zero = jnp.zeros_like(qt)
            qm = jnp.concatenate([jnp.where(masks[0], qt, zero), jnp.where(masks[1], qt, zero)], axis=0)
            sink = jnp.where(first_head, sink_ref[SW_HEAD_ORDER[2 * t]], sink_ref[SW_HEAD_ORDER[2 * t + 1]])
            s_cx = _dot_nt(qm, kc)
            m = jnp.maximum(jnp.max(s_cx, axis=-1, keepdims=True), sink)
            if local:
                s_loc = jnp.where(valid, _dot_nt(qm, kw), NEG_INF)
                m = jnp.maximum(m, jnp.max(s_loc, axis=-1, keepdims=True))
                p_loc = jnp.exp(s_loc - m)
            p_cx = jnp.exp(s_cx - m)
            den = jnp.sum(p_cx, axis=-1, keepdims=True) + jnp.exp(sink - m)
            o = _dot(p_cx.astype(MXU_DTYPE), vc)
            if local:
                den = den + jnp.sum(p_loc, axis=-1, keepdims=True)
                o = o + _dot(p_loc.astype(MXU_DTYPE), vw)
            o = o / den
            outs.append(jnp.where(masks[0], o[:tq], o[tq:]))
        o_ref[0] = jnp.concatenate(outs, axis=-1).astype(o_ref.dtype)

    @pl.when(i < n_ctx_q)
    def _():
        run(False)

    @pl.when(i >= n_ctx_q)
    def _():
        run(True)


def _sw_call(sinks, q, k, v, ctx_len):
    bsz, s, w = q.shape
    seq = s - ctx_len
    assert seq >= SW_KBLK and seq % SW_QBLK == 0 and ctx_len % SW_QBLK == 0
    whole = lambda b, i: (b, 0, 0)
    return pl.pallas_call(
        functools.partial(_sw_kernel, ctx_len=ctx_len, seq=seq),
        grid=(bsz, s // SW_QBLK),
        in_specs=[pl.BlockSpec(memory_space=pltpu.SMEM),
                  pl.BlockSpec((1, SW_QBLK, w), lambda b, i: (b, i, 0)),
                  pl.BlockSpec((1, s, SW_KV_W), whole),
                  pl.BlockSpec((1, s, SW_KV_W), whole)],
        out_specs=pl.BlockSpec((1, SW_QBLK, w), lambda b, i: (b, i, 0)),
        out_shape=jax.ShapeDtypeStruct((bsz, s, w), q.dtype),
        compiler_params=_params("parallel", "arbitrary"),
        name="sw_attn",
    )(sinks.astype(F32), q, k, v)


def _s5_kernel(u_ref, m_ref, wsr_ref, wsi_ref, wor_ref, woi_ref, lr_ref, li_ref, o_ref,
               sre, sim, xre, xim, acc, *, n_ctx_chunks):
    d = pl.program_id(1)
    bsz, nc, _ = u_ref.shape
    for b in range(bsz):
        ub = u_ref[b]
        sre[b] = _dot(ub, wsr_ref[0, 0])
        sim[b] = _dot(ub, wsi_ref[0, 0])
    lr = lr_ref[0, 0]
    li = li_ref[0, 0]

    def step(c, carry):
        new = []
        for b in range(bsz):
            xr, xi = carry[2 * b], carry[2 * b + 1]
            xre[b, pl.ds(c, 1), :] = xr
            xim[b, pl.ds(c, 1), :] = xi
            sr = sre[b, pl.ds(c, 1), :]
            si = sim[b, pl.ds(c, 1), :]
            new.append(lr * xr - li * xi + sr)
            new.append(lr * xi + li * xr + si)
        return tuple(new)

    zero = tuple(jnp.zeros((1, LANES), F32) for _ in range(2 * bsz))

    @pl.when(d == 0)
    def _():
        lax.fori_loop(0, nc, step, zero, unroll=4)

    @pl.when(d == 1)
    def _():
        carry = lax.fori_loop(0, n_ctx_chunks, lambda k, cr: step(n_ctx_chunks - 1 - k, cr), zero, unroll=4)
        lax.fori_loop(0, nc - n_ctx_chunks, lambda k, cr: step(nc - 1 - k, cr), carry, unroll=4)

    half = u_ref.shape[2] // 2
    for b in range(bsz):
        ub = u_ref[b]
        y_intra = jnp.concatenate([_dot(ub[:, :half], m_ref[0, 0]), _dot(ub[:, half:], m_ref[0, 1])], axis=-1)
        y = (y_intra + _dot(xre[b].astype(MXU_DTYPE), wor_ref[0, 0])
             + _dot(xim[b].astype(MXU_DTYPE), woi_ref[0, 0]))

        @pl.when(d == 0)
        def _():
            acc[b] = y

        @pl.when(d == 1)
        def _():
            o_ref[b] = (acc[b] + y).astype(o_ref.dtype)


def _s5_call(u_t, wts, n_ctx_chunks):
    m, wsr, wsi, wor, woi, lr, li = wts
    bsz, nc, width = u_t.shape
    pw = 2 * S5_CHUNK * S5_GROUP_CH
    n_pairs = width // pw
    blk = lambda shp: pl.BlockSpec((1, 1) + shp, lambda j, d: (d, j, 0, 0))
    return pl.pallas_call(
        functools.partial(_s5_kernel, n_ctx_chunks=n_ctx_chunks),
        grid=(n_pairs, 2),
        in_specs=[pl.BlockSpec((bsz, nc, pw), lambda j, d: (0, 0, j)),
                  pl.BlockSpec((1, 2, pw // 2, pw // 2), lambda j, d: (d, j, 0, 0)),
                  blk((pw, LANES)), blk((pw, LANES)), blk((LANES, pw)), blk((LANES, pw)),
                  blk((1, LANES)), blk((1, LANES))],
        out_specs=pl.BlockSpec((bsz, nc, pw), lambda j, d: (0, 0, j)),
        out_shape=jax.ShapeDtypeStruct((bsz, nc, width), MXU_DTYPE),
        scratch_shapes=[pltpu.VMEM((bsz, nc, LANES), F32) for _ in range(4)] + [pltpu.VMEM((bsz, nc, pw), F32)],
        compiler_params=_params("parallel", "arbitrary"),
        name="s5_scan",
    )(u_t, m, wsr, wsi, wor, woi, lr, li)


def _s5_weights(a_re, a_im, log_step, b_re, b_im, c_re, c_im):
    lc, g, p, h = S5_CHUNK, S5_GROUPS, S5_STATE, S5_GROUP_CH
    lam = lax.complex(jnp.minimum(a_re.astype(F32), S5_EIG_MAX), a_im.astype(F32))
    step = jnp.exp(log_step.astype(F32))[..., None]
    lam_bar = jnp.exp(lam * step)
    b_bar = ((lam_bar - 1.0) / lam)[..., None] * lax.complex(b_re.astype(F32), b_im.astype(F32))
    cc = lax.complex(c_re.astype(F32), c_im.astype(F32))
    dd = jnp.arange(lc + 1, dtype=F32)
    pw = jnp.exp((lam * step)[..., None] * dd)
    kern = jnp.real(jnp.einsum('zgop,zgpd,zgpi->zgdoi', cc, pw[..., :lc], b_bar))
    jj = np.arange(lc)[:, None]
    ii = np.arange(lc)[None, :]
    mats, wst, wout = [], [], []
    for z in range(2):
        lag = (ii - jj) if z == 0 else (jj - ii)
        ok = lag >= 0
        kz = kern[z][:, np.where(ok, lag, 0)]
        kz = jnp.where(ok[None, :, :, None, None], kz, 0.0)
        mats.append(kz.transpose(0, 1, 4, 2, 3).reshape(g, lc * h, lc * h))
        d_state = (lc - 1 - np.arange(lc)) if z == 0 else np.arange(lc)
        ws = pw[z][:, :, d_state][..., None] * b_bar[z][:, :, None, :]
        wst.append(ws.transpose(0, 2, 3, 1).reshape(g, lc * h, p))
        d_out = (np.arange(lc) + 1) if z == 0 else (lc - np.arange(lc))
        wo = cc[z][:, :, :, None] * pw[z][:, None, :, :][..., d_out]
        wout.append(wo.transpose(0, 2, 3, 1).reshape(g, p, lc * h))
    mats = jnp.stack(mats)
    wst = jnp.stack(wst)
    wout = jnp.stack(wout)

    def pair_rows(w):
        w = w.reshape(2, g // 2, 2, lc * h, p)
        z0 = jnp.zeros_like(w[:, :, 0])
        top = jnp.concatenate([w[:, :, 0], z0], axis=-1)
        bot = jnp.concatenate([z0, w[:, :, 1]], axis=-1)
        return jnp.concatenate([top, bot], axis=-2)

    def pair_cols(w):
        w = w.reshape(2, g // 2, 2, p, lc * h)
        z0 = jnp.zeros_like(w[:, :, 0])
        top = jnp.concatenate([w[:, :, 0], z0], axis=-1)
        bot = jnp.concatenate([z0, w[:, :, 1]], axis=-1)
        return jnp.concatenate([top, bot], axis=-2)

    lam_c = pw[..., lc].reshape(2, g // 2, 1, 2 * p)
    cast = lambda w: w.astype(MXU_DTYPE)
    return (cast(mats), cast(pair_rows(jnp.real(wst))), cast(pair_rows(jnp.imag(wst))),
            cast(pair_cols(jnp.real(wout))), cast(pair_cols(-jnp.imag(wout))),
            jnp.real(lam_c), jnp.imag(lam_c))


def _gelu_tanh(x):
    cdf = 0.5 * (1.0 + jnp.tanh(math.sqrt(2.0 / math.pi) * (x + 0.044715 * (x * x * x))))
    return x * cdf


def _outproj_kernel(x_ref, ya_ref, ysr_hbm, su_ref, yc_ref, modb_ref, modc_ref, woa_ref, wob_ref, woc_ref,
                    wglu_ref, bglu_ref, dsk_ref, g_ref, wr_ref, br_ref,
                    xo_ref, h_hbm, topi_ref, topw_ref, rank_ref, cnt_ref, carry, hbuf, hsem, ybuf, ysem,
                    *, ctx_len):
    segs = _mod_segments(modb_ref, modc_ref, ctx_len)
    n_steps = pl.num_programs(0) * pl.num_programs(1)
    step0 = pl.program_id(0) * pl.num_programs(1) + pl.program_id(1)
    slot0 = lax.rem(step0, 2)
    chunks = ybuf.shape[1]

    def y_fetch(st, sl_):
        for i in range(S5_CHUNK):
            pltpu.make_async_copy(ysr_hbm.at[pl.ds(st * chunks, chunks), pl.ds(S5_CH * i, S5_CH)],
                                  ybuf.at[sl_, :, i, :], ysem.at[sl_]).start()

    @pl.when(step0 == 0)
    def _():
        y_fetch(0, 0)

    @pl.when(step0 + 1 < n_steps)
    def _():
        y_fetch(step0 + 1, 1 - slot0)

    pltpu.make_async_copy(ybuf.at[slot0], ybuf.at[slot0], ysem.at[slot0]).wait()
    ys = ybuf[slot0].reshape(chunks * S5_CHUNK, S5_CH)
    y = dsk_ref[...] * su_ref[0] + ys
    gl = _gelu_tanh(y)
    yb = gl * jax.nn.sigmoid(_dot(gl.astype(MXU_DTYPE), wglu_ref[...]) + bglu_ref[...])
    mix = (_dot(ya_ref[0], woa_ref[...]) + _dot(yb.astype(MXU_DTYPE), wob_ref[...])
           + _dot(yc_ref[0], woc_ref[...]))
    x_in = x_ref[0]
    x = _by_segment(segs, lambda a, b, m: x_in[a:b] + m[2:3] * mix[a:b])
    xo_ref[0] = x
    y = _rms(x, g_ref[...])
    h = _by_segment(segs, lambda a, b, m: y[a:b] * (1.0 + m[4:5]) + m[3:4])
    dl = h_hbm.shape[-1]
    tile = pl.program_id(1)
    step = pl.program_id(0) * pl.num_programs(1) + tile
    slot = lax.rem(step, 2)

    def h_wait(sl_):
        pltpu.make_async_copy(hbuf.at[sl_], hbuf.at[sl_], hsem.at[sl_]).wait()

    @pl.when(step >= 2)
    def _():
        h_wait(slot)

    hbuf[slot] = h
    for sl in range(SUBLANES):
        pltpu.make_async_copy(hbuf.at[slot, :, pl.ds(dl * sl, dl)],
                              h_hbm.at[pl.program_id(0), pl.ds(tile * ROW_TILE, ROW_TILE), sl, :],
                              hsem.at[slot]).start()

    @pl.when(step == pl.num_programs(0) * pl.num_programs(1) - 1)
    def _():
        h_wait(1 - slot)
        h_wait(slot)

    logits = _dot3(wr_ref[...], h, nt=True) + br_ref[...]
    n_e, tm = logits.shape
    e_iota = lax.broadcasted_iota(jnp.int32, (n_e, tm), 0)
    vals, idxs = [], []
    for _ in range(TOP_K):
        mx = jnp.max(logits, axis=0, keepdims=True)
        ix = jnp.min(jnp.where(logits == mx, e_iota, n_e), axis=0, keepdims=True)
        vals.append(mx)
        idxs.append(ix)
        logits = jnp.where(e_iota == ix, -jnp.inf, logits)
    ex = [jnp.exp(v - vals[0]) for v in vals]
    den = ex[0] + ex[1] + ex[2] + ex[3]
    topi_ref[0] = jnp.concatenate(idxs, axis=0)
    topw_ref[0] = jnp.concatenate([e / den for e in ex], axis=0)

    @pl.when((pl.program_id(0) == 0) & (pl.program_id(1) == 0))
    def _():
        carry[...] = jnp.zeros_like(carry)

    sel = [ix == e_iota for ix in idxs]
    onehot = jnp.where(sel[0] | sel[1] | sel[2] | sel[3], 1.0, 0.0)
    before = (lax.broadcasted_iota(jnp.int32, (tm, tm), 0) < lax.broadcasted_iota(jnp.int32, (tm, tm), 1))
    pfx = _dot(onehot.astype(MXU_DTYPE), jnp.where(before, 1.0, 0.0).astype(MXU_DTYPE)) + carry[:, 0:1]
    rank_ref[0] = jnp.concatenate(
        [jnp.sum(jnp.where(sel[k], pfx, 0.0), axis=0, keepdims=True) for k in range(TOP_K)], axis=0)
    carry[...] = carry[...] + jnp.sum(onehot, axis=1, keepdims=True)
    cnt_ref[...] = carry[...]


def _outproj_call(xs, ya, ys, su, yc, mod_l, wts, ctx_len):
    bsz, s, d = xs.shape
    tm = ROW_TILE
    row = lambda b, j: (b, j, 0)
    const = lambda b, j: (0, 0)
    full = lambda a: pl.BlockSpec(a.shape, const)
    return pl.pallas_call(
        functools.partial(_outproj_kernel, ctx_len=ctx_len),
        grid=(bsz, s // tm),
        in_specs=[pl.BlockSpec((1, tm, d), row),
                  pl.BlockSpec((1, tm, NA_W), row),
                  pl.BlockSpec(memory_space=pl.ANY),
                  pl.BlockSpec((1, tm, S5_CH), row),
                  pl.BlockSpec((1, tm, SW_W), row),
                  pl.BlockSpec((1, 6, d), lambda b, j: (b, 0, 0)),
                  pl.BlockSpec((1, 6, d), lambda b, j: (bsz, 0, 0))] + [full(a) for a in wts],
        out_specs=[pl.BlockSpec((1, tm, d), row),
                   pl.BlockSpec(memory_space=pl.ANY),
                   pl.BlockSpec((1, TOP_K, tm), lambda b, j: (b, 0, j)),
                   pl.BlockSpec((1, TOP_K, tm), lambda b, j: (b, 0, j)),
                   pl.BlockSpec((1, TOP_K, tm), lambda b, j: (b, 0, j)),
                   pl.BlockSpec((N_EXPERTS, LANES), lambda b, j: (0, 0))],
        out_shape=[jax.ShapeDtypeStruct((bsz, s, d), F32),
                   jax.ShapeDtypeStruct((bsz, s, SUBLANES, d // SUBLANES), F32),
                   jax.ShapeDtypeStruct((bsz, TOP_K, s), jnp.int32),
                   jax.ShapeDtypeStruct((bsz, TOP_K, s), F32),
                   jax.ShapeDtypeStruct((bsz, TOP_K, s), F32),
                   jax.ShapeDtypeStruct((N_EXPERTS, LANES), F32)],
        scratch_shapes=[pltpu.VMEM((N_EXPERTS, LANES), F32), pltpu.VMEM((2, tm, d), F32),
                        pltpu.SemaphoreType.DMA((2,)),
                        pltpu.VMEM((2, tm // S5_CHUNK, S5_CHUNK, S5_CH), F32), pltpu.SemaphoreType.DMA((2,))],
        compiler_params=_params("arbitrary", "arbitrary"),
        name="outproj",
    )(xs, ya, ys, su, yc, mod_l, mod_l, *wts)


def _route_kernel(topi_ref, rank_ref, cnt_ref, dest_ref, blk_ref, ends_ref):
    idx = topi_ref[0]
    tm = idx.shape[1]
    e_iota = lax.broadcasted_iota(jnp.int32, (N_EXPERTS, tm), 0)
    counts = cnt_ref[:, 0:1]
    padded = jnp.ceil(counts * (1.0 / MOE_BLK)) * MOE_BLK
    r_i = lax.broadcasted_iota(jnp.int32, (N_EXPERTS, N_EXPERTS), 0)
    c_i = lax.broadcasted_iota(jnp.int32, (N_EXPERTS, N_EXPERTS), 1)
    padded_row = jnp.sum(jnp.where(r_i == c_i, padded, 0.0), axis=0, keepdims=True)
    pstart = jnp.sum(jnp.where(c_i < r_i, padded_row, 0.0), axis=1, keepdims=True)
    ends = pstart + padded
    rk = rank_ref[0]
    dest = [(jnp.sum(jnp.where(idx[k:k + 1] == e_iota, pstart, 0.0), axis=0, keepdims=True) + rk[k:k + 1]
             ).astype(jnp.int32) for k in range(TOP_K)]
    for u in range(tm // MOE_TILE):
        dest_ref[u] = jnp.concatenate([d[:, u * MOE_TILE:(u + 1) * MOE_TILE] for d in dest], axis=1)
    nb = blk_ref.shape[1]
    blk_start = (lax.broadcasted_iota(jnp.int32, (N_EXPERTS, nb), 1) * MOE_BLK).astype(F32)
    owner = jnp.sum(jnp.where(ends <= blk_start, 1.0, 0.0), axis=0, keepdims=True)
    blk_ref[...] = jnp.minimum(owner, N_EXPERTS - 1.0).astype(jnp.int32)
    ends_ref[...] = jnp.broadcast_to(ends, ends_ref.shape).astype(jnp.int32)


def _route_call(topi, rank, cnt, n_blocks):
    bsz, _, s = topi.shape
    tm = ROW_TILE
    sub = tm // MOE_TILE
    n_tiles = s // tm
    nb_pad = -(-n_blocks // LANES) * LANES
    tok = pl.BlockSpec((1, TOP_K, tm), lambda b, j: (b, 0, j))
    return pl.pallas_call(
        _route_kernel,
        grid=(bsz, n_tiles),
        in_specs=[tok, tok, pl.BlockSpec((N_EXPERTS, LANES), lambda b, j: (0, 0))],
        out_specs=[pl.BlockSpec((sub, 1, TOP_K * MOE_TILE), lambda b, j: (b * n_tiles + j, 0, 0)),
                   pl.BlockSpec((1, nb_pad), lambda b, j: (0, 0)),
                   pl.BlockSpec((N_EXPERTS, LANES), lambda b, j: (0, 0))],
        out_shape=[jax.ShapeDtypeStruct((bsz * n_tiles * sub, 1, TOP_K * MOE_TILE), jnp.int32),
                   jax.ShapeDtypeStruct((1, nb_pad), jnp.int32),
                   jax.ShapeDtypeStruct((N_EXPERTS, LANES), jnp.int32)],
        compiler_params=_params("arbitrary", "arbitrary"),
        name="route_dest",
    )(topi, rank, cnt)


def _dispatch_kernel(ends_ref, dest_ref, h_ref, xg_ref, zbuf, stage, sem, zsem, *, n_blocks):
    tm = h_ref.shape[1]

    @pl.when((pl.program_id(0) == 0) & (pl.program_id(1) == 0))
    def _():
        zbuf[...] = jnp.zeros_like(zbuf)

        def fill(row):
            return pltpu.make_async_copy(zbuf, xg_ref.at[pl.ds(pl.multiple_of(row, MOE_BLK), MOE_BLK)], zsem)

        def each(fn):
            for e in range(N_EXPERTS):
                begin = ends_ref[e - 1] if e else 0

                @pl.when(ends_ref[e] > begin)
                def _():
                    fn(fill(ends_ref[e] - MOE_BLK))

            def dead(i, c):
                fn(fill(i * MOE_BLK))
                return c

            lax.fori_loop(ends_ref[N_EXPERTS - 1] // MOE_BLK, n_blocks, dead, 0)

        each(lambda cp: cp.start())
        each(lambda cp: cp.wait())

    step = pl.program_id(0) * pl.num_programs(1) + pl.program_id(1)
    n_steps = pl.num_programs(0) * pl.num_programs(1)
    slot = lax.rem(step, 2)
    stage[slot] = h_ref[0]

    def body(t, c):
        for k in range(TOP_K):
            pltpu.make_async_copy(stage.at[slot, t], xg_ref.at[dest_ref[0, 0, k * tm + t]],
                                  sem.at[slot]).start(priority=k % 2)
        return c

    lax.fori_loop(0, tm, body, 0, unroll=8)

    def wait_tile(sl):
        pltpu.make_async_copy(xg_ref.at[pl.ds(0, TOP_K * tm)], xg_ref.at[pl.ds(0, TOP_K * tm)], sem.at[sl]).wait()

    @pl.when(step > 0)
    def _():
        wait_tile(1 - slot)

    @pl.when(step == n_steps - 1)
    def _():
        wait_tile(slot)


def _dispatch_call(ends, dest, h, n_blocks):
    bsz, s, _, dl = h.shape
    tm = MOE_TILE
    n_tiles = s // tm
    grid_spec = pltpu.PrefetchScalarGridSpec(
        num_scalar_prefetch=1,
        grid=(bsz, n_tiles),
        in_specs=[pl.BlockSpec((1, 1, TOP_K * tm), lambda b, j, en: (b * n_tiles + j, 0, 0), memory_space=pltpu.SMEM),
                  pl.BlockSpec((1, tm, SUBLANES, dl), lambda b, j, en: (b, j, 0, 0))],
        out_specs=pl.BlockSpec(memory_space=pl.ANY),
        scratch_shapes=[pltpu.VMEM((MOE_BLK, SUBLANES, dl), h.dtype), pltpu.VMEM((2, tm, SUBLANES, dl), h.dtype),
                        pltpu.SemaphoreType.DMA((2,)),
                        pltpu.SemaphoreType.DMA(())],
    )
    return pl.pallas_call(
        functools.partial(_dispatch_kernel, n_blocks=n_blocks),
        grid_spec=grid_spec,
        out_shape=jax.ShapeDtypeStruct((n_blocks * MOE_BLK, SUBLANES, dl), h.dtype),
        compiler_params=_params("arbitrary", "arbitrary"),
        name="moe_dispatch",
    )(ends, dest, h)


def _moe_kernel(blk_exp_ref, nact_ref, x_hbm, wgu_ref, bgu_ref, wd_ref, bd_ref, y_hbm, wgu_c, wd_c,
                xbuf, xsem, ybuf, ysem):
    i = pl.program_id(0)
    slot = lax.rem(i, 2)
    dl = x_hbm.shape[-1]

    def fetch(blk, sl_):
        for s in range(SUBLANES):
            pltpu.make_async_copy(x_hbm.at[pl.ds(blk * MOE_BLK, MOE_BLK), s, :],
                                  xbuf.at[sl_, :, pl.ds(dl * s, dl)], xsem.at[sl_]).start()

    @pl.when(i == 0)
    def _():
        fetch(0, 0)

    @pl.when(i + 1 < pl.num_programs(0))
    def _():
        fetch(i + 1, 1 - slot)

    def y_wait(sl_):
        pltpu.make_async_copy(ybuf.at[sl_], ybuf.at[sl_], ysem.at[sl_]).wait()

    @pl.when(i >= 2)
    def _():
        y_wait(slot)

    e = blk_exp_ref[i]
    prev = blk_exp_ref[jnp.maximum(i - 1, 0)]
    d, f2 = wgu_c.shape
    f = f2 // 2
    rows = 128

    @pl.when((i == 0) | (e != prev))
    def _():
        def cv(r, c):
            sl = pl.ds(pl.multiple_of(r * rows, rows), rows)
            wgu_c[sl, :] = wgu_ref[0, 0, sl, :].astype(wgu_c.dtype)
            return c
        lax.fori_loop(0, d // rows, cv, 0)

        def cv2(r, c):
            sl = pl.ds(pl.multiple_of(r * rows, rows), rows)
            wd_c[sl, :] = wd_ref[0, 0, sl, :].astype(wd_c.dtype)
            return c
        lax.fori_loop(0, f // rows, cv2, 0)

    pltpu.make_async_copy(xbuf.at[slot], xbuf.at[slot], xsem.at[slot]).wait()

    @pl.when(i < nact_ref[0])
    def _():
        gu = _dot(xbuf[slot].astype(MXU_DTYPE), wgu_c[...]) + bgu_ref[0, 0]
        gate = jnp.minimum(gu[:, :f], SWIGLU_LIMIT)
        up = jnp.clip(gu[:, f:], -SWIGLU_LIMIT, SWIGLU_LIMIT)
        act = gate * jax.nn.sigmoid(SWIGLU_ALPHA * gate) * (up + 1.0)
        ybuf[slot] = _dot(act.astype(MXU_DTYPE), wd_c[...]) + bd_ref[0, 0]

    @pl.when(i >= nact_ref[0])
    def _():
        ybuf[slot] = jnp.zeros(ybuf.shape[1:], ybuf.dtype)

    for s in range(SUBLANES):
        pltpu.make_async_copy(ybuf.at[slot, :, pl.ds(dl * s, dl)],
                              y_hbm.at[pl.ds(i * MOE_BLK, MOE_BLK), s, :], ysem.at[slot]).start()

    @pl.when(i == pl.num_programs(0) - 1)
    def _():
        y_wait(1 - slot)
        y_wait(slot)


def _moe_call(layer, blk_exp, n_active, xg, w_gate_up, b_gate_up, w_down, b_down):
    n_rows, _, dl = xg.shape
    depth, n_e, d, f2 = w_gate_up.shape
    f = f2 // 2
    n_blocks = n_rows // MOE_BLK
    assert n_blocks >= 2
    wsel = lambda i, be, na: (layer, be[i], 0, 0)
    grid_spec = pltpu.PrefetchScalarGridSpec(
        num_scalar_prefetch=2,
        grid=(n_blocks,),
        in_specs=[pl.BlockSpec(memory_space=pl.ANY),
                  pl.BlockSpec((1, 1, d, f2), wsel),
                  pl.BlockSpec((1, 1, 1, f2), wsel),
                  pl.BlockSpec((1, 1, f, d), wsel),
                  pl.BlockSpec((1, 1, 1, d), wsel)],
        out_specs=pl.BlockSpec(memory_space=pl.ANY),
        scratch_shapes=[pltpu.VMEM((d, f2), MXU_DTYPE), pltpu.VMEM((f, d), MXU_DTYPE),
                        pltpu.VMEM((2, MOE_BLK, d), F32), pltpu.SemaphoreType.DMA((2,)),
                        pltpu.VMEM((2, MOE_BLK, d), F32), pltpu.SemaphoreType.DMA((2,))],
    )
    return pl.pallas_call(
        _moe_kernel,
        grid_spec=grid_spec,
        out_shape=jax.ShapeDtypeStruct((n_rows, SUBLANES, dl), F32),
        compiler_params=_params("arbitrary"),
        name="moe_experts",
    )(blk_exp, n_active, xg, w_gate_up, b_gate_up.reshape(depth, n_e, 1, f2),
      w_down, b_down.reshape(depth, n_e, 1, d))


def _combine_kernel(dest_ref, dnext_ref, w_ref, gate_ref, x_hbm, yg_hbm, xo_hbm,
                    gbuf, xbuf, obuf, gsem, xsem, osem, *, n_tiles):
    tm = MOE_TILE
    dl = gbuf.shape[-1]
    step = pl.program_id(0) * n_tiles + pl.program_id(1)
    n_steps = pl.num_programs(0) * n_tiles
    slot = lax.rem(step, 2)

    def stream_copies(st, sl_, buf, hbm, sem, to_hbm):
        b = lax.div(st, n_tiles)
        r0 = lax.rem(st, n_tiles) * tm
        out = []
        for s in range(SUBLANES):
            rows = hbm.at[b, pl.ds(r0, tm), pl.ds(dl * s, dl)]
            tiles = buf.at[sl_, :, s, :]
            out.append(pltpu.make_async_copy(tiles, rows, sem.at[sl_]) if to_hbm
                       else pltpu.make_async_copy(rows, tiles, sem.at[sl_]))
        return out

    def fetch(dref, st, sl_):
        for cp in stream_copies(st, sl_, xbuf, x_hbm, xsem, False):
            cp.start()

        def body(t, c):
            for k in range(TOP_K):
                pltpu.make_async_copy(yg_hbm.at[dref[0, 0, k * tm + t]], gbuf.at[sl_, k, t],
                                      gsem.at[sl_]).start(priority=k % 2)
            return c

        lax.fori_loop(0, tm, body, 0, unroll=8)

    @pl.when(step == 0)
    def _():
        fetch(dest_ref, 0, 0)

    @pl.when(step + 1 < n_steps)
    def _():
        fetch(dnext_ref, step + 1, 1 - slot)

    def wait_all(buf, sem, sl_):
        pltpu.make_async_copy(buf.at[sl_], buf.at[sl_], sem.at[sl_]).wait()

    wait_all(gbuf, gsem, slot)
    wait_all(xbuf, xsem, slot)

    @pl.when(step >= 2)
    def _():
        wait_all(obuf, osem, slot)

    gate = gate_ref[0]

    def row(t, c):
        acc = gbuf[slot, 0, t] * w_ref[0, 0, t]
        for k in range(1, TOP_K):
            acc = acc + gbuf[slot, k, t] * w_ref[0, 0, k * tm + t]
        obuf[slot, t] = xbuf[slot, t] + gate * acc
        return c

    lax.fori_loop(0, tm, row, 0, unroll=8)
    for cp in stream_copies(step, slot, obuf, xo_hbm, osem, True):
        cp.start()

    @pl.when(step == n_steps - 1)
    def _():
        wait_all(obuf, osem, 1 - slot)
        wait_all(obuf, osem, slot)


def _combine_call(dest, topw, xs, mod_l, yg, ctx_len):
    bsz, s, d = xs.shape
    tm = MOE_TILE
    n_tiles = s // tm
    last = bsz * n_tiles - 1
    assert last >= 1
    dl = d // SUBLANES
    w_flat = topw.reshape(bsz, TOP_K, n_tiles, tm).transpose(0, 2, 1, 3).reshape(bsz * n_tiles, 1, TOP_K * tm)
    gate = mod_l[:, 5].reshape(MOD_ROWS, SUBLANES, dl)
    idx_spec = lambda ahead: pl.BlockSpec(
        (1, 1, TOP_K * tm), lambda b, j: (jnp.minimum(b * n_tiles + j + ahead, last), 0, 0), memory_space=pltpu.SMEM)
    tile_buf = lambda lead: pltpu.VMEM(lead + (tm, SUBLANES, dl), F32)
    return pl.pallas_call(
        functools.partial(_combine_kernel, n_tiles=n_tiles),
        grid=(bsz, n_tiles),
        in_specs=[idx_spec(0), idx_spec(1), idx_spec(0),
                  pl.BlockSpec((1, SUBLANES, dl), lambda b, j: (jnp.where(j < ctx_len // tm, bsz, b), 0, 0)),
                  pl.BlockSpec(memory_space=pl.ANY),
                  pl.BlockSpec(memory_space=pl.ANY)],
        out_specs=pl.BlockSpec(memory_space=pl.ANY),
        out_shape=jax.ShapeDtypeStruct((bsz, s, d), F32),
        scratch_shapes=[tile_buf((2, TOP_K)), tile_buf((2,)), tile_buf((2,)),
                        pltpu.SemaphoreType.DMA((2,)), pltpu.SemaphoreType.DMA((2,)), pltpu.SemaphoreType.DMA((2,))],
        compiler_params=_params("arbitrary", "arbitrary"),
        name="moe_combine",
    )(dest, dest, w_flat, gate, xs, yg)


def _final_kernel(x_ref, g_ref, o_ref):
    x = x_ref[0]
    o_ref[0] = x * lax.rsqrt(jnp.mean(x * x, axis=-1, keepdims=True) + RMS_EPS) * g_ref[...]


def _final_call(xs, g, ctx_len):
    bsz, s, d = xs.shape
    tm = MOE_TILE
    off = ctx_len // tm
    return pl.pallas_call(
        _final_kernel,
        grid=(bsz, (s - ctx_len) // tm),
        in_specs=[pl.BlockSpec((1, tm, d), lambda b, j: (b, j + off, 0)),
                  pl.BlockSpec((1, d), lambda b, j: (0, 0))],
        out_specs=pl.BlockSpec((1, tm, d), lambda b, j: (b, j, 0)),
        out_shape=jax.ShapeDtypeStruct((bsz, s - ctx_len, d), F32),
        compiler_params=_params("parallel", "parallel"),
        name="final_norm",
    )(xs, g)


def _rope_tables(seq, ctx_len):
    t = jnp.arange(seq)
    row = (t // GRID_W).astype(F32)
    col = (t % GRID_W).astype(F32)
    nf = HEAD_DIM // 4
    inv = ROPE_BASE ** (-jnp.arange(nf, dtype=F32) / nf)
    ar = row[:, None] * inv
    ac = col[:, None] * inv
    ang = jnp.concatenate([ar, ar, ac, ac], axis=-1)
    cos = jnp.concatenate([jnp.ones((ctx_len, HEAD_DIM), F32), jnp.cos(ang)], axis=0)
    sin = jnp.concatenate([jnp.zeros((ctx_len, HEAD_DIM), F32), jnp.sin(ang)], axis=0)
    reps = LANES // HEAD_DIM
    return jnp.tile(cos, (1, reps)), jnp.tile(sin, (1, reps))


def _rot_cols(w):
    q = HEAD_DIM // 4
    j = np.arange(HEAD_DIM)
    first = (j % (2 * q)) < q
    src = np.where(first, j + q, j - q)
    sign = np.where(first, -1.0, 1.0).astype(np.float32)
    n_heads = w.shape[1] // HEAD_DIM
    src_all = (np.arange(n_heads)[:, None] * HEAD_DIM + src[None, :]).reshape(-1)
    return w[:, src_all] * jnp.asarray(np.tile(sign, n_heads))


def _head_perm_cols(order):
    return (np.asarray(order)[:, None] * HEAD_DIM + np.arange(HEAD_DIM)[None, :]).reshape(-1)


def _inproj_weight(w_in_l):
    aq, ak, av, su, sq, sk, sv = jnp.split(
        w_in_l, np.cumsum([NA_W, NA_W, NA_W, S5_CH, SW_W, SW_KV_W])[:6].tolist(), axis=1)
    sq = sq[:, _head_perm_cols(SW_HEAD_ORDER)]
    return jnp.concatenate([aq, ak, av, sq, _rot_cols(sq), sk, _rot_cols(sk), sv, su], axis=1).astype(MXU_DTYPE)


def _swap_kernel(v_ref, o_ref, *, n_a, n_b, h):
    per = LANES // h
    lane_blk = lax.shift_right_logical(lax.broadcasted_iota(jnp.int32, (1, LANES), 1), h.bit_length() - 1)
    for b in range(n_b):
        for dv in range(n_a * h // LANES):
            acc = None
            for a in range(dv * per, (dv + 1) * per):
                col = a * n_b * h + (b // per) * LANES
                src = v_ref[:, col:col + LANES].astype(F32)
                moved = pltpu.roll(src, ((a - b) % per) * h, axis=1)
                acc = moved if acc is None else jnp.where(lane_blk == a % per, moved, acc)
            col = b * n_a * h + dv * LANES
            o_ref[:, col:col + LANES] = acc.astype(o_ref.dtype)


def _swap_call(v, rows, n_a, n_b, h, out_dtype):
    r, width = v.shape
    assert width == n_a * n_b * h and h & (h - 1) == 0 and LANES % h == 0 and r % rows == 0
    assert (n_a * h) % LANES == 0 and (n_b * h) % LANES == 0
    return pl.pallas_call(
        functools.partial(_swap_kernel, n_a=n_a, n_b=n_b, h=h),
        grid=(r // rows,),
        in_specs=[pl.BlockSpec((rows, width), lambda i: (i, 0))],
        out_specs=pl.BlockSpec((rows, width), lambda i: (i, 0)),
        out_shape=jax.ShapeDtypeStruct((r, width), out_dtype),
        compiler_params=_params("parallel"),
        name="s5_swap",
    )(v)


def _chunk_major(su_rows, bsz):
    nc = su_rows.shape[0] // bsz
    u = _swap_call(su_rows, nc, S5_CHUNK, S5_GROUPS, S5_GROUP_CH, MXU_DTYPE)
    return u.reshape(bsz, nc, S5_CHUNK * S5_CH)


def _token_rows(y_t):
    bsz, nc, width = y_t.shape
    return _swap_call(y_t.reshape(bsz * nc, width), nc, S5_GROUPS, S5_CHUNK, S5_GROUP_CH, F32)


def kernel(x, c, ctx, c_ctx, w_mod, b_mod, g_mix, w_in, w_out, na_rpb, s5_a_re, s5_a_im, s5_log_step,
           s5_b_re, s5_b_im, s5_c_re, s5_c_im, s5_d, s5_w_glu, s5_b_glu, sw_sinks, g_ffn, w_router, b_router,
           w_gate_up, b_gate_up, w_down, b_down, g_final):
    bsz, seq, d = x.shape
    ctx_len = ctx.shape[1]
    depth = w_mod.shape[0]
    s = ctx_len + seq
    assert bsz + 1 <= MOD_ROWS and s % ROW_TILE == 0 and ctx_len % MOE_TILE == 0 and seq % MOE_TILE == 0
    assert ROW_TILE % MOE_TILE == 0 and ROW_TILE % S5_CHUNK == 0 and bsz * (s // ROW_TILE) >= 2
    assert seq % GRID_W == 0 and ctx_len % S5_CHUNK == 0

    xs = jnp.concatenate([ctx, x], axis=1)
    cond = jnp.zeros((MOD_ROWS, d), F32).at[:bsz].set(c).at[bsz].set(c_ctx)
    mod = _mod_call(cond, w_mod, b_mod).reshape(depth, MOD_ROWS, 6, d)
    cos2, sin2 = _rope_tables(seq, ctx_len)

    n_assign = bsz * s * TOP_K
    n_blocks = -(-(n_assign + N_EXPERTS * (MOE_BLK - 1)) // MOE_BLK)
    sw_rows = _head_perm_cols(SW_HEAD_ORDER)

    w_cat = jax.vmap(_inproj_weight)(w_in)
    bias_tab = jax.vmap(_na_bias_table)(na_rpb)
    s5w = jax.vmap(_s5_weights)(s5_a_re, s5_a_im, s5_log_step, s5_b_re, s5_b_im, s5_c_re, s5_c_im)
    wo_a = w_out[:, :NA_W].astype(MXU_DTYPE)
    wo_b = w_out[:, NA_W:NA_W + S5_CH].astype(MXU_DTYPE)
    wo_c = w_out[:, NA_W + S5_CH:][:, sw_rows].astype(MXU_DTYPE)
    w_glu = s5_w_glu.astype(MXU_DTYPE)
    w_rt = jnp.swapaxes(w_router, 1, 2).astype(F32)

    for l in range(depth):
        mod_l = mod[l]
        naq, nak, nav, swq, swk, swv, su, su_rows = _inproj_call(
            xs, mod_l, g_mix[l].reshape(1, d), w_cat[l], cos2, sin2, ctx_len)
        ya = _na_call(naq, nak, nav, bias_tab[l], ctx_len)
        yc = _sw_call(sw_sinks[l], swq, swk, swv, ctx_len)
        ys = _token_rows(_s5_call(_chunk_major(su_rows, bsz), tuple(w[l] for w in s5w), ctx_len // S5_CHUNK))

        out_wts = (wo_a[l], wo_b[l], wo_c[l], w_glu[l], s5_b_glu[l].reshape(1, S5_CH).astype(F32),
                   s5_d[l].reshape(1, S5_CH).astype(F32), g_ffn[l].reshape(1, d),
                   w_rt[l], b_router[l].reshape(N_EXPERTS, 1).astype(F32))
        xs, h, topi, topw, rank, cnt = _outproj_call(xs, ya, ys, su, yc, mod_l, out_wts, ctx_len)

        dest, blk, ends = _route_call(topi, rank, cnt, n_blocks)
        ends = ends[:, 0]
        xg = _dispatch_call(ends, dest, h, n_blocks)
        yg = _moe_call(l, blk[0, :n_blocks], ends[-1:] // MOE_BLK, xg, w_gate_up, b_gate_up, w_down, b_down)
        xs = _combine_call(dest, topw, xs, mod_l, yg, ctx_len)

    return _final_call(xs, g_final.reshape(1, d), ctx_len)
```
